```python
import jax, jax.numpy as jnp
from jax import lax
import numpy as np

D_MODEL = 2048
BATCH = 16
SEQ = 2048
DEPTH = 4
DEC_BATCH = 4
DEC_SEQ = 4096
PAST_LEN = 128

N_META = 16
GRID_W = 64
NORM_EPS = 1e-6
N_MIXERS = 3
N_RG = (DEPTH + 2) // 3
N_NA = (DEPTH + 1) // 3
N_GDN = DEPTH // 3
CONV4_PAD_LEFT = 1

RG_WIDTH = D_MODEL
RG_BLOCK = 256
RG_BLOCKS = RG_WIDTH // RG_BLOCK
RG_CONV = 4
RG_C = 8.0

NA_HEAD_DIM = 128
NA_HEADS = D_MODEL // NA_HEAD_DIM
NA_WIN_R = 8
NA_WIN_C = 16
NA_QCB = 16
NA_KCB = NA_QCB + NA_WIN_C
NA_NCB = GRID_W // NA_QCB
NEG_INF = -1e30

GDN_DK = 128
GDN_DV = 128
GDN_QK_HEADS = D_MODEL // GDN_DK
GDN_V_HEADS = 2 * GDN_QK_HEADS
GDN_KEY_DIM = GDN_QK_HEADS * GDN_DK
GDN_VAL_DIM = GDN_V_HEADS * GDN_DV
GDN_CONV_DIM = 2 * GDN_KEY_DIM + GDN_VAL_DIM
GDN_PROJ = GDN_CONV_DIM + GDN_VAL_DIM + 4 * GDN_V_HEADS
GDN_CONV = 4
GDN_CHUNK = 64

D_FF = ((8 * D_MODEL // 3 + 255) // 256) * 256
FFN_CONV = 3

kernel_name = 'hybrid_bidir_encoder_rglru_natten_gdn'


def _rmsnorm(x, w):
    xf = x.astype(jnp.float32)
    y = xf * lax.rsqrt(jnp.mean(xf * xf, axis=-1, keepdims=True) + NORM_EPS)
    return (y * w.astype(jnp.float32)).astype(x.dtype)


def _dwconv(x, w, pad_left):
    K, T = w.shape[0], x.shape[1]
    xp = jnp.pad(x, ((0, 0), (pad_left, K - 1 - pad_left), (0, 0)))
    y = xp[:, 0:T] * w[0]
    for j in range(1, K):
        y = y + xp[:, j:j + T] * w[j]
    return y


def _linear_combine(left, right):
    a1, b1 = left
    a2, b2 = right
    return a1 * a2, a2 * b1 + b2


def _rglru_scan(xc, w_a, b_a, w_i, b_i, lam, reverse):
    B, T, W = xc.shape
    xb = xc.reshape(B, T, RG_BLOCKS, RG_BLOCK)
    r = jax.nn.sigmoid(jnp.einsum('btnc,ncd->btnd', xb, w_a.astype(jnp.float32)).reshape(B, T, W) + b_a.astype(jnp.float32))
    i = jax.nn.sigmoid(jnp.einsum('btnc,ncd->btnd', xb, w_i.astype(jnp.float32)).reshape(B, T, W) + b_i.astype(jnp.float32))
    log_a = -RG_C * r * jax.nn.softplus(-lam.astype(jnp.float32))
    a = jnp.exp(log_a)
    u = jnp.sqrt(-jnp.expm1(2.0 * log_a)) * (i * xc)
    _, hs = lax.associative_scan(_linear_combine, (a, u), axis=1, reverse=reverse)
    return hs


def _rglru_mixer(h, w_in, conv_w, conv_b, w_a, b_a, w_i, b_i, lam, w_out):
    y = h @ w_in
    gate = jax.nn.gelu(y[..., :RG_WIDTH], approximate=True)
    xc = (_dwconv(y[..., RG_WIDTH:], conv_w, CONV4_PAD_LEFT) + conv_b).astype(jnp.float32)
    hs = (_rglru_scan(xc, w_a[0], b_a[0], w_i[0], b_i[0], lam[0], False)
          + _rglru_scan(xc, w_a[1], b_a[1], w_i[1], b_i[1], lam[1], True))
    return (gate * hs.astype(h.dtype)) @ w_out


def _na_mixer(h, w_qkv, rpb, meta_bias, w_o):
    B, T, _ = h.shape
    L = T - N_META
    rows = L // GRID_W
    kh = min(NA_WIN_R, rows)
    qkv = (h @ w_qkv).reshape(B, T, 3, NA_HEADS, NA_HEAD_DIM)
    q = qkv[:, :, 0] * (NA_HEAD_DIM ** -0.5)
    k = qkv[:, :, 1]
    v = qkv[:, :, 2]
    qm, km, vm = q[:, :N_META], k[:, :N_META], v[:, :N_META]
    kg = k[:, N_META:].reshape(B, rows, GRID_W, NA_HEADS, NA_HEAD_DIM)
    vg = v[:, N_META:].reshape(B, rows, GRID_W, NA_HEADS, NA_HEAD_DIM)
    qg = q[:, N_META:].reshape(B, rows, GRID_W, NA_HEADS, NA_HEAD_DIM).transpose(1, 0, 2, 3, 4)
    mb = meta_bias.astype(jnp.float32)
    rpb_f = rpb.astype(jnp.float32)

    s_m = jnp.einsum('bqhd,bkhd->bhqk', qm, km).astype(jnp.float32) + mb[:, None, :]
    out_meta = jnp.einsum('bhqk,bkhd->bqhd', jax.nn.softmax(s_m, axis=-1).astype(v.dtype), vm)

    r_idx = np.arange(rows)
    row_start = np.clip(r_idx - NA_WIN_R // 2, 0, rows - kh)
    row_off = row_start[:, None] + np.arange(kh)[None, :] - r_idx[:, None] + NA_WIN_R - 1
    cols = np.arange(GRID_W)
    col_start = np.clip(cols - NA_WIN_C // 2, 0, GRID_W - NA_WIN_C).reshape(NA_NCB, NA_QCB)
    kblk_start = [int(np.clip(b * NA_QCB - NA_WIN_C // 2, 0, GRID_W - NA_KCB)) for b in range(NA_NCB)]
    key_col = np.array(kblk_start)[:, None] + np.arange(NA_KCB)[None, :]
    q_col = cols.reshape(NA_NCB, NA_QCB)
    col_valid = jnp.asarray((key_col[:, None, :] >= col_start[..., None]) & (key_col[:, None, :] < col_start[..., None] + NA_WIN_C))
    col_off = jnp.asarray(np.clip(key_col[:, None, :] - q_col[..., None], -(NA_WIN_C - 1), NA_WIN_C - 1) + NA_WIN_C - 1, jnp.int32)

    def row_attend(args):
        q_r, rs_r, roff_r = args
        k_r = lax.dynamic_slice_in_dim(kg, rs_r, kh, axis=1)
        v_r = lax.dynamic_slice_in_dim(vg, rs_r, kh, axis=1)
        k_b = jnp.stack([k_r[:, :, s:s + NA_KCB] for s in kblk_start], axis=1)
        v_b = jnp.stack([v_r[:, :, s:s + NA_KCB] for s in kblk_start], axis=1)
        q_b = q_r.reshape(B, NA_NCB, NA_QCB, NA_HEADS, NA_HEAD_DIM)
        s_loc = jnp.einsum('bnqhd,bnrkhd->bhnqrk', q_b, k_b).astype(jnp.float32)
        bias = rpb_f[:, roff_r[None, None, :, None], col_off[:, :, None, :]]
        s_loc = jnp.where(col_valid[:, :, None, :], s_loc + bias, NEG_INF)
        s_met = jnp.einsum('bnqhd,bmhd->bhnqm', q_b, km).astype(jnp.float32) + mb[:, None, None, :]
        s_all = jnp.concatenate([s_loc.reshape(B, NA_HEADS, NA_NCB, NA_QCB, kh * NA_KCB), s_met], axis=-1)
        p = jax.nn.softmax(s_all, axis=-1).astype(v.dtype)
        p_loc = p[..., :kh * NA_KCB].reshape(B, NA_HEADS, NA_NCB, NA_QCB, kh, NA_KCB)
        p_met = p[..., kh * NA_KCB:]
        o = jnp.einsum('bhnqrk,bnrkhd->bnqhd', p_loc, v_b) + jnp.einsum('bhnqm,bmhd->bnqhd', p_met, vm)
        return o.reshape(B, GRID_W, NA_HEADS, NA_HEAD_DIM)

    out_grid = lax.map(row_attend, (qg, jnp.asarray(row_start, jnp.int32), jnp.asarray(row_off, jnp.int32)))
    out_grid = out_grid.transpose(1, 0, 2, 3, 4).reshape(B, L, NA_HEADS, NA_HEAD_DIM)
    o = jnp.concatenate([out_meta, out_grid], axis=1).reshape(B, T, NA_HEADS * NA_HEAD_DIM)
    return o @ w_o


def _l2norm(x):
    return x * lax.rsqrt(jnp.sum(x * x, axis=-1, keepdims=True) + 1e-6)


def _chunk_gated_delta(q, k, v, beta, g):
    B, T, H, DK = q.shape
    DV = v.shape[-1]
    C = GDN_CHUNK
    N = T // C

    def to_chunks(t):
        return jnp.moveaxis(t.reshape((B, N, C) + t.shape[2:]), 3, 1)

    q = to_chunks(q) * (DK ** -0.5)
    k = to_chunks(k)
    v = to_chunks(v)
    beta = to_chunks(beta)
    g = jnp.cumsum(to_chunks(g), axis=-1)
    incl = jnp.tril(jnp.ones((C, C), dtype=bool))
    strict = jnp.tril(jnp.ones((C, C), dtype=bool), -1)
    decay = jnp.where(incl, jnp.exp(jnp.where(incl, g[..., :, None] - g[..., None, :], 0.0)), 0.0)
    kk = jnp.einsum('bhncd,bhnsd->bhncs', k, k)
    a_mat = jnp.where(strict, kk * decay, 0.0) * beta[..., :, None] + jnp.eye(C, dtype=q.dtype)
    rhs = jnp.concatenate([v * beta[..., None], k * (beta * jnp.exp(g))[..., None]], axis=-1)
    sol = lax.linalg.triangular_solve(a_mat, rhs, left_side=True, lower=True, unit_diagonal=True)
    u, w = sol[..., :DV], sol[..., DV:]
    qk = jnp.where(incl, jnp.einsum('bhncd,bhnsd->bhncs', q, k) * decay, 0.0)
    g_last = g[..., -1]
    k_st = k * jnp.exp(g_last[..., None] - g)[..., None]
    q_st = q * jnp.exp(g)[..., None]
    xs = tuple(jnp.moveaxis(t, 2, 0) for t in (q_st, k_st, u, w, qk, g_last))

    def step(S, inp):
        q_i, k_i, u_i, w_i, qk_i, gl_i = inp
        v_new = u_i - jnp.einsum('bhcd,bhde->bhce', w_i, S)
        o_i = jnp.einsum('bhcd,bhde->bhce', q_i, S) + jnp.einsum('bhcs,bhse->bhce', qk_i, v_new)
        S = S * jnp.exp(gl_i)[..., None, None] + jnp.einsum('bhcd,bhce->bhde', k_i, v_new)
        return S, o_i

    S0 = jnp.zeros((B, H, DK, DV), jnp.float32)
    _, o = lax.scan(step, S0, xs)
    return o.transpose(1, 0, 3, 2, 4).reshape(B, T, H, DV)


def _gdn_mixer(h, w_in, conv_w, a_log, dt_bias, norm_w, w_out):
    B, T, _ = h.shape
    proj = h @ w_in
    qkv = jax.nn.silu(_dwconv(proj[..., :GDN_CONV_DIM], conv_w, CONV4_PAD_LEFT)).astype(jnp.float32)
    z = proj[..., GDN_CONV_DIM:GDN_CONV_DIM + GDN_VAL_DIM].astype(jnp.float32).reshape(B, T, GDN_V_HEADS, GDN_DV)
    ba = proj[..., GDN_CONV_DIM + GDN_VAL_DIM:].astype(jnp.float32).reshape(B, T, 2, 2, GDN_V_HEADS)
    beta = jax.nn.sigmoid(ba[:, :, 0])
    g = -jnp.exp(a_log.astype(jnp.float32)) * jax.nn.softplus(ba[:, :, 1] + dt_bias.astype(jnp.float32))
    q = _l2norm(qkv[..., :GDN_KEY_DIM].reshape(B, T, GDN_QK_HEADS, GDN_DK))
    k = _l2norm(qkv[..., GDN_KEY_DIM:2 * GDN_KEY_DIM].reshape(B, T, GDN_QK_HEADS, GDN_DK))
    v = qkv[..., 2 * GDN_KEY_DIM:].reshape(B, T, GDN_V_HEADS, GDN_DV)
    rep = GDN_V_HEADS // GDN_QK_HEADS
    q = jnp.repeat(q, rep, axis=2)
    k = jnp.repeat(k, rep, axis=2)
    pad = (-T) % GDN_CHUNK

    def pad_t(t):
        return jnp.pad(t, [(0, 0), (pad, 0)] + [(0, 0)] * (t.ndim - 2))

    def flip(t):
        return jnp.flip(t, axis=1)

    q, k, v, beta, g = pad_t(q), pad_t(k), pad_t(v), pad_t(beta), pad_t(g)
    o_f = _chunk_gated_delta(q, k, v, beta[:, :, 0], g[:, :, 0])
    o_b = flip(_chunk_gated_delta(flip(q), flip(k), flip(v), flip(beta[:, :, 1]), flip(g[:, :, 1])))
    o = (o_f + o_b)[:, pad:]
    o = o * lax.rsqrt(jnp.mean(o * o, axis=-1, keepdims=True) + NORM_EPS) * norm_w.astype(jnp.float32) * jax.nn.silu(z)
    return o.reshape(B, T, GDN_VAL_DIM).astype(h.dtype) @ w_out


def _conv_ffn(h, w_gate, w_up, conv_w, conv_b, w_down):
    a = _dwconv(h @ w_gate, conv_w, FFN_CONV // 2) + conv_b
    return (jax.nn.silu(a) * (h @ w_up)) @ w_down


def _trunk(x, meta_tokens, mix_norm, ffn_norm, final_norm,
           rg_w_in, rg_conv_w, rg_conv_b, rg_w_a, rg_b_a, rg_w_i, rg_b_i, rg_lam, rg_w_out,
           na_w_qkv, na_rpb, na_meta_bias, na_w_o,
           gdn_w_in, gdn_conv_w, gdn_a_log, gdn_dt_bias, gdn_norm_w, gdn_w_out,
           ffn_w_gate, ffn_w_up, ffn_conv_w, ffn_conv_b, ffn_w_down):
    B = x.shape[0]
    meta = jnp.broadcast_to(meta_tokens.astype(x.dtype)[None], (B, N_META, D_MODEL))
    h = jnp.concatenate([meta, x], axis=1)
    for i in range(DEPTH):
        kind, j = i % N_MIXERS, i // N_MIXERS
        hn = _rmsnorm(h, mix_norm[i])
        if kind == 0:
            mix = _rglru_mixer(hn, rg_w_in[j], rg_conv_w[j], rg_conv_b[j], rg_w_a[j], rg_b_a[j],
                               rg_w_i[j], rg_b_i[j], rg_lam[j], rg_w_out[j])
        elif kind == 1:
            mix = _na_mixer(hn, na_w_qkv[j], na_rpb[j], na_meta_bias[j], na_w_o[j])
        else:
            mix = _gdn_mixer(hn, gdn_w_in[j], gdn_conv_w[j], gdn_a_log[j], gdn_dt_bias[j],
                             gdn_norm_w[j], gdn_w_out[j])
        h = h + mix
        h = h + _conv_ffn(_rmsnorm(h, ffn_norm[i]), ffn_w_gate[i], ffn_w_up[i], ffn_conv_w[i],
                          ffn_conv_b[i], ffn_w_down[i])
    return _rmsnorm(h, final_norm)[:, N_META:]


def setup_inputs(seed: int = 0) -> dict:
    key = jax.random.key(seed)
    ks = iter(jax.random.split(key, 40))

    def nrm(shape, scale):
        return jax.random.normal(next(ks), shape, jnp.float32) * scale

    def gain(shape):
        return 1.0 + nrm(shape, 0.02)

    u_lru = jax.random.uniform(next(ks), (N_RG, 2, RG_WIDTH), jnp.float32, 0.9, 0.999)
    a_lru = u_lru ** (1.0 / RG_C)
    rg_lam = jnp.log(a_lru) - jnp.log1p(-a_lru)
    gdn_a_log = jnp.log(jax.random.uniform(next(ks), (N_GDN, 2, GDN_V_HEADS), jnp.float32, 1.0, 16.0))
    return {
        'x_prompt': nrm((BATCH, SEQ, D_MODEL), 1.0),
        'x_sample': nrm((DEC_BATCH, DEC_SEQ, D_MODEL), 1.0),
        'meta_tokens': nrm((N_META, D_MODEL), 1.0),
        'mix_norm': gain((DEPTH, D_MODEL)),
        'ffn_norm': gain((DEPTH, D_MODEL)),
        'final_norm': gain((D_MODEL,)),
        'rg_w_in': nrm((N_RG, D_MODEL, 2 * RG_WIDTH), D_MODEL ** -0.5),
        'rg_conv_w': nrm((N_RG, RG_CONV, RG_WIDTH), RG_CONV ** -0.5),
        'rg_conv_b': nrm((N_RG, RG_WIDTH), 0.02),
        'rg_w_a': nrm((N_RG, 2, RG_BLOCKS, RG_BLOCK, RG_BLOCK), RG_BLOCK ** -0.5),
        'rg_b_a': nrm((N_RG, 2, RG_WIDTH), 0.02),
        'rg_w_i': nrm((N_RG, 2, RG_BLOCKS, RG_BLOCK, RG_BLOCK), RG_BLOCK ** -0.5),
        'rg_b_i': nrm((N_RG, 2, RG_WIDTH), 0.02),
        'rg_lam': rg_lam,
        'rg_w_out': nrm((N_RG, RG_WIDTH, D_MODEL), RG_WIDTH ** -0.5),
        'na_w_qkv': nrm((N_NA, D_MODEL, 3 * NA_HEADS * NA_HEAD_DIM), D_MODEL ** -0.5),
        'na_rpb': nrm((N_NA, NA_HEADS, 2 * NA_WIN_R - 1, 2 * NA_WIN_C - 1), 0.1),
        'na_meta_bias': nrm((N_NA, NA_HEADS, N_META), 0.1),
        'na_w_o': nrm((N_NA, NA_HEADS * NA_HEAD_DIM, D_MODEL), (NA_HEADS * NA_HEAD_DIM) ** -0.5),
        'gdn_w_in': nrm((N_GDN, D_MODEL, GDN_PROJ), D_MODEL ** -0.5),
        'gdn_conv_w': nrm((N_GDN, GDN_CONV, GDN_CONV_DIM), GDN_CONV ** -0.5),
        'gdn_a_log': gdn_a_log,
        'gdn_dt_bias': 1.0 + nrm((N_GDN, 2, GDN_V_HEADS), 0.1),
        'gdn_norm_w': gain((N_GDN, GDN_DV)),
        'gdn_w_out': nrm((N_GDN, GDN_VAL_DIM, D_MODEL), GDN_VAL_DIM ** -0.5),
        'ffn_w_gate': nrm((DEPTH, D_MODEL, D_FF), D_MODEL ** -0.5),
        'ffn_w_up': nrm((DEPTH, D_MODEL, D_FF), D_MODEL ** -0.5),
        'ffn_conv_w': nrm((DEPTH, FFN_CONV, D_FF), FFN_CONV ** -0.5),
        'ffn_conv_b': nrm((DEPTH, D_FF), 0.02),
        'ffn_w_down': nrm((DEPTH, D_FF, D_MODEL), D_FF ** -0.5),
    }


def reference(x_prompt, x_sample, meta_tokens, mix_norm, ffn_norm, final_norm,
              rg_w_in, rg_conv_w, rg_conv_b, rg_w_a, rg_b_a, rg_w_i, rg_b_i, rg_lam, rg_w_out,
              na_w_qkv, na_rpb, na_meta_bias, na_w_o,
              gdn_w_in, gdn_conv_w, gdn_a_log, gdn_dt_bias, gdn_norm_w, gdn_w_out,
              ffn_w_gate, ffn_w_up, ffn_conv_w, ffn_conv_b, ffn_w_down):
    weights = (meta_tokens, mix_norm, ffn_norm, final_norm,
               rg_w_in, rg_conv_w, rg_conv_b, rg_w_a, rg_b_a, rg_w_i, rg_b_i, rg_lam, rg_w_out,
               na_w_qkv, na_rpb, na_meta_bias, na_w_o,
               gdn_w_in, gdn_conv_w, gdn_a_log, gdn_dt_bias, gdn_norm_w, gdn_w_out,
               ffn_w_gate, ffn_w_up, ffn_conv_w, ffn_conv_b, ffn_w_down)
    y_prompt = _trunk(x_prompt, *weights)
    y_sample = _trunk(x_sample, *weights)
    return (y_prompt, y_sample)
```

```python
import functools

import numpy as np
import jax
import jax.numpy as jnp
from jax import lax
from jax.experimental import pallas as pl
from jax.experimental.pallas import tpu as pltpu

D_MODEL = 2048
N_META = 16
PAD = 48
HEAD_ROWS = PAD + N_META
CHUNK = 64
GRID_W = 64
NORM_EPS = 1e-6

RG_BLOCK = 256
RG_C = 8.0

NA_HEAD_DIM = 128
NA_HEADS = D_MODEL // NA_HEAD_DIM
NA_WIN_R = 8
NA_WIN_C = 16
NEG_INF = -1e30

GDN_DK = 128
GDN_QK_HEADS = D_MODEL // GDN_DK
GDN_V_HEADS = 2 * GDN_QK_HEADS
GDN_KEY_DIM = GDN_QK_HEADS * GDN_DK
GDN_VAL_DIM = GDN_V_HEADS * GDN_DK
GDN_CONV_DIM = 2 * GDN_KEY_DIM + GDN_VAL_DIM

V7X_VMEM_LIMIT = 56 * 1024 * 1024
HALO = 16

BF16 = jnp.bfloat16
F32 = jnp.float32


def _cparams(*sem):
    return pltpu.CompilerParams(dimension_semantics=sem, vmem_limit_bytes=V7X_VMEM_LIMIT)


def _pick_tile(n, cap, mult):
    best = None
    for t in range(mult, cap + 1, mult):
        if n % t == 0:
            best = t
    assert best is not None, (n, cap, mult)
    return best


def _sigmoid(x):
    return 1.0 / (1.0 + jnp.exp(-x))


def _softplus(x):
    return jnp.maximum(x, 0.0) + jnp.log(1.0 + jnp.exp(-jnp.abs(x)))


def _gelu_tanh(x):
    return 0.5 * x * (1.0 + jnp.tanh(0.7978845608028654 * (x + 0.044715 * x * x * x)))


def _rms_scale(x, w):
    ms = jnp.mean(x * x, axis=-1, keepdims=True)
    return x * lax.rsqrt(ms + NORM_EPS) * w


def _dot(a, b):
    return jnp.dot(a, b, preferred_element_type=F32)


def _dot_nt(a, b):
    return lax.dot_general(a, b, (((1,), (1,)), ((), ())), preferred_element_type=F32)


def _dot_tn(a, b):
    return lax.dot_general(a, b, (((0,), (0,)), ((), ())), preferred_element_type=F32)


def _rmsnorm_kernel(x_ref, w_ref, o_ref):
    o_ref[...] = _rms_scale(x_ref[...], w_ref[...]).astype(o_ref.dtype)


def _rmsnorm(h, w, out_dtype):
    M, D = h.shape
    tm = _pick_tile(M, 1024, 64)
    return pl.pallas_call(
        _rmsnorm_kernel,
        grid=(M // tm,),
        in_specs=[pl.BlockSpec((tm, D), lambda i: (i, 0)), pl.BlockSpec((1, D), lambda i: (0, 0))],
        out_specs=pl.BlockSpec((tm, D), lambda i: (i, 0)),
        out_shape=jax.ShapeDtypeStruct((M, D), out_dtype),
        compiler_params=_cparams("parallel"),
    )(h, w.reshape(1, D))


def _final_norm_kernel(x_ref, w_ref, o_ref):
    o_ref[0] = _rms_scale(x_ref[...], w_ref[...])


def _final_norm(h, w, B, T):
    D = h.shape[1]
    Tp = T + HEAD_ROWS
    tt = 512
    assert T % tt == 0
    return pl.pallas_call(
        _final_norm_kernel,
        grid=(B, T // tt),
        in_specs=[pl.BlockSpec((pl.Element(tt), pl.Element(D)),
                               lambda b, j: (pl.multiple_of(b * Tp + HEAD_ROWS + j * tt, CHUNK), 0)),
                  pl.BlockSpec((1, D), lambda b, j: (0, 0))],
        out_specs=pl.BlockSpec((1, tt, D), lambda b, j: (b, j, 0)),
        out_shape=jax.ShapeDtypeStruct((B, T, D), F32),
        compiler_params=_cparams("parallel", "parallel"),
    )(h, w.reshape(1, D))


def _matmul_kernel(x_ref, w_ref, o_ref, *, act):
    acc = _dot(x_ref[...], w_ref[...])
    if act == "gelu":
        acc = _gelu_tanh(acc)
    o_ref[...] = acc.astype(o_ref.dtype)


def _matmul(x, w, out_dtype, act=None):
    M, K = x.shape
    N = w.shape[1]
    tm = _pick_tile(M, 1024, 128)
    tn = _pick_tile(N, 512, 128)
    return pl.pallas_call(
        functools.partial(_matmul_kernel, act=act),
        grid=(M // tm, N // tn),
        in_specs=[pl.BlockSpec((tm, K), lambda i, j: (i, 0)), pl.BlockSpec((K, tn), lambda i, j: (0, j))],
        out_specs=pl.BlockSpec((tm, tn), lambda i, j: (i, j)),
        out_shape=jax.ShapeDtypeStruct((M, N), out_dtype),
        compiler_params=_cparams("parallel", "parallel"),
    )(x, w)


def _residual_norm_store(h_ref, acc, nw_ref, hout_ref, hn_ref):
    hnew = h_ref[...] + acc
    hout_ref[...] = hnew
    hn_ref[...] = _rms_scale(hnew, nw_ref[...]).astype(hn_ref.dtype)


def _proj_res_kernel(x_ref, w_ref, h_ref, nw_ref, hout_ref, hn_ref, acc_ref, *, nk):
    k = pl.program_id(1)

    @pl.when(k == 0)
    def _():
        acc_ref[...] = jnp.zeros_like(acc_ref)

    acc_ref[...] += _dot(x_ref[...], w_ref[...])

    @pl.when(k == nk - 1)
    def _():
        _residual_norm_store(h_ref, acc_ref[...], nw_ref, hout_ref, hn_ref)


def _proj_res(x, w, h, nw):
    M, K = x.shape
    D = w.shape[1]
    tm = _pick_tile(M, 512, 128)
    tk = 1024
    nk = K // tk
    return pl.pallas_call(
        functools.partial(_proj_res_kernel, nk=nk),
        grid=(M // tm, nk),
        in_specs=[pl.BlockSpec((tm, tk), lambda i, k: (i, k)),
                  pl.BlockSpec((tk, D), lambda i, k: (k, 0)),
                  pl.BlockSpec((tm, D), lambda i, k: (i, 0)),
                  pl.BlockSpec((1, D), lambda i, k: (0, 0))],
        out_specs=[pl.BlockSpec((tm, D), lambda i, k: (i, 0)), pl.BlockSpec((tm, D), lambda i, k: (i, 0))],
        out_shape=[jax.ShapeDtypeStruct((M, D), F32), jax.ShapeDtypeStruct((M, D), BF16)],
        scratch_shapes=[pltpu.VMEM((tm, D), F32)],
        compiler_params=_cparams("parallel", "arbitrary"),
    )(x, w, h, nw.reshape(1, D))


def _ffn_kernel(xp_ref, xm_ref, xn_ref, wg_ref, wu_ref, cw_ref, cb_ref, wd_ref, h_ref, nw_ref,
                hout_ref, hn_ref, xext_ref, acc_ref, *, tm, nf, ni):
    i = pl.program_id(0)
    f = pl.program_id(1)

    @pl.when(f == 0)
    def _():
        xext_ref[0:HALO, :] = xp_ref[...]
        xext_ref[HALO:HALO + tm, :] = xm_ref[...]
        xext_ref[HALO + tm:2 * HALO + tm, :] = xn_ref[...]
        acc_ref[...] = jnp.zeros_like(acc_ref)

    g = _dot(xext_ref[...], wg_ref[...])
    g_prev = g[HALO - 1:HALO - 1 + tm]
    g_here = g[HALO:HALO + tm]
    g_next = g[HALO + 1:HALO + 1 + tm]
    row = lax.broadcasted_iota(jnp.int32, (tm, 1), 0)
    g_next = jnp.where(jnp.logical_and(i == ni - 1, row == tm - 1), 0.0, g_next)
    cw = cw_ref[...]
    a = cw[0:1] * g_prev + cw[1:2] * g_here + cw[2:3] * g_next + cb_ref[...]
    up = _dot(xm_ref[...], wu_ref[...])
    y = (a * _sigmoid(a) * up).astype(BF16)
    acc_ref[...] += _dot(y, wd_ref[...])

    @pl.when(f == nf - 1)
    def _():
        _residual_norm_store(h_ref, acc_ref[...], nw_ref, hout_ref, hn_ref)


def _conv_ffn(hn, h, w_gate, w_up, conv_w, conv_b, w_down, nw):
    M, D = hn.shape
    F = w_gate.shape[1]
    tm = _pick_tile(M, 512, 128)
    tf = _pick_tile(F, 512, 128)
    ni, nf = M // tm, F // tf
    hb = tm // HALO
    nhb = M // HALO
    return pl.pallas_call(
        functools.partial(_ffn_kernel, tm=tm, nf=nf, ni=ni),
        grid=(ni, nf),
        in_specs=[pl.BlockSpec((HALO, D), lambda i, f: (jnp.maximum(i * hb - 1, 0), 0)),
                  pl.BlockSpec((tm, D), lambda i, f: (i, 0)),
                  pl.BlockSpec((HALO, D), lambda i, f: (jnp.minimum((i + 1) * hb, nhb - 1), 0)),
                  pl.BlockSpec((D, tf), lambda i, f: (0, f)),
                  pl.BlockSpec((D, tf), lambda i, f: (0, f)),
                  pl.BlockSpec((3, tf), lambda i, f: (0, f)),
                  pl.BlockSpec((1, tf), lambda i, f: (0, f)),
                  pl.BlockSpec((tf, D), lambda i, f: (f, 0)),
                  pl.BlockSpec((tm, D), lambda i, f: (i, 0)),
                  pl.BlockSpec((1, D), lambda i, f: (0, 0))],
        out_specs=[pl.BlockSpec((tm, D), lambda i, f: (i, 0)), pl.BlockSpec((tm, D), lambda i, f: (i, 0))],
        out_shape=[jax.ShapeDtypeStruct((M, D), F32), jax.ShapeDtypeStruct((M, D), BF16)],
        scratch_shapes=[pltpu.VMEM((tm + 2 * HALO, D), BF16), pltpu.VMEM((tm, D), F32)],
        compiler_params=_cparams("parallel", "arbitrary"),
    )(hn, hn, hn, w_gate, w_up, conv_w, conv_b.reshape(1, F), w_down, h, nw.reshape(1, D))


def _scan8(a, u, row, reverse):
    for s in (1, 2, 4):
        if reverse:
            a_sh = pltpu.roll(a, 8 - s, 0)
            u_sh = pltpu.roll(u, 8 - s, 0)
            m = row < 8 - s
        else:
            a_sh = pltpu.roll(a, s, 0)
            u_sh = pltpu.roll(u, s, 0)
            m = row >= s
        u = jnp.where(m, a * u_sh + u, u)
        a = jnp.where(m, a * a_sh, a)
    return a, u


def _rg_kernel(rec_ref, gate_ref, cw_ref, cb_ref, wa_ref, ba_ref, wi_ref, bi_ref, lam_ref, o_ref,
               xc_s, hs_s, *, Tp, rc):
    nch = Tp // rc
    W = RG_BLOCK
    cw = cw_ref[...]
    cb = cb_ref[...]
    row8 = lax.broadcasted_iota(jnp.int32, (8, W), 0)

    def gates(xc, d):
        xb = xc.astype(BF16)
        r = _sigmoid(_dot(xb, wa_ref[d, 0]) + ba_ref[d])
        ig = _sigmoid(_dot(xb, wi_ref[d, 0]) + bi_ref[d])
        log_a = (-RG_C) * r * _softplus(-lam_ref[d])
        a = jnp.exp(log_a)
        u = jnp.sqrt(1.0 - jnp.exp(2.0 * log_a)) * (ig * xc)
        return a, u

    def fwd_chunk(c, carry):
        r0 = pl.multiple_of(c * rc, CHUNK)
        top = rec_ref[0, pl.ds(pl.multiple_of(jnp.maximum(r0 - 8, 0), 8), 8), :]
        main = rec_ref[0, pl.ds(r0, rc), :]
        bot = rec_ref[0, pl.ds(pl.multiple_of(jnp.minimum(r0 + rc, Tp - 8), 8), 8), :]
        bot = jnp.where(r0 + rc < Tp, bot, 0.0)
        xw = jnp.concatenate([top, main, bot], axis=0)
        xc = cw[0:1] * xw[7:7 + rc] + cw[1:2] * xw[8:8 + rc] + cw[2:3] * xw[9:9 + rc] + cw[3:4] * xw[10:10 + rc] + cb
        t = r0 + lax.broadcasted_iota(jnp.int32, (rc, 1), 0)
        xc = jnp.where(t >= PAD, xc, 0.0)
        xc_s[pl.ds(r0, rc), :] = xc
        a, u = gates(xc, 0)
        hs = []
        for k in range(rc // 8):
            A, U = _scan8(a[8 * k:8 * k + 8], u[8 * k:8 * k + 8], row8, False)
            h = A * carry + U
            carry = h[7:8]
            hs.append(h)
        hs_s[pl.ds(r0, rc), :] = jnp.concatenate(hs, axis=0)
        return carry

    lax.fori_loop(0, nch, fwd_chunk, jnp.zeros((1, W), F32))

    def bwd_chunk(cc, carry):
        r0 = pl.multiple_of((nch - 1 - cc) * rc, CHUNK)
        xc = xc_s[pl.ds(r0, rc), :]
        a, u = gates(xc, 1)
        hs = [None] * (rc // 8)
        for k in reversed(range(rc // 8)):
            A, U = _scan8(a[8 * k:8 * k + 8], u[8 * k:8 * k + 8], row8, True)
            h = A * carry + U
            carry = h[0:1]
            hs[k] = h
        hsum = hs_s[pl.ds(r0, rc), :] + jnp.concatenate(hs, axis=0)
        o_ref[0, pl.ds(r0, rc), :] = (gate_ref[0, pl.ds(r0, rc), :].astype(F32) * hsum).astype(o_ref.dtype)
        return carry

    lax.fori_loop(0, nch, bwd_chunk, jnp.zeros((1, W), F32))


def _rg_scan(rec, gate, conv_w, conv_b, w_a, b_a, w_i, b_i, lam):
    B, Tp, W = rec.shape
    nb = W // RG_BLOCK
    rc = _pick_tile(Tp, 384, 64)
    blk = lambda b, n: (b, 0, n)
    vec = lambda b, n: (0, n)
    vec3 = lambda b, n: (0, 0, n)
    return pl.pallas_call(
        functools.partial(_rg_kernel, Tp=Tp, rc=rc),
        grid=(B, nb),
        in_specs=[pl.BlockSpec((1, Tp, RG_BLOCK), blk),
                  pl.BlockSpec((1, Tp, RG_BLOCK), blk),
                  pl.BlockSpec((4, RG_BLOCK), vec),
                  pl.BlockSpec((1, RG_BLOCK), vec),
                  pl.BlockSpec((2, 1, RG_BLOCK, RG_BLOCK), lambda b, n: (0, n, 0, 0)),
                  pl.BlockSpec((2, 1, RG_BLOCK), vec3),
                  pl.BlockSpec((2, 1, RG_BLOCK, RG_BLOCK), lambda b, n: (0, n, 0, 0)),
                  pl.BlockSpec((2, 1, RG_BLOCK), vec3),
                  pl.BlockSpec((2, 1, RG_BLOCK), vec3)],
        out_specs=pl.BlockSpec((1, Tp, RG_BLOCK), blk),
        out_shape=jax.ShapeDtypeStruct((B, Tp, W), BF16),
        scratch_shapes=[pltpu.VMEM((Tp, RG_BLOCK), F32), pltpu.VMEM((Tp, RG_BLOCK), F32)],
        compiler_params=_cparams("parallel", "parallel"),
    )(rec, gate, conv_w, conv_b.reshape(1, W), w_a.astype(BF16), b_a.reshape(2, 1, W),
      w_i.astype(BF16), b_i.reshape(2, 1, W), lam.reshape(2, 1, W))


def _rglru_mixer(hn, h, B, Tp, w_in, conv_w, conv_b, w_a, b_a, w_i, b_i, lam, w_out, nw):
    W = D_MODEL
    w_in = w_in.astype(BF16)
    gate = _matmul(hn, w_in[:, :W], BF16, act="gelu")
    rec = _matmul(hn, w_in[:, W:], F32)
    y = _rg_scan(rec.reshape(B, Tp, W), gate.reshape(B, Tp, W), conv_w, conv_b, w_a, b_a, w_i, b_i, lam)
    return _proj_res(y.reshape(B * Tp, W), w_out.astype(BF16), h, nw)


def _na_kernel(q_ref, k_ref, v_ref, bias_ref, mb_ref, o_ref, *, rows):
    scale = NA_HEAD_DIM ** -0.5
    kh = NA_WIN_R
    mb = mb_ref[0]
    km = k_ref[0, PAD:HEAD_ROWS, :]
    vm = v_ref[0, PAD:HEAD_ROWS, :]

    o_ref[0, 0:PAD, :] = jnp.zeros((PAD, NA_HEAD_DIM), o_ref.dtype)
    qm = q_ref[0, PAD:HEAD_ROWS, :]
    s_m = _dot_nt(qm, km) * scale + mb
    p_m = jnp.exp(s_m - jnp.max(s_m, axis=-1, keepdims=True))
    o_m = _dot(p_m.astype(BF16), vm) / jnp.sum(p_m, axis=-1, keepdims=True)
    o_ref[0, PAD:HEAD_ROWS, :] = o_m.astype(o_ref.dtype)

    def row_body(r, _):
        rs = jnp.clip(r - NA_WIN_R // 2, 0, rows - kh)
        q0 = pl.multiple_of(HEAD_ROWS + r * GRID_W, GRID_W)
        k0 = pl.multiple_of(HEAD_ROWS + rs * GRID_W, GRID_W)
        q_r = q_ref[0, pl.ds(q0, GRID_W), :]
        k_w = k_ref[0, pl.ds(k0, kh * GRID_W), :]
        v_w = v_ref[0, pl.ds(k0, kh * GRID_W), :]
        s = _dot_nt(q_r, k_w) * scale + bias_ref[0, r - rs]
        s_met = _dot_nt(q_r, km) * scale + mb
        m = jnp.maximum(jnp.max(s, axis=-1, keepdims=True), jnp.max(s_met, axis=-1, keepdims=True))
        p = jnp.exp(s - m)
        p_met = jnp.exp(s_met - m)
        l = jnp.sum(p, axis=-1, keepdims=True) + jnp.sum(p_met, axis=-1, keepdims=True)
        o = (_dot(p.astype(BF16), v_w) + _dot(p_met.astype(BF16), vm)) / l
        o_ref[0, pl.ds(q0, GRID_W), :] = o.astype(o_ref.dtype)
        return 0

    lax.fori_loop(0, rows, row_body, 0)


def _na_bias_table(rpb, rows):
    kh = min(NA_WIN_R, rows)
    cols = np.arange(GRID_W)
    col_start = np.clip(cols - NA_WIN_C // 2, 0, GRID_W - NA_WIN_C)
    valid = (cols[None, :] >= col_start[:, None]) & (cols[None, :] < col_start[:, None] + NA_WIN_C)
    col_off = np.clip(cols[None, :] - cols[:, None], -(NA_WIN_C - 1), NA_WIN_C - 1) + NA_WIN_C - 1
    d = np.arange(NA_WIN_R)
    row_off = np.arange(kh)[None, :] - d[:, None] + NA_WIN_R - 1
    tab = rpb.astype(F32)[:, row_off[:, None, :, None], col_off[None, :, None, :]]
    tab = jnp.where(jnp.asarray(valid)[None, None, :, None, :], tab, NEG_INF)
    return tab.reshape(rpb.shape[0], NA_WIN_R, GRID_W, kh * GRID_W)


def _na_attention(qkv, rpb, meta_bias):
    B, Tp, _ = qkv.shape
    rows = (Tp - HEAD_ROWS) // GRID_W
    assert rows >= NA_WIN_R
    H = NA_HEADS
    bias = _na_bias_table(rpb, rows)
    return pl.pallas_call(
        functools.partial(_na_kernel, rows=rows),
        grid=(B, H),
        in_specs=[pl.BlockSpec((1, Tp, NA_HEAD_DIM), lambda b, h: (b, 0, h)),
                  pl.BlockSpec((1, Tp, NA_HEAD_DIM), lambda b, h: (b, 0, H + h)),
                  pl.BlockSpec((1, Tp, NA_HEAD_DIM), lambda b, h: (b, 0, 2 * H + h)),
                  pl.BlockSpec((1, NA_WIN_R, GRID_W, NA_WIN_R * GRID_W), lambda b, h: (h, 0, 0, 0)),
                  pl.BlockSpec((1, 1, N_META), lambda b, h: (h, 0, 0))],
        out_specs=pl.BlockSpec((1, Tp, NA_HEAD_DIM), lambda b, h: (b, 0, h)),
        out_shape=jax.ShapeDtypeStruct((B, Tp, H * NA_HEAD_DIM), BF16),
        compiler_params=_cparams("parallel", "parallel"),
    )(qkv, qkv, qkv, bias, meta_bias.astype(F32).reshape(H, 1, N_META))


def _na_mixer(hn, h, B, Tp, w_qkv, rpb, meta_bias, w_o, nw):
    qkv = _matmul(hn, w_qkv.astype(BF16), BF16)
    o = _na_attention(qkv.reshape(B, Tp, -1), rpb, meta_bias)
    return _proj_res(o.reshape(B * Tp, -1), w_o.astype(BF16), h, nw)


def _gdn_gate_kernel(x_ref, w_ref, al_ref, dt_ref, o_ref):
    y = _dot(x_ref[...], w_ref[...])
    lane = lax.broadcasted_iota(jnp.int32, y.shape, 1)
    g = -jnp.exp(al_ref[...]) * _softplus(y + dt_ref[...])
    o_ref[...] = jnp.where(lane < 2 * GDN_V_HEADS, _sigmoid(y), g)


def _gdn_gates(hn, w_ba, a_log, dt_bias):
    M, K = hn.shape
    N = 4 * GDN_V_HEADS
    tm = _pick_tile(M, 1024, 128)
    zeros = jnp.zeros((2 * GDN_V_HEADS,), F32)
    al = jnp.concatenate([zeros, a_log.astype(F32).reshape(-1)]).reshape(1, N)
    dt = jnp.concatenate([zeros, dt_bias.astype(F32).reshape(-1)]).reshape(1, N)
    return pl.pallas_call(
        _gdn_gate_kernel,
        grid=(M // tm,),
        in_specs=[pl.BlockSpec((tm, K), lambda i: (i, 0)), pl.BlockSpec((K, N), lambda i: (0, 0)),
                  pl.BlockSpec((1, N), lambda i: (0, 0)), pl.BlockSpec((1, N), lambda i: (0, 0))],
        out_specs=pl.BlockSpec((tm, N), lambda i: (i, 0)),
        out_shape=jax.ShapeDtypeStruct((M, N), F32),
        compiler_params=_cparams("parallel"),
    )(hn, w_ba, al, dt)


def _gdn_conv_kernel(x_ref, cw_ref, o_ref, *, Tp, rc, n_qk_blocks):
    j = pl.program_id(1)
    cw = cw_ref[...]
    nch = Tp // rc

    def chunk(c, _):
        r0 = pl.multiple_of(c * rc, CHUNK)
        top = x_ref[0, pl.ds(pl.multiple_of(jnp.maximum(r0 - 8, 0), 8), 8), :]
        main = x_ref[0, pl.ds(r0, rc), :]
        bot = x_ref[0, pl.ds(pl.multiple_of(jnp.minimum(r0 + rc, Tp - 8), 8), 8), :]
        bot = jnp.where(r0 + rc < Tp, bot, 0.0)
        xw = jnp.concatenate([top, main, bot], axis=0)
        a = cw[0:1] * xw[7:7 + rc] + cw[1:2] * xw[8:8 + rc] + cw[2:3] * xw[9:9 + rc] + cw[3:4] * xw[10:10 + rc]
        y = a * _sigmoid(a)
        t = r0 + lax.broadcasted_iota(jnp.int32, (rc, 1), 0)
        y = jnp.where(t >= PAD, y, 0.0)
        nrm = lax.rsqrt(jnp.sum(y * y, axis=-1, keepdims=True) + 1e-6)
        y = y * jnp.where(j < n_qk_blocks, nrm, 1.0)
        o_ref[0, pl.ds(r0, rc), :] = y.astype(o_ref.dtype)
        return 0

    lax.fori_loop(0, nch, chunk, 0)


def _gdn_conv(proj3, conv_w):
    B, Tp, _ = proj3.shape
    nblk = GDN_CONV_DIM // GDN_DK
    rc = _pick_tile(Tp, 384, 64)
    return pl.pallas_call(
        functools.partial(_gdn_conv_kernel, Tp=Tp, rc=rc, n_qk_blocks=2 * GDN_QK_HEADS),
        grid=(B, nblk),
        in_specs=[pl.BlockSpec((1, Tp, GDN_DK), lambda b, j: (b, 0, j)),
                  pl.BlockSpec((4, GDN_DK), lambda b, j: (0, j))],
        out_specs=pl.BlockSpec((1, Tp, GDN_DK), lambda b, j: (b, 0, j)),
        out_shape=jax.ShapeDtypeStruct((B, Tp, GDN_CONV_DIM), BF16),
        compiler_params=_cparams("parallel", "parallel"),
    )(proj3, conv_w)


def _split3_dot(x, m):
    hi = x.astype(BF16)
    r1 = x - hi.astype(F32)
    mid = r1.astype(BF16)
    lo = (r1 - mid.astype(F32)).astype(BF16)
    return _dot(hi, m) + _dot(mid, m) + _dot(lo, m)


def _gdn_kernel(q_ref, k_ref, v_ref, z_ref, bgc_ref, bgr_ref, nw_ref, o_ref, S_s, oacc_s, *, Tp):
    kh = pl.program_id(1)
    C = CHUNK
    NC = Tp // C
    HV = GDN_V_HEADS
    DK = GDN_DK
    scale = DK ** -0.5

    oacc_s[...] = jnp.zeros_like(oacc_s)
    S_s[...] = jnp.zeros_like(S_s)

    ri = lax.broadcasted_iota(jnp.int32, (C, C), 0)
    ci = lax.broadcasted_iota(jnp.int32, (C, C), 1)
    lower = ri >= ci
    upper = ri <= ci
    eye = (ri == ci).astype(F32)
    ones_lower = lower.astype(BF16)
    ones_upper = upper.astype(BF16)
    lane = lax.broadcasted_iota(jnp.int32, (C, 4 * HV), 1)
    rowi = lax.broadcasted_iota(jnp.int32, (C, 4 * HV), 0)

    def col(x, c):
        return jnp.sum(jnp.where(lane == c, x, 0.0), axis=1, keepdims=True)

    def one_chunk(n, reverse):
        r0 = pl.multiple_of(n * C, C)
        q = q_ref[0, pl.ds(r0, C), :]
        k = k_ref[0, pl.ds(r0, C), :]
        kf = k.astype(F32)
        bg = bgc_ref[0, pl.ds(r0, C), :]
        pre = bg
        for s in (1, 2, 4, 8, 16, 32):
            pre = pre + jnp.where(rowi >= s, pltpu.roll(pre, s, 0), 0.0)
        if reverse:
            cum = pre[C - 1:C] - pre + bg
        else:
            cum = pre
        rows = bgr_ref[0, n, 0]
        cum_rows = _split3_dot(rows, ones_lower if reverse else ones_upper)
        kk = _dot_nt(k, k)
        qk = _dot_nt(q, k) * scale
        incl = upper if reverse else lower
        strict = (ri < ci) if reverse else (ri > ci)
        d = 1 if reverse else 0
        for j in range(2):
            inst = 2 * d + j
            vh = 2 * kh + j
            beta_c = col(bg, d * HV + vh)
            gc_c = col(cum, 2 * HV + d * HV + vh)
            gc_r = cum_rows[4 + 2 * d + j:5 + 2 * d + j]
            decay = jnp.where(incl, jnp.exp(jnp.where(incl, gc_c - gc_r, 0.0)), 0.0)
            L = jnp.where(strict, kk * decay, 0.0) * beta_c
            Tm = eye - L
            P = L
            for _ in range(5):
                Pb = P.astype(BF16)
                P = _dot(Pb, Pb)
                Tm = Tm + _dot(Tm.astype(BF16), P.astype(BF16))
            Tb = Tm.astype(BF16)
            v = v_ref[0, pl.ds(r0, C), j * DK:(j + 1) * DK].astype(F32)
            e_c = jnp.exp(gc_c)
            u = _dot(Tb, (v * beta_c).astype(BF16))
            w = _dot(Tb, (kf * (beta_c * e_c)).astype(BF16))
            qkd = jnp.where(incl, qk * decay, 0.0)
            g_last = gc_c[0:1] if reverse else gc_c[C - 1:C]
            k_st = kf * jnp.exp(g_last - gc_c)
            q_st = q.astype(F32) * (scale * e_c)
            S = S_s[inst]
            Sb = S.astype(BF16)
            v_new = u - _dot(w.astype(BF16), Sb)
            vb = v_new.astype(BF16)
            o = _dot(q_st.astype(BF16), Sb) + _dot(qkd.astype(BF16), vb)
            S_s[inst] = S * jnp.exp(g_last) + _dot_tn(k_st.astype(BF16), vb)
            oacc_s[pl.ds(r0, C), j * DK:(j + 1) * DK] += o

    def body(it, _):
        one_chunk(it, False)
        one_chunk(NC - 1 - it, True)
        return 0

    lax.fori_loop(0, NC, body, 0)

    nw = nw_ref[...]

    def fin(n, _):
        r0 = pl.multiple_of(n * C, C)
        for j in range(2):
            o = oacc_s[pl.ds(r0, C), j * DK:(j + 1) * DK]
            z = z_ref[0, pl.ds(r0, C), j * DK:(j + 1) * DK]
            y = o * lax.rsqrt(jnp.mean(o * o, axis=-1, keepdims=True) + NORM_EPS) * nw * (z * _sigmoid(z))
            o_ref[0, pl.ds(r0, C), j * DK:(j + 1) * DK] = y.astype(o_ref.dtype)
        return 0

    lax.fori_loop(0, NC, fin, 0)


def _gdn_delta(qkv, proj3, bg, norm_w):
    B, Tp, _ = qkv.shape
    NC = Tp // CHUNK
    KH = GDN_QK_HEADS
    HV = GDN_V_HEADS
    bgr = bg.reshape(B, NC, CHUNK, 4, KH, 2).transpose(0, 1, 4, 3, 5, 2).reshape(B, NC, KH, 8, CHUNK)
    zoff = GDN_CONV_DIM // (2 * GDN_DK)
    return pl.pallas_call(
        functools.partial(_gdn_kernel, Tp=Tp),
        grid=(B, KH),
        in_specs=[pl.BlockSpec((1, Tp, GDN_DK), lambda b, h: (b, 0, h)),
                  pl.BlockSpec((1, Tp, GDN_DK), lambda b, h: (b, 0, KH + h)),
                  pl.BlockSpec((1, Tp, 2 * GDN_DK), lambda b, h: (b, 0, KH + h)),
                  pl.BlockSpec((1, Tp, 2 * GDN_DK), lambda b, h: (b, 0, zoff + h)),
                  pl.BlockSpec((1, Tp, 4 * HV), lambda b, h: (b, 0, 0)),
                  pl.BlockSpec((1, NC, 1, 8, CHUNK), lambda b, h: (b, 0, h, 0, 0)),
                  pl.BlockSpec((1, GDN_DK), lambda b, h: (0, 0))],
        out_specs=pl.BlockSpec((1, Tp, 2 * GDN_DK), lambda b, h: (b, 0, h)),
        out_shape=jax.ShapeDtypeStruct((B, Tp, GDN_VAL_DIM), BF16),
        scratch_shapes=[pltpu.VMEM((4, GDN_DK, GDN_DK), F32), pltpu.VMEM((Tp, 2 * GDN_DK), F32)],
        compiler_params=_cparams("parallel", "parallel"),
    )(qkv, qkv, qkv, proj3, bg, bgr, norm_w.astype(F32).reshape(1, GDN_DK))


def _gdn_mixer(hn, h, B, Tp, w_in, conv_w, a_log, dt_bias, norm_w, w_out, nw):
    nz = GDN_CONV_DIM + GDN_VAL_DIM
    w_in = w_in.astype(BF16)
    proj = _matmul(hn, w_in[:, :nz], F32)
    bg = _gdn_gates(hn, w_in[:, nz:], a_log, dt_bias)
    proj3 = proj.reshape(B, Tp, nz)
    qkv = _gdn_conv(proj3, conv_w)
    o = _gdn_delta(qkv, proj3, bg.reshape(B, Tp, -1), norm_w)
    return _proj_res(o.reshape(B * Tp, -1), w_out.astype(BF16), h, nw)


def _trunk(x, meta_tokens, mix_norm, ffn_norm, final_norm, rg, na, gdn, ffn):
    B, T, D = x.shape
    Tp = T + HEAD_ROWS
    depth = mix_norm.shape[0]
    h3 = jnp.concatenate([jnp.zeros((B, PAD, D), F32),
                          jnp.broadcast_to(meta_tokens.astype(F32)[None], (B, N_META, D)),
                          x.astype(F32)], axis=1)
    h = h3.reshape(B * Tp, D)
    hn = _rmsnorm(h, mix_norm[0], BF16)
    for i in range(depth):
        kind, j = i % 3, i // 3
        if kind == 0:
            h, hn = _rglru_mixer(hn, h, B, Tp, *[p[j] for p in rg], ffn_norm[i])
        elif kind == 1:
            h, hn = _na_mixer(hn, h, B, Tp, *[p[j] for p in na], ffn_norm[i])
        else:
            h, hn = _gdn_mixer(hn, h, B, Tp, *[p[j] for p in gdn], ffn_norm[i])
        w_gate, w_up, conv_w, conv_b, w_down = [p[i] for p in ffn]
        next_norm = mix_norm[i + 1] if i + 1 < depth else final_norm
        h, hn = _conv_ffn(hn, h, w_gate.astype(BF16), w_up.astype(BF16), conv_w, conv_b,
                          w_down.astype(BF16), next_norm)
    return _final_norm(h, final_norm, B, T)


def kernel(x_prompt, x_sample, meta_tokens, mix_norm, ffn_norm, final_norm, rg_w_in, rg_conv_w, rg_conv_b, rg_w_a, rg_b_a, rg_w_i, rg_b_i, rg_lam, rg_w_out, na_w_qkv, na_rpb, na_meta_bias, na_w_o, gdn_w_in, gdn_conv_w, gdn_a_log, gdn_dt_bias, gdn_norm_w, gdn_w_out, ffn_w_gate, ffn_w_up, ffn_conv_w, ffn_conv_b, ffn_w_down):
    rg = (rg_w_in, rg_conv_w, rg_conv_b, rg_w_a, rg_b_a, rg_w_i, rg_b_i, rg_lam, rg_w_out)
    na = (na_w_qkv, na_rpb, na_meta_bias, na_w_o)
    gdn = (gdn_w_in, gdn_conv_w, gdn_a_log, gdn_dt_bias, gdn_norm_w, gdn_w_out)
    ffn = (ffn_w_gate, ffn_w_up, ffn_conv_w, ffn_conv_b, ffn_w_down)
    y_prompt = _trunk(x_prompt, meta_tokens, mix_norm, ffn_norm, final_norm, rg, na, gdn, ffn)
    y_sample = _trunk(x_sample, meta_tokens, mix_norm, ffn_norm, final_norm, rg, na, gdn, ffn)
    return (y_prompt, y_sample)
```

```python
import functools

import numpy as np
import jax
import jax.numpy as jnp
from jax import lax
from jax.experimental import pallas as pl
from jax.experimental.pallas import tpu as pltpu

D_MODEL = 2048
N_META = 16
PAD = 48
HEAD_ROWS = PAD + N_META
CHUNK = 64
GRID_W = 64
NORM_EPS = 1e-6

RG_BLOCK = 256
RG_C = 8.0

NA_HEAD_DIM = 128
NA_HEADS = D_MODEL // NA_HEAD_DIM
NA_WIN_R = 8
NA_WIN_C = 16
NEG_INF = -1e30

GDN_DK = 128
GDN_QK_HEADS = D_MODEL // GDN_DK
GDN_V_HEADS = 2 * GDN_QK_HEADS
GDN_KEY_DIM = GDN_QK_HEADS * GDN_DK
GDN_VAL_DIM = GDN_V_HEADS * GDN_DK
GDN_CONV_DIM = 2 * GDN_KEY_DIM + GDN_VAL_DIM

V7X_VMEM_LIMIT = 56 * 1024 * 1024
HALO = 16

BF16 = jnp.bfloat16
F32 = jnp.float32


def _cparams(*sem):
    return pltpu.CompilerParams(dimension_semantics=sem, vmem_limit_bytes=V7X_VMEM_LIMIT)


def _pick_tile(n, cap, mult):
    best = None
    for t in range(mult, cap + 1, mult):
        if n % t == 0:
            best = t
    assert best is not None, (n, cap, mult)
    return best


def _sigmoid(x):
    return 1.0 / (1.0 + jnp.exp(-x))


def _softplus(x):
    return jnp.maximum(x, 0.0) + jnp.log(1.0 + jnp.exp(-jnp.abs(x)))


def _gelu_tanh(x):
    return 0.5 * x * (1.0 + jnp.tanh(0.7978845608028654 * (x + 0.044715 * x * x * x)))


def _rms_scale(x, w):
    ms = jnp.mean(x * x, axis=-1, keepdims=True)
    return x * lax.rsqrt(ms + NORM_EPS) * w


def _dot(a, b):
    return jnp.dot(a, b, preferred_element_type=F32)


def _dot_nt(a, b):
    return lax.dot_general(a, b, (((1,), (1,)), ((), ())), preferred_element_type=F32)


def _dot_tn(a, b):
    return lax.dot_general(a, b, (((0,), (0,)), ((), ())), preferred_element_type=F32)


def _rmsnorm_kernel(x_ref, w_ref, o_ref):
    o_ref[...] = _rms_scale(x_ref[...], w_ref[...]).astype(o_ref.dtype)


def _rmsnorm(h, w, out_dtype):
    M, D = h.shape
    tm = _pick_tile(M, 1024, 64)
    return pl.pallas_call(
        _rmsnorm_kernel,
        grid=(M // tm,),
        in_specs=[pl.BlockSpec((tm, D), lambda i: (i, 0)), pl.BlockSpec((1, D), lambda i: (0, 0))],
        out_specs=pl.BlockSpec((tm, D), lambda i: (i, 0)),
        out_shape=jax.ShapeDtypeStruct((M, D), out_dtype),
        compiler_params=_cparams("parallel"),
    )(h, w.reshape(1, D))


def _final_norm_kernel(x_ref, w_ref, o_ref):
    o_ref[0] = _rms_scale(x_ref[...], w_ref[...])


def _final_norm(h, w, B, T):
    D = h.shape[1]
    Tp = T + HEAD_ROWS
    tt = 512
    assert T % tt == 0
    return pl.pallas_call(
        _final_norm_kernel,
        grid=(B, T // tt),
        in_specs=[pl.BlockSpec((pl.Element(tt), pl.Element(D)),
                               lambda b, j: (pl.multiple_of(b * Tp + HEAD_ROWS + j * tt, CHUNK), 0)),
                  pl.BlockSpec((1, D), lambda b, j: (0, 0))],
        out_specs=pl.BlockSpec((1, tt, D), lambda b, j: (b, j, 0)),
        out_shape=jax.ShapeDtypeStruct((B, T, D), F32),
        compiler_params=_cparams("parallel", "parallel"),
    )(h, w.reshape(1, D))


def _matmul_kernel(x_ref, w_ref, o_ref, *, act):
    acc = _dot(x_ref[...], w_ref[...])
    if act == "gelu":
        acc = _gelu_tanh(acc)
    o_ref[...] = acc.astype(o_ref.dtype)


def _matmul(x, w, out_dtype, act=None):
    M, K = x.shape
    N = w.shape[1]
    tm = _pick_tile(M, 1024, 128)
    tn = _pick_tile(N, 512, 128)
    return pl.pallas_call(
        functools.partial(_matmul_kernel, act=act),
        grid=(M // tm, N // tn),
        in_specs=[pl.BlockSpec((tm, K), lambda i, j: (i, 0)), pl.BlockSpec((K, tn), lambda i, j: (0, j))],
        out_specs=pl.BlockSpec((tm, tn), lambda i, j: (i, j)),
        out_shape=jax.ShapeDtypeStruct((M, N), out_dtype),
        compiler_params=_cparams("parallel", "parallel"),
    )(x, w)


def _residual_norm_store(h_ref, acc, nw_ref, hout_ref, hn_ref):
    hnew = h_ref[...] + acc
    hout_ref[...] = hnew
    hn_ref[...] = _rms_scale(hnew, nw_ref[...]).astype(hn_ref.dtype)


def _proj_res_kernel(x_ref, w_ref, h_ref, nw_ref, hout_ref, hn_ref, acc_ref, *, nk):
    k = pl.program_id(1)

    @pl.when(k == 0)
    def _():
        acc_ref[...] = jnp.zeros_like(acc_ref)

    acc_ref[...] += _dot(x_ref[...], w_ref[...])

    @pl.when(k == nk - 1)
    def _():
        _residual_norm_store(h_ref, acc_ref[...], nw_ref, hout_ref, hn_ref)


def _proj_res(x, w, h, nw):
    M, K = x.shape
    D = w.shape[1]
    tm = _pick_tile(M, 512, 128)
    tk = 1024
    nk = K // tk
    return pl.pallas_call(
        functools.partial(_proj_res_kernel, nk=nk),
        grid=(M // tm, nk),
        in_specs=[pl.BlockSpec((tm, tk), lambda i, k: (i, k)),
                  pl.BlockSpec((tk, D), lambda i, k: (k, 0)),
                  pl.BlockSpec((tm, D), lambda i, k: (i, 0)),
                  pl.BlockSpec((1, D), lambda i, k: (0, 0))],
        out_specs=[pl.BlockSpec((tm, D), lambda i, k: (i, 0)), pl.BlockSpec((tm, D), lambda i, k: (i, 0))],
        out_shape=[jax.ShapeDtypeStruct((M, D), F32), jax.ShapeDtypeStruct((M, D), BF16)],
        scratch_shapes=[pltpu.VMEM((tm, D), F32)],
        compiler_params=_cparams("parallel", "arbitrary"),
    )(x, w, h, nw.reshape(1, D))


def _ffn_kernel(xp_ref, xm_ref, xn_ref, wg_ref, wu_ref, cw_ref, cb_ref, wd_ref, h_ref, nw_ref,
                hout_ref, hn_ref, xext_ref, acc_ref, *, tm, nf, ni):
    i = pl.program_id(0)
    f = pl.program_id(1)

    @pl.when(f == 0)
    def _():
        xext_ref[0:HALO, :] = xp_ref[...]
        xext_ref[HALO:HALO + tm, :] = xm_ref[...]
        xext_ref[HALO + tm:2 * HALO + tm, :] = xn_ref[...]
        acc_ref[...] = jnp.zeros_like(acc_ref)

    g = _dot(xext_ref[...], wg_ref[...])
    g_prev = g[HALO - 1:HALO - 1 + tm]
    g_here = g[HALO:HALO + tm]
    g_next = g[HALO + 1:HALO + 1 + tm]
    row = lax.broadcasted_iota(jnp.int32, (tm, 1), 0)
    g_next = jnp.where(jnp.logical_and(i == ni - 1, row == tm - 1), 0.0, g_next)
    cw = cw_ref[...]
    a = cw[0:1] * g_prev + cw[1:2] * g_here + cw[2:3] * g_next + cb_ref[...]
    up = _dot(xm_ref[...], wu_ref[...])
    y = (a * _sigmoid(a) * up).astype(BF16)
    acc_ref[...] += _dot(y, wd_ref[...])

    @pl.when(f == nf - 1)
    def _():
        _residual_norm_store(h_ref, acc_ref[...], nw_ref, hout_ref, hn_ref)


def _conv_ffn(hn, h, w_gate, w_up, conv_w, conv_b, w_down, nw):
    M, D = hn.shape
    F = w_gate.shape[1]
    tm = _pick_tile(M, 512, 128)
    tf = _pick_tile(F, 512, 128)
    ni, nf = M // tm, F // tf
    hb = tm // HALO
    nhb = M // HALO
    return pl.pallas_call(
        functools.partial(_ffn_kernel, tm=tm, nf=nf, ni=ni),
        grid=(ni, nf),
        in_specs=[pl.BlockSpec((HALO, D), lambda i, f: (jnp.maximum(i * hb - 1, 0), 0)),
                  pl.BlockSpec((tm, D), lambda i, f: (i, 0)),
                  pl.BlockSpec((HALO, D), lambda i, f: (jnp.minimum((i + 1) * hb, nhb - 1), 0)),
                  pl.BlockSpec((D, tf), lambda i, f: (0, f)),
                  pl.BlockSpec((D, tf), lambda i, f: (0, f)),
                  pl.BlockSpec((3, tf), lambda i, f: (0, f)),
                  pl.BlockSpec((1, tf), lambda i, f: (0, f)),
                  pl.BlockSpec((tf, D), lambda i, f: (f, 0)),
                  pl.BlockSpec((tm, D), lambda i, f: (i, 0)),
                  pl.BlockSpec((1, D), lambda i, f: (0, 0))],
        out_specs=[pl.BlockSpec((tm, D), lambda i, f: (i, 0)), pl.BlockSpec((tm, D), lambda i, f: (i, 0))],
        out_shape=[jax.ShapeDtypeStruct((M, D), F32), jax.ShapeDtypeStruct((M, D), BF16)],
        scratch_shapes=[pltpu.VMEM((tm + 2 * HALO, D), BF16), pltpu.VMEM((tm, D), F32)],
        compiler_params=_cparams("parallel", "arbitrary"),
    )(hn, hn, hn, w_gate, w_up, conv_w, conv_b.reshape(1, F), w_down, h, nw.reshape(1, D))


def _scan8(a, u, row, reverse):
    for s in (1, 2, 4):
        if reverse:
            a_sh = pltpu.roll(a, 8 - s, 0)
            u_sh = pltpu.roll(u, 8 - s, 0)
            m = row < 8 - s
        else:
            a_sh = pltpu.roll(a, s, 0)
            u_sh = pltpu.roll(u, s, 0)
            m = row >= s
        u = jnp.where(m, a * u_sh + u, u)
        a = jnp.where(m, a * a_sh, a)
    return a, u


def _rg_kernel(rec_ref, gate_ref, cw_ref, cb_ref, wa_ref, ba_ref, wi_ref, bi_ref, lam_ref, o_ref,
               xc_s, hs_s, *, Tp, rc):
    nch = Tp // rc
    W = RG_BLOCK
    cw = cw_ref[...]
    cb = cb_ref[...]
    row8 = lax.broadcasted_iota(jnp.int32, (8, W), 0)

    def gates(xc, d):
        xb = xc.astype(BF16)
        r = _sigmoid(_dot(xb, wa_ref[d, 0]) + ba_ref[d])
        ig = _sigmoid(_dot(xb, wi_ref[d, 0]) + bi_ref[d])
        log_a = (-RG_C) * r * _softplus(-lam_ref[d])
        a = jnp.exp(log_a)
        u = jnp.sqrt(1.0 - jnp.exp(2.0 * log_a)) * (ig * xc)
        return a, u

    def fwd_chunk(c, carry):
        r0 = pl.multiple_of(c * rc, CHUNK)
        top = rec_ref[0, pl.ds(pl.multiple_of(jnp.maximum(r0 - 8, 0), 8), 8), :]
        main = rec_ref[0, pl.ds(r0, rc), :]
        bot = rec_ref[0, pl.ds(pl.multiple_of(jnp.minimum(r0 + rc, Tp - 8), 8), 8), :]
        bot = jnp.where(r0 + rc < Tp, bot, 0.0)
        xw = jnp.concatenate([top, main, bot], axis=0)
        xc = cw[0:1] * xw[7:7 + rc] + cw[1:2] * xw[8:8 + rc] + cw[2:3] * xw[9:9 + rc] + cw[3:4] * xw[10:10 + rc] + cb
        t = r0 + lax.broadcasted_iota(jnp.int32, (rc, 1), 0)
        xc = jnp.where(t >= PAD, xc, 0.0)
        xc_s[pl.ds(r0, rc), :] = xc
        a, u = gates(xc, 0)
        hs = []
        for k in range(rc // 8):
            A, U = _scan8(a[8 * k:8 * k + 8], u[8 * k:8 * k + 8], row8, False)
            h = A * carry + U
            carry = h[7:8]
            hs.append(h)
        hs_s[pl.ds(r0, rc), :] = jnp.concatenate(hs, axis=0)
        return carry

    lax.fori_loop(0, nch, fwd_chunk, jnp.zeros((1, W), F32))

    def bwd_chunk(cc, carry):
        r0 = pl.multiple_of((nch - 1 - cc) * rc, CHUNK)
        xc = xc_s[pl.ds(r0, rc), :]
        a, u = gates(xc, 1)
        hs = [None] * (rc // 8)
        for k in reversed(range(rc // 8)):
            A, U = _scan8(a[8 * k:8 * k + 8], u[8 * k:8 * k + 8], row8, True)
            h = A * carry + U
            carry = h[0:1]
            hs[k] = h
        hsum = hs_s[pl.ds(r0, rc), :] + jnp.concatenate(hs, axis=0)
        o_ref[0, pl.ds(r0, rc), :] = (gate_ref[0, pl.ds(r0, rc), :].astype(F32) * hsum).astype(o_ref.dtype)
        return carry

    lax.fori_loop(0, nch, bwd_chunk, jnp.zeros((1, W), F32))


def _rg_scan(rec, gate, conv_w, conv_b, w_a, b_a, w_i, b_i, lam):
    B, Tp, W = rec.shape
    nb = W // RG_BLOCK
    rc = _pick_tile(Tp, 384, 64)
    blk = lambda b, n: (b, 0, n)
    vec = lambda b, n: (0, n)
    vec3 = lambda b, n: (0, 0, n)
    return pl.pallas_call(
        functools.partial(_rg_kernel, Tp=Tp, rc=rc),
        grid=(B, nb),
        in_specs=[pl.BlockSpec((1, Tp, RG_BLOCK), blk),
                  pl.BlockSpec((1, Tp, RG_BLOCK), blk),
                  pl.BlockSpec((4, RG_BLOCK), vec),
                  pl.BlockSpec((1, RG_BLOCK), vec),
                  pl.BlockSpec((2, 1, RG_BLOCK, RG_BLOCK), lambda b, n: (0, n, 0, 0)),
                  pl.BlockSpec((2, 1, RG_BLOCK), vec3),
                  pl.BlockSpec((2, 1, RG_BLOCK, RG_BLOCK), lambda b, n: (0, n, 0, 0)),
                  pl.BlockSpec((2, 1, RG_BLOCK), vec3),
                  pl.BlockSpec((2, 1, RG_BLOCK), vec3)],
        out_specs=pl.BlockSpec((1, Tp, RG_BLOCK), blk),
        out_shape=jax.ShapeDtypeStruct((B, Tp, W), BF16),
        scratch_shapes=[pltpu.VMEM((Tp, RG_BLOCK), F32), pltpu.VMEM((Tp, RG_BLOCK), F32)],
        compiler_params=_cparams("parallel", "parallel"),
    )(rec, gate, conv_w, conv_b.reshape(1, W), w_a.astype(BF16), b_a.reshape(2, 1, W),
      w_i.astype(BF16), b_i.reshape(2, 1, W), lam.reshape(2, 1, W))


def _rglru_mixer(hn, h, B, Tp, w_in, conv_w, conv_b, w_a, b_a, w_i, b_i, lam, w_out, nw):
    W = D_MODEL
    w_in = w_in.astype(BF16)
    gate = _matmul(hn, w_in[:, :W], BF16, act="gelu")
    rec = _matmul(hn, w_in[:, W:], F32)
    y = _rg_scan(rec.reshape(B, Tp, W), gate.reshape(B, Tp, W), conv_w, conv_b, w_a, b_a, w_i, b_i, lam)
    return _proj_res(y.reshape(B * Tp, W), w_out.astype(BF16), h, nw)


def _na_kernel(q_ref, k_ref, v_ref, bias_ref, mb_ref, o_ref, *, rows):
    scale = NA_HEAD_DIM ** -0.5
    kh = NA_WIN_R
    mb = mb_ref[0]
    km = k_ref[0, PAD:HEAD_ROWS, :]
    vm = v_ref[0, PAD:HEAD_ROWS, :]

    o_ref[0, 0:PAD, :] = jnp.zeros((PAD, NA_HEAD_DIM), o_ref.dtype)
    qm = q_ref[0, PAD:HEAD_ROWS, :]
    s_m = _dot_nt(qm, km) * scale + mb
    p_m = jnp.exp(s_m - jnp.max(s_m, axis=-1, keepdims=True))
    o_m = _dot(p_m.astype(BF16), vm) / jnp.sum(p_m, axis=-1, keepdims=True)
    o_ref[0, PAD:HEAD_ROWS, :] = o_m.astype(o_ref.dtype)

    def row_body(r, _):
        rs = jnp.clip(r - NA_WIN_R // 2, 0, rows - kh)
        q0 = pl.multiple_of(HEAD_ROWS + r * GRID_W, GRID_W)
        k0 = pl.multiple_of(HEAD_ROWS + rs * GRID_W, GRID_W)
        q_r = q_ref[0, pl.ds(q0, GRID_W), :]
        k_w = k_ref[0, pl.ds(k0, kh * GRID_W), :]
        v_w = v_ref[0, pl.ds(k0, kh * GRID_W), :]
        s = _dot_nt(q_r, k_w) * scale + bias_ref[0, r - rs]
        s_met = _dot_nt(q_r, km) * scale + mb
        m = jnp.maximum(jnp.max(s, axis=-1, keepdims=True), jnp.max(s_met, axis=-1, keepdims=True))
        p = jnp.exp(s - m)
        p_met = jnp.exp(s_met - m)
        l = jnp.sum(p, axis=-1, keepdims=True) + jnp.sum(p_met, axis=-1, keepdims=True)
        o = (_dot(p.astype(BF16), v_w) + _dot(p_met.astype(BF16), vm)) / l
        o_ref[0, pl.ds(q0, GRID_W), :] = o.astype(o_ref.dtype)
        return 0

    lax.fori_loop(0, rows, row_body, 0)


def _na_bias_table(rpb):
    kh = NA_WIN_R
    cols = np.arange(GRID_W)
    col_start = np.clip(cols - NA_WIN_C // 2, 0, GRID_W - NA_WIN_C)
    valid = (cols[None, :] >= col_start[:, None]) & (cols[None, :] < col_start[:, None] + NA_WIN_C)
    col_off = np.clip(cols[None, :] - cols[:, None], -(NA_WIN_C - 1), NA_WIN_C - 1) + NA_WIN_C - 1
    d = np.arange(NA_WIN_R)
    row_off = np.arange(kh)[None, :] - d[:, None] + NA_WIN_R - 1
    tab = rpb.astype(F32)[:, row_off[:, None, :, None], col_off[None, :, None, :]]
    tab = jnp.where(jnp.asarray(valid)[None, None, :, None, :], tab, NEG_INF)
    return tab.reshape(rpb.shape[0], NA_WIN_R, GRID_W, kh * GRID_W)


def _na_attention(qkv, bias, meta_bias):
    B, Tp, _ = qkv.shape
    rows = (Tp - HEAD_ROWS) // GRID_W
    assert rows >= NA_WIN_R
    H = NA_HEADS
    return pl.pallas_call(
        functools.partial(_na_kernel, rows=rows),
        grid=(B, H),
        in_specs=[pl.BlockSpec((1, Tp, NA_HEAD_DIM), lambda b, h: (b, 0, h)),
                  pl.BlockSpec((1, Tp, NA_HEAD_DIM), lambda b, h: (b, 0, H + h)),
                  pl.BlockSpec((1, Tp, NA_HEAD_DIM), lambda b, h: (b, 0, 2 * H + h)),
                  pl.BlockSpec((1, NA_WIN_R, GRID_W, NA_WIN_R * GRID_W), lambda b, h: (h, 0, 0, 0)),
                  pl.BlockSpec((1, 1, N_META), lambda b, h: (h, 0, 0))],
        out_specs=pl.BlockSpec((1, Tp, NA_HEAD_DIM), lambda b, h: (b, 0, h)),
        out_shape=jax.ShapeDtypeStruct((B, Tp, H * NA_HEAD_DIM), BF16),
        compiler_params=_cparams("parallel", "parallel"),
    )(qkv, qkv, qkv, bias, meta_bias.astype(F32).reshape(H, 1, N_META))


def _na_mixer(hn, h, B, Tp, w_qkv, bias, meta_bias, w_o, nw):
    qkv = _matmul(hn, w_qkv.astype(BF16), BF16)
    o = _na_attention(qkv.reshape(B, Tp, -1), bias, meta_bias)
    return _proj_res(o.reshape(B * Tp, -1), w_o.astype(BF16), h, nw)


def _gdn_gate_kernel(x_ref, w_ref, al_ref, dt_ref, o_ref):
    y = _dot(x_ref[...], w_ref[...])
    lane = lax.broadcasted_iota(jnp.int32, y.shape, 1)
    g = -jnp.exp(al_ref[...]) * _softplus(y + dt_ref[...])
    o_ref[...] = jnp.where(lane < 2 * GDN_V_HEADS, _sigmoid(y), g)


def _gdn_gates(hn, w_ba, a_log, dt_bias):
    M, K = hn.shape
    N = 4 * GDN_V_HEADS
    tm = _pick_tile(M, 1024, 128)
    zeros = jnp.zeros((2 * GDN_V_HEADS,), F32)
    al = jnp.concatenate([zeros, a_log.astype(F32).reshape(-1)]).reshape(1, N)
    dt = jnp.concatenate([zeros, dt_bias.astype(F32).reshape(-1)]).reshape(1, N)
    return pl.pallas_call(
        _gdn_gate_kernel,
        grid=(M // tm,),
        in_specs=[pl.BlockSpec((tm, K), lambda i: (i, 0)), pl.BlockSpec((K, N), lambda i: (0, 0)),
                  pl.BlockSpec((1, N), lambda i: (0, 0)), pl.BlockSpec((1, N), lambda i: (0, 0))],
        out_specs=pl.BlockSpec((tm, N), lambda i: (i, 0)),
        out_shape=jax.ShapeDtypeStruct((M, N), F32),
        compiler_params=_cparams("parallel"),
    )(hn, w_ba, al, dt)


def _gdn_conv_kernel(x_ref, cw_ref, o_ref, *, Tp, rc, n_qk_blocks):
    j = pl.program_id(1)
    cw = cw_ref[...]
    nch = Tp // rc

    def chunk(c, _):
        r0 = pl.multiple_of(c * rc, CHUNK)
        top = x_ref[0, pl.ds(pl.multiple_of(jnp.maximum(r0 - 8, 0), 8), 8), :]
        main = x_ref[0, pl.ds(r0, rc), :]
        bot = x_ref[0, pl.ds(pl.multiple_of(jnp.minimum(r0 + rc, Tp - 8), 8), 8), :]
        bot = jnp.where(r0 + rc < Tp, bot, 0.0)
        xw = jnp.concatenate([top, main, bot], axis=0)
        a = cw[0:1] * xw[7:7 + rc] + cw[1:2] * xw[8:8 + rc] + cw[2:3] * xw[9:9 + rc] + cw[3:4] * xw[10:10 + rc]
        y = a * _sigmoid(a)
        t = r0 + lax.broadcasted_iota(jnp.int32, (rc, 1), 0)
        y = jnp.where(t >= PAD, y, 0.0)
        nrm = lax.rsqrt(jnp.sum(y * y, axis=-1, keepdims=True) + 1e-6)
        y = y * jnp.where(j < n_qk_blocks, nrm, 1.0)
        o_ref[0, pl.ds(r0, rc), :] = y.astype(o_ref.dtype)
        return 0

    lax.fori_loop(0, nch, chunk, 0)


def _gdn_conv(proj3, conv_w):
    B, Tp, _ = proj3.shape
    nblk = GDN_CONV_DIM // GDN_DK
    rc = _pick_tile(Tp, 384, 64)
    return pl.pallas_call(
        functools.partial(_gdn_conv_kernel, Tp=Tp, rc=rc, n_qk_blocks=2 * GDN_QK_HEADS),
        grid=(B, nblk),
        in_specs=[pl.BlockSpec((1, Tp, GDN_DK), lambda b, j: (b, 0, j)),
                  pl.BlockSpec((4, GDN_DK), lambda b, j: (0, j))],
        out_specs=pl.BlockSpec((1, Tp, GDN_DK), lambda b, j: (b, 0, j)),
        out_shape=jax.ShapeDtypeStruct((B, Tp, GDN_CONV_DIM), BF16),
        compiler_params=_cparams("parallel", "parallel"),
    )(proj3, conv_w)


def _split3_dot(x, m):
    hi = x.astype(BF16)
    r1 = x - hi.astype(F32)
    mid = r1.astype(BF16)
    lo = (r1 - mid.astype(F32)).astype(BF16)
    return _dot(hi, m) + _dot(mid, m) + _dot(lo, m)


def _gdn_kernel_unused(q_ref, k_ref, v_ref, z_ref, bgc_ref, bgr_ref, nw_ref, o_ref, S_s, oacc_s, *, Tp):
    kh = pl.program_id(1)
    C = CHUNK
    NC = Tp // C
    HV = GDN_V_HEADS
    DK = GDN_DK
    scale = DK ** -0.5

    oacc_s[...] = jnp.zeros_like(oacc_s)
    S_s[...] = jnp.zeros_like(S_s)

    ri = lax.broadcasted_iota(jnp.int32, (C, C), 0)
    ci = lax.broadcasted_iota(jnp.int32, (C, C), 1)
    lower = ri >= ci
    upper = ri <= ci
    eye = (ri == ci).astype(F32)
    ones_lower = lower.astype(BF16)
    ones_upper = upper.astype(BF16)
    lane = lax.broadcasted_iota(jnp.int32, (C, 4 * HV), 1)
    rowi = lax.broadcasted_iota(jnp.int32, (C, 4 * HV), 0)

    def col(x, c):
        return jnp.sum(jnp.where(lane == c, x, 0.0), axis=1, keepdims=True)

    def one_chunk(n, reverse):
        r0 = pl.multiple_of(n * C, C)
        q = q_ref[0, pl.ds(r0, C), :]
        k = k_ref[0, pl.ds(r0, C), :]
        kf = k.astype(F32)
        bg = bgc_ref[0, pl.ds(r0, C), :]
        pre = bg
        for s in (1, 2, 4, 8, 16, 32):
            pre = pre + jnp.where(rowi >= s, pltpu.roll(pre, s, 0), 0.0)
        if reverse:
            cum = pre[C - 1:C] - pre + bg
        else:
            cum = pre
        rows = bgr_ref[0, n, 0]
        cum_rows = _split3_dot(rows, ones_lower if reverse else ones_upper)
        kk = _dot_nt(k, k)
        qk = _dot_nt(q, k) * scale
        incl = upper if reverse else lower
        strict = (ri < ci) if reverse else (ri > ci)
        d = 1 if reverse else 0
        for j in range(2):
            inst = 2 * d + j
            vh = 2 * kh + j
            beta_c = col(bg, d * HV + vh)
            gc_c = col(cum, 2 * HV + d * HV + vh)
            gc_r = cum_rows[4 + 2 * d + j:5 + 2 * d + j]
            decay = jnp.where(incl, jnp.exp(jnp.where(incl, gc_c - gc_r, 0.0)), 0.0)
            L = jnp.where(strict, kk * decay, 0.0) * beta_c
            Tm = eye - L
            P = L
            for _ in range(5):
                Pb = P.astype(BF16)
                P = _dot(Pb, Pb)
                Tm = Tm + _dot(Tm.astype(BF16), P.astype(BF16))
            Tb = Tm.astype(BF16)
            v = v_ref[0, pl.ds(r0, C), j * DK:(j + 1) * DK].astype(F32)
            e_c = jnp.exp(gc_c)
            u = _dot(Tb, (v * beta_c).astype(BF16))
            w = _dot(Tb, (kf * (beta_c * e_c)).astype(BF16))
            qkd = jnp.where(incl, qk * decay, 0.0)
            g_last = gc_c[0:1] if reverse else gc_c[C - 1:C]
            k_st = kf * jnp.exp(g_last - gc_c)
            q_st = q.astype(F32) * (scale * e_c)
            S = S_s[inst]
            Sb = S.astype(BF16)
            v_new = u - _dot(w.astype(BF16), Sb)
            vb = v_new.astype(BF16)
            o = _dot(q_st.astype(BF16), Sb) + _dot(qkd.astype(BF16), vb)
            S_s[inst] = S * jnp.exp(g_last) + _dot_tn(k_st.astype(BF16), vb)
            oacc_s[pl.ds(r0, C), j * DK:(j + 1) * DK] += o

    def body(it, _):
        one_chunk(it, False)
        one_chunk(NC - 1 - it, True)
        return 0

    lax.fori_loop(0, NC, body, 0)

    nw = nw_ref[...]

    def fin(n, _):
        r0 = pl.multiple_of(n * C, C)
        for j in range(2):
            o = oacc_s[pl.ds(r0, C), j * DK:(j + 1) * DK]
            z = z_ref[0, pl.ds(r0, C), j * DK:(j + 1) * DK]
            y = o * lax.rsqrt(jnp.mean(o * o, axis=-1, keepdims=True) + NORM_EPS) * nw * (z * _sigmoid(z))
            o_ref[0, pl.ds(r0, C), j * DK:(j + 1) * DK] = y.astype(o_ref.dtype)
        return 0

    lax.fori_loop(0, NC, fin, 0)


def _gdn_delta_unused(qkv, proj3, bg, norm_w):
    B, Tp, _ = qkv.shape
    NC = Tp // CHUNK
    KH = GDN_QK_HEADS
    HV = GDN_V_HEADS
    bgr = bg.reshape(B, NC, CHUNK, 4, KH, 2).transpose(0, 1, 4, 3, 5, 2).reshape(B, NC, KH, 8, CHUNK)
    zoff = GDN_CONV_DIM // (2 * GDN_DK)
    return pl.pallas_call(
        functools.partial(_gdn_kernel, Tp=Tp),
        grid=(B, KH),
        in_specs=[pl.BlockSpec((1, Tp, GDN_DK), lambda b, h: (b, 0, h)),
                  pl.BlockSpec((1, Tp, GDN_DK), lambda b, h: (b, 0, KH + h)),
                  pl.BlockSpec((1, Tp, 2 * GDN_DK), lambda b, h: (b, 0, KH + h)),
                  pl.BlockSpec((1, Tp, 2 * GDN_DK), lambda b, h: (b, 0, zoff + h)),
                  pl.BlockSpec((1, Tp, 4 * HV), lambda b, h: (b, 0, 0)),
                  pl.BlockSpec((1, NC, 1, 8, CHUNK), lambda b, h: (b, 0, h, 0, 0)),
                  pl.BlockSpec((1, GDN_DK), lambda b, h: (0, 0))],
        out_specs=pl.BlockSpec((1, Tp, 2 * GDN_DK), lambda b, h: (b, 0, h)),
        out_shape=jax.ShapeDtypeStruct((B, Tp, GDN_VAL_DIM), BF16),
        scratch_shapes=[pltpu.VMEM((4, GDN_DK, GDN_DK), F32), pltpu.VMEM((Tp, 2 * GDN_DK), F32)],
        compiler_params=_cparams("parallel", "parallel"),
    )(qkv, qkv, qkv, proj3, bg, bgr, norm_w.astype(F32).reshape(1, GDN_DK))


def _gdn_kernel(q_ref, k_ref, v_ref, z_ref, bgc_ref, bgr_ref, nw_ref, o_ref,
                S_s, oacc_s, u_s, wq_s, kst_s, qkd_s, gam_s, *, Tp):
    kh = pl.program_id(1)
    C = CHUNK
    NC = Tp // C
    HV = GDN_V_HEADS
    DK = GDN_DK
    G = 4
    W = G * C
    P1_GROUP = next((g for g in (5, 4, 3) if NC % g == 0), 4)
    scale = DK ** -0.5

    oacc_s[...] = jnp.zeros_like(oacc_s)
    S_s[...] = jnp.zeros_like(S_s)

    ri = lax.broadcasted_iota(jnp.int32, (C, W), 0)
    cl = lax.broadcasted_iota(jnp.int32, (C, W), 1)
    cj = jnp.bitwise_and(cl, C - 1)
    blk = jnp.right_shift(cl, 6)
    dd = jnp.where(blk >= 2, cj - ri, ri - cj)
    incl4 = dd >= 0
    strict4 = dd > 0
    eye4 = (ri == cj).astype(F32)
    cum4 = jnp.where(dd <= 0, 1.0, 0.0).astype(BF16)
    bdm = (jnp.right_shift(lax.broadcasted_iota(jnp.int32, (W, W), 0), 6)
           == jnp.right_shift(lax.broadcasted_iota(jnp.int32, (W, W), 1), 6))
    blk_row = jnp.right_shift(lax.broadcasted_iota(jnp.int32, (1, W), 1), 6)
    lane = lax.broadcasted_iota(jnp.int32, (C, 4 * HV), 1)
    rowi = lax.broadcasted_iota(jnp.int32, (C, 4 * HV), 0)

    def col(x, c):
        return jnp.sum(jnp.where(lane == c, x, 0.0), axis=1, keepdims=True)

    def by_block(parts, b):
        return jnp.where(b == 0, parts[0], jnp.where(b == 1, parts[1], jnp.where(b == 2, parts[2], parts[3])))

    def blockdiag(x):
        return jnp.where(bdm, jnp.concatenate([x] * G, axis=0), 0.0).astype(BF16)

    def phase1(ns):
        K = range(len(ns))
        r0 = [pl.multiple_of(n * C, C) for n in ns]
        q = [q_ref[0, pl.ds(r, C), :] for r in r0]
        k = [k_ref[0, pl.ds(r, C), :] for r in r0]
        bg = [bgc_ref[0, pl.ds(r, C), :] for r in r0]
        v = [v_ref[0, pl.ds(r, C), :].astype(F32) for r in r0]
        rows = [bgr_ref[0, n, 0] for n in ns]
        cum_rows = [_split3_dot(rows[i], cum4) for i in K]
        k4 = [jnp.concatenate([k[i]] * G, axis=0) for i in K]
        kk4 = [_dot_nt(k[i], k4[i]) for i in K]
        qk4 = [_dot_nt(q[i], k4[i]) * scale for i in K]
        beta_c, gc_c, decay4, L = [], [], [], []
        for i in K:
            pre = bg[i]
            for s in (1, 2, 4, 8, 16, 32):
                pre = pre + jnp.where(rowi >= s, pltpu.roll(pre, s, 0), 0.0)
            suf = pre[C - 1:C] - pre + bg[i]
            gc_r = by_block([cum_rows[i][4 + g:5 + g] for g in range(G)], blk_row)
            bc, gc = [], []
            for g in range(G):
                d, j = g // 2, g % 2
                vh = 2 * kh + j
                bc.append(col(bg[i], d * HV + vh))
                gc.append(col(suf if d else pre, 2 * HV + d * HV + vh))
            beta_c.append(bc)
            gc_c.append(gc)
            dec = jnp.where(incl4, jnp.exp(jnp.where(incl4, by_block(gc, blk) - gc_r, 0.0)), 0.0)
            decay4.append(dec)
            L.append(jnp.where(strict4, kk4[i] * dec, 0.0) * by_block(bc, blk))
        Tm = [eye4 - L[i] for i in K]
        P = [_dot(L[i].astype(BF16), blockdiag(L[i])) for i in K]
        for lvl in range(5):
            bd = [blockdiag(P[i]) for i in K]
            if lvl < 4:
                tp = [_dot(jnp.concatenate([Tm[i], P[i]], axis=0).astype(BF16), bd[i]) for i in K]
                Tm = [Tm[i] + tp[i][:C] for i in K]
                P = [tp[i][C:] for i in K]
            else:
                tp = [_dot(Tm[i].astype(BF16), bd[i]) for i in K]
                Tm = [Tm[i] + tp[i] for i in K]
        rhs, kst, qst, gam = [], [], [], []
        for i in K:
            qf = q[i].astype(F32)
            kf = k[i].astype(F32)
            rhs_i, kst_i, qst_i, gam_i = [], [], [], []
            for g in range(G):
                j = g % 2
                bc, gc = beta_c[i][g], gc_c[i][g]
                e_c = jnp.exp(gc)
                rhs_i.append(jnp.concatenate([v[i][:, j * DK:(j + 1) * DK] * bc, kf * (bc * e_c)], axis=1))
                g_last = gc[0:1] if g >= 2 else gc[C - 1:C]
                kst_i.append((kf * jnp.exp(g_last - gc)).astype(BF16))
                qst_i.append((qf * (scale * e_c)).astype(BF16))
                gam_i.append(jnp.broadcast_to(jnp.exp(g_last), (1, DK)))
            rhs.append(jnp.concatenate(rhs_i, axis=0).astype(BF16))
            kst.append(kst_i)
            qst.append(qst_i)
            gam.append(gam_i)
        sol = [_dot(blockdiag(Tm[i]), rhs[i]) for i in K]
        for i in K:
            n = ns[i]
            u_s[n] = sol[i][:, :DK]
            qkd_s[n] = jnp.where(incl4, qk4[i] * decay4[i], 0.0).astype(BF16)
            for g in range(G):
                wq_s[n, g, 0:C, :] = sol[i][g * C:(g + 1) * C, DK:].astype(BF16)
                wq_s[n, g, C:2 * C, :] = qst[i][g]
                kst_s[n, g] = kst[i][g]
                gam_s[n, g:g + 1, :] = gam[i][g]

    def phase1_group(i, _):
        phase1([P1_GROUP * i + c for c in range(P1_GROUP)])
        return 0

    lax.fori_loop(0, NC // P1_GROUP, phase1_group, 0)
    if NC % P1_GROUP:
        phase1(list(range(NC - NC % P1_GROUP, NC)))

    def phase2(it, _):
        ops = []
        for g in range(G):
            n = it if g < 2 else NC - 1 - it
            ops.append((S_s[g], wq_s[n, g], u_s[n, g * C:(g + 1) * C, :], qkd_s[n, :, g * C:(g + 1) * C],
                        gam_s[n, g:g + 1, :], kst_s[n, g]))
        r_f = pl.multiple_of(it * C, C)
        r_b = pl.multiple_of((NC - 1 - it) * C, C)
        o_f = oacc_s[pl.ds(r_f, C), :]
        ws = [_dot(wq, S.astype(BF16)) for (S, wq, u, qkd, gam, kst) in ops]
        vb = [(ops[g][2] - ws[g][:C]).astype(BF16) for g in range(G)]
        outs = [ws[g][C:] + _dot(ops[g][3], vb[g]) for g in range(G)]
        states = [ops[g][0] * ops[g][4] + _dot_tn(ops[g][5], vb[g]) for g in range(G)]
        for g in range(G):
            S_s[g] = states[g]
        oacc_s[pl.ds(r_f, C), :] = o_f + jnp.concatenate(outs[0:2], axis=1)
        oacc_s[pl.ds(r_b, C), :] = oacc_s[pl.ds(r_b, C), :] + jnp.concatenate(outs[2:4], axis=1)
        return 0

    lax.fori_loop(0, NC, phase2, 0)

    nw = nw_ref[...]

    def fin(n, _):
        r0 = pl.multiple_of(n * C, C)
        for j in range(2):
            o = oacc_s[pl.ds(r0, C), j * DK:(j + 1) * DK]
            z = z_ref[0, pl.ds(r0, C), j * DK:(j + 1) * DK].astype(F32)
            y = o * lax.rsqrt(jnp.mean(o * o, axis=-1, keepdims=True) + NORM_EPS) * nw * (z * _sigmoid(z))
            o_ref[0, pl.ds(r0, C), j * DK:(j + 1) * DK] = y.astype(o_ref.dtype)
        return 0

    lax.fori_loop(0, NC, fin, 0)


def _gdn_delta(qkv, z, bg, norm_w):
    B, Tp, _ = qkv.shape
    NC = Tp // CHUNK
    KH = GDN_QK_HEADS
    HV = GDN_V_HEADS
    C = CHUNK
    bgr = bg.reshape(B, NC, CHUNK, 4, KH, 2).transpose(0, 1, 4, 3, 5, 2).reshape(B, NC, KH, 8, CHUNK)
    return pl.pallas_call(
        functools.partial(_gdn_kernel, Tp=Tp),
        grid=(B, KH),
        in_specs=[pl.BlockSpec((1, Tp, GDN_DK), lambda b, h: (b, 0, h)),
                  pl.BlockSpec((1, Tp, GDN_DK), lambda b, h: (b, 0, KH + h)),
                  pl.BlockSpec((1, Tp, 2 * GDN_DK), lambda b, h: (b, 0, KH + h)),
                  pl.BlockSpec((1, Tp, 2 * GDN_DK), lambda b, h: (b, 0, h)),
                  pl.BlockSpec((1, Tp, 4 * HV), lambda b, h: (b, 0, 0)),
                  pl.BlockSpec((1, NC, 1, 8, CHUNK), lambda b, h: (b, 0, h, 0, 0)),
                  pl.BlockSpec((1, GDN_DK), lambda b, h: (0, 0))],
        out_specs=pl.BlockSpec((1, Tp, 2 * GDN_DK), lambda b, h: (b, 0, h)),
        out_shape=jax.ShapeDtypeStruct((B, Tp, GDN_VAL_DIM), BF16),
        scratch_shapes=[pltpu.VMEM((4, GDN_DK, GDN_DK), F32),
                        pltpu.VMEM((Tp, 2 * GDN_DK), F32),
                        pltpu.VMEM((NC, 4 * C, GDN_DK), F32),
                        pltpu.VMEM((NC, 4, 2 * C, GDN_DK), BF16),
                        pltpu.VMEM((NC, 4, C, GDN_DK), BF16),
                        pltpu.VMEM((NC, C, 4 * C), BF16),
                        pltpu.VMEM((NC, 8, GDN_DK), F32)],
        compiler_params=_cparams("parallel", "parallel"),
    )(qkv, qkv, qkv, z, bg, bgr, norm_w.astype(F32).reshape(1, GDN_DK))


def _gdn_mixer(hn, h, B, Tp, w_in, conv_w, a_log, dt_bias, norm_w, w_out, nw):
    nz = GDN_CONV_DIM + GDN_VAL_DIM
    w_in = w_in.astype(BF16)
    proj = _matmul(hn, w_in[:, :GDN_CONV_DIM], F32)
    z = _matmul(hn, w_in[:, GDN_CONV_DIM:nz], BF16)
    bg = _gdn_gates(hn, w_in[:, nz:], a_log, dt_bias)
    qkv = _gdn_conv(proj.reshape(B, Tp, GDN_CONV_DIM), conv_w)
    o = _gdn_delta(qkv, z.reshape(B, Tp, -1), bg.reshape(B, Tp, -1), norm_w)
    return _proj_res(o.reshape(B * Tp, -1), w_out.astype(BF16), h, nw)


def _trunk(x, meta_tokens, mix_norm, ffn_norm, final_norm, rg, na, gdn, ffn):
    B, T, D = x.shape
    Tp = T + HEAD_ROWS
    depth = mix_norm.shape[0]
    h3 = jnp.concatenate([jnp.zeros((B, PAD, D), F32),
                          jnp.broadcast_to(meta_tokens.astype(F32)[None], (B, N_META, D)),
                          x.astype(F32)], axis=1)
    h = h3.reshape(B * Tp, D)
    hn = _rmsnorm(h, mix_norm[0], BF16)
    for i in range(depth):
        kind, j = i % 3, i // 3
        if kind == 0:
            h, hn = _rglru_mixer(hn, h, B, Tp, *[p[j] for p in rg], ffn_norm[i])
        elif kind == 1:
            h, hn = _na_mixer(hn, h, B, Tp, *[p[j] for p in na], ffn_norm[i])
        else:
            h, hn = _gdn_mixer(hn, h, B, Tp, *[p[j] for p in gdn], ffn_norm[i])
        w_gate, w_up, conv_w, conv_b, w_down = [p[i] for p in ffn]
        next_norm = mix_norm[i + 1] if i + 1 < depth else final_norm
        h, hn = _conv_ffn(hn, h, w_gate.astype(BF16), w_up.astype(BF16), conv_w, conv_b,
                          w_down.astype(BF16), next_norm)
    return _final_norm(h, final_norm, B, T)


def kernel(x_prompt, x_sample, meta_tokens, mix_norm, ffn_norm, final_norm, rg_w_in, rg_conv_w, rg_conv_b, rg_w_a, rg_b_a, rg_w_i, rg_b_i, rg_lam, rg_w_out, na_w_qkv, na_rpb, na_meta_bias, na_w_o, gdn_w_in, gdn_conv_w, gdn_a_log, gdn_dt_bias, gdn_norm_w, gdn_w_out, ffn_w_gate, ffn_w_up, ffn_conv_w, ffn_conv_b, ffn_w_down):
    rg = (rg_w_in, rg_conv_w, rg_conv_b, rg_w_a, rg_b_a, rg_w_i, rg_b_i, rg_lam, rg_w_out)
    na_bias = jnp.stack([_na_bias_table(na_rpb[j]) for j in range(na_rpb.shape[0])])
    na = (na_w_qkv, na_bias, na_meta_bias, na_w_o)
    gdn = (gdn_w_in, gdn_conv_w, gdn_a_log, gdn_dt_bias, gdn_norm_w, gdn_w_out)
    ffn = (ffn_w_gate, ffn_w_up, ffn_conv_w, ffn_conv_b, ffn_w_down)
    y_prompt = _trunk(x_prompt, meta_tokens, mix_norm, ffn_norm, final_norm, rg, na, gdn, ffn)
    y_sample = _trunk(x_sample, meta_tokens, mix_norm, ffn_norm, final_norm, rg, na, gdn, ffn)
    return (y_prompt, y_sample)
```

```python
import functools

import numpy as np
import jax
import jax.numpy as jnp
from jax import lax
from jax.experimental import pallas as pl
from jax.experimental.pallas import tpu as pltpu

D_MODEL = 2048
N_META = 16
PAD = 48
HEAD_ROWS = PAD + N_META
CHUNK = 64
GRID_W = 64
NORM_EPS = 1e-6

RG_BLOCK = 256
RG_C = 8.0

NA_HEAD_DIM = 128
NA_HEADS = D_MODEL // NA_HEAD_DIM
NA_WIN_R = 8
NA_WIN_C = 16
NA_ROW_GROUP = 8
NEG_INF = -1e30

GDN_DK = 128
GDN_QK_HEADS = D_MODEL // GDN_DK
GDN_V_HEADS = 2 * GDN_QK_HEADS
GDN_KEY_DIM = GDN_QK_HEADS * GDN_DK
GDN_VAL_DIM = GDN_V_HEADS * GDN_DK
GDN_CONV_DIM = 2 * GDN_KEY_DIM + GDN_VAL_DIM

V7X_VMEM_LIMIT = 56 * 1024 * 1024
HALO = 16

BF16 = jnp.bfloat16
F32 = jnp.float32


def _cparams(*sem):
    return pltpu.CompilerParams(dimension_semantics=sem, vmem_limit_bytes=V7X_VMEM_LIMIT)


def _pick_tile(n, cap, mult):
    best = None
    for t in range(mult, cap + 1, mult):
        if n % t == 0:
            best = t
    assert best is not None, (n, cap, mult)
    return best


def _sigmoid(x):
    return 1.0 / (1.0 + jnp.exp(-x))


def _softplus(x):
    return jnp.maximum(x, 0.0) + jnp.log(1.0 + jnp.exp(-jnp.abs(x)))


def _gelu_tanh(x):
    return 0.5 * x * (1.0 + jnp.tanh(0.7978845608028654 * (x + 0.044715 * x * x * x)))


def _rms_scale(x, w):
    ms = jnp.mean(x * x, axis=-1, keepdims=True)
    return x * lax.rsqrt(ms + NORM_EPS) * w


def _dot(a, b):
    return jnp.dot(a, b, preferred_element_type=F32)


def _dot_nt(a, b):
    return lax.dot_general(a, b, (((1,), (1,)), ((), ())), preferred_element_type=F32)


def _dot_tn(a, b):
    return lax.dot_general(a, b, (((0,), (0,)), ((), ())), preferred_element_type=F32)


def _rmsnorm_kernel(x_ref, w_ref, o_ref):
    o_ref[...] = _rms_scale(x_ref[...], w_ref[...]).astype(o_ref.dtype)


def _rmsnorm(h, w, out_dtype):
    M, D = h.shape
    tm = _pick_tile(M, 1024, 64)
    return pl.pallas_call(
        _rmsnorm_kernel,
        grid=(M // tm,),
        in_specs=[pl.BlockSpec((tm, D), lambda i: (i, 0)), pl.BlockSpec((1, D), lambda i: (0, 0))],
        out_specs=pl.BlockSpec((tm, D), lambda i: (i, 0)),
        out_shape=jax.ShapeDtypeStruct((M, D), out_dtype),
        compiler_params=_cparams("parallel"),
    )(h, w.reshape(1, D))


def _final_norm_kernel(x_ref, w_ref, o_ref):
    o_ref[0] = _rms_scale(x_ref[...], w_ref[...])


def _final_norm(h, w, B, T):
    D = h.shape[1]
    Tp = T + HEAD_ROWS
    tt = 512
    assert T % tt == 0
    return pl.pallas_call(
        _final_norm_kernel,
        grid=(B, T // tt),
        in_specs=[pl.BlockSpec((pl.Element(tt), pl.Element(D)),
                               lambda b, j: (pl.multiple_of(b * Tp + HEAD_ROWS + j * tt, CHUNK), 0)),
                  pl.BlockSpec((1, D), lambda b, j: (0, 0))],
        out_specs=pl.BlockSpec((1, tt, D), lambda b, j: (b, j, 0)),
        out_shape=jax.ShapeDtypeStruct((B, T, D), F32),
        compiler_params=_cparams("parallel", "parallel"),
    )(h, w.reshape(1, D))


def _matmul_kernel(x_ref, w_ref, o_ref, *, act):
    acc = _dot(x_ref[...], w_ref[...])
    if act == "gelu":
        acc = _gelu_tanh(acc)
    o_ref[...] = acc.astype(o_ref.dtype)


def _matmul(x, w, out_dtype, act=None):
    M, K = x.shape
    N = w.shape[1]
    tm = _pick_tile(M, 1024, 128)
    tn = _pick_tile(N, 512, 128)
    return pl.pallas_call(
        functools.partial(_matmul_kernel, act=act),
        grid=(M // tm, N // tn),
        in_specs=[pl.BlockSpec((tm, K), lambda i, j: (i, 0)), pl.BlockSpec((K, tn), lambda i, j: (0, j))],
        out_specs=pl.BlockSpec((tm, tn), lambda i, j: (i, j)),
        out_shape=jax.ShapeDtypeStruct((M, N), out_dtype),
        compiler_params=_cparams("parallel", "parallel"),
    )(x, w)


def _proj_res_kernel(x_ref, w_ref, h_ref, nw_ref, hout_ref, hn_ref, *, nk):
    k = pl.program_id(1)

    @pl.when(k == 0)
    def _():
        hout_ref[...] = h_ref[...]

    hout_ref[...] += _dot(x_ref[...], w_ref[...])

    @pl.when(k == nk - 1)
    def _():
        hn_ref[...] = _rms_scale(hout_ref[...], nw_ref[...]).astype(hn_ref.dtype)


def _proj_res(x, w, h, nw):
    M, K = x.shape
    D = w.shape[1]
    tm = _pick_tile(M, 640, 64)
    tk = 1024
    nk = K // tk
    return pl.pallas_call(
        functools.partial(_proj_res_kernel, nk=nk),
        grid=(M // tm, nk),
        in_specs=[pl.BlockSpec((tm, tk), lambda i, k: (i, k)),
                  pl.BlockSpec((tk, D), lambda i, k: (k, 0)),
                  pl.BlockSpec((tm, D), lambda i, k: (i, 0)),
                  pl.BlockSpec((1, D), lambda i, k: (0, 0))],
        out_specs=[pl.BlockSpec((tm, D), lambda i, k: (i, 0)), pl.BlockSpec((tm, D), lambda i, k: (i, 0))],
        out_shape=[jax.ShapeDtypeStruct((M, D), F32), jax.ShapeDtypeStruct((M, D), BF16)],
        compiler_params=_cparams("parallel", "arbitrary"),
    )(x, w, h, nw.reshape(1, D))


def _ffn_kernel(xp_ref, xm_ref, xn_ref, wg_ref, wu_ref, cw_ref, cb_ref, wd_ref, h_ref, nw_ref,
                hout_ref, hn_ref, xext_ref, *, tm, nf, ni):
    i = pl.program_id(0)
    f = pl.program_id(1)
    th = tm // 2

    @pl.when(f == 0)
    def _():
        xext_ref[0:HALO, :] = xp_ref[...]
        xext_ref[HALO:HALO + tm, :] = xm_ref[...]
        xext_ref[HALO + tm:2 * HALO + tm, :] = xn_ref[...]
        hout_ref[...] = h_ref[...]

    cw = cw_ref[...]
    cb = cb_ref[...]
    g, up = [], []
    for s in range(2):
        g.append(_dot(xext_ref[s * th:s * th + th + 2 * HALO, :], wg_ref[...]))
        up.append(_dot(xm_ref[s * th:(s + 1) * th, :], wu_ref[...]))
    row = lax.broadcasted_iota(jnp.int32, (th, 1), 0)
    y = []
    for s in range(2):
        g_prev = g[s][HALO - 1:HALO - 1 + th]
        g_here = g[s][HALO:HALO + th]
        g_next = g[s][HALO + 1:HALO + 1 + th]
        if s == 1:
            g_next = jnp.where(jnp.logical_and(i == ni - 1, row == th - 1), 0.0, g_next)
        a = cw[0:1] * g_prev + cw[1:2] * g_here + cw[2:3] * g_next + cb
        y.append((a * _sigmoid(a) * up[s]).astype(BF16))
    down = [_dot(y[s], wd_ref[...]) for s in range(2)]
    for s in range(2):
        hout_ref[s * th:(s + 1) * th, :] += down[s]

    @pl.when(f == nf - 1)
    def _():
        hn_ref[...] = _rms_scale(hout_ref[...], nw_ref[...]).astype(hn_ref.dtype)


def _conv_ffn(hn, h, w_gate, w_up, conv_w, conv_b, w_down, nw):
    M, D = hn.shape
    F = w_gate.shape[1]
    tm = _pick_tile(M, 640, 64)
    tf = _pick_tile(F, 512, 128)
    ni, nf = M // tm, F // tf
    hb = tm // HALO
    nhb = M // HALO
    return pl.pallas_call(
        functools.partial(_ffn_kernel, tm=tm, nf=nf, ni=ni),
        grid=(ni, nf),
        in_specs=[pl.BlockSpec((HALO, D), lambda i, f: (jnp.maximum(i * hb - 1, 0), 0)),
                  pl.BlockSpec((tm, D), lambda i, f: (i, 0)),
                  pl.BlockSpec((HALO, D), lambda i, f: (jnp.minimum((i + 1) * hb, nhb - 1), 0)),
                  pl.BlockSpec((D, tf), lambda i, f: (0, f)),
                  pl.BlockSpec((D, tf), lambda i, f: (0, f)),
                  pl.BlockSpec((3, tf), lambda i, f: (0, f)),
                  pl.BlockSpec((1, tf), lambda i, f: (0, f)),
                  pl.BlockSpec((tf, D), lambda i, f: (f, 0)),
                  pl.BlockSpec((tm, D), lambda i, f: (i, 0)),
                  pl.BlockSpec((1, D), lambda i, f: (0, 0))],
        out_specs=[pl.BlockSpec((tm, D), lambda i, f: (i, 0)), pl.BlockSpec((tm, D), lambda i, f: (i, 0))],
        out_shape=[jax.ShapeDtypeStruct((M, D), F32), jax.ShapeDtypeStruct((M, D), BF16)],
        scratch_shapes=[pltpu.VMEM((tm + 2 * HALO, D), BF16)],
        compiler_params=_cparams("parallel", "arbitrary"),
    )(hn, hn, hn, w_gate, w_up, conv_w, conv_b.reshape(1, F), w_down, h, nw.reshape(1, D))


def _scan8(a, u, row, reverse):
    for s in (1, 2, 4):
        if reverse:
            a_sh = pltpu.roll(a, 8 - s, 0)
            u_sh = pltpu.roll(u, 8 - s, 0)
            m = row < 8 - s
        else:
            a_sh = pltpu.roll(a, s, 0)
            u_sh = pltpu.roll(u, s, 0)
            m = row >= s
        u = jnp.where(m, a * u_sh + u, u)
        a = jnp.where(m, a * a_sh, a)
    return a, u


def _rg_kernel(rec_ref, gate_ref, cw_ref, cb_ref, wa_ref, ba_ref, wi_ref, bi_ref, lam_ref, o_ref,
               xc_s, hs_s, *, Tp, rc):
    nch = Tp // rc
    W = RG_BLOCK
    cw = cw_ref[...]
    cb = cb_ref[...]
    row8 = lax.broadcasted_iota(jnp.int32, (8, W), 0)

    def gates(xc, d):
        xb = xc.astype(BF16)
        r = _sigmoid(_dot(xb, wa_ref[d, 0]) + ba_ref[d])
        ig = _sigmoid(_dot(xb, wi_ref[d, 0]) + bi_ref[d])
        log_a = (-RG_C) * r * _softplus(-lam_ref[d])
        a = jnp.exp(log_a)
        u = jnp.sqrt(1.0 - jnp.exp(2.0 * log_a)) * (ig * xc)
        return a, u

    def fwd_chunk(c, carry):
        r0 = pl.multiple_of(c * rc, CHUNK)
        top = rec_ref[0, pl.ds(pl.multiple_of(jnp.maximum(r0 - 8, 0), 8), 8), :]
        main = rec_ref[0, pl.ds(r0, rc), :]
        bot = rec_ref[0, pl.ds(pl.multiple_of(jnp.minimum(r0 + rc, Tp - 8), 8), 8), :]
        bot = jnp.where(r0 + rc < Tp, bot, 0.0)
        xw = jnp.concatenate([top, main, bot], axis=0)
        xc = cw[0:1] * xw[7:7 + rc] + cw[1:2] * xw[8:8 + rc] + cw[2:3] * xw[9:9 + rc] + cw[3:4] * xw[10:10 + rc] + cb
        t = r0 + lax.broadcasted_iota(jnp.int32, (rc, 1), 0)
        xc = jnp.where(t >= PAD, xc, 0.0)
        xc_s[pl.ds(r0, rc), :] = xc
        a, u = gates(xc, 0)
        hs = []
        for k in range(rc // 8):
            A, U = _scan8(a[8 * k:8 * k + 8], u[8 * k:8 * k + 8], row8, False)
            h = A * carry + U
            carry = h[7:8]
            hs.append(h)
        hs_s[pl.ds(r0, rc), :] = jnp.concatenate(hs, axis=0)
        return carry

    lax.fori_loop(0, nch, fwd_chunk, jnp.zeros((1, W), F32))

    def bwd_chunk(cc, carry):
        r0 = pl.multiple_of((nch - 1 - cc) * rc, CHUNK)
        xc = xc_s[pl.ds(r0, rc), :]
        a, u = gates(xc, 1)
        hs = [None] * (rc // 8)
        for k in reversed(range(rc // 8)):
            A, U = _scan8(a[8 * k:8 * k + 8], u[8 * k:8 * k + 8], row8, True)
            h = A * carry + U
            carry = h[0:1]
            hs[k] = h
        hsum = hs_s[pl.ds(r0, rc), :] + jnp.concatenate(hs, axis=0)
        o_ref[0, pl.ds(r0, rc), :] = (gate_ref[0, pl.ds(r0, rc), :].astype(F32) * hsum).astype(o_ref.dtype)
        return carry

    lax.fori_loop(0, nch, bwd_chunk, jnp.zeros((1, W), F32))


def _rg_scan(rec, gate, conv_w, conv_b, w_a, b_a, w_i, b_i, lam):
    B, Tp, W = rec.shape
    nb = W // RG_BLOCK
    rc = _pick_tile(Tp, 384, 64)
    blk = lambda b, n: (b, 0, n)
    vec = lambda b, n: (0, n)
    vec3 = lambda b, n: (0, 0, n)
    return pl.pallas_call(
        functools.partial(_rg_kernel, Tp=Tp, rc=rc),
        grid=(B, nb),
        in_specs=[pl.BlockSpec((1, Tp, RG_BLOCK), blk),
                  pl.BlockSpec((1, Tp, RG_BLOCK), blk),
                  pl.BlockSpec((4, RG_BLOCK), vec),
                  pl.BlockSpec((1, RG_BLOCK), vec),
                  pl.BlockSpec((2, 1, RG_BLOCK, RG_BLOCK), lambda b, n: (0, n, 0, 0)),
                  pl.BlockSpec((2, 1, RG_BLOCK), vec3),
                  pl.BlockSpec((2, 1, RG_BLOCK, RG_BLOCK), lambda b, n: (0, n, 0, 0)),
                  pl.BlockSpec((2, 1, RG_BLOCK), vec3),
                  pl.BlockSpec((2, 1, RG_BLOCK), vec3)],
        out_specs=pl.BlockSpec((1, Tp, RG_BLOCK), blk),
        out_shape=jax.ShapeDtypeStruct((B, Tp, W), BF16),
        scratch_shapes=[pltpu.VMEM((Tp, RG_BLOCK), F32), pltpu.VMEM((Tp, RG_BLOCK), F32)],
        compiler_params=_cparams("parallel", "parallel"),
    )(rec, gate, conv_w, conv_b.reshape(1, W), w_a.astype(BF16), b_a.reshape(2, 1, W),
      w_i.astype(BF16), b_i.reshape(2, 1, W), lam.reshape(2, 1, W))


def _rglru_mixer(hn, h, B, Tp, w_in, conv_w, conv_b, w_a, b_a, w_i, b_i, lam, w_out, nw):
    W = D_MODEL
    w_in = w_in.astype(BF16)
    gate = _matmul(hn, w_in[:, :W], BF16, act="gelu")
    rec = _matmul(hn, w_in[:, W:], F32)
    y = _rg_scan(rec.reshape(B, Tp, W), gate.reshape(B, Tp, W), conv_w, conv_b, w_a, b_a, w_i, b_i, lam)
    return _proj_res(y.reshape(B * Tp, W), w_out.astype(BF16), h, nw)


def _na_kernel(q_ref, k_ref, v_ref, bias_ref, mb_ref, o_ref, *, rows):
    scale = NA_HEAD_DIM ** -0.5
    kh = NA_WIN_R
    mb = mb_ref[0]
    km = k_ref[0, PAD:HEAD_ROWS, :]
    vm = v_ref[0, PAD:HEAD_ROWS, :]

    o_ref[0, 0:PAD, :] = jnp.zeros((PAD, NA_HEAD_DIM), o_ref.dtype)
    qm = q_ref[0, PAD:HEAD_ROWS, :]
    s_m = _dot_nt(qm, km) * scale + mb
    p_m = jnp.exp(s_m - jnp.max(s_m, axis=-1, keepdims=True))
    o_m = _dot(p_m.astype(BF16), vm) / jnp.sum(p_m, axis=-1, keepdims=True)
    o_ref[0, PAD:HEAD_ROWS, :] = o_m.astype(o_ref.dtype)

    R = range(NA_ROW_GROUP)

    def group_body(i, _):
        r = [i * NA_ROW_GROUP + c for c in R]
        rs = [jnp.clip(r[c] - NA_WIN_R // 2, 0, rows - kh) for c in R]
        q0 = [pl.multiple_of(HEAD_ROWS + r[c] * GRID_W, GRID_W) for c in R]
        k0 = [pl.multiple_of(HEAD_ROWS + rs[c] * GRID_W, GRID_W) for c in R]
        q_r = [q_ref[0, pl.ds(q0[c], GRID_W), :] for c in R]
        s = [_dot_nt(q_r[c], k_ref[0, pl.ds(k0[c], kh * GRID_W), :]) for c in R]
        s_met = [_dot_nt(q_r[c], km) for c in R]
        p, p_met, l = [], [], []
        for c in R:
            sc = s[c] * scale + bias_ref[0, r[c] - rs[c]]
            sm = s_met[c] * scale + mb
            m = jnp.maximum(jnp.max(sc, axis=-1, keepdims=True), jnp.max(sm, axis=-1, keepdims=True))
            pc = jnp.exp(sc - m)
            pm = jnp.exp(sm - m)
            l.append(jnp.sum(pc, axis=-1, keepdims=True) + jnp.sum(pm, axis=-1, keepdims=True))
            p.append(pc.astype(BF16))
            p_met.append(pm.astype(BF16))
        o = [_dot(p[c], v_ref[0, pl.ds(k0[c], kh * GRID_W), :]) + _dot(p_met[c], vm) for c in R]
        for c in R:
            o_ref[0, pl.ds(q0[c], GRID_W), :] = (o[c] / l[c]).astype(o_ref.dtype)
        return 0

    assert rows % NA_ROW_GROUP == 0
    lax.fori_loop(0, rows // NA_ROW_GROUP, group_body, 0)


def _na_bias_table(rpb):
    kh = NA_WIN_R
    cols = np.arange(GRID_W)
    col_start = np.clip(cols - NA_WIN_C // 2, 0, GRID_W - NA_WIN_C)
    valid = (cols[None, :] >= col_start[:, None]) & (cols[None, :] < col_start[:, None] + NA_WIN_C)
    col_off = np.clip(cols[None, :] - cols[:, None], -(NA_WIN_C - 1), NA_WIN_C - 1) + NA_WIN_C - 1
    H, n_ro, n_co = rpb.shape
    onehot = (col_off[None] == np.arange(n_co)[:, None, None]).astype(np.float32)
    a = jnp.einsum('hrc,cqk->hqrk', rpb.astype(F32), jnp.asarray(onehot), precision=lax.Precision.HIGHEST)
    a = jnp.where(jnp.asarray(valid)[None, :, None, :], a, NEG_INF)
    tabs = [a[:, :, NA_WIN_R - 1 - d:NA_WIN_R - 1 - d + kh, :].reshape(H, GRID_W, kh * GRID_W)
            for d in range(NA_WIN_R)]
    return jnp.stack(tabs, axis=1)


def _na_attention(qkv, bias, meta_bias):
    B, Tp, _ = qkv.shape
    rows = (Tp - HEAD_ROWS) // GRID_W
    assert rows >= NA_WIN_R
    H = NA_HEADS
    return pl.pallas_call(
        functools.partial(_na_kernel, rows=rows),
        grid=(B, H),
        in_specs=[pl.BlockSpec((1, Tp, NA_HEAD_DIM), lambda b, h: (b, 0, h)),
                  pl.BlockSpec((1, Tp, NA_HEAD_DIM), lambda b, h: (b, 0, H + h)),
                  pl.BlockSpec((1, Tp, NA_HEAD_DIM), lambda b, h: (b, 0, 2 * H + h)),
                  pl.BlockSpec((1, NA_WIN_R, GRID_W, NA_WIN_R * GRID_W), lambda b, h: (h, 0, 0, 0)),
                  pl.BlockSpec((1, 1, N_META), lambda b, h: (h, 0, 0))],
        out_specs=pl.BlockSpec((1, Tp, NA_HEAD_DIM), lambda b, h: (b, 0, h)),
        out_shape=jax.ShapeDtypeStruct((B, Tp, H * NA_HEAD_DIM), BF16),
        compiler_params=_cparams("parallel", "parallel"),
    )(qkv, qkv, qkv, bias, meta_bias.astype(F32).reshape(H, 1, N_META))


def _na_mixer(hn, h, B, Tp, w_qkv, bias, meta_bias, w_o, nw):
    qkv = _matmul(hn, w_qkv.astype(BF16), BF16)
    o = _na_attention(qkv.reshape(B, Tp, -1), bias, meta_bias)
    return _proj_res(o.reshape(B * Tp, -1), w_o.astype(BF16), h, nw)


def _gdn_gate_kernel(x_ref, w_ref, al_ref, dt_ref, o_ref):
    y = _dot(x_ref[...], w_ref[...])
    lane = lax.broadcasted_iota(jnp.int32, y.shape, 1)
    g = -jnp.exp(al_ref[...]) * _softplus(y + dt_ref[...])
    o_ref[...] = jnp.where(lane < 2 * GDN_V_HEADS, _sigmoid(y), g)


def _gdn_gates(hn, w_ba, a_log, dt_bias):
    M, K = hn.shape
    N = 4 * GDN_V_HEADS
    tm = _pick_tile(M, 1024, 128)
    zeros = jnp.zeros((2 * GDN_V_HEADS,), F32)
    al = jnp.concatenate([zeros, a_log.astype(F32).reshape(-1)]).reshape(1, N)
    dt = jnp.concatenate([zeros, dt_bias.astype(F32).reshape(-1)]).reshape(1, N)
    return pl.pallas_call(
        _gdn_gate_kernel,
        grid=(M // tm,),
        in_specs=[pl.BlockSpec((tm, K), lambda i: (i, 0)), pl.BlockSpec((K, N), lambda i: (0, 0)),
                  pl.BlockSpec((1, N), lambda i: (0, 0)), pl.BlockSpec((1, N), lambda i: (0, 0))],
        out_specs=pl.BlockSpec((tm, N), lambda i: (i, 0)),
        out_shape=jax.ShapeDtypeStruct((M, N), F32),
        compiler_params=_cparams("parallel"),
    )(hn, w_ba, al, dt)


def _gdn_conv_kernel(x_ref, cw_ref, o_ref, *, Tp, rc, n_qk_blocks):
    j = pl.program_id(1)
    cw = cw_ref[...]
    nch = Tp // rc

    def chunk(c, _):
        r0 = pl.multiple_of(c * rc, CHUNK)
        top = x_ref[0, pl.ds(pl.multiple_of(jnp.maximum(r0 - 8, 0), 8), 8), :]
        main = x_ref[0, pl.ds(r0, rc), :]
        bot = x_ref[0, pl.ds(pl.multiple_of(jnp.minimum(r0 + rc, Tp - 8), 8), 8), :]
        bot = jnp.where(r0 + rc < Tp, bot, 0.0)
        xw = jnp.concatenate([top, main, bot], axis=0)
        a = cw[0:1] * xw[7:7 + rc] + cw[1:2] * xw[8:8 + rc] + cw[2:3] * xw[9:9 + rc] + cw[3:4] * xw[10:10 + rc]
        y = a * _sigmoid(a)
        t = r0 + lax.broadcasted_iota(jnp.int32, (rc, 1), 0)
        y = jnp.where(t >= PAD, y, 0.0)
        nrm = lax.rsqrt(jnp.sum(y * y, axis=-1, keepdims=True) + 1e-6)
        y = y * jnp.where(j < n_qk_blocks, nrm, 1.0)
        o_ref[0, pl.ds(r0, rc), :] = y.astype(o_ref.dtype)
        return 0

    lax.fori_loop(0, nch, chunk, 0)


def _gdn_conv(proj3, conv_w):
    B, Tp, _ = proj3.shape
    nblk = GDN_CONV_DIM // GDN_DK
    rc = _pick_tile(Tp, 384, 64)
    return pl.pallas_call(
        functools.partial(_gdn_conv_kernel, Tp=Tp, rc=rc, n_qk_blocks=2 * GDN_QK_HEADS),
        grid=(B, nblk),
        in_specs=[pl.BlockSpec((1, Tp, GDN_DK), lambda b, j: (b, 0, j)),
                  pl.BlockSpec((4, GDN_DK), lambda b, j: (0, j))],
        out_specs=pl.BlockSpec((1, Tp, GDN_DK), lambda b, j: (b, 0, j)),
        out_shape=jax.ShapeDtypeStruct((B, Tp, GDN_CONV_DIM), BF16),
        compiler_params=_cparams("parallel", "parallel"),
    )(proj3, conv_w)


def _split3_dot(x, m):
    hi = x.astype(BF16)
    r1 = x - hi.astype(F32)
    mid = r1.astype(BF16)
    lo = (r1 - mid.astype(F32)).astype(BF16)
    return _dot(hi, m) + _dot(mid, m) + _dot(lo, m)


def _gdn_kernel_unused(q_ref, k_ref, v_ref, z_ref, bgc_ref, bgr_ref, nw_ref, o_ref, S_s, oacc_s, *, Tp):
    kh = pl.program_id(1)
    C = CHUNK
    NC = Tp // C
    HV = GDN_V_HEADS
    DK = GDN_DK
    scale = DK ** -0.5

    oacc_s[...] = jnp.zeros_like(oacc_s)
    S_s[...] = jnp.zeros_like(S_s)

    ri = lax.broadcasted_iota(jnp.int32, (C, C), 0)
    ci = lax.broadcasted_iota(jnp.int32, (C, C), 1)
    lower = ri >= ci
    upper = ri <= ci
    eye = (ri == ci).astype(F32)
    ones_lower = lower.astype(BF16)
    ones_upper = upper.astype(BF16)
    lane = lax.broadcasted_iota(jnp.int32, (C, 4 * HV), 1)
    rowi = lax.broadcasted_iota(jnp.int32, (C, 4 * HV), 0)

    def col(x, c):
        return jnp.sum(jnp.where(lane == c, x, 0.0), axis=1, keepdims=True)

    def one_chunk(n, reverse):
        r0 = pl.multiple_of(n * C, C)
        q = q_ref[0, pl.ds(r0, C), :]
        k = k_ref[0, pl.ds(r0, C), :]
        kf = k.astype(F32)
        bg = bgc_ref[0, pl.ds(r0, C), :]
        pre = bg
        for s in (1, 2, 4, 8, 16, 32):
            pre = pre + jnp.where(rowi >= s, pltpu.roll(pre, s, 0), 0.0)
        if reverse:
            cum = pre[C - 1:C] - pre + bg
        else:
            cum = pre
        rows = bgr_ref[0, n, 0]
        cum_rows = _split3_dot(rows, ones_lower if reverse else ones_upper)
        kk = _dot_nt(k, k)
        qk = _dot_nt(q, k) * scale
        incl = upper if reverse else lower
        strict = (ri < ci) if reverse else (ri > ci)
        d = 1 if reverse else 0
        for j in range(2):
            inst = 2 * d + j
            vh = 2 * kh + j
            beta_c = col(bg, d * HV + vh)
            gc_c = col(cum, 2 * HV + d * HV + vh)
            gc_r = cum_rows[4 + 2 * d + j:5 + 2 * d + j]
            decay = jnp.where(incl, jnp.exp(jnp.where(incl, gc_c - gc_r, 0.0)), 0.0)
            L = jnp.where(strict, kk * decay, 0.0) * beta_c
            Tm = eye - L
            P = L
            for _ in range(5):
                Pb = P.astype(BF16)
                P = _dot(Pb, Pb)
                Tm = Tm + _dot(Tm.astype(BF16), P.astype(BF16))
            Tb = Tm.astype(BF16)
            v = v_ref[0, pl.ds(r0, C), j * DK:(j + 1) * DK].astype(F32)
            e_c = jnp.exp(gc_c)
            u = _dot(Tb, (v * beta_c).astype(BF16))
            w = _dot(Tb, (kf * (beta_c * e_c)).astype(BF16))
            qkd = jnp.where(incl, qk * decay, 0.0)
            g_last = gc_c[0:1] if reverse else gc_c[C - 1:C]
            k_st = kf * jnp.exp(g_last - gc_c)
            q_st = q.astype(F32) * (scale * e_c)
            S = S_s[inst]
            Sb = S.astype(BF16)
            v_new = u - _dot(w.astype(BF16), Sb)
            vb = v_new.astype(BF16)
            o = _dot(q_st.astype(BF16), Sb) + _dot(qkd.astype(BF16), vb)
            S_s[inst] = S * jnp.exp(g_last) + _dot_tn(k_st.astype(BF16), vb)
            oacc_s[pl.ds(r0, C), j * DK:(j + 1) * DK] += o

    def body(it, _):
        one_chunk(it, False)
        one_chunk(NC - 1 - it, True)
        return 0

    lax.fori_loop(0, NC, body, 0)

    nw = nw_ref[...]

    def fin(n, _):
        r0 = pl.multiple_of(n * C, C)
        for j in range(2):
            o = oacc_s[pl.ds(r0, C), j * DK:(j + 1) * DK]
            z = z_ref[0, pl.ds(r0, C), j * DK:(j + 1) * DK]
            y = o * lax.rsqrt(jnp.mean(o * o, axis=-1, keepdims=True) + NORM_EPS) * nw * (z * _sigmoid(z))
            o_ref[0, pl.ds(r0, C), j * DK:(j + 1) * DK] = y.astype(o_ref.dtype)
        return 0

    lax.fori_loop(0, NC, fin, 0)


def _gdn_delta_unused(qkv, proj3, bg, norm_w):
    B, Tp, _ = qkv.shape
    NC = Tp // CHUNK
    KH = GDN_QK_HEADS
    HV = GDN_V_HEADS
    bgr = bg.reshape(B, NC, CHUNK, 4, KH, 2).transpose(0, 1, 4, 3, 5, 2).reshape(B, NC, KH, 8, CHUNK)
    zoff = GDN_CONV_DIM // (2 * GDN_DK)
    return pl.pallas_call(
        functools.partial(_gdn_kernel, Tp=Tp),
        grid=(B, KH),
        in_specs=[pl.BlockSpec((1, Tp, GDN_DK), lambda b, h: (b, 0, h)),
                  pl.BlockSpec((1, Tp, GDN_DK), lambda b, h: (b, 0, KH + h)),
                  pl.BlockSpec((1, Tp, 2 * GDN_DK), lambda b, h: (b, 0, KH + h)),
                  pl.BlockSpec((1, Tp, 2 * GDN_DK), lambda b, h: (b, 0, zoff + h)),
                  pl.BlockSpec((1, Tp, 4 * HV), lambda b, h: (b, 0, 0)),
                  pl.BlockSpec((1, NC, 1, 8, CHUNK), lambda b, h: (b, 0, h, 0, 0)),
                  pl.BlockSpec((1, GDN_DK), lambda b, h: (0, 0))],
        out_specs=pl.BlockSpec((1, Tp, 2 * GDN_DK), lambda b, h: (b, 0, h)),
        out_shape=jax.ShapeDtypeStruct((B, Tp, GDN_VAL_DIM), BF16),
        scratch_shapes=[pltpu.VMEM((4, GDN_DK, GDN_DK), F32), pltpu.VMEM((Tp, 2 * GDN_DK), F32)],
        compiler_params=_cparams("parallel", "parallel"),
    )(qkv, qkv, qkv, proj3, bg, bgr, norm_w.astype(F32).reshape(1, GDN_DK))


def _gdn_kernel(q_ref, k_ref, v_ref, z_ref, bgc_ref, bgr_ref, nw_ref, o_ref,
                S_s, oacc_s, u_s, wq_s, kst_s, qkd_s, gam_s, *, Tp):
    kh = pl.program_id(1)
    C = CHUNK
    NC = Tp // C
    HV = GDN_V_HEADS
    DK = GDN_DK
    G = 4
    W = G * C
    P1_GROUP = next((g for g in (5, 4, 3) if NC % g == 0), 4)
    scale = DK ** -0.5

    oacc_s[...] = jnp.zeros_like(oacc_s)
    S_s[...] = jnp.zeros_like(S_s)

    ri = lax.broadcasted_iota(jnp.int32, (C, W), 0)
    cl = lax.broadcasted_iota(jnp.int32, (C, W), 1)
    cj = jnp.bitwise_and(cl, C - 1)
    blk = jnp.right_shift(cl, 6)
    dd = jnp.where(blk >= 2, cj - ri, ri - cj)
    incl4 = dd >= 0
    strict4 = dd > 0
    eye4 = (ri == cj).astype(F32)
    cum4 = jnp.where(dd <= 0, 1.0, 0.0).astype(BF16)
    bdm = (jnp.right_shift(lax.broadcasted_iota(jnp.int32, (W, W), 0), 6)
           == jnp.right_shift(lax.broadcasted_iota(jnp.int32, (W, W), 1), 6))
    blk_row = jnp.right_shift(lax.broadcasted_iota(jnp.int32, (1, W), 1), 6)
    lane = lax.broadcasted_iota(jnp.int32, (C, 4 * HV), 1)
    rowi = lax.broadcasted_iota(jnp.int32, (C, 4 * HV), 0)

    def col(x, c):
        return jnp.sum(jnp.where(lane == c, x, 0.0), axis=1, keepdims=True)

    def by_block(parts, b):
        return jnp.where(b == 0, parts[0], jnp.where(b == 1, parts[1], jnp.where(b == 2, parts[2], parts[3])))

    def blockdiag(x):
        return jnp.where(bdm, jnp.concatenate([x] * G, axis=0), 0.0).astype(BF16)

    def phase1(ns):
        K = range(len(ns))
        r0 = [pl.multiple_of(n * C, C) for n in ns]
        q = [q_ref[0, pl.ds(r, C), :] for r in r0]
        k = [k_ref[0, pl.ds(r, C), :] for r in r0]
        bg = [bgc_ref[0, pl.ds(r, C), :] for r in r0]
        v = [v_ref[0, pl.ds(r, C), :].astype(F32) for r in r0]
        rows = [bgr_ref[0, n, 0] for n in ns]
        cum_rows = [_split3_dot(rows[i], cum4) for i in K]
        k4 = [jnp.concatenate([k[i]] * G, axis=0) for i in K]
        kk4 = [_dot_nt(k[i], k4[i]) for i in K]
        qk4 = [_dot_nt(q[i], k4[i]) * scale for i in K]
        beta_c, gc_c, decay4, L = [], [], [], []
        for i in K:
            pre = bg[i]
            for s in (1, 2, 4, 8, 16, 32):
                pre = pre + jnp.where(rowi >= s, pltpu.roll(pre, s, 0), 0.0)
            suf = pre[C - 1:C] - pre + bg[i]
            gc_r = by_block([cum_rows[i][4 + g:5 + g] for g in range(G)], blk_row)
            bc, gc = [], []
            for g in range(G):
                d, j = g // 2, g % 2
                vh = 2 * kh + j
                bc.append(col(bg[i], d * HV + vh))
                gc.append(col(suf if d else pre, 2 * HV + d * HV + vh))
            beta_c.append(bc)
            gc_c.append(gc)
            dec = jnp.where(incl4, jnp.exp(jnp.where(incl4, by_block(gc, blk) - gc_r, 0.0)), 0.0)
            decay4.append(dec)
            L.append(jnp.where(strict4, kk4[i] * dec, 0.0) * by_block(bc, blk))
        Tm = [eye4 - L[i] for i in K]
        P = [_dot(L[i].astype(BF16), blockdiag(L[i])) for i in K]
        for lvl in range(5):
            bd = [blockdiag(P[i]) for i in K]
            if lvl < 4:
                tp = [_dot(jnp.concatenate([Tm[i], P[i]], axis=0).astype(BF16), bd[i]) for i in K]
                Tm = [Tm[i] + tp[i][:C] for i in K]
                P = [tp[i][C:] for i in K]
            else:
                tp = [_dot(Tm[i].astype(BF16), bd[i]) for i in K]
                Tm = [Tm[i] + tp[i] for i in K]
        rhs, kst, qst, gam = [], [], [], []
        for i in K:
            qf = q[i].astype(F32)
            kf = k[i].astype(F32)
            rhs_i, kst_i, qst_i, gam_i = [], [], [], []
            for g in range(G):
                j = g % 2
                bc, gc = beta_c[i][g], gc_c[i][g]
                e_c = jnp.exp(gc)
                rhs_i.append(jnp.concatenate([v[i][:, j * DK:(j + 1) * DK] * bc, kf * (bc * e_c)], axis=1))
                g_last = gc[0:1] if g >= 2 else gc[C - 1:C]
                kst_i.append((kf * jnp.exp(g_last - gc)).astype(BF16))
                qst_i.append((qf * (scale * e_c)).astype(BF16))
                gam_i.append(jnp.broadcast_to(jnp.exp(g_last), (1, DK)))
            rhs.append(jnp.concatenate(rhs_i, axis=0).astype(BF16))
            kst.append(kst_i)
            qst.append(qst_i)
            gam.append(gam_i)
        sol = [_dot(blockdiag(Tm[i]), rhs[i]) for i in K]
        for i in K:
            n = ns[i]
            u_s[n] = sol[i][:, :DK]
            qkd_s[n] = jnp.where(incl4, qk4[i] * decay4[i], 0.0).astype(BF16)
            for g in range(G):
                wq_s[n, g, 0:C, :] = sol[i][g * C:(g + 1) * C, DK:].astype(BF16)
                wq_s[n, g, C:2 * C, :] = qst[i][g]
                kst_s[n, g] = kst[i][g]
                gam_s[n, g:g + 1, :] = gam[i][g]

    def phase1_group(i, _):
        phase1([P1_GROUP * i + c for c in range(P1_GROUP)])
        return 0

    lax.fori_loop(0, NC // P1_GROUP, phase1_group, 0)
    if NC % P1_GROUP:
        phase1(list(range(NC - NC % P1_GROUP, NC)))

    def phase2(it, _):
        ops = []
        for g in range(G):
            n = it if g < 2 else NC - 1 - it
            ops.append((S_s[g], wq_s[n, g], u_s[n, g * C:(g + 1) * C, :], qkd_s[n, :, g * C:(g + 1) * C],
                        gam_s[n, g:g + 1, :], kst_s[n, g]))
        r_f = pl.multiple_of(it * C, C)
        r_b = pl.multiple_of((NC - 1 - it) * C, C)
        o_f = oacc_s[pl.ds(r_f, C), :]
        ws = [_dot(wq, S.astype(BF16)) for (S, wq, u, qkd, gam, kst) in ops]
        vb = [(ops[g][2] - ws[g][:C]).astype(BF16) for g in range(G)]
        outs = [ws[g][C:] + _dot(ops[g][3], vb[g]) for g in range(G)]
        states = [ops[g][0] * ops[g][4] + _dot_tn(ops[g][5], vb[g]) for g in range(G)]
        for g in range(G):
            S_s[g] = states[g]
        oacc_s[pl.ds(r_f, C), :] = o_f + jnp.concatenate(outs[0:2], axis=1)
        oacc_s[pl.ds(r_b, C), :] = oacc_s[pl.ds(r_b, C), :] + jnp.concatenate(outs[2:4], axis=1)
        return 0

    lax.fori_loop(0, NC, phase2, 0)

    nw = nw_ref[...]

    def fin(n, _):
        r0 = pl.multiple_of(n * C, C)
        for j in range(2):
            o = oacc_s[pl.ds(r0, C), j * DK:(j + 1) * DK]
            z = z_ref[0, pl.ds(r0, C), j * DK:(j + 1) * DK].astype(F32)
            y = o * lax.rsqrt(jnp.mean(o * o, axis=-1, keepdims=True) + NORM_EPS) * nw * (z * _sigmoid(z))
            o_ref[0, pl.ds(r0, C), j * DK:(j + 1) * DK] = y.astype(o_ref.dtype)
        return 0

    lax.fori_loop(0, NC, fin, 0)


def _gdn_delta(qkv, z, bg, norm_w):
    B, Tp, _ = qkv.shape
    NC = Tp // CHUNK
    KH = GDN_QK_HEADS
    HV = GDN_V_HEADS
    C = CHUNK
    bgr = bg.reshape(B, NC, CHUNK, 4, KH, 2).transpose(0, 1, 4, 3, 5, 2).reshape(B, NC, KH, 8, CHUNK)
    return pl.pallas_call(
        functools.partial(_gdn_kernel, Tp=Tp),
        grid=(B, KH),
        in_specs=[pl.BlockSpec((1, Tp, GDN_DK), lambda b, h: (b, 0, h)),
                  pl.BlockSpec((1, Tp, GDN_DK), lambda b, h: (b, 0, KH + h)),
                  pl.BlockSpec((1, Tp, 2 * GDN_DK), lambda b, h: (b, 0, KH + h)),
                  pl.BlockSpec((1, Tp, 2 * GDN_DK), lambda b, h: (b, 0, h)),
                  pl.BlockSpec((1, Tp, 4 * HV), lambda b, h: (b, 0, 0)),
                  pl.BlockSpec((1, NC, 1, 8, CHUNK), lambda b, h: (b, 0, h, 0, 0)),
                  pl.BlockSpec((1, GDN_DK), lambda b, h: (0, 0))],
        out_specs=pl.BlockSpec((1, Tp, 2 * GDN_DK), lambda b, h: (b, 0, h)),
        out_shape=jax.ShapeDtypeStruct((B, Tp, GDN_VAL_DIM), BF16),
        scratch_shapes=[pltpu.VMEM((4, GDN_DK, GDN_DK), F32),
                        pltpu.VMEM((Tp, 2 * GDN_DK), F32),
                        pltpu.VMEM((NC, 4 * C, GDN_DK), F32),
                        pltpu.VMEM((NC, 4, 2 * C, GDN_DK), BF16),
                        pltpu.VMEM((NC, 4, C, GDN_DK), BF16),
                        pltpu.VMEM((NC, C, 4 * C), BF16),
                        pltpu.VMEM((NC, 8, GDN_DK), F32)],
        compiler_params=_cparams("parallel", "parallel"),
    )(qkv, qkv, qkv, z, bg, bgr, norm_w.astype(F32).reshape(1, GDN_DK))


def _gdn_mixer(hn, h, B, Tp, w_in, conv_w, a_log, dt_bias, norm_w, w_out, nw):
    nz = GDN_CONV_DIM + GDN_VAL_DIM
    w_in = w_in.astype(BF16)
    proj = _matmul(hn, w_in[:, :GDN_CONV_DIM], F32)
    z = _matmul(hn, w_in[:, GDN_CONV_DIM:nz], BF16)
    bg = _gdn_gates(hn, w_in[:, nz:], a_log, dt_bias)
    qkv = _gdn_conv(proj.reshape(B, Tp, GDN_CONV_DIM), conv_w)
    o = _gdn_delta(qkv, z.reshape(B, Tp, -1), bg.reshape(B, Tp, -1), norm_w)
    return _proj_res(o.reshape(B * Tp, -1), w_out.astype(BF16), h, nw)


def _trunk(x, meta_tokens, mix_norm, ffn_norm, final_norm, rg, na, gdn, ffn):
    B, T, D = x.shape
    Tp = T + HEAD_ROWS
    depth = mix_norm.shape[0]
    h3 = jnp.concatenate([jnp.zeros((B, PAD, D), F32),
                          jnp.broadcast_to(meta_tokens.astype(F32)[None], (B, N_META, D)),
                          x.astype(F32)], axis=1)
    h = h3.reshape(B * Tp, D)
    hn = _rmsnorm(h, mix_norm[0], BF16)
    for i in range(depth):
        kind, j = i % 3, i // 3
        if kind == 0:
            h, hn = _rglru_mixer(hn, h, B, Tp, *[p[j] for p in rg], ffn_norm[i])
        elif kind == 1:
            h, hn = _na_mixer(hn, h, B, Tp, *[p[j] for p in na], ffn_norm[i])
        else:
            h, hn = _gdn_mixer(hn, h, B, Tp, *[p[j] for p in gdn], ffn_norm[i])
        w_gate, w_up, conv_w, conv_b, w_down = [p[i] for p in ffn]
        next_norm = mix_norm[i + 1] if i + 1 < depth else final_norm
        h, hn = _conv_ffn(hn, h, w_gate.astype(BF16), w_up.astype(BF16), conv_w, conv_b,
                          w_down.astype(BF16), next_norm)
    return _final_norm(h, final_norm, B, T)


def kernel(x_prompt, x_sample, meta_tokens, mix_norm, ffn_norm, final_norm, rg_w_in, rg_conv_w, rg_conv_b, rg_w_a, rg_b_a, rg_w_i, rg_b_i, rg_lam, rg_w_out, na_w_qkv, na_rpb, na_meta_bias, na_w_o, gdn_w_in, gdn_conv_w, gdn_a_log, gdn_dt_bias, gdn_norm_w, gdn_w_out, ffn_w_gate, ffn_w_up, ffn_conv_w, ffn_conv_b, ffn_w_down):
    rg = (rg_w_in, rg_conv_w, rg_conv_b, rg_w_a, rg_b_a, rg_w_i, rg_b_i, rg_lam, rg_w_out)
    na_bias = jnp.stack([_na_bias_table(na_rpb[j]) for j in range(na_rpb.shape[0])])
    na = (na_w_qkv, na_bias, na_meta_bias, na_w_o)
    gdn = (gdn_w_in, gdn_conv_w, gdn_a_log, gdn_dt_bias, gdn_norm_w, gdn_w_out)
    ffn = (ffn_w_gate, ffn_w_up, ffn_conv_w, ffn_conv_b, ffn_w_down)
    y_prompt = _trunk(x_prompt, meta_tokens, mix_norm, ffn_norm, final_norm, rg, na, gdn, ffn)
    y_sample = _trunk(x_sample, meta_tokens, mix_norm, ffn_norm, final_norm, rg, na, gdn, ffn)
    return (y_prompt, y_sample)
```

```python
import functools

import numpy as np
import jax
import jax.numpy as jnp
from jax import lax
from jax.experimental import pallas as pl
from jax.experimental.pallas import tpu as pltpu

D_MODEL = 2048
N_META = 16
PAD = 48
HEAD_ROWS = PAD + N_META
CHUNK = 64
GRID_W = 64
NORM_EPS = 1e-6

RG_BLOCK = 256
RG_C = 8.0

NA_HEAD_DIM = 128
NA_HEADS = D_MODEL // NA_HEAD_DIM
NA_WIN_R = 8
NA_WIN_C = 16
NA_ROW_GROUP = 8
NEG_INF = -1e30

GDN_DK = 128
GDN_QK_HEADS = D_MODEL // GDN_DK
GDN_V_HEADS = 2 * GDN_QK_HEADS
GDN_KEY_DIM = GDN_QK_HEADS * GDN_DK
GDN_VAL_DIM = GDN_V_HEADS * GDN_DK
GDN_CONV_DIM = 2 * GDN_KEY_DIM + GDN_VAL_DIM

V7X_VMEM_LIMIT = 56 * 1024 * 1024
HALO = 16

BF16 = jnp.bfloat16
F32 = jnp.float32


def _cparams(*sem):
    return pltpu.CompilerParams(dimension_semantics=sem, vmem_limit_bytes=V7X_VMEM_LIMIT)


def _pick_tile(n, cap, mult):
    best = None
    for t in range(mult, cap + 1, mult):
        if n % t == 0:
            best = t
    assert best is not None, (n, cap, mult)
    return best


def _sigmoid(x):
    return 1.0 / (1.0 + jnp.exp(-x))


def _softplus(x):
    return jnp.maximum(x, 0.0) + jnp.log(1.0 + jnp.exp(-jnp.abs(x)))


def _gelu_tanh(x):
    return 0.5 * x * (1.0 + jnp.tanh(0.7978845608028654 * (x + 0.044715 * x * x * x)))


def _rms_scale(x, w):
    ms = jnp.mean(x * x, axis=-1, keepdims=True)
    return x * lax.rsqrt(ms + NORM_EPS) * w


def _dot(a, b):
    return jnp.dot(a, b, preferred_element_type=F32)


def _dot_nt(a, b):
    return lax.dot_general(a, b, (((1,), (1,)), ((), ())), preferred_element_type=F32)


def _dot_tn(a, b):
    return lax.dot_general(a, b, (((0,), (0,)), ((), ())), preferred_element_type=F32)


def _rmsnorm_kernel(x_ref, w_ref, o_ref):
    o_ref[...] = _rms_scale(x_ref[...], w_ref[...]).astype(o_ref.dtype)


def _rmsnorm(h, w, out_dtype):
    M, D = h.shape
    tm = _pick_tile(M, 1024, 64)
    return pl.pallas_call(
        _rmsnorm_kernel,
        grid=(M // tm,),
        in_specs=[pl.BlockSpec((tm, D), lambda i: (i, 0)), pl.BlockSpec((1, D), lambda i: (0, 0))],
        out_specs=pl.BlockSpec((tm, D), lambda i: (i, 0)),
        out_shape=jax.ShapeDtypeStruct((M, D), out_dtype),
        compiler_params=_cparams("parallel"),
    )(h, w.reshape(1, D))


def _final_norm_kernel(x_ref, w_ref, o_ref):
    o_ref[0] = _rms_scale(x_ref[...], w_ref[...])


def _final_norm(h, w, B, T):
    D = h.shape[1]
    Tp = T + HEAD_ROWS
    tt = 512
    assert T % tt == 0
    return pl.pallas_call(
        _final_norm_kernel,
        grid=(B, T // tt),
        in_specs=[pl.BlockSpec((pl.Element(tt), pl.Element(D)),
                               lambda b, j: (pl.multiple_of(b * Tp + HEAD_ROWS + j * tt, CHUNK), 0)),
                  pl.BlockSpec((1, D), lambda b, j: (0, 0))],
        out_specs=pl.BlockSpec((1, tt, D), lambda b, j: (b, j, 0)),
        out_shape=jax.ShapeDtypeStruct((B, T, D), F32),
        compiler_params=_cparams("parallel", "parallel"),
    )(h, w.reshape(1, D))


def _matmul_kernel(x_ref, w_ref, o_ref, *, act):
    acc = _dot(x_ref[...], w_ref[...])
    if act == "gelu":
        acc = _gelu_tanh(acc)
    o_ref[...] = acc.astype(o_ref.dtype)


def _matmul(x, w, out_dtype, act=None):
    M, K = x.shape
    N = w.shape[1]
    tm = _pick_tile(M, 1024, 128)
    tn = _pick_tile(N, 512, 128)
    return pl.pallas_call(
        functools.partial(_matmul_kernel, act=act),
        grid=(M // tm, N // tn),
        in_specs=[pl.BlockSpec((tm, K), lambda i, j: (i, 0)), pl.BlockSpec((K, tn), lambda i, j: (0, j))],
        out_specs=pl.BlockSpec((tm, tn), lambda i, j: (i, j)),
        out_shape=jax.ShapeDtypeStruct((M, N), out_dtype),
        compiler_params=_cparams("parallel", "parallel"),
    )(x, w)


def _proj_res_kernel(x_ref, w_ref, h_ref, nw_ref, hout_ref, hn_ref, *, nk):
    k = pl.program_id(1)

    @pl.when(k == 0)
    def _():
        hout_ref[...] = h_ref[...]

    hout_ref[...] += _dot(x_ref[...], w_ref[...])

    @pl.when(k == nk - 1)
    def _():
        hn_ref[...] = _rms_scale(hout_ref[...], nw_ref[...]).astype(hn_ref.dtype)


def _proj_res(x, w, h, nw):
    M, K = x.shape
    D = w.shape[1]
    tm = _pick_tile(M, 640, 64)
    tk = 1024
    nk = K // tk
    return pl.pallas_call(
        functools.partial(_proj_res_kernel, nk=nk),
        grid=(M // tm, nk),
        in_specs=[pl.BlockSpec((tm, tk), lambda i, k: (i, k)),
                  pl.BlockSpec((tk, D), lambda i, k: (k, 0)),
                  pl.BlockSpec((tm, D), lambda i, k: (i, 0)),
                  pl.BlockSpec((1, D), lambda i, k: (0, 0))],
        out_specs=[pl.BlockSpec((tm, D), lambda i, k: (i, 0)), pl.BlockSpec((tm, D), lambda i, k: (i, 0))],
        out_shape=[jax.ShapeDtypeStruct((M, D), F32), jax.ShapeDtypeStruct((M, D), BF16)],
        compiler_params=_cparams("parallel", "arbitrary"),
    )(x, w, h, nw.reshape(1, D))


def _ffn_kernel(xp_ref, xm_ref, xn_ref, wg_ref, wu_ref, cw_ref, cb_ref, wd_ref, h_ref, nw_ref,
                hout_ref, hn_ref, xext_ref, *, tm, nf, ni):
    i = pl.program_id(0)
    f = pl.program_id(1)
    th = tm // 2

    @pl.when(f == 0)
    def _():
        xext_ref[0:HALO, :] = xp_ref[...]
        xext_ref[HALO:HALO + tm, :] = xm_ref[...]
        xext_ref[HALO + tm:2 * HALO + tm, :] = xn_ref[...]
        hout_ref[...] = h_ref[...]

    cw = cw_ref[...]
    cb = cb_ref[...]
    g, up = [], []
    for s in range(2):
        g.append(_dot(xext_ref[s * th:s * th + th + 2 * HALO, :], wg_ref[0]))
        up.append(_dot(xm_ref[s * th:(s + 1) * th, :], wu_ref[0]))
    row = lax.broadcasted_iota(jnp.int32, (th, 1), 0)
    y = []
    for s in range(2):
        g_prev = g[s][HALO - 1:HALO - 1 + th]
        g_here = g[s][HALO:HALO + th]
        g_next = g[s][HALO + 1:HALO + 1 + th]
        if s == 1:
            g_next = jnp.where(jnp.logical_and(i == ni - 1, row == th - 1), 0.0, g_next)
        a = cw[0:1] * g_prev + cw[1:2] * g_here + cw[2:3] * g_next + cb
        y.append((a * _sigmoid(a) * up[s]).astype(BF16))
    down = [_dot(y[s], wd_ref[...]) for s in range(2)]
    for s in range(2):
        hout_ref[s * th:(s + 1) * th, :] += down[s]

    @pl.when(f == nf - 1)
    def _():
        hn_ref[...] = _rms_scale(hout_ref[...], nw_ref[...]).astype(hn_ref.dtype)


def _conv_ffn(hn, h, w_gate, w_up, conv_w, conv_b, w_down, nw):
    M, D = hn.shape
    F = w_gate.shape[1]
    tm = _pick_tile(M, 640, 64)
    tf = _pick_tile(F, 512, 128)
    ni, nf = M // tm, F // tf
    hb = tm // HALO
    nhb = M // HALO

    def col_tiles(w):
        return w.reshape(D, nf, tf).transpose(1, 0, 2)

    return pl.pallas_call(
        functools.partial(_ffn_kernel, tm=tm, nf=nf, ni=ni),
        grid=(ni, nf),
        in_specs=[pl.BlockSpec((HALO, D), lambda i, f: (jnp.maximum(i * hb - 1, 0), 0)),
                  pl.BlockSpec((tm, D), lambda i, f: (i, 0)),
                  pl.BlockSpec((HALO, D), lambda i, f: (jnp.minimum((i + 1) * hb, nhb - 1), 0)),
                  pl.BlockSpec((1, D, tf), lambda i, f: (f, 0, 0)),
                  pl.BlockSpec((1, D, tf), lambda i, f: (f, 0, 0)),
                  pl.BlockSpec((3, tf), lambda i, f: (0, f)),
                  pl.BlockSpec((1, tf), lambda i, f: (0, f)),
                  pl.BlockSpec((tf, D), lambda i, f: (f, 0)),
                  pl.BlockSpec((tm, D), lambda i, f: (i, 0)),
                  pl.BlockSpec((1, D), lambda i, f: (0, 0))],
        out_specs=[pl.BlockSpec((tm, D), lambda i, f: (i, 0)), pl.BlockSpec((tm, D), lambda i, f: (i, 0))],
        out_shape=[jax.ShapeDtypeStruct((M, D), F32), jax.ShapeDtypeStruct((M, D), BF16)],
        scratch_shapes=[pltpu.VMEM((tm + 2 * HALO, D), BF16)],
        compiler_params=_cparams("parallel", "arbitrary"),
    )(hn, hn, hn, col_tiles(w_gate), col_tiles(w_up), conv_w, conv_b.reshape(1, F), w_down, h, nw.reshape(1, D))


def _scan8(a, u, row, reverse):
    for s in (1, 2, 4):
        if reverse:
            a_sh = pltpu.roll(a, 8 - s, 0)
            u_sh = pltpu.roll(u, 8 - s, 0)
            m = row < 8 - s
        else:
            a_sh = pltpu.roll(a, s, 0)
            u_sh = pltpu.roll(u, s, 0)
            m = row >= s
        u = jnp.where(m, a * u_sh + u, u)
        a = jnp.where(m, a * a_sh, a)
    return a, u


def _rg_kernel(rec_ref, gate_ref, cw_ref, cb_ref, wa_ref, ba_ref, wi_ref, bi_ref, lam_ref, o_ref,
               xc_s, hs_s, *, Tp, rc):
    nch = Tp // rc
    W = RG_BLOCK
    cw = cw_ref[...]
    cb = cb_ref[...]
    row8 = lax.broadcasted_iota(jnp.int32, (8, W), 0)

    def gates(xc, d):
        xb = xc.astype(BF16)
        r = _sigmoid(_dot(xb, wa_ref[d, 0]) + ba_ref[d])
        ig = _sigmoid(_dot(xb, wi_ref[d, 0]) + bi_ref[d])
        log_a = (-RG_C) * r * _softplus(-lam_ref[d])
        a = jnp.exp(log_a)
        u = jnp.sqrt(1.0 - a * a) * (ig * xc)
        return a, u

    def fwd_chunk(c, carry):
        r0 = pl.multiple_of(c * rc, CHUNK)
        top = rec_ref[0, pl.ds(pl.multiple_of(jnp.maximum(r0 - 8, 0), 8), 8), :]
        main = rec_ref[0, pl.ds(r0, rc), :]
        bot = rec_ref[0, pl.ds(pl.multiple_of(jnp.minimum(r0 + rc, Tp - 8), 8), 8), :]
        bot = jnp.where(r0 + rc < Tp, bot, 0.0)
        xw = jnp.concatenate([top, main, bot], axis=0)
        xc = cw[0:1] * xw[7:7 + rc] + cw[1:2] * xw[8:8 + rc] + cw[2:3] * xw[9:9 + rc] + cw[3:4] * xw[10:10 + rc] + cb
        t = r0 + lax.broadcasted_iota(jnp.int32, (rc, 1), 0)
        xc = jnp.where(t >= PAD, xc, 0.0)
        xc_s[pl.ds(r0, rc), :] = xc
        a, u = gates(xc, 0)
        hs = []
        for k in range(rc // 8):
            A, U = _scan8(a[8 * k:8 * k + 8], u[8 * k:8 * k + 8], row8, False)
            h = A * carry + U
            carry = h[7:8]
            hs.append(h)
        hs_s[pl.ds(r0, rc), :] = jnp.concatenate(hs, axis=0)
        return carry

    lax.fori_loop(0, nch, fwd_chunk, jnp.zeros((1, W), F32))

    def bwd_chunk(cc, carry):
        r0 = pl.multiple_of((nch - 1 - cc) * rc, CHUNK)
        xc = xc_s[pl.ds(r0, rc), :]
        a, u = gates(xc, 1)
        hs = [None] * (rc // 8)
        for k in reversed(range(rc // 8)):
            A, U = _scan8(a[8 * k:8 * k + 8], u[8 * k:8 * k + 8], row8, True)
            h = A * carry + U
            carry = h[0:1]
            hs[k] = h
        hsum = hs_s[pl.ds(r0, rc), :] + jnp.concatenate(hs, axis=0)
        o_ref[0, pl.ds(r0, rc), :] = (gate_ref[0, pl.ds(r0, rc), :].astype(F32) * hsum).astype(o_ref.dtype)
        return carry

    lax.fori_loop(0, nch, bwd_chunk, jnp.zeros((1, W), F32))


def _rg_scan(rec, gate, conv_w, conv_b, w_a, b_a, w_i, b_i, lam):
    B, Tp, W = rec.shape
    nb = W // RG_BLOCK
    rc = _pick_tile(Tp, 384, 64)
    blk = lambda b, n: (b, 0, n)
    vec = lambda b, n: (0, n)
    vec3 = lambda b, n: (0, 0, n)
    return pl.pallas_call(
        functools.partial(_rg_kernel, Tp=Tp, rc=rc),
        grid=(B, nb),
        in_specs=[pl.BlockSpec((1, Tp, RG_BLOCK), blk),
                  pl.BlockSpec((1, Tp, RG_BLOCK), blk),
                  pl.BlockSpec((4, RG_BLOCK), vec),
                  pl.BlockSpec((1, RG_BLOCK), vec),
                  pl.BlockSpec((2, 1, RG_BLOCK, RG_BLOCK), lambda b, n: (0, n, 0, 0)),
                  pl.BlockSpec((2, 1, RG_BLOCK), vec3),
                  pl.BlockSpec((2, 1, RG_BLOCK, RG_BLOCK), lambda b, n: (0, n, 0, 0)),
                  pl.BlockSpec((2, 1, RG_BLOCK), vec3),
                  pl.BlockSpec((2, 1, RG_BLOCK), vec3)],
        out_specs=pl.BlockSpec((1, Tp, RG_BLOCK), blk),
        out_shape=jax.ShapeDtypeStruct((B, Tp, W), BF16),
        scratch_shapes=[pltpu.VMEM((Tp, RG_BLOCK), F32), pltpu.VMEM((Tp, RG_BLOCK), F32)],
        compiler_params=_cparams("parallel", "parallel"),
    )(rec, gate, conv_w, conv_b.reshape(1, W), w_a.astype(BF16), b_a.reshape(2, 1, W),
      w_i.astype(BF16), b_i.reshape(2, 1, W), lam.reshape(2, 1, W))


def _rglru_mixer(hn, h, B, Tp, w_in, conv_w, conv_b, w_a, b_a, w_i, b_i, lam, w_out, nw):
    W = D_MODEL
    w_in = w_in.astype(BF16)
    gate = _matmul(hn, w_in[:, :W], BF16, act="gelu")
    rec = _matmul(hn, w_in[:, W:], F32)
    y = _rg_scan(rec.reshape(B, Tp, W), gate.reshape(B, Tp, W), conv_w, conv_b, w_a, b_a, w_i, b_i, lam)
    return _proj_res(y.reshape(B * Tp, W), w_out.astype(BF16), h, nw)


def _na_kernel(q_ref, k_ref, v_ref, bias_ref, mb_ref, o_ref, *, rows):
    scale = NA_HEAD_DIM ** -0.5
    kh = NA_WIN_R
    mb = mb_ref[0]
    km = k_ref[0, PAD:HEAD_ROWS, :]
    vm = v_ref[0, PAD:HEAD_ROWS, :]

    o_ref[0, 0:PAD, :] = jnp.zeros((PAD, NA_HEAD_DIM), o_ref.dtype)
    qm = q_ref[0, PAD:HEAD_ROWS, :]
    s_m = _dot_nt(qm, km) * scale + mb
    p_m = jnp.exp(s_m - jnp.max(s_m, axis=-1, keepdims=True))
    o_m = _dot(p_m.astype(BF16), vm) / jnp.sum(p_m, axis=-1, keepdims=True)
    o_ref[0, PAD:HEAD_ROWS, :] = o_m.astype(o_ref.dtype)

    R = range(NA_ROW_GROUP)

    def group_body(i, _):
        r = [i * NA_ROW_GROUP + c for c in R]
        rs = [jnp.clip(r[c] - NA_WIN_R // 2, 0, rows - kh) for c in R]
        q0 = [pl.multiple_of(HEAD_ROWS + r[c] * GRID_W, GRID_W) for c in R]
        k0 = [pl.multiple_of(HEAD_ROWS + rs[c] * GRID_W, GRID_W) for c in R]
        q_r = [q_ref[0, pl.ds(q0[c], GRID_W), :] for c in R]
        s = [_dot_nt(q_r[c], k_ref[0, pl.ds(k0[c], kh * GRID_W), :]) for c in R]
        s_met = [_dot_nt(q_r[c], km) for c in R]
        p, p_met, l = [], [], []
        for c in R:
            sc = s[c] * scale + bias_ref[0, r[c] - rs[c]]
            sm = s_met[c] * scale + mb
            m = jnp.maximum(jnp.max(sc, axis=-1, keepdims=True), jnp.max(sm, axis=-1, keepdims=True))
            pc = jnp.exp(sc - m)
            pm = jnp.exp(sm - m)
            l.append(jnp.sum(pc, axis=-1, keepdims=True) + jnp.sum(pm, axis=-1, keepdims=True))
            p.append(pc.astype(BF16))
            p_met.append(pm.astype(BF16))
        o = [_dot(p[c], v_ref[0, pl.ds(k0[c], kh * GRID_W), :]) + _dot(p_met[c], vm) for c in R]
        for c in R:
            o_ref[0, pl.ds(q0[c], GRID_W), :] = (o[c] / l[c]).astype(o_ref.dtype)
        return 0

    assert rows % NA_ROW_GROUP == 0
    lax.fori_loop(0, rows // NA_ROW_GROUP, group_body, 0)


def _na_bias_table(rpb):
    kh = NA_WIN_R
    cols = np.arange(GRID_W)
    col_start = np.clip(cols - NA_WIN_C // 2, 0, GRID_W - NA_WIN_C)
    valid = (cols[None, :] >= col_start[:, None]) & (cols[None, :] < col_start[:, None] + NA_WIN_C)
    col_off = np.clip(cols[None, :] - cols[:, None], -(NA_WIN_C - 1), NA_WIN_C - 1) + NA_WIN_C - 1
    H, n_ro, n_co = rpb.shape
    onehot = (col_off[None] == np.arange(n_co)[:, None, None]).astype(np.float32)
    a = jnp.einsum('hrc,cqk->hqrk', rpb.astype(F32), jnp.asarray(onehot), precision=lax.Precision.HIGHEST)
    a = jnp.where(jnp.asarray(valid)[None, :, None, :], a, NEG_INF)
    tabs = [a[:, :, NA_WIN_R - 1 - d:NA_WIN_R - 1 - d + kh, :].reshape(H, GRID_W, kh * GRID_W)
            for d in range(NA_WIN_R)]
    return jnp.stack(tabs, axis=1)


def _na_attention(qkv, bias, meta_bias):
    B, Tp, _ = qkv.shape
    rows = (Tp - HEAD_ROWS) // GRID_W
    assert rows >= NA_WIN_R
    H = NA_HEADS
    return pl.pallas_call(
        functools.partial(_na_kernel, rows=rows),
        grid=(B, H),
        in_specs=[pl.BlockSpec((1, Tp, NA_HEAD_DIM), lambda b, h: (b, 0, h)),
                  pl.BlockSpec((1, Tp, NA_HEAD_DIM), lambda b, h: (b, 0, H + h)),
                  pl.BlockSpec((1, Tp, NA_HEAD_DIM), lambda b, h: (b, 0, 2 * H + h)),
                  pl.BlockSpec((1, NA_WIN_R, GRID_W, NA_WIN_R * GRID_W), lambda b, h: (h, 0, 0, 0)),
                  pl.BlockSpec((1, 1, N_META), lambda b, h: (h, 0, 0))],
        out_specs=pl.BlockSpec((1, Tp, NA_HEAD_DIM), lambda b, h: (b, 0, h)),
        out_shape=jax.ShapeDtypeStruct((B, Tp, H * NA_HEAD_DIM), BF16),
        compiler_params=_cparams("parallel", "parallel"),
    )(qkv, qkv, qkv, bias, meta_bias.astype(F32).reshape(H, 1, N_META))


def _na_mixer(hn, h, B, Tp, w_qkv, bias, meta_bias, w_o, nw):
    qkv = _matmul(hn, w_qkv.astype(BF16), BF16)
    o = _na_attention(qkv.reshape(B, Tp, -1), bias, meta_bias)
    return _proj_res(o.reshape(B * Tp, -1), w_o.astype(BF16), h, nw)


def _gdn_gate_kernel(x_ref, w_ref, al_ref, dt_ref, o_ref):
    y = _dot(x_ref[...], w_ref[...])
    lane = lax.broadcasted_iota(jnp.int32, y.shape, 1)
    g = -jnp.exp(al_ref[...]) * _softplus(y + dt_ref[...])
    o_ref[...] = jnp.where(lane < 2 * GDN_V_HEADS, _sigmoid(y), g)


def _gdn_gates(hn, w_ba, a_log, dt_bias):
    M, K = hn.shape
    N = 4 * GDN_V_HEADS
    tm = _pick_tile(M, 1024, 128)
    zeros = jnp.zeros((2 * GDN_V_HEADS,), F32)
    al = jnp.concatenate([zeros, a_log.astype(F32).reshape(-1)]).reshape(1, N)
    dt = jnp.concatenate([zeros, dt_bias.astype(F32).reshape(-1)]).reshape(1, N)
    return pl.pallas_call(
        _gdn_gate_kernel,
        grid=(M // tm,),
        in_specs=[pl.BlockSpec((tm, K), lambda i: (i, 0)), pl.BlockSpec((K, N), lambda i: (0, 0)),
                  pl.BlockSpec((1, N), lambda i: (0, 0)), pl.BlockSpec((1, N), lambda i: (0, 0))],
        out_specs=pl.BlockSpec((tm, N), lambda i: (i, 0)),
        out_shape=jax.ShapeDtypeStruct((M, N), F32),
        compiler_params=_cparams("parallel"),
    )(hn, w_ba, al, dt)


def _gdn_conv_kernel(x_ref, cw_ref, o_ref, *, Tp, rc, n_qk_blocks):
    j = pl.program_id(1)
    cw = cw_ref[...]
    nch = Tp // rc

    def chunk(c, _):
        r0 = pl.multiple_of(c * rc, CHUNK)
        top = x_ref[0, pl.ds(pl.multiple_of(jnp.maximum(r0 - 8, 0), 8), 8), :]
        main = x_ref[0, pl.ds(r0, rc), :]
        bot = x_ref[0, pl.ds(pl.multiple_of(jnp.minimum(r0 + rc, Tp - 8), 8), 8), :]
        bot = jnp.where(r0 + rc < Tp, bot, 0.0)
        xw = jnp.concatenate([top, main, bot], axis=0)
        a = cw[0:1] * xw[7:7 + rc] + cw[1:2] * xw[8:8 + rc] + cw[2:3] * xw[9:9 + rc] + cw[3:4] * xw[10:10 + rc]
        y = a * _sigmoid(a)
        t = r0 + lax.broadcasted_iota(jnp.int32, (rc, 1), 0)
        y = jnp.where(t >= PAD, y, 0.0)
        nrm = lax.rsqrt(jnp.sum(y * y, axis=-1, keepdims=True) + 1e-6)
        y = y * jnp.where(j < n_qk_blocks, nrm, 1.0)
        o_ref[0, pl.ds(r0, rc), :] = y.astype(o_ref.dtype)
        return 0

    lax.fori_loop(0, nch, chunk, 0)


def _gdn_conv(proj3, conv_w):
    B, Tp, _ = proj3.shape
    nblk = GDN_CONV_DIM // GDN_DK
    rc = _pick_tile(Tp, 384, 64)
    return pl.pallas_call(
        functools.partial(_gdn_conv_kernel, Tp=Tp, rc=rc, n_qk_blocks=2 * GDN_QK_HEADS),
        grid=(B, nblk),
        in_specs=[pl.BlockSpec((1, Tp, GDN_DK), lambda b, j: (b, 0, j)),
                  pl.BlockSpec((4, GDN_DK), lambda b, j: (0, j))],
        out_specs=pl.BlockSpec((1, Tp, GDN_DK), lambda b, j: (b, 0, j)),
        out_shape=jax.ShapeDtypeStruct((B, Tp, GDN_CONV_DIM), BF16),
        compiler_params=_cparams("parallel", "parallel"),
    )(proj3, conv_w)


def _split3_dot(x, m):
    hi = x.astype(BF16)
    r1 = x - hi.astype(F32)
    mid = r1.astype(BF16)
    lo = (r1 - mid.astype(F32)).astype(BF16)
    return _dot(hi, m) + _dot(mid, m) + _dot(lo, m)


def _gdn_kernel_unused(q_ref, k_ref, v_ref, z_ref, bgc_ref, bgr_ref, nw_ref, o_ref, S_s, oacc_s, *, Tp):
    kh = pl.program_id(1)
    C = CHUNK
    NC = Tp // C
    HV = GDN_V_HEADS
    DK = GDN_DK
    scale = DK ** -0.5

    oacc_s[...] = jnp.zeros_like(oacc_s)
    S_s[...] = jnp.zeros_like(S_s)

    ri = lax.broadcasted_iota(jnp.int32, (C, C), 0)
    ci = lax.broadcasted_iota(jnp.int32, (C, C), 1)
    lower = ri >= ci
    upper = ri <= ci
    eye = (ri == ci).astype(F32)
    ones_lower = lower.astype(BF16)
    ones_upper = upper.astype(BF16)
    lane = lax.broadcasted_iota(jnp.int32, (C, 4 * HV), 1)
    rowi = lax.broadcasted_iota(jnp.int32, (C, 4 * HV), 0)

    def col(x, c):
        return jnp.sum(jnp.where(lane == c, x, 0.0), axis=1, keepdims=True)

    def one_chunk(n, reverse):
        r0 = pl.multiple_of(n * C, C)
        q = q_ref[0, pl.ds(r0, C), :]
        k = k_ref[0, pl.ds(r0, C), :]
        kf = k.astype(F32)
        bg = bgc_ref[0, pl.ds(r0, C), :]
        pre = bg
        for s in (1, 2, 4, 8, 16, 32):
            pre = pre + jnp.where(rowi >= s, pltpu.roll(pre, s, 0), 0.0)
        if reverse:
            cum = pre[C - 1:C] - pre + bg
        else:
            cum = pre
        rows = bgr_ref[0, n, 0]
        cum_rows = _split3_dot(rows, ones_lower if reverse else ones_upper)
        kk = _dot_nt(k, k)
        qk = _dot_nt(q, k) * scale
        incl = upper if reverse else lower
        strict = (ri < ci) if reverse else (ri > ci)
        d = 1 if reverse else 0
        for j in range(2):
            inst = 2 * d + j
            vh = 2 * kh + j
            beta_c = col(bg, d * HV + vh)
            gc_c = col(cum, 2 * HV + d * HV + vh)
            gc_r = cum_rows[4 + 2 * d + j:5 + 2 * d + j]
            decay = jnp.where(incl, jnp.exp(jnp.where(incl, gc_c - gc_r, 0.0)), 0.0)
            L = jnp.where(strict, kk * decay, 0.0) * beta_c
            Tm = eye - L
            P = L
            for _ in range(5):
                Pb = P.astype(BF16)
                P = _dot(Pb, Pb)
                Tm = Tm + _dot(Tm.astype(BF16), P.astype(BF16))
            Tb = Tm.astype(BF16)
            v = v_ref[0, pl.ds(r0, C), j * DK:(j + 1) * DK].astype(F32)
            e_c = jnp.exp(gc_c)
            u = _dot(Tb, (v * beta_c).astype(BF16))
            w = _dot(Tb, (kf * (beta_c * e_c)).astype(BF16))
            qkd = jnp.where(incl, qk * decay, 0.0)
            g_last = gc_c[0:1] if reverse else gc_c[C - 1:C]
            k_st = kf * jnp.exp(g_last - gc_c)
            q_st = q.astype(F32) * (scale * e_c)
            S = S_s[inst]
            Sb = S.astype(BF16)
            v_new = u - _dot(w.astype(BF16), Sb)
            vb = v_new.astype(BF16)
            o = _dot(q_st.astype(BF16), Sb) + _dot(qkd.astype(BF16), vb)
            S_s[inst] = S * jnp.exp(g_last) + _dot_tn(k_st.astype(BF16), vb)
            oacc_s[pl.ds(r0, C), j * DK:(j + 1) * DK] += o

    def body(it, _):
        one_chunk(it, False)
        one_chunk(NC - 1 - it, True)
        return 0

    lax.fori_loop(0, NC, body, 0)

    nw = nw_ref[...]

    def fin(n, _):
        r0 = pl.multiple_of(n * C, C)
        for j in range(2):
            o = oacc_s[pl.ds(r0, C), j * DK:(j + 1) * DK]
            z = z_ref[0, pl.ds(r0, C), j * DK:(j + 1) * DK]
            y = o * lax.rsqrt(jnp.mean(o * o, axis=-1, keepdims=True) + NORM_EPS) * nw * (z * _sigmoid(z))
            o_ref[0, pl.ds(r0, C), j * DK:(j + 1) * DK] = y.astype(o_ref.dtype)
        return 0

    lax.fori_loop(0, NC, fin, 0)


def _gdn_delta_unused(qkv, proj3, bg, norm_w):
    B, Tp, _ = qkv.shape
    NC = Tp // CHUNK
    KH = GDN_QK_HEADS
    HV = GDN_V_HEADS
    bgr = bg.reshape(B, NC, CHUNK, 4, KH, 2).transpose(0, 1, 4, 3, 5, 2).reshape(B, NC, KH, 8, CHUNK)
    zoff = GDN_CONV_DIM // (2 * GDN_DK)
    return pl.pallas_call(
        functools.partial(_gdn_kernel, Tp=Tp),
        grid=(B, KH),
        in_specs=[pl.BlockSpec((1, Tp, GDN_DK), lambda b, h: (b, 0, h)),
                  pl.BlockSpec((1, Tp, GDN_DK), lambda b, h: (b, 0, KH + h)),
                  pl.BlockSpec((1, Tp, 2 * GDN_DK), lambda b, h: (b, 0, KH + h)),
                  pl.BlockSpec((1, Tp, 2 * GDN_DK), lambda b, h: (b, 0, zoff + h)),
                  pl.BlockSpec((1, Tp, 4 * HV), lambda b, h: (b, 0, 0)),
                  pl.BlockSpec((1, NC, 1, 8, CHUNK), lambda b, h: (b, 0, h, 0, 0)),
                  pl.BlockSpec((1, GDN_DK), lambda b, h: (0, 0))],
        out_specs=pl.BlockSpec((1, Tp, 2 * GDN_DK), lambda b, h: (b, 0, h)),
        out_shape=jax.ShapeDtypeStruct((B, Tp, GDN_VAL_DIM), BF16),
        scratch_shapes=[pltpu.VMEM((4, GDN_DK, GDN_DK), F32), pltpu.VMEM((Tp, 2 * GDN_DK), F32)],
        compiler_params=_cparams("parallel", "parallel"),
    )(qkv, qkv, qkv, proj3, bg, bgr, norm_w.astype(F32).reshape(1, GDN_DK))


def _gdn_kernel(q_ref, k_ref, v_ref, z_ref, bgc_ref, bgr_ref, nw_ref, o_ref,
                S_s, oacc_s, u_s, wq_s, kst_s, qkd_s, gam_s, *, Tp):
    kh = pl.program_id(1)
    C = CHUNK
    NC = Tp // C
    HV = GDN_V_HEADS
    DK = GDN_DK
    G = 4
    W = G * C
    P1_GROUP = 6
    scale = DK ** -0.5

    oacc_s[...] = jnp.zeros_like(oacc_s)
    S_s[...] = jnp.zeros_like(S_s)

    ri = lax.broadcasted_iota(jnp.int32, (C, W), 0)
    cl = lax.broadcasted_iota(jnp.int32, (C, W), 1)
    cj = jnp.bitwise_and(cl, C - 1)
    blk = jnp.right_shift(cl, 6)
    dd = jnp.where(blk >= 2, cj - ri, ri - cj)
    incl4 = dd >= 0
    strict4 = dd > 0
    eye4 = (ri == cj).astype(F32)
    cum4 = jnp.where(dd <= 0, 1.0, 0.0).astype(BF16)
    bdm = jnp.where(jnp.right_shift(lax.broadcasted_iota(jnp.int32, (W, W), 0), 6)
                    == jnp.right_shift(lax.broadcasted_iota(jnp.int32, (W, W), 1), 6), 1.0, 0.0).astype(BF16)
    blk_row = jnp.right_shift(lax.broadcasted_iota(jnp.int32, (1, W), 1), 6)
    lane = lax.broadcasted_iota(jnp.int32, (C, 4 * HV), 1)
    rowi = lax.broadcasted_iota(jnp.int32, (C, 4 * HV), 0)

    def col(x, c):
        return jnp.sum(jnp.where(lane == c, x, 0.0), axis=1, keepdims=True)

    def by_block(parts, b):
        return jnp.where(b == 0, parts[0], jnp.where(b == 1, parts[1], jnp.where(b == 2, parts[2], parts[3])))

    def blockdiag(xb):
        return jnp.concatenate([xb] * G, axis=0) * bdm

    def phase1(ns):
        K = range(len(ns))
        r0 = [pl.multiple_of(n * C, C) for n in ns]
        q = [q_ref[0, pl.ds(r, C), :] for r in r0]
        k = [k_ref[0, pl.ds(r, C), :] for r in r0]
        bg = [bgc_ref[0, pl.ds(r, C), :] for r in r0]
        v = [v_ref[0, pl.ds(r, C), :].astype(F32) for r in r0]
        rows = [bgr_ref[0, n, 0] for n in ns]
        cum_rows = [_split3_dot(rows[i], cum4) for i in K]
        k4 = [jnp.concatenate([k[i]] * G, axis=0) for i in K]
        kk4 = [_dot_nt(k[i], k4[i]) for i in K]
        qk4 = [_dot_nt(q[i], k4[i]) * scale for i in K]
        beta_c, gc_c, decay4, L = [], [], [], []
        for i in K:
            pre = bg[i]
            for s in (1, 2, 4, 8, 16, 32):
                pre = pre + jnp.where(rowi >= s, pltpu.roll(pre, s, 0), 0.0)
            suf = pre[C - 1:C] - pre + bg[i]
            gc_r = by_block([cum_rows[i][4 + g:5 + g] for g in range(G)], blk_row)
            bc, gc = [], []
            for g in range(G):
                d, j = g // 2, g % 2
                vh = 2 * kh + j
                bc.append(col(bg[i], d * HV + vh))
                gc.append(col(suf if d else pre, 2 * HV + d * HV + vh))
            beta_c.append(bc)
            gc_c.append(gc)
            dec = jnp.where(incl4, jnp.exp(jnp.where(incl4, by_block(gc, blk) - gc_r, 0.0)), 0.0)
            decay4.append(dec)
            L.append(jnp.where(strict4, kk4[i] * dec, 0.0) * by_block(bc, blk))
        Tm = [eye4 - L[i] for i in K]
        Lb = [L[i].astype(BF16) for i in K]
        P = [_dot(Lb[i], blockdiag(Lb[i])) for i in K]
        for lvl in range(5):
            Pb = [P[i].astype(BF16) for i in K]
            bd = [blockdiag(Pb[i]) for i in K]
            if lvl < 4:
                tp = [_dot(jnp.concatenate([Tm[i].astype(BF16), Pb[i]], axis=0), bd[i]) for i in K]
                Tm = [Tm[i] + tp[i][:C] for i in K]
                P = [tp[i][C:] for i in K]
            else:
                tp = [_dot(Tm[i].astype(BF16), bd[i]) for i in K]
                Tm = [Tm[i] + tp[i] for i in K]
        rhs, kst, qst, gam = [], [], [], []
        for i in K:
            qf = q[i].astype(F32)
            kf = k[i].astype(F32)
            rhs_i, kst_i, qst_i, gam_i = [], [], [], []
            for g in range(G):
                j = g % 2
                bc, gc = beta_c[i][g], gc_c[i][g]
                e_c = jnp.exp(gc)
                rhs_i.append(jnp.concatenate([v[i][:, j * DK:(j + 1) * DK] * bc, kf * (bc * e_c)], axis=1))
                g_last = gc[0:1] if g >= 2 else gc[C - 1:C]
                kst_i.append((kf * jnp.exp(g_last - gc)).astype(BF16))
                qst_i.append((qf * (scale * e_c)).astype(BF16))
                gam_i.append(jnp.broadcast_to(jnp.exp(g_last), (1, DK)))
            rhs.append(jnp.concatenate(rhs_i, axis=0).astype(BF16))
            kst.append(kst_i)
            qst.append(qst_i)
            gam.append(gam_i)
        sol = [_dot(blockdiag(Tm[i].astype(BF16)), rhs[i]) for i in K]
        for i in K:
            n = ns[i]
            u_s[n] = sol[i][:, :DK]
            qkd_s[n] = jnp.where(incl4, qk4[i] * decay4[i], 0.0).astype(BF16)
            for g in range(G):
                wq_s[n, g, 0:C, :] = sol[i][g * C:(g + 1) * C, DK:].astype(BF16)
                wq_s[n, g, C:2 * C, :] = qst[i][g]
                kst_s[n, g] = kst[i][g]
                gam_s[n, g:g + 1, :] = gam[i][g]

    def phase1_group(i, _):
        phase1([P1_GROUP * i + c for c in range(P1_GROUP)])
        return 0

    lax.fori_loop(0, NC // P1_GROUP, phase1_group, 0)
    if NC % P1_GROUP:
        phase1(list(range(NC - NC % P1_GROUP, NC)))

    nw = nw_ref[...]

    def gated_norm(o, z):
        ys = []
        for j in range(2):
            oj = o[:, j * DK:(j + 1) * DK]
            zj = z[:, j * DK:(j + 1) * DK].astype(F32)
            ys.append(oj * lax.rsqrt(jnp.mean(oj * oj, axis=-1, keepdims=True) + NORM_EPS) * nw * (zj * _sigmoid(zj)))
        return jnp.concatenate(ys, axis=1)

    def phase2(it, finalize):
        ops = []
        for g in range(G):
            n = it if g < 2 else NC - 1 - it
            ops.append((S_s[g], wq_s[n, g], u_s[n, g * C:(g + 1) * C, :], qkd_s[n, :, g * C:(g + 1) * C],
                        gam_s[n, g:g + 1, :], kst_s[n, g]))
        r_f = pl.multiple_of(it * C, C)
        r_b = pl.multiple_of((NC - 1 - it) * C, C)
        o_f = oacc_s[pl.ds(r_f, C), :]
        ws = [_dot(wq, S.astype(BF16)) for (S, wq, u, qkd, gam, kst) in ops]
        vb = [(ops[g][2] - ws[g][:C]).astype(BF16) for g in range(G)]
        outs = [ws[g][C:] + _dot(ops[g][3], vb[g]) for g in range(G)]
        states = [ops[g][0] * ops[g][4] + _dot_tn(ops[g][5], vb[g]) for g in range(G)]
        for g in range(G):
            S_s[g] = states[g]
        o_f = o_f + jnp.concatenate(outs[0:2], axis=1)
        oacc_s[pl.ds(r_f, C), :] = o_f
        o_b = oacc_s[pl.ds(r_b, C), :] + jnp.concatenate(outs[2:4], axis=1)
        oacc_s[pl.ds(r_b, C), :] = o_b
        if finalize:
            o_ref[0, pl.ds(r_f, C), :] = gated_norm(o_f, z_ref[0, pl.ds(r_f, C), :]).astype(o_ref.dtype)
            o_ref[0, pl.ds(r_b, C), :] = gated_norm(o_b, z_ref[0, pl.ds(r_b, C), :]).astype(o_ref.dtype)
        return 0

    lax.fori_loop(0, NC // 2, lambda it, c: phase2(it, False), 0)
    lax.fori_loop(NC // 2, NC, lambda it, c: phase2(it, True), 0)


def _gdn_delta(qkv, z, bg, norm_w):
    B, Tp, _ = qkv.shape
    NC = Tp // CHUNK
    KH = GDN_QK_HEADS
    HV = GDN_V_HEADS
    C = CHUNK
    bgr = bg.reshape(B, NC, CHUNK, 4, KH, 2).transpose(0, 1, 4, 3, 5, 2).reshape(B, NC, KH, 8, CHUNK)
    return pl.pallas_call(
        functools.partial(_gdn_kernel, Tp=Tp),
        grid=(B, KH),
        in_specs=[pl.BlockSpec((1, Tp, GDN_DK), lambda b, h: (b, 0, h)),
                  pl.BlockSpec((1, Tp, GDN_DK), lambda b, h: (b, 0, KH + h)),
                  pl.BlockSpec((1, Tp, 2 * GDN_DK), lambda b, h: (b, 0, KH + h)),
                  pl.BlockSpec((1, Tp, 2 * GDN_DK), lambda b, h: (b, 0, h)),
                  pl.BlockSpec((1, Tp, 4 * HV), lambda b, h: (b, 0, 0)),
                  pl.BlockSpec((1, NC, 1, 8, CHUNK), lambda b, h: (b, 0, h, 0, 0)),
                  pl.BlockSpec((1, GDN_DK), lambda b, h: (0, 0))],
        out_specs=pl.BlockSpec((1, Tp, 2 * GDN_DK), lambda b, h: (b, 0, h)),
        out_shape=jax.ShapeDtypeStruct((B, Tp, GDN_VAL_DIM), BF16),
        scratch_shapes=[pltpu.VMEM((4, GDN_DK, GDN_DK), F32),
                        pltpu.VMEM((Tp, 2 * GDN_DK), F32),
                        pltpu.VMEM((NC, 4 * C, GDN_DK), F32),
                        pltpu.VMEM((NC, 4, 2 * C, GDN_DK), BF16),
                        pltpu.VMEM((NC, 4, C, GDN_DK), BF16),
                        pltpu.VMEM((NC, C, 4 * C), BF16),
                        pltpu.VMEM((NC, 8, GDN_DK), F32)],
        compiler_params=_cparams("parallel", "parallel"),
    )(qkv, qkv, qkv, z, bg, bgr, norm_w.astype(F32).reshape(1, GDN_DK))


def _gdn_mixer(hn, h, B, Tp, w_in, conv_w, a_log, dt_bias, norm_w, w_out, nw):
    nz = GDN_CONV_DIM + GDN_VAL_DIM
    w_in = w_in.astype(BF16)
    proj = _matmul(hn, w_in[:, :GDN_CONV_DIM], F32)
    z = _matmul(hn, w_in[:, GDN_CONV_DIM:nz], BF16)
    bg = _gdn_gates(hn, w_in[:, nz:], a_log, dt_bias)
    qkv = _gdn_conv(proj.reshape(B, Tp, GDN_CONV_DIM), conv_w)
    o = _gdn_delta(qkv, z.reshape(B, Tp, -1), bg.reshape(B, Tp, -1), norm_w)
    return _proj_res(o.reshape(B * Tp, -1), w_out.astype(BF16), h, nw)


def _trunk(x, meta_tokens, mix_norm, ffn_norm, final_norm, rg, na, gdn, ffn):
    B, T, D = x.shape
    Tp = T + HEAD_ROWS
    depth = mix_norm.shape[0]
    h3 = jnp.concatenate([jnp.zeros((B, PAD, D), F32),
                          jnp.broadcast_to(meta_tokens.astype(F32)[None], (B, N_META, D)),
                          x.astype(F32)], axis=1)
    h = h3.reshape(B * Tp, D)
    hn = _rmsnorm(h, mix_norm[0], BF16)
    for i in range(depth):
        kind, j = i % 3, i // 3
        if kind == 0:
            h, hn = _rglru_mixer(hn, h, B, Tp, *[p[j] for p in rg], ffn_norm[i])
        elif kind == 1:
            h, hn = _na_mixer(hn, h, B, Tp, *[p[j] for p in na], ffn_norm[i])
        else:
            h, hn = _gdn_mixer(hn, h, B, Tp, *[p[j] for p in gdn], ffn_norm[i])
        w_gate, w_up, conv_w, conv_b, w_down = [p[i] for p in ffn]
        next_norm = mix_norm[i + 1] if i + 1 < depth else final_norm
        h, hn = _conv_ffn(hn, h, w_gate.astype(BF16), w_up.astype(BF16), conv_w, conv_b,
                          w_down.astype(BF16), next_norm)
    return _final_norm(h, final_norm, B, T)


def kernel(x_prompt, x_sample, meta_tokens, mix_norm, ffn_norm, final_norm, rg_w_in, rg_conv_w, rg_conv_b, rg_w_a, rg_b_a, rg_w_i, rg_b_i, rg_lam, rg_w_out, na_w_qkv, na_rpb, na_meta_bias, na_w_o, gdn_w_in, gdn_conv_w, gdn_a_log, gdn_dt_bias, gdn_norm_w, gdn_w_out, ffn_w_gate, ffn_w_up, ffn_conv_w, ffn_conv_b, ffn_w_down):
    rg = (rg_w_in, rg_conv_w, rg_conv_b, rg_w_a, rg_b_a, rg_w_i, rg_b_i, rg_lam, rg_w_out)
    na_bias = jnp.stack([_na_bias_table(na_rpb[j]) for j in range(na_rpb.shape[0])])
    na = (na_w_qkv, na_bias, na_meta_bias, na_w_o)
    gdn = (gdn_w_in, gdn_conv_w, gdn_a_log, gdn_dt_bias, gdn_norm_w, gdn_w_out)
    ffn = (ffn_w_gate, ffn_w_up, ffn_conv_w, ffn_conv_b, ffn_w_down)
    y_prompt = _trunk(x_prompt, meta_tokens, mix_norm, ffn_norm, final_norm, rg, na, gdn, ffn)
    y_sample = _trunk(x_sample, meta_tokens, mix_norm, ffn_norm, final_norm, rg, na, gdn, ffn)
    return (y_prompt, y_sample)
```

```python
import functools

import numpy as np
import jax
import jax.numpy as jnp
from jax import lax
from jax.experimental import pallas as pl
from jax.experimental.pallas import tpu as pltpu

D_MODEL = 2048
N_META = 16
PAD = 48
HEAD_ROWS = PAD + N_META
CHUNK = 64
GRID_W = 64
NORM_EPS = 1e-6

RG_BLOCK = 256
RG_C = 8.0

NA_HEAD_DIM = 128
NA_HEADS = D_MODEL // NA_HEAD_DIM
NA_WIN_R = 8
NA_WIN_C = 16
NA_ROW_GROUP = 8
NEG_INF = -1e30

GDN_DK = 128
GDN_QK_HEADS = D_MODEL // GDN_DK
GDN_V_HEADS = 2 * GDN_QK_HEADS
GDN_KEY_DIM = GDN_QK_HEADS * GDN_DK
GDN_VAL_DIM = GDN_V_HEADS * GDN_DK
GDN_CONV_DIM = 2 * GDN_KEY_DIM + GDN_VAL_DIM

V7X_VMEM_LIMIT = 56 * 1024 * 1024
HALO = 16

BF16 = jnp.bfloat16
F32 = jnp.float32


def _cparams(*sem):
    return pltpu.CompilerParams(dimension_semantics=sem, vmem_limit_bytes=V7X_VMEM_LIMIT)


def _pick_tile(n, cap, mult):
    best = None
    for t in range(mult, cap + 1, mult):
        if n % t == 0:
            best = t
    assert best is not None, (n, cap, mult)
    return best


def _sigmoid(x):
    return 1.0 / (1.0 + jnp.exp(-x))


def _softplus(x):
    return jnp.maximum(x, 0.0) + jnp.log(1.0 + jnp.exp(-jnp.abs(x)))


def _gelu_tanh(x):
    return 0.5 * x * (1.0 + jnp.tanh(0.7978845608028654 * (x + 0.044715 * x * x * x)))


def _rms_scale(x, w):
    ms = jnp.mean(x * x, axis=-1, keepdims=True)
    return x * lax.rsqrt(ms + NORM_EPS) * w


def _dot(a, b):
    return jnp.dot(a, b, preferred_element_type=F32)


def _dot_nt(a, b):
    return lax.dot_general(a, b, (((1,), (1,)), ((), ())), preferred_element_type=F32)


def _dot_tn(a, b):
    return lax.dot_general(a, b, (((0,), (0,)), ((), ())), preferred_element_type=F32)


def _rmsnorm_kernel(x_ref, w_ref, o_ref):
    o_ref[...] = _rms_scale(x_ref[...], w_ref[...]).astype(o_ref.dtype)


def _rmsnorm(h, w, out_dtype):
    M, D = h.shape
    tm = _pick_tile(M, 1024, 64)
    return pl.pallas_call(
        _rmsnorm_kernel,
        grid=(M // tm,),
        in_specs=[pl.BlockSpec((tm, D), lambda i: (i, 0)), pl.BlockSpec((1, D), lambda i: (0, 0))],
        out_specs=pl.BlockSpec((tm, D), lambda i: (i, 0)),
        out_shape=jax.ShapeDtypeStruct((M, D), out_dtype),
        compiler_params=_cparams("parallel"),
    )(h, w.reshape(1, D))


def _final_norm_kernel(x_ref, w_ref, o_ref):
    o_ref[0] = _rms_scale(x_ref[...], w_ref[...])


def _final_norm(h, w, B, T):
    D = h.shape[1]
    Tp = T + HEAD_ROWS
    tt = 512
    assert T % tt == 0
    return pl.pallas_call(
        _final_norm_kernel,
        grid=(B, T // tt),
        in_specs=[pl.BlockSpec((pl.Element(tt), pl.Element(D)),
                               lambda b, j: (pl.multiple_of(b * Tp + HEAD_ROWS + j * tt, CHUNK), 0)),
                  pl.BlockSpec((1, D), lambda b, j: (0, 0))],
        out_specs=pl.BlockSpec((1, tt, D), lambda b, j: (b, j, 0)),
        out_shape=jax.ShapeDtypeStruct((B, T, D), F32),
        compiler_params=_cparams("parallel", "parallel"),
    )(h, w.reshape(1, D))


def _matmul_kernel(x_ref, w_ref, o_ref, *, act):
    acc = _dot(x_ref[...], w_ref[...])
    if act == "gelu":
        acc = _gelu_tanh(acc)
    o_ref[...] = acc.astype(o_ref.dtype)


def _matmul(x, w, out_dtype, act=None):
    M, K = x.shape
    N = w.shape[1]
    tm = _pick_tile(M, 1024, 128)
    tn = _pick_tile(N, 512, 128)
    return pl.pallas_call(
        functools.partial(_matmul_kernel, act=act),
        grid=(M // tm, N // tn),
        in_specs=[pl.BlockSpec((tm, K), lambda i, j: (i, 0)), pl.BlockSpec((K, tn), lambda i, j: (0, j))],
        out_specs=pl.BlockSpec((tm, tn), lambda i, j: (i, j)),
        out_shape=jax.ShapeDtypeStruct((M, N), out_dtype),
        compiler_params=_cparams("parallel", "parallel"),
    )(x, w)


def _proj_res_kernel(x_ref, w_ref, h_ref, nw_ref, hout_ref, hn_ref):
    hnew = h_ref[...] + _dot(x_ref[...], w_ref[...])
    hout_ref[...] = hnew
    hn_ref[...] = _rms_scale(hnew, nw_ref[...]).astype(hn_ref.dtype)


def _proj_res(x, w, h, nw):
    M, K = x.shape
    D = w.shape[1]
    tm = _pick_tile(M, 640 if K <= D_MODEL else 512, 64)
    return pl.pallas_call(
        _proj_res_kernel,
        grid=(M // tm,),
        in_specs=[pl.BlockSpec((tm, K), lambda i: (i, 0)),
                  pl.BlockSpec((K, D), lambda i: (0, 0), pipeline_mode=pl.Buffered(1)),
                  pl.BlockSpec((tm, D), lambda i: (i, 0)),
                  pl.BlockSpec((1, D), lambda i: (0, 0))],
        out_specs=[pl.BlockSpec((tm, D), lambda i: (i, 0)), pl.BlockSpec((tm, D), lambda i: (i, 0))],
        out_shape=[jax.ShapeDtypeStruct((M, D), F32), jax.ShapeDtypeStruct((M, D), BF16)],
        compiler_params=_cparams("parallel"),
    )(x, w, h, nw.reshape(1, D))


def _proj_conv_kernel(xp_ref, xm_ref, xn_ref, w_ref, cw_ref, o_ref, xext_ref, *, tm, tc, ni, Tp, n_norm):
    i = pl.program_id(0)
    j = pl.program_id(1)
    npc = o_ref.shape[1] // tc

    @pl.when(j == 0)
    def _():
        xext_ref[0:HALO, :] = xp_ref[...]
        xext_ref[HALO:HALO + tm, :] = xm_ref[...]
        xext_ref[HALO + tm:2 * HALO + tm, :] = xn_ref[...]

    @pl.when(jnp.logical_and(j == 0, i == ni - 1))
    def _():
        xext_ref[HALO + tm:2 * HALO + tm, :] = jnp.zeros((HALO, xext_ref.shape[1]), xext_ref.dtype)

    rel = i * tm - (i * tm // Tp) * Tp + lax.broadcasted_iota(jnp.int32, (tm, 1), 0)
    live = jnp.where(rel >= Tp, rel - Tp, rel) >= PAD
    cw = cw_ref[...]
    xe = xext_ref[...]

    def piece_matmul(c):
        return _dot(xe, w_ref[:, c * tc:(c + 1) * tc])

    def piece_finish(c, g):
        cwc = cw[:, c * tc:(c + 1) * tc]
        a = (cwc[0:1] * g[HALO - 1:HALO - 1 + tm] + cwc[1:2] * g[HALO:HALO + tm]
             + cwc[2:3] * g[HALO + 1:HALO + 1 + tm] + cwc[3:4] * g[HALO + 2:HALO + 2 + tm])
        y = jnp.where(live, a * _sigmoid(a), 0.0)
        normalise = j * npc + c < n_norm
        heads = []
        for hh in range(tc // GDN_DK):
            yh = y[:, hh * GDN_DK:(hh + 1) * GDN_DK]
            nrm = lax.rsqrt(jnp.sum(yh * yh, axis=-1, keepdims=True) + 1e-6)
            heads.append(yh * jnp.where(normalise, nrm, 1.0))
        o_ref[:, c * tc:(c + 1) * tc] = jnp.concatenate(heads, axis=1).astype(o_ref.dtype)

    g = piece_matmul(0)
    for c in range(npc):
        g_next = piece_matmul(c + 1) if c + 1 < npc else None
        piece_finish(c, g)
        g = g_next


def _proj_conv(x, w, conv_w, Tp, n_norm_cols):
    M, K = x.shape
    N = w.shape[1]
    tm = _pick_tile(M, 1024, 64)
    assert tm <= Tp
    tn, tc = 2048, 512
    ni = M // tm
    hb = tm // HALO
    nhb = M // HALO
    return pl.pallas_call(
        functools.partial(_proj_conv_kernel, tm=tm, tc=tc, ni=ni, Tp=Tp, n_norm=n_norm_cols // tc),
        grid=(ni, N // tn),
        in_specs=[pl.BlockSpec((HALO, K), lambda i, j: (jnp.maximum(i * hb - 1, 0), 0)),
                  pl.BlockSpec((tm, K), lambda i, j: (i, 0)),
                  pl.BlockSpec((HALO, K), lambda i, j: (jnp.minimum((i + 1) * hb, nhb - 1), 0)),
                  pl.BlockSpec((K, tn), lambda i, j: (0, j)),
                  pl.BlockSpec((4, tn), lambda i, j: (0, j))],
        out_specs=pl.BlockSpec((tm, tn), lambda i, j: (i, j)),
        out_shape=jax.ShapeDtypeStruct((M, N), BF16),
        scratch_shapes=[pltpu.VMEM((tm + 2 * HALO, K), BF16)],
        compiler_params=_cparams("parallel", "arbitrary"),
    )(x, x, x, w, conv_w)


def _ffn_kernel(xp_ref, xm_ref, xn_ref, wg_ref, wu_ref, cw_ref, cb_ref, wd_ref, h_ref, nw_ref,
                hout_ref, hn_ref, xext_ref, *, tm, nf, ni):
    i = pl.program_id(0)
    f = pl.program_id(1)
    th = tm // 2

    @pl.when(f == 0)
    def _():
        xext_ref[0:HALO, :] = xp_ref[...]
        xext_ref[HALO:HALO + tm, :] = xm_ref[...]
        xext_ref[HALO + tm:2 * HALO + tm, :] = xn_ref[...]
        hout_ref[...] = h_ref[...]

    cw = cw_ref[...]
    cb = cb_ref[...]
    g, up = [], []
    for s in range(2):
        g.append(_dot(xext_ref[s * th:s * th + th + 2 * HALO, :], wg_ref[...]))
        up.append(_dot(xm_ref[s * th:(s + 1) * th, :], wu_ref[...]))
    row = lax.broadcasted_iota(jnp.int32, (th, 1), 0)
    y = []
    for s in range(2):
        g_prev = g[s][HALO - 1:HALO - 1 + th]
        g_here = g[s][HALO:HALO + th]
        g_next = g[s][HALO + 1:HALO + 1 + th]
        if s == 1:
            g_next = jnp.where(jnp.logical_and(i == ni - 1, row == th - 1), 0.0, g_next)
        a = cw[0:1] * g_prev + cw[1:2] * g_here + cw[2:3] * g_next + cb
        y.append((a * _sigmoid(a) * up[s]).astype(BF16))
    down = [_dot(y[s], wd_ref[...]) for s in range(2)]
    for s in range(2):
        hout_ref[s * th:(s + 1) * th, :] += down[s]

    @pl.when(f == nf - 1)
    def _():
        hn_ref[...] = _rms_scale(hout_ref[...], nw_ref[...]).astype(hn_ref.dtype)


def _conv_ffn(hn, h, w_gate, w_up, conv_w, conv_b, w_down, nw):
    M, D = hn.shape
    F = w_gate.shape[1]
    tm = _pick_tile(M, 640, 64)
    tf = _pick_tile(F, 512, 128)
    ni, nf = M // tm, F // tf
    hb = tm // HALO
    nhb = M // HALO

    return pl.pallas_call(
        functools.partial(_ffn_kernel, tm=tm, nf=nf, ni=ni),
        grid=(ni, nf),
        in_specs=[pl.BlockSpec((HALO, D), lambda i, f: (jnp.maximum(i * hb - 1, 0), 0)),
                  pl.BlockSpec((tm, D), lambda i, f: (i, 0)),
                  pl.BlockSpec((HALO, D), lambda i, f: (jnp.minimum((i + 1) * hb, nhb - 1), 0)),
                  pl.BlockSpec((D, tf), lambda i, f: (0, f)),
                  pl.BlockSpec((D, tf), lambda i, f: (0, f)),
                  pl.BlockSpec((3, tf), lambda i, f: (0, f)),
                  pl.BlockSpec((1, tf), lambda i, f: (0, f)),
                  pl.BlockSpec((tf, D), lambda i, f: (f, 0)),
                  pl.BlockSpec((tm, D), lambda i, f: (i, 0)),
                  pl.BlockSpec((1, D), lambda i, f: (0, 0))],
        out_specs=[pl.BlockSpec((tm, D), lambda i, f: (i, 0)), pl.BlockSpec((tm, D), lambda i, f: (i, 0))],
        out_shape=[jax.ShapeDtypeStruct((M, D), F32), jax.ShapeDtypeStruct((M, D), BF16)],
        scratch_shapes=[pltpu.VMEM((tm + 2 * HALO, D), BF16)],
        compiler_params=_cparams("parallel", "arbitrary"),
    )(hn, hn, hn, w_gate, w_up, conv_w, conv_b.reshape(1, F), w_down, h, nw.reshape(1, D))


def _scan8(a, u, row, reverse):
    for s in (1, 2, 4):
        if reverse:
            a_sh = pltpu.roll(a, 8 - s, 0)
            u_sh = pltpu.roll(u, 8 - s, 0)
            m = row < 8 - s
        else:
            a_sh = pltpu.roll(a, s, 0)
            u_sh = pltpu.roll(u, s, 0)
            m = row >= s
        u = jnp.where(m, a * u_sh + u, u)
        a = jnp.where(m, a * a_sh, a)
    return a, u


def _rg_kernel(rec_ref, gate_ref, cw_ref, cb_ref, wa_ref, ba_ref, wi_ref, bi_ref, lam_ref, o_ref,
               xc_s, hs_s, *, Tp, rc):
    nch = Tp // rc
    W = RG_BLOCK
    cw = cw_ref[...]
    cb = cb_ref[...]
    row8 = lax.broadcasted_iota(jnp.int32, (8, W), 0)

    def gates(xc, d):
        xb = xc.astype(BF16)
        r = _sigmoid(_dot(xb, wa_ref[d, 0]) + ba_ref[d])
        ig = _sigmoid(_dot(xb, wi_ref[d, 0]) + bi_ref[d])
        log_a = (-RG_C) * r * _softplus(-lam_ref[d])
        a = jnp.exp(log_a)
        u = jnp.sqrt(1.0 - a * a) * (ig * xc)
        return a, u

    def fwd_chunk(c, carry):
        r0 = pl.multiple_of(c * rc, CHUNK)
        top = rec_ref[0, pl.ds(pl.multiple_of(jnp.maximum(r0 - 8, 0), 8), 8), :]
        main = rec_ref[0, pl.ds(r0, rc), :]
        bot = rec_ref[0, pl.ds(pl.multiple_of(jnp.minimum(r0 + rc, Tp - 8), 8), 8), :]
        bot = jnp.where(r0 + rc < Tp, bot, 0.0)
        xw = jnp.concatenate([top, main, bot], axis=0)
        xc = cw[0:1] * xw[7:7 + rc] + cw[1:2] * xw[8:8 + rc] + cw[2:3] * xw[9:9 + rc] + cw[3:4] * xw[10:10 + rc] + cb
        t = r0 + lax.broadcasted_iota(jnp.int32, (rc, 1), 0)
        xc = jnp.where(t >= PAD, xc, 0.0)
        xc_s[pl.ds(r0, rc), :] = xc
        a, u = gates(xc, 0)
        hs = []
        for k in range(rc // 8):
            A, U = _scan8(a[8 * k:8 * k + 8], u[8 * k:8 * k + 8], row8, False)
            h = A * carry + U
            carry = h[7:8]
            hs.append(h)
        hs_s[pl.ds(r0, rc), :] = jnp.concatenate(hs, axis=0)
        return carry

    lax.fori_loop(0, nch, fwd_chunk, jnp.zeros((1, W), F32))

    def bwd_chunk(cc, carry):
        r0 = pl.multiple_of((nch - 1 - cc) * rc, CHUNK)
        xc = xc_s[pl.ds(r0, rc), :]
        a, u = gates(xc, 1)
        hs = [None] * (rc // 8)
        for k in reversed(range(rc // 8)):
            A, U = _scan8(a[8 * k:8 * k + 8], u[8 * k:8 * k + 8], row8, True)
            h = A * carry + U
            carry = h[0:1]
            hs[k] = h
        hsum = hs_s[pl.ds(r0, rc), :] + jnp.concatenate(hs, axis=0)
        o_ref[0, pl.ds(r0, rc), :] = (gate_ref[0, pl.ds(r0, rc), :].astype(F32) * hsum).astype(o_ref.dtype)
        return carry

    lax.fori_loop(0, nch, bwd_chunk, jnp.zeros((1, W), F32))


def _rg_scan(rec, gate, conv_w, conv_b, w_a, b_a, w_i, b_i, lam):
    B, Tp, W = rec.shape
    nb = W // RG_BLOCK
    rc = _pick_tile(Tp, 384, 64)
    blk = lambda b, n: (b, 0, n)
    vec = lambda b, n: (0, n)
    vec3 = lambda b, n: (0, 0, n)
    return pl.pallas_call(
        functools.partial(_rg_kernel, Tp=Tp, rc=rc),
        grid=(B, nb),
        in_specs=[pl.BlockSpec((1, Tp, RG_BLOCK), blk),
                  pl.BlockSpec((1, Tp, RG_BLOCK), blk),
                  pl.BlockSpec((4, RG_BLOCK), vec),
                  pl.BlockSpec((1, RG_BLOCK), vec),
                  pl.BlockSpec((2, 1, RG_BLOCK, RG_BLOCK), lambda b, n: (0, n, 0, 0)),
                  pl.BlockSpec((2, 1, RG_BLOCK), vec3),
                  pl.BlockSpec((2, 1, RG_BLOCK, RG_BLOCK), lambda b, n: (0, n, 0, 0)),
                  pl.BlockSpec((2, 1, RG_BLOCK), vec3),
                  pl.BlockSpec((2, 1, RG_BLOCK), vec3)],
        out_specs=pl.BlockSpec((1, Tp, RG_BLOCK), blk),
        out_shape=jax.ShapeDtypeStruct((B, Tp, W), BF16),
        scratch_shapes=[pltpu.VMEM((Tp, RG_BLOCK), F32), pltpu.VMEM((Tp, RG_BLOCK), F32)],
        compiler_params=_cparams("parallel", "parallel"),
    )(rec, gate, conv_w, conv_b.reshape(1, W), w_a.astype(BF16), b_a.reshape(2, 1, W),
      w_i.astype(BF16), b_i.reshape(2, 1, W), lam.reshape(2, 1, W))


def _rglru_mixer(hn, h, B, Tp, w_in, conv_w, conv_b, w_a, b_a, w_i, b_i, lam, w_out, nw):
    W = D_MODEL
    w_in = w_in.astype(BF16)
    gate = _matmul(hn, w_in[:, :W], BF16, act="gelu")
    rec = _matmul(hn, w_in[:, W:], F32)
    y = _rg_scan(rec.reshape(B, Tp, W), gate.reshape(B, Tp, W), conv_w, conv_b, w_a, b_a, w_i, b_i, lam)
    return _proj_res(y.reshape(B * Tp, W), w_out.astype(BF16), h, nw)


def _na_kernel(q_ref, k_ref, v_ref, bias_ref, mb_ref, o_ref, *, rows):
    scale = NA_HEAD_DIM ** -0.5
    kh = NA_WIN_R
    mb = mb_ref[0]
    km = k_ref[0, PAD:HEAD_ROWS, :]
    vm = v_ref[0, PAD:HEAD_ROWS, :]

    o_ref[0, 0:PAD, :] = jnp.zeros((PAD, NA_HEAD_DIM), o_ref.dtype)
    qm = q_ref[0, PAD:HEAD_ROWS, :]
    s_m = _dot_nt(qm, km) * scale + mb
    p_m = jnp.exp(s_m - jnp.max(s_m, axis=-1, keepdims=True))
    o_m = _dot(p_m.astype(BF16), vm) / jnp.sum(p_m, axis=-1, keepdims=True)
    o_ref[0, PAD:HEAD_ROWS, :] = o_m.astype(o_ref.dtype)

    R = range(NA_ROW_GROUP)

    def group_body(i, _):
        r = [i * NA_ROW_GROUP + c for c in R]
        rs = [jnp.clip(r[c] - NA_WIN_R // 2, 0, rows - kh) for c in R]
        q0 = [pl.multiple_of(HEAD_ROWS + r[c] * GRID_W, GRID_W) for c in R]
        k0 = [pl.multiple_of(HEAD_ROWS + rs[c] * GRID_W, GRID_W) for c in R]
        q_r = [q_ref[0, pl.ds(q0[c], GRID_W), :] for c in R]
        s = [_dot_nt(q_r[c], k_ref[0, pl.ds(k0[c], kh * GRID_W), :]) for c in R]
        s_met = [_dot_nt(q_r[c], km) for c in R]
        p, p_met, l = [], [], []
        for c in R:
            sc = s[c] * scale + bias_ref[0, r[c] - rs[c]]
            sm = s_met[c] * scale + mb
            m = jnp.maximum(jnp.max(sc, axis=-1, keepdims=True), jnp.max(sm, axis=-1, keepdims=True))
            pc = jnp.exp(sc - m)
            pm = jnp.exp(sm - m)
            l.append(jnp.sum(pc, axis=-1, keepdims=True) + jnp.sum(pm, axis=-1, keepdims=True))
            p.append(pc.astype(BF16))
            p_met.append(pm.astype(BF16))
        o = [_dot(p[c], v_ref[0, pl.ds(k0[c], kh * GRID_W), :]) + _dot(p_met[c], vm) for c in R]
        for c in R:
            o_ref[0, pl.ds(q0[c], GRID_W), :] = (o[c] / l[c]).astype(o_ref.dtype)
        return 0

    assert rows % NA_ROW_GROUP == 0
    lax.fori_loop(0, rows // NA_ROW_GROUP, group_body, 0)


def _na_bias_table(rpb):
    kh = NA_WIN_R
    cols = np.arange(GRID_W)
    col_start = np.clip(cols - NA_WIN_C // 2, 0, GRID_W - NA_WIN_C)
    valid = (cols[None, :] >= col_start[:, None]) & (cols[None, :] < col_start[:, None] + NA_WIN_C)
    col_off = np.clip(cols[None, :] - cols[:, None], -(NA_WIN_C - 1), NA_WIN_C - 1) + NA_WIN_C - 1
    H, n_ro, n_co = rpb.shape
    onehot = (col_off[None] == np.arange(n_co)[:, None, None]).astype(np.float32)
    a = jnp.einsum('hrc,cqk->hqrk', rpb.astype(F32), jnp.asarray(onehot), precision=lax.Precision.HIGHEST)
    a = jnp.where(jnp.asarray(valid)[None, :, None, :], a, NEG_INF)
    tabs = [a[:, :, NA_WIN_R - 1 - d:NA_WIN_R - 1 - d + kh, :].reshape(H, GRID_W, kh * GRID_W)
            for d in range(NA_WIN_R)]
    return jnp.stack(tabs, axis=1)


def _na_attention(qkv, bias, meta_bias):
    B, Tp, _ = qkv.shape
    rows = (Tp - HEAD_ROWS) // GRID_W
    assert rows >= NA_WIN_R
    H = NA_HEADS
    return pl.pallas_call(
        functools.partial(_na_kernel, rows=rows),
        grid=(B, H),
        in_specs=[pl.BlockSpec((1, Tp, NA_HEAD_DIM), lambda b, h: (b, 0, h)),
                  pl.BlockSpec((1, Tp, NA_HEAD_DIM), lambda b, h: (b, 0, H + h)),
                  pl.BlockSpec((1, Tp, NA_HEAD_DIM), lambda b, h: (b, 0, 2 * H + h)),
                  pl.BlockSpec((1, NA_WIN_R, GRID_W, NA_WIN_R * GRID_W), lambda b, h: (h, 0, 0, 0)),
                  pl.BlockSpec((1, 1, N_META), lambda b, h: (h, 0, 0))],
        out_specs=pl.BlockSpec((1, Tp, NA_HEAD_DIM), lambda b, h: (b, 0, h)),
        out_shape=jax.ShapeDtypeStruct((B, Tp, H * NA_HEAD_DIM), BF16),
        compiler_params=_cparams("parallel", "parallel"),
    )(qkv, qkv, qkv, bias, meta_bias.astype(F32).reshape(H, 1, N_META))


def _na_mixer(hn, h, B, Tp, w_qkv, bias, meta_bias, w_o, nw):
    qkv = _matmul(hn, w_qkv.astype(BF16), BF16)
    o = _na_attention(qkv.reshape(B, Tp, -1), bias, meta_bias)
    return _proj_res(o.reshape(B * Tp, -1), w_o.astype(BF16), h, nw)


def _gdn_gate_kernel(x_ref, w_ref, al_ref, dt_ref, o_ref):
    y = _dot(x_ref[...], w_ref[...])
    lane = lax.broadcasted_iota(jnp.int32, y.shape, 1)
    g = -jnp.exp(al_ref[...]) * _softplus(y + dt_ref[...])
    o_ref[...] = jnp.where(lane < 2 * GDN_V_HEADS, _sigmoid(y), g)


def _gdn_gates(hn, w_ba, a_log, dt_bias):
    M, K = hn.shape
    N = 4 * GDN_V_HEADS
    tm = _pick_tile(M, 1024, 128)
    zeros = jnp.zeros((2 * GDN_V_HEADS,), F32)
    al = jnp.concatenate([zeros, a_log.astype(F32).reshape(-1)]).reshape(1, N)
    dt = jnp.concatenate([zeros, dt_bias.astype(F32).reshape(-1)]).reshape(1, N)
    return pl.pallas_call(
        _gdn_gate_kernel,
        grid=(M // tm,),
        in_specs=[pl.BlockSpec((tm, K), lambda i: (i, 0)), pl.BlockSpec((K, N), lambda i: (0, 0)),
                  pl.BlockSpec((1, N), lambda i: (0, 0)), pl.BlockSpec((1, N), lambda i: (0, 0))],
        out_specs=pl.BlockSpec((tm, N), lambda i: (i, 0)),
        out_shape=jax.ShapeDtypeStruct((M, N), F32),
        compiler_params=_cparams("parallel"),
    )(hn, w_ba, al, dt)


def _gdn_conv_kernel(x_ref, cw_ref, o_ref, *, Tp, rc, n_qk_blocks):
    j = pl.program_id(1)
    cw = cw_ref[...]
    nch = Tp // rc

    def chunk(c, _):
        r0 = pl.multiple_of(c * rc, CHUNK)
        top = x_ref[0, pl.ds(pl.multiple_of(jnp.maximum(r0 - 8, 0), 8), 8), :]
        main = x_ref[0, pl.ds(r0, rc), :]
        bot = x_ref[0, pl.ds(pl.multiple_of(jnp.minimum(r0 + rc, Tp - 8), 8), 8), :]
        bot = jnp.where(r0 + rc < Tp, bot, 0.0)
        xw = jnp.concatenate([top, main, bot], axis=0)
        a = cw[0:1] * xw[7:7 + rc] + cw[1:2] * xw[8:8 + rc] + cw[2:3] * xw[9:9 + rc] + cw[3:4] * xw[10:10 + rc]
        y = a * _sigmoid(a)
        t = r0 + lax.broadcasted_iota(jnp.int32, (rc, 1), 0)
        y = jnp.where(t >= PAD, y, 0.0)
        nrm = lax.rsqrt(jnp.sum(y * y, axis=-1, keepdims=True) + 1e-6)
        y = y * jnp.where(j < n_qk_blocks, nrm, 1.0)
        o_ref[0, pl.ds(r0, rc), :] = y.astype(o_ref.dtype)
        return 0

    lax.fori_loop(0, nch, chunk, 0)


def _gdn_conv(proj3, conv_w):
    B, Tp, _ = proj3.shape
    nblk = GDN_CONV_DIM // GDN_DK
    rc = _pick_tile(Tp, 384, 64)
    return pl.pallas_call(
        functools.partial(_gdn_conv_kernel, Tp=Tp, rc=rc, n_qk_blocks=2 * GDN_QK_HEADS),
        grid=(B, nblk),
        in_specs=[pl.BlockSpec((1, Tp, GDN_DK), lambda b, j: (b, 0, j)),
                  pl.BlockSpec((4, GDN_DK), lambda b, j: (0, j))],
        out_specs=pl.BlockSpec((1, Tp, GDN_DK), lambda b, j: (b, 0, j)),
        out_shape=jax.ShapeDtypeStruct((B, Tp, GDN_CONV_DIM), BF16),
        compiler_params=_cparams("parallel", "parallel"),
    )(proj3, conv_w)


def _split3_dot(x, m):
    hi = x.astype(BF16)
    r1 = x - hi.astype(F32)
    mid = r1.astype(BF16)
    lo = (r1 - mid.astype(F32)).astype(BF16)
    return _dot(hi, m) + _dot(mid, m) + _dot(lo, m)


def _gdn_kernel_unused(q_ref, k_ref, v_ref, z_ref, bgc_ref, bgr_ref, nw_ref, o_ref, S_s, oacc_s, *, Tp):
    kh = pl.program_id(1)
    C = CHUNK
    NC = Tp // C
    HV = GDN_V_HEADS
    DK = GDN_DK
    scale = DK ** -0.5

    oacc_s[...] = jnp.zeros_like(oacc_s)
    S_s[...] = jnp.zeros_like(S_s)

    ri = lax.broadcasted_iota(jnp.int32, (C, C), 0)
    ci = lax.broadcasted_iota(jnp.int32, (C, C), 1)
    lower = ri >= ci
    upper = ri <= ci
    eye = (ri == ci).astype(F32)
    ones_lower = lower.astype(BF16)
    ones_upper = upper.astype(BF16)
    lane = lax.broadcasted_iota(jnp.int32, (C, 4 * HV), 1)
    rowi = lax.broadcasted_iota(jnp.int32, (C, 4 * HV), 0)

    def col(x, c):
        return jnp.sum(jnp.where(lane == c, x, 0.0), axis=1, keepdims=True)

    def one_chunk(n, reverse):
        r0 = pl.multiple_of(n * C, C)
        q = q_ref[0, pl.ds(r0, C), :]
        k = k_ref[0, pl.ds(r0, C), :]
        kf = k.astype(F32)
        bg = bgc_ref[0, pl.ds(r0, C), :]
        pre = bg
        for s in (1, 2, 4, 8, 16, 32):
            pre = pre + jnp.where(rowi >= s, pltpu.roll(pre, s, 0), 0.0)
        if reverse:
            cum = pre[C - 1:C] - pre + bg
        else:
            cum = pre
        rows = bgr_ref[0, n, 0]
        cum_rows = _split3_dot(rows, ones_lower if reverse else ones_upper)
        kk = _dot_nt(k, k)
        qk = _dot_nt(q, k) * scale
        incl = upper if reverse else lower
        strict = (ri < ci) if reverse else (ri > ci)
        d = 1 if reverse else 0
        for j in range(2):
            inst = 2 * d + j
            vh = 2 * kh + j
            beta_c = col(bg, d * HV + vh)
            gc_c = col(cum, 2 * HV + d * HV + vh)
            gc_r = cum_rows[4 + 2 * d + j:5 + 2 * d + j]
            decay = jnp.where(incl, jnp.exp(jnp.where(incl, gc_c - gc_r, 0.0)), 0.0)
            L = jnp.where(strict, kk * decay, 0.0) * beta_c
            Tm = eye - L
            P = L
            for _ in range(5):
                Pb = P.astype(BF16)
                P = _dot(Pb, Pb)
                Tm = Tm + _dot(Tm.astype(BF16), P.astype(BF16))
            Tb = Tm.astype(BF16)
            v = v_ref[0, pl.ds(r0, C), j * DK:(j + 1) * DK].astype(F32)
            e_c = jnp.exp(gc_c)
            u = _dot(Tb, (v * beta_c).astype(BF16))
            w = _dot(Tb, (kf * (beta_c * e_c)).astype(BF16))
            qkd = jnp.where(incl, qk * decay, 0.0)
            g_last = gc_c[0:1] if reverse else gc_c[C - 1:C]
            k_st = kf * jnp.exp(g_last - gc_c)
            q_st = q.astype(F32) * (scale * e_c)
            S = S_s[inst]
            Sb = S.astype(BF16)
            v_new = u - _dot(w.astype(BF16), Sb)
            vb = v_new.astype(BF16)
            o = _dot(q_st.astype(BF16), Sb) + _dot(qkd.astype(BF16), vb)
            S_s[inst] = S * jnp.exp(g_last) + _dot_tn(k_st.astype(BF16), vb)
            oacc_s[pl.ds(r0, C), j * DK:(j + 1) * DK] += o

    def body(it, _):
        one_chunk(it, False)
        one_chunk(NC - 1 - it, True)
        return 0

    lax.fori_loop(0, NC, body, 0)

    nw = nw_ref[...]

    def fin(n, _):
        r0 = pl.multiple_of(n * C, C)
        for j in range(2):
            o = oacc_s[pl.ds(r0, C), j * DK:(j + 1) * DK]
            z = z_ref[0, pl.ds(r0, C), j * DK:(j + 1) * DK]
            y = o * lax.rsqrt(jnp.mean(o * o, axis=-1, keepdims=True) + NORM_EPS) * nw * (z * _sigmoid(z))
            o_ref[0, pl.ds(r0, C), j * DK:(j + 1) * DK] = y.astype(o_ref.dtype)
        return 0

    lax.fori_loop(0, NC, fin, 0)


def _gdn_delta_unused(qkv, proj3, bg, norm_w):
    B, Tp, _ = qkv.shape
    NC = Tp // CHUNK
    KH = GDN_QK_HEADS
    HV = GDN_V_HEADS
    bgr = bg.reshape(B, NC, CHUNK, 4, KH, 2).transpose(0, 1, 4, 3, 5, 2).reshape(B, NC, KH, 8, CHUNK)
    zoff = GDN_CONV_DIM // (2 * GDN_DK)
    return pl.pallas_call(
        functools.partial(_gdn_kernel, Tp=Tp),
        grid=(B, KH),
        in_specs=[pl.BlockSpec((1, Tp, GDN_DK), lambda b, h: (b, 0, h)),
                  pl.BlockSpec((1, Tp, GDN_DK), lambda b, h: (b, 0, KH + h)),
                  pl.BlockSpec((1, Tp, 2 * GDN_DK), lambda b, h: (b, 0, KH + h)),
                  pl.BlockSpec((1, Tp, 2 * GDN_DK), lambda b, h: (b, 0, zoff + h)),
                  pl.BlockSpec((1, Tp, 4 * HV), lambda b, h: (b, 0, 0)),
                  pl.BlockSpec((1, NC, 1, 8, CHUNK), lambda b, h: (b, 0, h, 0, 0)),
                  pl.BlockSpec((1, GDN_DK), lambda b, h: (0, 0))],
        out_specs=pl.BlockSpec((1, Tp, 2 * GDN_DK), lambda b, h: (b, 0, h)),
        out_shape=jax.ShapeDtypeStruct((B, Tp, GDN_VAL_DIM), BF16),
        scratch_shapes=[pltpu.VMEM((4, GDN_DK, GDN_DK), F32), pltpu.VMEM((Tp, 2 * GDN_DK), F32)],
        compiler_params=_cparams("parallel", "parallel"),
    )(qkv, qkv, qkv, proj3, bg, bgr, norm_w.astype(F32).reshape(1, GDN_DK))


def _gdn_kernel(q_ref, k_ref, v_ref, z_ref, bgc_ref, bgr_ref, nw_ref, o_ref,
                S_s, oacc_s, u_s, wq_s, kst_s, qkd_s, gam_s, *, Tp):
    kh = pl.program_id(1)
    C = CHUNK
    NC = Tp // C
    HV = GDN_V_HEADS
    DK = GDN_DK
    G = 4
    W = G * C
    P1_GROUP = 6
    scale = DK ** -0.5

    oacc_s[...] = jnp.zeros_like(oacc_s)
    S_s[...] = jnp.zeros_like(S_s)

    ri = lax.broadcasted_iota(jnp.int32, (C, W), 0)
    cl = lax.broadcasted_iota(jnp.int32, (C, W), 1)
    cj = jnp.bitwise_and(cl, C - 1)
    blk = jnp.right_shift(cl, 6)
    dd = jnp.where(blk >= 2, cj - ri, ri - cj)
    incl4 = dd >= 0
    strict4 = dd > 0
    eye4 = (ri == cj).astype(F32)
    cum4 = jnp.where(dd <= 0, 1.0, 0.0).astype(BF16)
    bdm = jnp.where(jnp.right_shift(lax.broadcasted_iota(jnp.int32, (W, W), 0), 6)
                    == jnp.right_shift(lax.broadcasted_iota(jnp.int32, (W, W), 1), 6), 1.0, 0.0).astype(BF16)
    blk_row = jnp.right_shift(lax.broadcasted_iota(jnp.int32, (1, W), 1), 6)
    lane = lax.broadcasted_iota(jnp.int32, (C, 4 * HV), 1)
    rowi = lax.broadcasted_iota(jnp.int32, (C, 4 * HV), 0)

    def col(x, c):
        return jnp.sum(jnp.where(lane == c, x, 0.0), axis=1, keepdims=True)

    def by_block(parts, b):
        return jnp.where(b == 0, parts[0], jnp.where(b == 1, parts[1], jnp.where(b == 2, parts[2], parts[3])))

    def blockdiag(xb):
        return jnp.concatenate([xb] * G, axis=0) * bdm

    def phase1(ns):
        K = range(len(ns))
        r0 = [pl.multiple_of(n * C, C) for n in ns]
        q = [q_ref[0, pl.ds(r, C), :] for r in r0]
        k = [k_ref[0, pl.ds(r, C), :] for r in r0]
        bg = [bgc_ref[0, pl.ds(r, C), :] for r in r0]
        v = [v_ref[0, pl.ds(r, C), :].astype(F32) for r in r0]
        rows = [bgr_ref[0, n, 0] for n in ns]
        cum_rows = [_split3_dot(rows[i], cum4) for i in K]
        k4 = [jnp.concatenate([k[i]] * G, axis=0) for i in K]
        kk4 = [_dot_nt(k[i], k4[i]) for i in K]
        qk4 = [_dot_nt(q[i], k4[i]) * scale for i in K]
        beta_c, gc_c, decay4, L = [], [], [], []
        for i in K:
            pre = bg[i]
            for s in (1, 2, 4, 8, 16, 32):
                pre = pre + jnp.where(rowi >= s, pltpu.roll(pre, s, 0), 0.0)
            suf = pre[C - 1:C] - pre + bg[i]
            gc_r = by_block([cum_rows[i][4 + g:5 + g] for g in range(G)], blk_row)
            bc, gc = [], []
            for g in range(G):
                d, j = g // 2, g % 2
                vh = 2 * kh + j
                bc.append(col(bg[i], d * HV + vh))
                gc.append(col(suf if d else pre, 2 * HV + d * HV + vh))
            beta_c.append(bc)
            gc_c.append(gc)
            dec = jnp.where(incl4, jnp.exp(jnp.where(incl4, by_block(gc, blk) - gc_r, 0.0)), 0.0)
            decay4.append(dec)
            L.append(jnp.where(strict4, kk4[i] * dec, 0.0) * by_block(bc, blk))
        Tm = [eye4 - L[i] for i in K]
        Lb = [L[i].astype(BF16) for i in K]
        P = [_dot(Lb[i], blockdiag(Lb[i])) for i in K]
        for lvl in range(5):
            Pb = [P[i].astype(BF16) for i in K]
            bd = [blockdiag(Pb[i]) for i in K]
            if lvl < 4:
                tp = [_dot(jnp.concatenate([Tm[i].astype(BF16), Pb[i]], axis=0), bd[i]) for i in K]
                Tm = [Tm[i] + tp[i][:C] for i in K]
                P = [tp[i][C:] for i in K]
            else:
                tp = [_dot(Tm[i].astype(BF16), bd[i]) for i in K]
                Tm = [Tm[i] + tp[i] for i in K]
        rhs, kst, qst, gam = [], [], [], []
        for i in K:
            qf = q[i].astype(F32)
            kf = k[i].astype(F32)
            rhs_i, kst_i, qst_i, gam_i = [], [], [], []
            for g in range(G):
                j = g % 2
                bc, gc = beta_c[i][g], gc_c[i][g]
                e_c = jnp.exp(gc)
                rhs_i.append(jnp.concatenate([v[i][:, j * DK:(j + 1) * DK] * bc, kf * (bc * e_c)], axis=1))
                g_last = gc[0:1] if g >= 2 else gc[C - 1:C]
                kst_i.append((kf * jnp.exp(g_last - gc)).astype(BF16))
                qst_i.append((qf * (scale * e_c)).astype(BF16))
                gam_i.append(jnp.broadcast_to(jnp.exp(g_last), (1, DK)))
            rhs.append(jnp.concatenate(rhs_i, axis=0).astype(BF16))
            kst.append(kst_i)
            qst.append(qst_i)
            gam.append(gam_i)
        sol = [_dot(blockdiag(Tm[i].astype(BF16)), rhs[i]) for i in K]
        for i in K:
            n = ns[i]
            u_s[n] = sol[i][:, :DK]
            qkd_s[n] = jnp.where(incl4, qk4[i] * decay4[i], 0.0).astype(BF16)
            for g in range(G):
                wq_s[n, g, 0:C, :] = sol[i][g * C:(g + 1) * C, DK:].astype(BF16)
                wq_s[n, g, C:2 * C, :] = qst[i][g]
                kst_s[n, g] = kst[i][g]
                gam_s[n, g:g + 1, :] = gam[i][g]

    def phase1_group(i, _):
        phase1([P1_GROUP * i + c for c in range(P1_GROUP)])
        return 0

    lax.fori_loop(0, NC // P1_GROUP, phase1_group, 0)
    if NC % P1_GROUP:
        phase1(list(range(NC - NC % P1_GROUP, NC)))

    nw = nw_ref[...]

    def gated_norm(o, z):
        ys = []
        for j in range(2):
            oj = o[:, j * DK:(j + 1) * DK]
            zj = z[:, j * DK:(j + 1) * DK].astype(F32)
            ys.append(oj * lax.rsqrt(jnp.mean(oj * oj, axis=-1, keepdims=True) + NORM_EPS) * nw * (zj * _sigmoid(zj)))
        return jnp.concatenate(ys, axis=1)

    def phase2(it, finalize):
        ops = []
        for g in range(G):
            n = it if g < 2 else NC - 1 - it
            ops.append((S_s[g], wq_s[n, g], u_s[n, g * C:(g + 1) * C, :], qkd_s[n, :, g * C:(g + 1) * C],
                        gam_s[n, g:g + 1, :], kst_s[n, g]))
        r_f = pl.multiple_of(it * C, C)
        r_b = pl.multiple_of((NC - 1 - it) * C, C)
        o_f = oacc_s[pl.ds(r_f, C), :]
        ws = [_dot(wq, S.astype(BF16)) for (S, wq, u, qkd, gam, kst) in ops]
        vb = [(ops[g][2] - ws[g][:C]).astype(BF16) for g in range(G)]
        outs = [ws[g][C:] + _dot(ops[g][3], vb[g]) for g in range(G)]
        states = [ops[g][0] * ops[g][4] + _dot_tn(ops[g][5], vb[g]) for g in range(G)]
        for g in range(G):
            S_s[g] = states[g]
        o_f = o_f + jnp.concatenate(outs[0:2], axis=1)
        oacc_s[pl.ds(r_f, C), :] = o_f
        o_b = oacc_s[pl.ds(r_b, C), :] + jnp.concatenate(outs[2:4], axis=1)
        oacc_s[pl.ds(r_b, C), :] = o_b
        if finalize:
            o_ref[0, pl.ds(r_f, C), :] = gated_norm(o_f, z_ref[0, pl.ds(r_f, C), :]).astype(o_ref.dtype)
            o_ref[0, pl.ds(r_b, C), :] = gated_norm(o_b, z_ref[0, pl.ds(r_b, C), :]).astype(o_ref.dtype)
        return 0

    lax.fori_loop(0, NC // 2, lambda it, c: phase2(it, False), 0)
    lax.fori_loop(NC // 2, NC, lambda it, c: phase2(it, True), 0)


def _gdn_delta(qkv, z, bg, norm_w):
    B, Tp, _ = qkv.shape
    NC = Tp // CHUNK
    KH = GDN_QK_HEADS
    HV = GDN_V_HEADS
    C = CHUNK
    bgr = bg.reshape(B, NC, CHUNK, 4, KH, 2).transpose(0, 1, 4, 3, 5, 2).reshape(B, NC, KH, 8, CHUNK)
    return pl.pallas_call(
        functools.partial(_gdn_kernel, Tp=Tp),
        grid=(B, KH),
        in_specs=[pl.BlockSpec((1, Tp, GDN_DK), lambda b, h: (b, 0, h)),
                  pl.BlockSpec((1, Tp, GDN_DK), lambda b, h: (b, 0, KH + h)),
                  pl.BlockSpec((1, Tp, 2 * GDN_DK), lambda b, h: (b, 0, KH + h)),
                  pl.BlockSpec((1, Tp, 2 * GDN_DK), lambda b, h: (b, 0, h)),
                  pl.BlockSpec((1, Tp, 4 * HV), lambda b, h: (b, 0, 0)),
                  pl.BlockSpec((1, NC, 1, 8, CHUNK), lambda b, h: (b, 0, h, 0, 0)),
                  pl.BlockSpec((1, GDN_DK), lambda b, h: (0, 0))],
        out_specs=pl.BlockSpec((1, Tp, 2 * GDN_DK), lambda b, h: (b, 0, h)),
        out_shape=jax.ShapeDtypeStruct((B, Tp, GDN_VAL_DIM), BF16),
        scratch_shapes=[pltpu.VMEM((4, GDN_DK, GDN_DK), F32),
                        pltpu.VMEM((Tp, 2 * GDN_DK), F32),
                        pltpu.VMEM((NC, 4 * C, GDN_DK), F32),
                        pltpu.VMEM((NC, 4, 2 * C, GDN_DK), BF16),
                        pltpu.VMEM((NC, 4, C, GDN_DK), BF16),
                        pltpu.VMEM((NC, C, 4 * C), BF16),
                        pltpu.VMEM((NC, 8, GDN_DK), F32)],
        compiler_params=_cparams("parallel", "parallel"),
    )(qkv, qkv, qkv, z, bg, bgr, norm_w.astype(F32).reshape(1, GDN_DK))


def _gdn_mixer(hn, h, B, Tp, w_in, conv_w, a_log, dt_bias, norm_w, w_out, nw):
    nz = GDN_CONV_DIM + GDN_VAL_DIM
    w_in = w_in.astype(BF16)
    qkv = _proj_conv(hn, w_in[:, :GDN_CONV_DIM], conv_w, Tp, 2 * GDN_KEY_DIM)
    z = _matmul(hn, w_in[:, GDN_CONV_DIM:nz], BF16)
    bg = _gdn_gates(hn, w_in[:, nz:], a_log, dt_bias)
    o = _gdn_delta(qkv.reshape(B, Tp, -1), z.reshape(B, Tp, -1), bg.reshape(B, Tp, -1), norm_w)
    return _proj_res(o.reshape(B * Tp, -1), w_out.astype(BF16), h, nw)


def _trunk(x, meta_tokens, mix_norm, ffn_norm, final_norm, rg, na, gdn, ffn):
    B, T, D = x.shape
    Tp = T + HEAD_ROWS
    depth = mix_norm.shape[0]
    h3 = jnp.concatenate([jnp.zeros((B, PAD, D), F32),
                          jnp.broadcast_to(meta_tokens.astype(F32)[None], (B, N_META, D)),
                          x.astype(F32)], axis=1)
    h = h3.reshape(B * Tp, D)
    hn = _rmsnorm(h, mix_norm[0], BF16)
    for i in range(depth):
        kind, j = i % 3, i // 3
        if kind == 0:
            h, hn = _rglru_mixer(hn, h, B, Tp, *[p[j] for p in rg], ffn_norm[i])
        elif kind == 1:
            h, hn = _na_mixer(hn, h, B, Tp, *[p[j] for p in na], ffn_norm[i])
        else:
            h, hn = _gdn_mixer(hn, h, B, Tp, *[p[j] for p in gdn], ffn_norm[i])
        w_gate, w_up, conv_w, conv_b, w_down = [p[i] for p in ffn]
        next_norm = mix_norm[i + 1] if i + 1 < depth else final_norm
        h, hn = _conv_ffn(hn, h, w_gate.astype(BF16), w_up.astype(BF16), conv_w, conv_b,
                          w_down.astype(BF16), next_norm)
    return _final_norm(h, final_norm, B, T)


def kernel(x_prompt, x_sample, meta_tokens, mix_norm, ffn_norm, final_norm, rg_w_in, rg_conv_w, rg_conv_b, rg_w_a, rg_b_a, rg_w_i, rg_b_i, rg_lam, rg_w_out, na_w_qkv, na_rpb, na_meta_bias, na_w_o, gdn_w_in, gdn_conv_w, gdn_a_log, gdn_dt_bias, gdn_norm_w, gdn_w_out, ffn_w_gate, ffn_w_up, ffn_conv_w, ffn_conv_b, ffn_w_down):
    rg = (rg_w_in, rg_conv_w, rg_conv_b, rg_w_a, rg_b_a, rg_w_i, rg_b_i, rg_lam, rg_w_out)
    na_bias = jnp.stack([_na_bias_table(na_rpb[j]) for j in range(na_rpb.shape[0])])
    na = (na_w_qkv, na_bias, na_meta_bias, na_w_o)
    gdn = (gdn_w_in, gdn_conv_w, gdn_a_log, gdn_dt_bias, gdn_norm_w, gdn_w_out)
    ffn = (ffn_w_gate, ffn_w_up, ffn_conv_w, ffn_conv_b, ffn_w_down)
    y_prompt = _trunk(x_prompt, meta_tokens, mix_norm, ffn_norm, final_norm, rg, na, gdn, ffn)
    y_sample = _trunk(x_sample, meta_tokens, mix_norm, ffn_norm, final_norm, rg, na, gdn, ffn)
    return (y_prompt, y_sample)
```

```python
import functools

import numpy as np
import jax
import jax.numpy as jnp
from jax import lax
from jax.experimental import pallas as pl
from jax.experimental.pallas import tpu as pltpu

D_MODEL = 2048
N_META = 16
PAD = 48
HEAD_ROWS = PAD + N_META
CHUNK = 64
GRID_W = 64
NORM_EPS = 1e-6

RG_BLOCK = 256
RG_C = 8.0

NA_HEAD_DIM = 128
NA_HEADS = D_MODEL // NA_HEAD_DIM
NA_WIN_R = 8
NA_WIN_C = 16
NA_ROW_GROUP = 16
NEG_INF = -1e30

GDN_DK = 128
GDN_QK_HEADS = D_MODEL // GDN_DK
GDN_V_HEADS = 2 * GDN_QK_HEADS
GDN_KEY_DIM = GDN_QK_HEADS * GDN_DK
GDN_VAL_DIM = GDN_V_HEADS * GDN_DK
GDN_CONV_DIM = 2 * GDN_KEY_DIM + GDN_VAL_DIM

V7X_VMEM_LIMIT = 56 * 1024 * 1024
HALO = 16

BF16 = jnp.bfloat16
F32 = jnp.float32


def _cparams(*sem):
    return pltpu.CompilerParams(dimension_semantics=sem, vmem_limit_bytes=V7X_VMEM_LIMIT)


def _pick_tile(n, cap, mult):
    best = None
    for t in range(mult, cap + 1, mult):
        if n % t == 0:
            best = t
    assert best is not None, (n, cap, mult)
    return best


def _sigmoid(x):
    return 1.0 / (1.0 + jnp.exp(-x))


def _softplus(x):
    return jnp.maximum(x, 0.0) + jnp.log(1.0 + jnp.exp(-jnp.abs(x)))


def _gelu_tanh(x):
    return 0.5 * x * (1.0 + jnp.tanh(0.7978845608028654 * (x + 0.044715 * x * x * x)))


def _rms_scale(x, w):
    ms = jnp.mean(x * x, axis=-1, keepdims=True)
    return x * lax.rsqrt(ms + NORM_EPS) * w


def _dot(a, b):
    return jnp.dot(a, b, preferred_element_type=F32)


def _dot_nt(a, b):
    return lax.dot_general(a, b, (((1,), (1,)), ((), ())), preferred_element_type=F32)


def _dot_tn(a, b):
    return lax.dot_general(a, b, (((0,), (0,)), ((), ())), preferred_element_type=F32)


def _rmsnorm_kernel(x_ref, w_ref, o_ref):
    o_ref[...] = _rms_scale(x_ref[...], w_ref[...]).astype(o_ref.dtype)


def _rmsnorm(h, w, out_dtype):
    M, D = h.shape
    tm = _pick_tile(M, 1024, 64)
    return pl.pallas_call(
        _rmsnorm_kernel,
        grid=(M // tm,),
        in_specs=[pl.BlockSpec((tm, D), lambda i: (i, 0)), pl.BlockSpec((1, D), lambda i: (0, 0))],
        out_specs=pl.BlockSpec((tm, D), lambda i: (i, 0)),
        out_shape=jax.ShapeDtypeStruct((M, D), out_dtype),
        compiler_params=_cparams("parallel"),
    )(h, w.reshape(1, D))


def _final_norm_kernel(x_ref, w_ref, o_ref):
    o_ref[0] = _rms_scale(x_ref[...], w_ref[...])


def _final_norm(h, w, B, T):
    D = h.shape[1]
    Tp = T + HEAD_ROWS
    tt = 512
    assert T % tt == 0
    return pl.pallas_call(
        _final_norm_kernel,
        grid=(B, T // tt),
        in_specs=[pl.BlockSpec((pl.Element(tt), pl.Element(D)),
                               lambda b, j: (pl.multiple_of(b * Tp + HEAD_ROWS + j * tt, CHUNK), 0)),
                  pl.BlockSpec((1, D), lambda b, j: (0, 0))],
        out_specs=pl.BlockSpec((1, tt, D), lambda b, j: (b, j, 0)),
        out_shape=jax.ShapeDtypeStruct((B, T, D), F32),
        compiler_params=_cparams("parallel", "parallel"),
    )(h, w.reshape(1, D))


def _matmul_kernel(x_ref, w_ref, o_ref, *, act):
    acc = _dot(x_ref[...], w_ref[...])
    if act == "gelu":
        acc = _gelu_tanh(acc)
    o_ref[...] = acc.astype(o_ref.dtype)


def _matmul(x, w, out_dtype, act=None):
    M, K = x.shape
    N = w.shape[1]
    tm = _pick_tile(M, 1024, 128)
    tn = _pick_tile(N, 512, 128)
    return pl.pallas_call(
        functools.partial(_matmul_kernel, act=act),
        grid=(M // tm, N // tn),
        in_specs=[pl.BlockSpec((tm, K), lambda i, j: (i, 0)), pl.BlockSpec((K, tn), lambda i, j: (0, j))],
        out_specs=pl.BlockSpec((tm, tn), lambda i, j: (i, j)),
        out_shape=jax.ShapeDtypeStruct((M, N), out_dtype),
        compiler_params=_cparams("parallel", "parallel"),
    )(x, w)


def _proj_res_kernel(x_ref, w_ref, h_ref, nw_ref, hout_ref, hn_ref):
    hnew = h_ref[...] + _dot(x_ref[...], w_ref[...])
    hout_ref[...] = hnew
    hn_ref[...] = _rms_scale(hnew, nw_ref[...]).astype(hn_ref.dtype)


def _proj_res(x, w, h, nw):
    M, K = x.shape
    D = w.shape[1]
    tm = _pick_tile(M, 640 if K <= D_MODEL else 512, 64)
    return pl.pallas_call(
        _proj_res_kernel,
        grid=(M // tm,),
        in_specs=[pl.BlockSpec((tm, K), lambda i: (i, 0)),
                  pl.BlockSpec((K, D), lambda i: (0, 0), pipeline_mode=pl.Buffered(1)),
                  pl.BlockSpec((tm, D), lambda i: (i, 0)),
                  pl.BlockSpec((1, D), lambda i: (0, 0))],
        out_specs=[pl.BlockSpec((tm, D), lambda i: (i, 0)), pl.BlockSpec((tm, D), lambda i: (i, 0))],
        out_shape=[jax.ShapeDtypeStruct((M, D), F32), jax.ShapeDtypeStruct((M, D), BF16)],
        compiler_params=_cparams("parallel"),
    )(x, w, h, nw.reshape(1, D))


def _proj_conv_kernel(xp_ref, xm_ref, xn_ref, w_ref, cw_ref, o_ref, xext_ref, *, tm, tc, ni, Tp, n_norm):
    i = pl.program_id(0)
    j = pl.program_id(1)
    npc = o_ref.shape[1] // tc

    @pl.when(j == 0)
    def _():
        xext_ref[0:HALO, :] = xp_ref[...]
        xext_ref[HALO:HALO + tm, :] = xm_ref[...]
        xext_ref[HALO + tm:2 * HALO + tm, :] = xn_ref[...]

    @pl.when(jnp.logical_and(j == 0, i == ni - 1))
    def _():
        xext_ref[HALO + tm:2 * HALO + tm, :] = jnp.zeros((HALO, xext_ref.shape[1]), xext_ref.dtype)

    rel = i * tm - (i * tm // Tp) * Tp + lax.broadcasted_iota(jnp.int32, (tm, 1), 0)
    live = jnp.where(rel >= Tp, rel - Tp, rel) >= PAD
    cw = cw_ref[...]
    xe = xext_ref[...]

    def piece_matmul(c):
        return _dot(xe, w_ref[:, c * tc:(c + 1) * tc])

    def piece_finish(c, g):
        cwc = cw[:, c * tc:(c + 1) * tc]
        a = (cwc[0:1] * g[HALO - 1:HALO - 1 + tm] + cwc[1:2] * g[HALO:HALO + tm]
             + cwc[2:3] * g[HALO + 1:HALO + 1 + tm] + cwc[3:4] * g[HALO + 2:HALO + 2 + tm])
        y = jnp.where(live, a * _sigmoid(a), 0.0)
        normalise = j * npc + c < n_norm
        heads = []
        for hh in range(tc // GDN_DK):
            yh = y[:, hh * GDN_DK:(hh + 1) * GDN_DK]
            nrm = lax.rsqrt(jnp.sum(yh * yh, axis=-1, keepdims=True) + 1e-6)
            heads.append(yh * jnp.where(normalise, nrm, 1.0))
        o_ref[:, c * tc:(c + 1) * tc] = jnp.concatenate(heads, axis=1).astype(o_ref.dtype)

    g = piece_matmul(0)
    for c in range(npc):
        g_next = piece_matmul(c + 1) if c + 1 < npc else None
        piece_finish(c, g)
        g = g_next


def _proj_conv(x, w, conv_w, Tp, n_norm_cols):
    M, K = x.shape
    N = w.shape[1]
    tm = _pick_tile(M, 1024, 64)
    assert tm <= Tp
    tn, tc = 2048, 512
    ni = M // tm
    hb = tm // HALO
    nhb = M // HALO
    return pl.pallas_call(
        functools.partial(_proj_conv_kernel, tm=tm, tc=tc, ni=ni, Tp=Tp, n_norm=n_norm_cols // tc),
        grid=(ni, N // tn),
        in_specs=[pl.BlockSpec((HALO, K), lambda i, j: (jnp.maximum(i * hb - 1, 0), 0)),
                  pl.BlockSpec((tm, K), lambda i, j: (i, 0)),
                  pl.BlockSpec((HALO, K), lambda i, j: (jnp.minimum((i + 1) * hb, nhb - 1), 0)),
                  pl.BlockSpec((K, tn), lambda i, j: (0, j)),
                  pl.BlockSpec((4, tn), lambda i, j: (0, j))],
        out_specs=pl.BlockSpec((tm, tn), lambda i, j: (i, j)),
        out_shape=jax.ShapeDtypeStruct((M, N), BF16),
        scratch_shapes=[pltpu.VMEM((tm + 2 * HALO, K), BF16)],
        compiler_params=_cparams("parallel", "arbitrary"),
    )(x, x, x, w, conv_w)


def _ffn_kernel(xp_ref, xm_ref, xn_ref, wg_ref, wu_ref, cw_ref, cb_ref, wd_ref, h_ref, nw_ref,
                hout_ref, hn_ref, xext_ref, *, tm, nf, ni):
    i = pl.program_id(0)
    f = pl.program_id(1)
    th = tm // 2

    @pl.when(f == 0)
    def _():
        xext_ref[0:HALO, :] = xp_ref[...]
        xext_ref[HALO:HALO + tm, :] = xm_ref[...]
        xext_ref[HALO + tm:2 * HALO + tm, :] = xn_ref[...]
        hout_ref[...] = h_ref[...]

    cw = cw_ref[...]
    cb = cb_ref[...]
    g, up = [], []
    for s in range(2):
        g.append(_dot(xext_ref[s * th:s * th + th + 2 * HALO, :], wg_ref[...]))
        up.append(_dot(xm_ref[s * th:(s + 1) * th, :], wu_ref[...]))
    row = lax.broadcasted_iota(jnp.int32, (th, 1), 0)
    y = []
    for s in range(2):
        g_prev = g[s][HALO - 1:HALO - 1 + th]
        g_here = g[s][HALO:HALO + th]
        g_next = g[s][HALO + 1:HALO + 1 + th]
        if s == 1:
            g_next = jnp.where(jnp.logical_and(i == ni - 1, row == th - 1), 0.0, g_next)
        a = cw[0:1] * g_prev + cw[1:2] * g_here + cw[2:3] * g_next + cb
        y.append((a * _sigmoid(a) * up[s]).astype(BF16))
    down = [_dot(y[s], wd_ref[...]) for s in range(2)]
    for s in range(2):
        hout_ref[s * th:(s + 1) * th, :] += down[s]

    @pl.when(f == nf - 1)
    def _():
        hn_ref[...] = _rms_scale(hout_ref[...], nw_ref[...]).astype(hn_ref.dtype)


def _conv_ffn(hn, h, w_gate, w_up, conv_w, conv_b, w_down, nw):
    M, D = hn.shape
    F = w_gate.shape[1]
    tm = _pick_tile(M, 640, 64)
    tf = _pick_tile(F, 512, 128)
    ni, nf = M // tm, F // tf
    hb = tm // HALO
    nhb = M // HALO
    return pl.pallas_call(
        functools.partial(_ffn_kernel, tm=tm, nf=nf, ni=ni),
        grid=(ni, nf),
        in_specs=[pl.BlockSpec((HALO, D), lambda i, f: (jnp.maximum(i * hb - 1, 0), 0)),
                  pl.BlockSpec((tm, D), lambda i, f: (i, 0)),
                  pl.BlockSpec((HALO, D), lambda i, f: (jnp.minimum((i + 1) * hb, nhb - 1), 0)),
                  pl.BlockSpec((D, tf), lambda i, f: (0, f)),
                  pl.BlockSpec((D, tf), lambda i, f: (0, f)),
                  pl.BlockSpec((3, tf), lambda i, f: (0, f)),
                  pl.BlockSpec((1, tf), lambda i, f: (0, f)),
                  pl.BlockSpec((tf, D), lambda i, f: (f, 0)),
                  pl.BlockSpec((tm, D), lambda i, f: (i, 0)),
                  pl.BlockSpec((1, D), lambda i, f: (0, 0))],
        out_specs=[pl.BlockSpec((tm, D), lambda i, f: (i, 0)), pl.BlockSpec((tm, D), lambda i, f: (i, 0))],
        out_shape=[jax.ShapeDtypeStruct((M, D), F32), jax.ShapeDtypeStruct((M, D), BF16)],
        scratch_shapes=[pltpu.VMEM((tm + 2 * HALO, D), BF16)],
        compiler_params=_cparams("parallel", "arbitrary"),
    )(hn, hn, hn, w_gate, w_up, conv_w, conv_b.reshape(1, F), w_down, h, nw.reshape(1, D))


def _scan8(a, u, row, reverse):
    for s in (1, 2, 4):
        if reverse:
            a_sh = pltpu.roll(a, 8 - s, 0)
            u_sh = pltpu.roll(u, 8 - s, 0)
            m = row < 8 - s
        else:
            a_sh = pltpu.roll(a, s, 0)
            u_sh = pltpu.roll(u, s, 0)
            m = row >= s
        u = jnp.where(m, a * u_sh + u, u)
        a = jnp.where(m, a * a_sh, a)
    return a, u


def _rg_kernel(rec_ref, gate_ref, cw_ref, cb_ref, wa_ref, ba_ref, wi_ref, bi_ref, lam_ref, o_ref,
               xc_s, hs_s, *, Tp, rc):
    nch = Tp // rc
    W = RG_BLOCK
    cw = cw_ref[...]
    cb = cb_ref[...]
    row8 = lax.broadcasted_iota(jnp.int32, (8, W), 0)

    def gates(xc, d):
        xb = xc.astype(BF16)
        r = _sigmoid(_dot(xb, wa_ref[d, 0]) + ba_ref[d])
        ig = _sigmoid(_dot(xb, wi_ref[d, 0]) + bi_ref[d])
        log_a = (-RG_C) * r * _softplus(-lam_ref[d])
        a = jnp.exp(log_a)
        u = jnp.sqrt(1.0 - a * a) * (ig * xc)
        return a, u

    def fwd_chunk(c, carry):
        r0 = pl.multiple_of(c * rc, CHUNK)
        top = rec_ref[0, pl.ds(pl.multiple_of(jnp.maximum(r0 - 8, 0), 8), 8), :]
        main = rec_ref[0, pl.ds(r0, rc), :]
        bot = rec_ref[0, pl.ds(pl.multiple_of(jnp.minimum(r0 + rc, Tp - 8), 8), 8), :]
        bot = jnp.where(r0 + rc < Tp, bot, 0.0)
        xw = jnp.concatenate([top, main, bot], axis=0)
        xc = cw[0:1] * xw[7:7 + rc] + cw[1:2] * xw[8:8 + rc] + cw[2:3] * xw[9:9 + rc] + cw[3:4] * xw[10:10 + rc] + cb
        t = r0 + lax.broadcasted_iota(jnp.int32, (rc, 1), 0)
        xc = jnp.where(t >= PAD, xc, 0.0)
        xc_s[pl.ds(r0, rc), :] = xc
        a, u = gates(xc, 0)
        hs = []
        for k in range(rc // 8):
            A, U = _scan8(a[8 * k:8 * k + 8], u[8 * k:8 * k + 8], row8, False)
            h = A * carry + U
            carry = h[7:8]
            hs.append(h)
        hs_s[pl.ds(r0, rc), :] = jnp.concatenate(hs, axis=0)
        return carry

    lax.fori_loop(0, nch, fwd_chunk, jnp.zeros((1, W), F32))

    def bwd_chunk(cc, carry):
        r0 = pl.multiple_of((nch - 1 - cc) * rc, CHUNK)
        a, u = gates(xc_s[pl.ds(r0, rc), :], 1)
        hs = [None] * (rc // 8)
        for k in reversed(range(rc // 8)):
            A, U = _scan8(a[8 * k:8 * k + 8], u[8 * k:8 * k + 8], row8, True)
            h = A * carry + U
            carry = h[0:1]
            hs[k] = h
        hsum = hs_s[pl.ds(r0, rc), :] + jnp.concatenate(hs, axis=0)
        o_ref[0, pl.ds(r0, rc), :] = (gate_ref[0, pl.ds(r0, rc), :].astype(F32) * hsum).astype(o_ref.dtype)
        return carry

    lax.fori_loop(0, nch, bwd_chunk, jnp.zeros((1, W), F32))


def _rg_scan(rec, gate, conv_w, conv_b, w_a, b_a, w_i, b_i, lam):
    B, Tp, W = rec.shape
    nb = W // RG_BLOCK
    rc = _pick_tile(Tp, 384, 64)
    blk = lambda b, n: (b, 0, n)
    vec = lambda b, n: (0, n)
    vec3 = lambda b, n: (0, 0, n)
    return pl.pallas_call(
        functools.partial(_rg_kernel, Tp=Tp, rc=rc),
        grid=(B, nb),
        in_specs=[pl.BlockSpec((1, Tp, RG_BLOCK), blk),
                  pl.BlockSpec((1, Tp, RG_BLOCK), blk),
                  pl.BlockSpec((4, RG_BLOCK), vec),
                  pl.BlockSpec((1, RG_BLOCK), vec),
                  pl.BlockSpec((2, 1, RG_BLOCK, RG_BLOCK), lambda b, n: (0, n, 0, 0)),
                  pl.BlockSpec((2, 1, RG_BLOCK), vec3),
                  pl.BlockSpec((2, 1, RG_BLOCK, RG_BLOCK), lambda b, n: (0, n, 0, 0)),
                  pl.BlockSpec((2, 1, RG_BLOCK), vec3),
                  pl.BlockSpec((2, 1, RG_BLOCK), vec3)],
        out_specs=pl.BlockSpec((1, Tp, RG_BLOCK), blk),
        out_shape=jax.ShapeDtypeStruct((B, Tp, W), BF16),
        scratch_shapes=[pltpu.VMEM((Tp, RG_BLOCK), F32), pltpu.VMEM((Tp, RG_BLOCK), F32)],
        compiler_params=_cparams("parallel", "parallel"),
    )(rec, gate, conv_w, conv_b.reshape(1, W), w_a.astype(BF16), b_a.reshape(2, 1, W),
      w_i.astype(BF16), b_i.reshape(2, 1, W), lam.reshape(2, 1, W))


def _rglru_mixer(hn, h, B, Tp, w_in, conv_w, conv_b, w_a, b_a, w_i, b_i, lam, w_out, nw):
    W = D_MODEL
    w_in = w_in.astype(BF16)
    gate = _matmul(hn, w_in[:, :W], BF16, act="gelu")
    rec = _matmul(hn, w_in[:, W:], F32)
    y = _rg_scan(rec.reshape(B, Tp, W), gate.reshape(B, Tp, W), conv_w, conv_b, w_a, b_a, w_i, b_i, lam)
    return _proj_res(y.reshape(B * Tp, W), w_out.astype(BF16), h, nw)


def _na_kernel(q_ref, k_ref, v_ref, bias_ref, mb_ref, o_ref, *, rows):
    scale = NA_HEAD_DIM ** -0.5
    kh = NA_WIN_R
    mb = mb_ref[0]
    km = k_ref[0, PAD:HEAD_ROWS, :]
    vm = v_ref[0, PAD:HEAD_ROWS, :]

    o_ref[0, 0:PAD, :] = jnp.zeros((PAD, NA_HEAD_DIM), o_ref.dtype)
    qm = q_ref[0, PAD:HEAD_ROWS, :]
    s_m = _dot_nt(qm, km) * scale + mb
    p_m = jnp.exp(s_m - jnp.max(s_m, axis=-1, keepdims=True))
    o_m = _dot(p_m.astype(BF16), vm) / jnp.sum(p_m, axis=-1, keepdims=True)
    o_ref[0, PAD:HEAD_ROWS, :] = o_m.astype(o_ref.dtype)

    group = np.gcd(rows, NA_ROW_GROUP)
    R = range(group)

    def group_body(i, _):
        r = [i * group + c for c in R]
        rs = [jnp.clip(r[c] - NA_WIN_R // 2, 0, rows - kh) for c in R]
        q0 = [pl.multiple_of(HEAD_ROWS + r[c] * GRID_W, GRID_W) for c in R]
        k0 = [pl.multiple_of(HEAD_ROWS + rs[c] * GRID_W, GRID_W) for c in R]
        q_r = [q_ref[0, pl.ds(q0[c], GRID_W), :] for c in R]
        s = [_dot_nt(q_r[c], k_ref[0, pl.ds(k0[c], kh * GRID_W), :]) for c in R]
        s_met = [_dot_nt(q_r[c], km) for c in R]
        p, p_met, l = [], [], []
        for c in R:
            sc = s[c] * scale + bias_ref[0, r[c] - rs[c]]
            sm = s_met[c] * scale + mb
            m = jnp.maximum(jnp.max(sc, axis=-1, keepdims=True), jnp.max(sm, axis=-1, keepdims=True))
            pc = jnp.exp(sc - m)
            pm = jnp.exp(sm - m)
            l.append(jnp.sum(pc, axis=-1, keepdims=True) + jnp.sum(pm, axis=-1, keepdims=True))
            p.append(pc.astype(BF16))
            p_met.append(pm.astype(BF16))
        o = [_dot(p[c], v_ref[0, pl.ds(k0[c], kh * GRID_W), :]) + _dot(p_met[c], vm) for c in R]
        for c in R:
            o_ref[0, pl.ds(q0[c], GRID_W), :] = (o[c] / l[c]).astype(o_ref.dtype)
        return 0

    lax.fori_loop(0, rows // group, group_body, 0)


def _na_bias_table(rpb):
    kh = NA_WIN_R
    cols = np.arange(GRID_W)
    col_start = np.clip(cols - NA_WIN_C // 2, 0, GRID_W - NA_WIN_C)
    valid = (cols[None, :] >= col_start[:, None]) & (cols[None, :] < col_start[:, None] + NA_WIN_C)
    col_off = np.clip(cols[None, :] - cols[:, None], -(NA_WIN_C - 1), NA_WIN_C - 1) + NA_WIN_C - 1
    H, n_ro, n_co = rpb.shape
    onehot = (col_off[None] == np.arange(n_co)[:, None, None]).astype(np.float32)
    a = jnp.einsum('hrc,cqk->hqrk', rpb.astype(F32), jnp.asarray(onehot), precision=lax.Precision.HIGHEST)
    a = jnp.where(jnp.asarray(valid)[None, :, None, :], a, NEG_INF)
    tabs = [a[:, :, NA_WIN_R - 1 - d:NA_WIN_R - 1 - d + kh, :].reshape(H, GRID_W, kh * GRID_W)
            for d in range(NA_WIN_R)]
    return jnp.stack(tabs, axis=1)


def _na_attention(qkv, bias, meta_bias):
    B, Tp, _ = qkv.shape
    rows = (Tp - HEAD_ROWS) // GRID_W
    assert rows >= NA_WIN_R
    H = NA_HEADS
    return pl.pallas_call(
        functools.partial(_na_kernel, rows=rows),
        grid=(B, H),
        in_specs=[pl.BlockSpec((1, Tp, NA_HEAD_DIM), lambda b, h: (b, 0, h)),
                  pl.BlockSpec((1, Tp, NA_HEAD_DIM), lambda b, h: (b, 0, H + h)),
                  pl.BlockSpec((1, Tp, NA_HEAD_DIM), lambda b, h: (b, 0, 2 * H + h)),
                  pl.BlockSpec((1, NA_WIN_R, GRID_W, NA_WIN_R * GRID_W), lambda b, h: (h, 0, 0, 0)),
                  pl.BlockSpec((1, 1, N_META), lambda b, h: (h, 0, 0))],
        out_specs=pl.BlockSpec((1, Tp, NA_HEAD_DIM), lambda b, h: (b, 0, h)),
        out_shape=jax.ShapeDtypeStruct((B, Tp, H * NA_HEAD_DIM), BF16),
        compiler_params=_cparams("parallel", "parallel"),
    )(qkv, qkv, qkv, bias, meta_bias.astype(F32).reshape(H, 1, N_META))


def _na_mixer(hn, h, B, Tp, w_qkv, bias, meta_bias, w_o, nw):
    qkv = _matmul(hn, w_qkv.astype(BF16), BF16)
    o = _na_attention(qkv.reshape(B, Tp, -1), bias, meta_bias)
    return _proj_res(o.reshape(B * Tp, -1), w_o.astype(BF16), h, nw)


def _gdn_gate_kernel(x_ref, w_ref, al_ref, dt_ref, o_ref):
    y = _dot(x_ref[...], w_ref[...])
    lane = lax.broadcasted_iota(jnp.int32, y.shape, 1)
    g = -jnp.exp(al_ref[...]) * _softplus(y + dt_ref[...])
    o_ref[...] = jnp.where(lane < 2 * GDN_V_HEADS, _sigmoid(y), g)


def _gdn_gates(hn, w_ba, a_log, dt_bias):
    M, K = hn.shape
    N = 4 * GDN_V_HEADS
    tm = _pick_tile(M, 1024, 128)
    zeros = jnp.zeros((2 * GDN_V_HEADS,), F32)
    al = jnp.concatenate([zeros, a_log.astype(F32).reshape(-1)]).reshape(1, N)
    dt = jnp.concatenate([zeros, dt_bias.astype(F32).reshape(-1)]).reshape(1, N)
    return pl.pallas_call(
        _gdn_gate_kernel,
        grid=(M // tm,),
        in_specs=[pl.BlockSpec((tm, K), lambda i: (i, 0)), pl.BlockSpec((K, N), lambda i: (0, 0)),
                  pl.BlockSpec((1, N), lambda i: (0, 0)), pl.BlockSpec((1, N), lambda i: (0, 0))],
        out_specs=pl.BlockSpec((tm, N), lambda i: (i, 0)),
        out_shape=jax.ShapeDtypeStruct((M, N), F32),
        compiler_params=_cparams("parallel"),
    )(hn, w_ba, al, dt)


def _split3_dot(x, m):
    hi = x.astype(BF16)
    r1 = x - hi.astype(F32)
    mid = r1.astype(BF16)
    lo = (r1 - mid.astype(F32)).astype(BF16)
    return _dot(hi, m) + _dot(mid, m) + _dot(lo, m)


def _gdn_kernel(q_ref, k_ref, v_ref, z_ref, bgc_ref, bgr_ref, nw_ref, o_ref,
                S_s, oacc_s, u_s, wq_s, kst_s, qkd_s, gam_s, *, Tp, HPS):
    hb = pl.program_id(1)
    C = CHUNK
    NC = Tp // C
    HV = GDN_V_HEADS
    DK = GDN_DK
    G = 4
    W = G * C
    P1_GROUP = 6 // HPS
    scale = DK ** -0.5

    oacc_s[...] = jnp.zeros_like(oacc_s)
    S_s[...] = jnp.zeros_like(S_s)

    ri = lax.broadcasted_iota(jnp.int32, (C, W), 0)
    cl = lax.broadcasted_iota(jnp.int32, (C, W), 1)
    cj = jnp.bitwise_and(cl, C - 1)
    blk = jnp.right_shift(cl, 6)
    dd = jnp.where(blk >= 2, cj - ri, ri - cj)
    incl4 = dd >= 0
    strict4 = dd > 0
    eye4 = (ri == cj).astype(F32)
    cum4 = jnp.where(dd <= 0, 1.0, 0.0).astype(BF16)
    bdm = jnp.where(jnp.right_shift(lax.broadcasted_iota(jnp.int32, (W, W), 0), 6)
                    == jnp.right_shift(lax.broadcasted_iota(jnp.int32, (W, W), 1), 6), 1.0, 0.0).astype(BF16)
    blk_row = jnp.right_shift(lax.broadcasted_iota(jnp.int32, (1, W), 1), 6)
    lane = lax.broadcasted_iota(jnp.int32, (C, 4 * HV), 1)
    rowi = lax.broadcasted_iota(jnp.int32, (C, 4 * HV), 0)

    def col(x, c):
        return jnp.sum(jnp.where(lane == c, x, 0.0), axis=1, keepdims=True)

    def by_block(parts, b):
        return jnp.where(b == 0, parts[0], jnp.where(b == 1, parts[1], jnp.where(b == 2, parts[2], parts[3])))

    def blockdiag(xb):
        return jnp.concatenate([xb] * G, axis=0) * bdm

    def phase1(ns):
        items = [(hd, n) for n in ns for hd in range(HPS)]
        K = range(len(items))
        r0 = [pl.multiple_of(n * C, C) for (hd, n) in items]
        q = [q_ref[0, pl.ds(r0[i], C), items[i][0] * DK:(items[i][0] + 1) * DK] for i in K]
        k = [k_ref[0, pl.ds(r0[i], C), items[i][0] * DK:(items[i][0] + 1) * DK] for i in K]
        bg = [bgc_ref[0, pl.ds(r0[i], C), :] for i in K]
        v = [v_ref[0, pl.ds(r0[i], C), items[i][0] * 2 * DK:(items[i][0] + 1) * 2 * DK].astype(F32) for i in K]
        rows = [bgr_ref[0, n, hd] for (hd, n) in items]
        cum_rows = [_split3_dot(rows[i], cum4) for i in K]
        k4 = [jnp.concatenate([k[i]] * G, axis=0) for i in K]
        kk4 = [_dot_nt(k[i], k4[i]) for i in K]
        qk4 = [_dot_nt(q[i], k4[i]) * scale for i in K]
        beta_c, gc_c, decay4, L = [], [], [], []
        for i in K:
            pre = bg[i]
            for s in (1, 2, 4, 8, 16, 32):
                pre = pre + jnp.where(rowi >= s, pltpu.roll(pre, s, 0), 0.0)
            suf = pre[C - 1:C] - pre + bg[i]
            gc_r = by_block([cum_rows[i][4 + g:5 + g] for g in range(G)], blk_row)
            bc, gc = [], []
            for g in range(G):
                d, j = g // 2, g % 2
                vh = 2 * (hb * HPS + items[i][0]) + j
                bc.append(col(bg[i], d * HV + vh))
                gc.append(col(suf if d else pre, 2 * HV + d * HV + vh))
            beta_c.append(bc)
            gc_c.append(gc)
            dec = jnp.where(incl4, jnp.exp(jnp.where(incl4, by_block(gc, blk) - gc_r, 0.0)), 0.0)
            decay4.append(dec)
            L.append(jnp.where(strict4, kk4[i] * dec, 0.0) * by_block(bc, blk))
        Tm = [eye4 - L[i] for i in K]
        Lb = [L[i].astype(BF16) for i in K]
        P = [_dot(Lb[i], blockdiag(Lb[i])) for i in K]
        for lvl in range(5):
            Pb = [P[i].astype(BF16) for i in K]
            bd = [blockdiag(Pb[i]) for i in K]
            if lvl < 4:
                tp = [_dot(jnp.concatenate([Tm[i].astype(BF16), Pb[i]], axis=0), bd[i]) for i in K]
                Tm = [Tm[i] + tp[i][:C] for i in K]
                P = [tp[i][C:] for i in K]
            else:
                tp = [_dot(Tm[i].astype(BF16), bd[i]) for i in K]
                Tm = [Tm[i] + tp[i] for i in K]
        rhs, kst, qst, gam = [], [], [], []
        for i in K:
            qf = q[i].astype(F32)
            kf = k[i].astype(F32)
            rhs_i, kst_i, qst_i, gam_i = [], [], [], []
            for g in range(G):
                j = g % 2
                bc, gc = beta_c[i][g], gc_c[i][g]
                e_c = jnp.exp(gc)
                rhs_i.append(jnp.concatenate([v[i][:, j * DK:(j + 1) * DK] * bc, kf * (bc * e_c)], axis=1))
                g_last = gc[0:1] if g >= 2 else gc[C - 1:C]
                kst_i.append((kf * jnp.exp(g_last - gc)).astype(BF16))
                qst_i.append((qf * (scale * e_c)).astype(BF16))
                gam_i.append(jnp.broadcast_to(jnp.exp(g_last), (1, DK)))
            rhs.append(jnp.concatenate(rhs_i, axis=0).astype(BF16))
            kst.append(kst_i)
            qst.append(qst_i)
            gam.append(gam_i)
        sol = [_dot(blockdiag(Tm[i].astype(BF16)), rhs[i]) for i in K]
        for i in K:
            hd, n = items[i]
            u_s[hd, n] = sol[i][:, :DK]
            qkd_s[hd, n] = jnp.where(incl4, qk4[i] * decay4[i], 0.0).astype(BF16)
            for g in range(G):
                wq_s[hd, n, g, 0:C, :] = sol[i][g * C:(g + 1) * C, DK:].astype(BF16)
                wq_s[hd, n, g, C:2 * C, :] = qst[i][g]
                kst_s[hd, n, g] = kst[i][g]
                gam_s[hd, n, g:g + 1, :] = gam[i][g]

    def phase1_group(i, _):
        phase1([P1_GROUP * i + c for c in range(P1_GROUP)])
        return 0

    lax.fori_loop(0, NC // P1_GROUP, phase1_group, 0)
    if NC % P1_GROUP:
        phase1(list(range(NC - NC % P1_GROUP, NC)))

    nw = nw_ref[...]

    def gated_norm(o, z):
        ys = []
        for j in range(2 * HPS):
            oj = o[:, j * DK:(j + 1) * DK]
            zj = z[:, j * DK:(j + 1) * DK].astype(F32)
            ys.append(oj * lax.rsqrt(jnp.mean(oj * oj, axis=-1, keepdims=True) + NORM_EPS) * nw * (zj * _sigmoid(zj)))
        return jnp.concatenate(ys, axis=1)

    def phase2(it, finalize):
        ops = []
        for hd in range(HPS):
            for g in range(G):
                n = it if g < 2 else NC - 1 - it
                ops.append((S_s[hd * G + g], wq_s[hd, n, g], u_s[hd, n, g * C:(g + 1) * C, :],
                            qkd_s[hd, n, :, g * C:(g + 1) * C], gam_s[hd, n, g:g + 1, :], kst_s[hd, n, g]))
        NI = HPS * G
        r_f = pl.multiple_of(it * C, C)
        r_b = pl.multiple_of((NC - 1 - it) * C, C)
        o_f = oacc_s[pl.ds(r_f, C), :]
        ws = [_dot(wq, S.astype(BF16)) for (S, wq, u, qkd, gam, kst) in ops]
        vb = [(ops[g][2] - ws[g][:C]).astype(BF16) for g in range(NI)]
        outs = [ws[g][C:] + _dot(ops[g][3], vb[g]) for g in range(NI)]
        states = [ops[g][0] * ops[g][4] + _dot_tn(ops[g][5], vb[g]) for g in range(NI)]
        for g in range(NI):
            S_s[g] = states[g]
        fwd = [outs[hd * G + j] for hd in range(HPS) for j in range(2)]
        bwd = [outs[hd * G + 2 + j] for hd in range(HPS) for j in range(2)]
        o_f = o_f + jnp.concatenate(fwd, axis=1)
        oacc_s[pl.ds(r_f, C), :] = o_f
        o_b = oacc_s[pl.ds(r_b, C), :] + jnp.concatenate(bwd, axis=1)
        oacc_s[pl.ds(r_b, C), :] = o_b
        if finalize:
            o_ref[0, pl.ds(r_f, C), :] = gated_norm(o_f, z_ref[0, pl.ds(r_f, C), :]).astype(o_ref.dtype)
            o_ref[0, pl.ds(r_b, C), :] = gated_norm(o_b, z_ref[0, pl.ds(r_b, C), :]).astype(o_ref.dtype)
        return 0

    lax.fori_loop(0, NC // 2, lambda it, c: phase2(it, False), 0)
    lax.fori_loop(NC // 2, NC, lambda it, c: phase2(it, True), 0)


_GDN_CHUNK_BYTES = CHUNK * GDN_DK * (4 * 4 + 8 * 2 + 4 * 2 + 4 * 2) + 8 * GDN_DK * 4
_GDN_PAIR_BUDGET = 26 * 1024 * 1024


def _gdn_delta(qkv, z, bg, norm_w):
    B, Tp, _ = qkv.shape
    NC = Tp // CHUNK
    KH = GDN_QK_HEADS
    HV = GDN_V_HEADS
    C = CHUNK
    HPS = 2 if NC * _GDN_CHUNK_BYTES * 2 <= _GDN_PAIR_BUDGET else 1
    NH = KH // HPS
    bgr = bg.reshape(B, NC, CHUNK, 4, KH, 2).transpose(0, 1, 4, 3, 5, 2).reshape(B, NC, KH, 8, CHUNK)
    return pl.pallas_call(
        functools.partial(_gdn_kernel, Tp=Tp, HPS=HPS),
        grid=(B, NH),
        in_specs=[pl.BlockSpec((1, Tp, HPS * GDN_DK), lambda b, h: (b, 0, h)),
                  pl.BlockSpec((1, Tp, HPS * GDN_DK), lambda b, h: (b, 0, NH + h)),
                  pl.BlockSpec((1, Tp, 2 * HPS * GDN_DK), lambda b, h: (b, 0, NH + h)),
                  pl.BlockSpec((1, Tp, 2 * HPS * GDN_DK), lambda b, h: (b, 0, h)),
                  pl.BlockSpec((1, Tp, 4 * HV), lambda b, h: (b, 0, 0)),
                  pl.BlockSpec((1, NC, HPS, 8, CHUNK), lambda b, h: (b, 0, h, 0, 0)),
                  pl.BlockSpec((1, GDN_DK), lambda b, h: (0, 0))],
        out_specs=pl.BlockSpec((1, Tp, 2 * HPS * GDN_DK), lambda b, h: (b, 0, h)),
        out_shape=jax.ShapeDtypeStruct((B, Tp, GDN_VAL_DIM), BF16),
        scratch_shapes=[pltpu.VMEM((4 * HPS, GDN_DK, GDN_DK), F32),
                        pltpu.VMEM((Tp, 2 * HPS * GDN_DK), F32),
                        pltpu.VMEM((HPS, NC, 4 * C, GDN_DK), F32),
                        pltpu.VMEM((HPS, NC, 4, 2 * C, GDN_DK), BF16),
                        pltpu.VMEM((HPS, NC, 4, C, GDN_DK), BF16),
                        pltpu.VMEM((HPS, NC, C, 4 * C), BF16),
                        pltpu.VMEM((HPS, NC, 8, GDN_DK), F32)],
        compiler_params=_cparams("parallel", "parallel"),
    )(qkv, qkv, qkv, z, bg, bgr, norm_w.astype(F32).reshape(1, GDN_DK))


def _gdn_mixer(hn, h, B, Tp, w_in, conv_w, a_log, dt_bias, norm_w, w_out, nw):
    nz = GDN_CONV_DIM + GDN_VAL_DIM
    w_in = w_in.astype(BF16)
    qkv = _proj_conv(hn, w_in[:, :GDN_CONV_DIM], conv_w, Tp, 2 * GDN_KEY_DIM)
    z = _matmul(hn, w_in[:, GDN_CONV_DIM:nz], BF16)
    bg = _gdn_gates(hn, w_in[:, nz:], a_log, dt_bias)
    o = _gdn_delta(qkv.reshape(B, Tp, -1), z.reshape(B, Tp, -1), bg.reshape(B, Tp, -1), norm_w)
    return _proj_res(o.reshape(B * Tp, -1), w_out.astype(BF16), h, nw)


def _trunk(x, meta_tokens, mix_norm, ffn_norm, final_norm, rg, na, gdn, ffn):
    B, T, D = x.shape
    Tp = T + HEAD_ROWS
    depth = mix_norm.shape[0]
    h3 = jnp.concatenate([jnp.zeros((B, PAD, D), F32),
                          jnp.broadcast_to(meta_tokens.astype(F32)[None], (B, N_META, D)),
                          x.astype(F32)], axis=1)
    h = h3.reshape(B * Tp, D)
    hn = _rmsnorm(h, mix_norm[0], BF16)
    for i in range(depth):
        kind, j = i % 3, i // 3
        if kind == 0:
            h, hn = _rglru_mixer(hn, h, B, Tp, *[p[j] for p in rg], ffn_norm[i])
        elif kind == 1:
            h, hn = _na_mixer(hn, h, B, Tp, *[p[j] for p in na], ffn_norm[i])
        else:
            h, hn = _gdn_mixer(hn, h, B, Tp, *[p[j] for p in gdn], ffn_norm[i])
        w_gate, w_up, conv_w, conv_b, w_down = [p[i] for p in ffn]
        next_norm = mix_norm[i + 1] if i + 1 < depth else final_norm
        h, hn = _conv_ffn(hn, h, w_gate.astype(BF16), w_up.astype(BF16), conv_w, conv_b,
                          w_down.astype(BF16), next_norm)
    return _final_norm(h, final_norm, B, T)


def kernel(x_prompt, x_sample, meta_tokens, mix_norm, ffn_norm, final_norm, rg_w_in, rg_conv_w, rg_conv_b, rg_w_a, rg_b_a, rg_w_i, rg_b_i, rg_lam, rg_w_out, na_w_qkv, na_rpb, na_meta_bias, na_w_o, gdn_w_in, gdn_conv_w, gdn_a_log, gdn_dt_bias, gdn_norm_w, gdn_w_out, ffn_w_gate, ffn_w_up, ffn_conv_w, ffn_conv_b, ffn_w_down):
    rg = (rg_w_in, rg_conv_w, rg_conv_b, rg_w_a, rg_b_a, rg_w_i, rg_b_i, rg_lam, rg_w_out)
    na_bias = jnp.stack([_na_bias_table(na_rpb[j]) for j in range(na_rpb.shape[0])])
    na = (na_w_qkv, na_bias, na_meta_bias, na_w_o)
    gdn = (gdn_w_in, gdn_conv_w, gdn_a_log, gdn_dt_bias, gdn_norm_w, gdn_w_out)
    ffn = (ffn_w_gate, ffn_w_up, ffn_conv_w, ffn_conv_b, ffn_w_down)
    y_prompt = _trunk(x_prompt, meta_tokens, mix_norm, ffn_norm, final_norm, rg, na, gdn, ffn)
    y_sample = _trunk(x_sample, meta_tokens, mix_norm, ffn_norm, final_norm, rg, na, gdn, ffn)
    return (y_prompt, y_sample)
```

```python
import functools

import numpy as np
import jax
import jax.numpy as jnp
from jax import lax
from jax.experimental import pallas as pl
from jax.experimental.pallas import tpu as pltpu

D_MODEL = 2048
N_META = 16
PAD = 48
HEAD_ROWS = PAD + N_META
CHUNK = 64
GRID_W = 64
NORM_EPS = 1e-6

RG_BLOCK = 256
RG_C = 8.0

NA_HEAD_DIM = 128
NA_HEADS = D_MODEL // NA_HEAD_DIM
NA_WIN_R = 8
NA_WIN_C = 16
NA_ROW_GROUP = 16
NEG_INF = -1e30

GDN_DK = 128
GDN_QK_HEADS = D_MODEL // GDN_DK
GDN_V_HEADS = 2 * GDN_QK_HEADS
GDN_KEY_DIM = GDN_QK_HEADS * GDN_DK
GDN_VAL_DIM = GDN_V_HEADS * GDN_DK
GDN_CONV_DIM = 2 * GDN_KEY_DIM + GDN_VAL_DIM

V7X_VMEM_LIMIT = 56 * 1024 * 1024
HALO = 16

BF16 = jnp.bfloat16
F32 = jnp.float32


def _cparams(*sem):
    return pltpu.CompilerParams(dimension_semantics=sem, vmem_limit_bytes=V7X_VMEM_LIMIT)


def _pick_tile(n, cap, mult):
    best = None
    for t in range(mult, cap + 1, mult):
        if n % t == 0:
            best = t
    assert best is not None, (n, cap, mult)
    return best


def _sigmoid(x):
    return 1.0 / (1.0 + jnp.exp(-x))


def _softplus(x):
    return jnp.maximum(x, 0.0) + jnp.log(1.0 + jnp.exp(-jnp.abs(x)))


def _gelu_tanh(x):
    return 0.5 * x * (1.0 + jnp.tanh(0.7978845608028654 * (x + 0.044715 * x * x * x)))


def _rms_scale(x, w):
    ms = jnp.mean(x * x, axis=-1, keepdims=True)
    return x * lax.rsqrt(ms + NORM_EPS) * w


def _dot(a, b):
    return jnp.dot(a, b, preferred_element_type=F32)


def _dot_nt(a, b):
    return lax.dot_general(a, b, (((1,), (1,)), ((), ())), preferred_element_type=F32)


def _dot_tn(a, b):
    return lax.dot_general(a, b, (((0,), (0,)), ((), ())), preferred_element_type=F32)


def _rmsnorm_kernel(x_ref, w_ref, o_ref):
    o_ref[...] = _rms_scale(x_ref[...], w_ref[...]).astype(o_ref.dtype)


def _rmsnorm(h, w, out_dtype):
    M, D = h.shape
    tm = _pick_tile(M, 1024, 64)
    return pl.pallas_call(
        _rmsnorm_kernel,
        grid=(M // tm,),
        in_specs=[pl.BlockSpec((tm, D), lambda i: (i, 0)), pl.BlockSpec((1, D), lambda i: (0, 0))],
        out_specs=pl.BlockSpec((tm, D), lambda i: (i, 0)),
        out_shape=jax.ShapeDtypeStruct((M, D), out_dtype),
        compiler_params=_cparams("parallel"),
    )(h, w.reshape(1, D))


def _final_norm_kernel(x_ref, w_ref, o_ref):
    o_ref[0] = _rms_scale(x_ref[...], w_ref[...])


def _final_norm(h, w, B, T):
    D = h.shape[1]
    Tp = T + HEAD_ROWS
    tt = 512
    assert T % tt == 0
    return pl.pallas_call(
        _final_norm_kernel,
        grid=(B, T // tt),
        in_specs=[pl.BlockSpec((pl.Element(tt), pl.Element(D)),
                               lambda b, j: (pl.multiple_of(b * Tp + HEAD_ROWS + j * tt, CHUNK), 0)),
                  pl.BlockSpec((1, D), lambda b, j: (0, 0))],
        out_specs=pl.BlockSpec((1, tt, D), lambda b, j: (b, j, 0)),
        out_shape=jax.ShapeDtypeStruct((B, T, D), F32),
        compiler_params=_cparams("parallel", "parallel"),
    )(h, w.reshape(1, D))


def _matmul_kernel(x_ref, w_ref, o_ref, *, act):
    acc = _dot(x_ref[...], w_ref[...])
    if act == "gelu":
        acc = _gelu_tanh(acc)
    o_ref[...] = acc.astype(o_ref.dtype)


def _matmul(x, w, out_dtype, act=None):
    M, K = x.shape
    N = w.shape[1]
    tm = _pick_tile(M, 1024, 128)
    tn = _pick_tile(N, 1024, 128)
    return pl.pallas_call(
        functools.partial(_matmul_kernel, act=act),
        grid=(M // tm, N // tn),
        in_specs=[pl.BlockSpec((tm, K), lambda i, j: (i, 0)), pl.BlockSpec((K, tn), lambda i, j: (0, j))],
        out_specs=pl.BlockSpec((tm, tn), lambda i, j: (i, j)),
        out_shape=jax.ShapeDtypeStruct((M, N), out_dtype),
        compiler_params=_cparams("parallel", "parallel"),
    )(x, w)


def _proj_res_kernel(x_ref, w_ref, h_ref, nw_ref, hout_ref, hn_ref):
    hnew = h_ref[...] + _dot(x_ref[...], w_ref[...])
    hout_ref[...] = hnew
    hn_ref[...] = _rms_scale(hnew, nw_ref[...]).astype(hn_ref.dtype)


def _proj_res(x, w, h, nw):
    M, K = x.shape
    D = w.shape[1]
    tm = _pick_tile(M, 640 if K <= D_MODEL else 512, 64)
    return pl.pallas_call(
        _proj_res_kernel,
        grid=(M // tm,),
        in_specs=[pl.BlockSpec((tm, K), lambda i: (i, 0)),
                  pl.BlockSpec((K, D), lambda i: (0, 0), pipeline_mode=pl.Buffered(1)),
                  pl.BlockSpec((tm, D), lambda i: (i, 0)),
                  pl.BlockSpec((1, D), lambda i: (0, 0))],
        out_specs=[pl.BlockSpec((tm, D), lambda i: (i, 0)), pl.BlockSpec((tm, D), lambda i: (i, 0))],
        out_shape=[jax.ShapeDtypeStruct((M, D), F32), jax.ShapeDtypeStruct((M, D), BF16)],
        compiler_params=_cparams("parallel"),
    )(x, w, h, nw.reshape(1, D))


def _proj_conv_kernel(xp_ref, xm_ref, xn_ref, w_ref, cw_ref, o_ref, xext_ref, *, tm, tc, ni, Tp, n_norm):
    i = pl.program_id(0)
    j = pl.program_id(1)
    npc = o_ref.shape[1] // tc

    @pl.when(j == 0)
    def _():
        xext_ref[0:HALO, :] = xp_ref[...]
        xext_ref[HALO:HALO + tm, :] = xm_ref[...]
        xext_ref[HALO + tm:2 * HALO + tm, :] = xn_ref[...]

    @pl.when(jnp.logical_and(j == 0, i == ni - 1))
    def _():
        xext_ref[HALO + tm:2 * HALO + tm, :] = jnp.zeros((HALO, xext_ref.shape[1]), xext_ref.dtype)

    rel = i * tm - (i * tm // Tp) * Tp + lax.broadcasted_iota(jnp.int32, (tm, 1), 0)
    live = jnp.where(rel >= Tp, rel - Tp, rel) >= PAD
    cw = cw_ref[...]
    xe = xext_ref[...]

    def piece_matmul(c):
        return _dot(xe, w_ref[:, c * tc:(c + 1) * tc])

    def piece_finish(c, g):
        cwc = cw[:, c * tc:(c + 1) * tc]
        a = (cwc[0:1] * g[HALO - 1:HALO - 1 + tm] + cwc[1:2] * g[HALO:HALO + tm]
             + cwc[2:3] * g[HALO + 1:HALO + 1 + tm] + cwc[3:4] * g[HALO + 2:HALO + 2 + tm])
        y = jnp.where(live, a * _sigmoid(a), 0.0)
        normalise = j * npc + c < n_norm
        heads = []
        for hh in range(tc // GDN_DK):
            yh = y[:, hh * GDN_DK:(hh + 1) * GDN_DK]
            nrm = lax.rsqrt(jnp.sum(yh * yh, axis=-1, keepdims=True) + 1e-6)
            heads.append(yh * jnp.where(normalise, nrm, 1.0))
        o_ref[:, c * tc:(c + 1) * tc] = jnp.concatenate(heads, axis=1).astype(o_ref.dtype)

    g = piece_matmul(0)
    for c in range(npc):
        g_next = piece_matmul(c + 1) if c + 1 < npc else None
        piece_finish(c, g)
        g = g_next


def _proj_conv(x, w, conv_w, Tp, n_norm_cols):
    M, K = x.shape
    N = w.shape[1]
    tm = _pick_tile(M, 1024, 64)
    assert tm <= Tp
    tn, tc = 2048, 512
    ni = M // tm
    hb = tm // HALO
    nhb = M // HALO
    return pl.pallas_call(
        functools.partial(_proj_conv_kernel, tm=tm, tc=tc, ni=ni, Tp=Tp, n_norm=n_norm_cols // tc),
        grid=(ni, N // tn),
        in_specs=[pl.BlockSpec((HALO, K), lambda i, j: (jnp.maximum(i * hb - 1, 0), 0)),
                  pl.BlockSpec((tm, K), lambda i, j: (i, 0)),
                  pl.BlockSpec((HALO, K), lambda i, j: (jnp.minimum((i + 1) * hb, nhb - 1), 0)),
                  pl.BlockSpec((K, tn), lambda i, j: (0, j)),
                  pl.BlockSpec((4, tn), lambda i, j: (0, j))],
        out_specs=pl.BlockSpec((tm, tn), lambda i, j: (i, j)),
        out_shape=jax.ShapeDtypeStruct((M, N), BF16),
        scratch_shapes=[pltpu.VMEM((tm + 2 * HALO, K), BF16)],
        compiler_params=_cparams("parallel", "arbitrary"),
    )(x, x, x, w, conv_w)


def _ffn_kernel(xp_ref, xm_ref, xn_ref, wg_ref, wu_ref, cw_ref, cb_ref, wd_ref, h_ref, nw_ref,
                hout_ref, hn_ref, xext_ref, *, tm, nf, ni):
    i = pl.program_id(0)
    f = pl.program_id(1)
    th = tm // 2

    @pl.when(f == 0)
    def _():
        xext_ref[0:HALO, :] = xp_ref[...]
        xext_ref[HALO:HALO + tm, :] = xm_ref[...]
        xext_ref[HALO + tm:2 * HALO + tm, :] = xn_ref[...]
        hout_ref[...] = h_ref[...]

    cw = cw_ref[...]
    cb = cb_ref[...]
    g, up = [], []
    for s in range(2):
        g.append(_dot(xext_ref[s * th:s * th + th + 2 * HALO, :], wg_ref[...]))
        up.append(_dot(xm_ref[s * th:(s + 1) * th, :], wu_ref[...]))
    row = lax.broadcasted_iota(jnp.int32, (th, 1), 0)
    y = []
    for s in range(2):
        g_prev = g[s][HALO - 1:HALO - 1 + th]
        g_here = g[s][HALO:HALO + th]
        g_next = g[s][HALO + 1:HALO + 1 + th]
        if s == 1:
            g_next = jnp.where(jnp.logical_and(i == ni - 1, row == th - 1), 0.0, g_next)
        a = cw[0:1] * g_prev + cw[1:2] * g_here + cw[2:3] * g_next + cb
        y.append((a * _sigmoid(a) * up[s]).astype(BF16))
    down = [_dot(y[s], wd_ref[...]) for s in range(2)]
    for s in range(2):
        hout_ref[s * th:(s + 1) * th, :] += down[s]

    @pl.when(f == nf - 1)
    def _():
        hn_ref[...] = _rms_scale(hout_ref[...], nw_ref[...]).astype(hn_ref.dtype)


def _conv_ffn(hn, h, w_gate, w_up, conv_w, conv_b, w_down, nw):
    M, D = hn.shape
    F = w_gate.shape[1]
    tm = _pick_tile(M, 640, 64)
    tf = _pick_tile(F, 512, 128)
    ni, nf = M // tm, F // tf
    hb = tm // HALO
    nhb = M // HALO
    return pl.pallas_call(
        functools.partial(_ffn_kernel, tm=tm, nf=nf, ni=ni),
        grid=(ni, nf),
        in_specs=[pl.BlockSpec((HALO, D), lambda i, f: (jnp.maximum(i * hb - 1, 0), 0)),
                  pl.BlockSpec((tm, D), lambda i, f: (i, 0)),
                  pl.BlockSpec((HALO, D), lambda i, f: (jnp.minimum((i + 1) * hb, nhb - 1), 0)),
                  pl.BlockSpec((D, tf), lambda i, f: (0, f)),
                  pl.BlockSpec((D, tf), lambda i, f: (0, f)),
                  pl.BlockSpec((3, tf), lambda i, f: (0, f)),
                  pl.BlockSpec((1, tf), lambda i, f: (0, f)),
                  pl.BlockSpec((tf, D), lambda i, f: (f, 0)),
                  pl.BlockSpec((tm, D), lambda i, f: (i, 0)),
                  pl.BlockSpec((1, D), lambda i, f: (0, 0))],
        out_specs=[pl.BlockSpec((tm, D), lambda i, f: (i, 0)), pl.BlockSpec((tm, D), lambda i, f: (i, 0))],
        out_shape=[jax.ShapeDtypeStruct((M, D), F32), jax.ShapeDtypeStruct((M, D), BF16)],
        scratch_shapes=[pltpu.VMEM((tm + 2 * HALO, D), BF16)],
        compiler_params=_cparams("parallel", "arbitrary"),
    )(hn, hn, hn, w_gate, w_up, conv_w, conv_b.reshape(1, F), w_down, h, nw.reshape(1, D))


def _scan8(a, u, row, reverse):
    for s in (1, 2, 4):
        if reverse:
            a_sh = pltpu.roll(a, 8 - s, 0)
            u_sh = pltpu.roll(u, 8 - s, 0)
            m = row < 8 - s
        else:
            a_sh = pltpu.roll(a, s, 0)
            u_sh = pltpu.roll(u, s, 0)
            m = row >= s
        u = jnp.where(m, a * u_sh + u, u)
        a = jnp.where(m, a * a_sh, a)
    return a, u


def _rg_kernel(rec_ref, gate_ref, cw_ref, cb_ref, wa_ref, ba_ref, wi_ref, bi_ref, lam_ref, o_ref,
               xc_s, hs_s, *, Tp, rc):
    nch = Tp // rc
    W = RG_BLOCK
    cw = cw_ref[...]
    cb = cb_ref[...]
    row8 = lax.broadcasted_iota(jnp.int32, (8, W), 0)

    def gates(xc, d):
        xb = xc.astype(BF16)
        r = _sigmoid(_dot(xb, wa_ref[d, 0]) + ba_ref[d])
        ig = _sigmoid(_dot(xb, wi_ref[d, 0]) + bi_ref[d])
        log_a = (-RG_C) * r * _softplus(-lam_ref[d])
        a = jnp.exp(log_a)
        u = jnp.sqrt(1.0 - a * a) * (ig * xc)
        return a, u

    def fwd_chunk(c, carry):
        r0 = pl.multiple_of(c * rc, CHUNK)
        top = rec_ref[0, pl.ds(pl.multiple_of(jnp.maximum(r0 - 8, 0), 8), 8), :]
        main = rec_ref[0, pl.ds(r0, rc), :]
        bot = rec_ref[0, pl.ds(pl.multiple_of(jnp.minimum(r0 + rc, Tp - 8), 8), 8), :]
        bot = jnp.where(r0 + rc < Tp, bot, 0.0)
        xw = jnp.concatenate([top, main, bot], axis=0)
        xc = cw[0:1] * xw[7:7 + rc] + cw[1:2] * xw[8:8 + rc] + cw[2:3] * xw[9:9 + rc] + cw[3:4] * xw[10:10 + rc] + cb
        t = r0 + lax.broadcasted_iota(jnp.int32, (rc, 1), 0)
        xc = jnp.where(t >= PAD, xc, 0.0)
        xc_s[pl.ds(r0, rc), :] = xc
        a, u = gates(xc, 0)
        hs = []
        for k in range(rc // 8):
            A, U = _scan8(a[8 * k:8 * k + 8], u[8 * k:8 * k + 8], row8, False)
            h = A * carry + U
            carry = h[7:8]
            hs.append(h)
        hs_s[pl.ds(r0, rc), :] = jnp.concatenate(hs, axis=0)
        return carry

    lax.fori_loop(0, nch, fwd_chunk, jnp.zeros((1, W), F32))

    def bwd_chunk(cc, carry):
        r0 = pl.multiple_of((nch - 1 - cc) * rc, CHUNK)
        a, u = gates(xc_s[pl.ds(r0, rc), :], 1)
        hs = [None] * (rc // 8)
        for k in reversed(range(rc // 8)):
            A, U = _scan8(a[8 * k:8 * k + 8], u[8 * k:8 * k + 8], row8, True)
            h = A * carry + U
            carry = h[0:1]
            hs[k] = h
        hsum = hs_s[pl.ds(r0, rc), :] + jnp.concatenate(hs, axis=0)
        o_ref[0, pl.ds(r0, rc), :] = (gate_ref[0, pl.ds(r0, rc), :].astype(F32) * hsum).astype(o_ref.dtype)
        return carry

    lax.fori_loop(0, nch, bwd_chunk, jnp.zeros((1, W), F32))


def _rg_scan(rec, gate, conv_w, conv_b, w_a, b_a, w_i, b_i, lam):
    B, Tp, W = rec.shape
    nb = W // RG_BLOCK
    rc = _pick_tile(Tp, 384, 64)
    blk = lambda b, n: (b, 0, n)
    vec = lambda b, n: (0, n)
    vec3 = lambda b, n: (0, 0, n)
    return pl.pallas_call(
        functools.partial(_rg_kernel, Tp=Tp, rc=rc),
        grid=(B, nb),
        in_specs=[pl.BlockSpec((1, Tp, RG_BLOCK), blk),
                  pl.BlockSpec((1, Tp, RG_BLOCK), blk),
                  pl.BlockSpec((4, RG_BLOCK), vec),
                  pl.BlockSpec((1, RG_BLOCK), vec),
                  pl.BlockSpec((2, 1, RG_BLOCK, RG_BLOCK), lambda b, n: (0, n, 0, 0)),
                  pl.BlockSpec((2, 1, RG_BLOCK), vec3),
                  pl.BlockSpec((2, 1, RG_BLOCK, RG_BLOCK), lambda b, n: (0, n, 0, 0)),
                  pl.BlockSpec((2, 1, RG_BLOCK), vec3),
                  pl.BlockSpec((2, 1, RG_BLOCK), vec3)],
        out_specs=pl.BlockSpec((1, Tp, RG_BLOCK), blk),
        out_shape=jax.ShapeDtypeStruct((B, Tp, W), BF16),
        scratch_shapes=[pltpu.VMEM((Tp, RG_BLOCK), F32), pltpu.VMEM((Tp, RG_BLOCK), F32)],
        compiler_params=_cparams("parallel", "parallel"),
    )(rec, gate, conv_w, conv_b.reshape(1, W), w_a.astype(BF16), b_a.reshape(2, 1, W),
      w_i.astype(BF16), b_i.reshape(2, 1, W), lam.reshape(2, 1, W))


def _rglru_mixer(hn, h, B, Tp, w_in, conv_w, conv_b, w_a, b_a, w_i, b_i, lam, w_out, nw):
    W = D_MODEL
    w_in = w_in.astype(BF16)
    gate = _matmul(hn, w_in[:, :W], BF16, act="gelu")
    rec = _matmul(hn, w_in[:, W:], F32)
    y = _rg_scan(rec.reshape(B, Tp, W), gate.reshape(B, Tp, W), conv_w, conv_b, w_a, b_a, w_i, b_i, lam)
    return _proj_res(y.reshape(B * Tp, W), w_out.astype(BF16), h, nw)


def _na_kernel(q_ref, k_ref, v_ref, bias_ref, mb_ref, o_ref, *, rows):
    scale = NA_HEAD_DIM ** -0.5
    kh = NA_WIN_R
    mb = mb_ref[0]
    km = k_ref[0, PAD:HEAD_ROWS, :]
    vm = v_ref[0, PAD:HEAD_ROWS, :]

    o_ref[0, 0:PAD, :] = jnp.zeros((PAD, NA_HEAD_DIM), o_ref.dtype)
    qm = q_ref[0, PAD:HEAD_ROWS, :]
    s_m = _dot_nt(qm, km) * scale + mb
    p_m = jnp.exp(s_m - jnp.max(s_m, axis=-1, keepdims=True))
    o_m = _dot(p_m.astype(BF16), vm) / jnp.sum(p_m, axis=-1, keepdims=True)
    o_ref[0, PAD:HEAD_ROWS, :] = o_m.astype(o_ref.dtype)

    group = np.gcd(rows, NA_ROW_GROUP)
    R = range(group)

    def group_body(i, _):
        r = [i * group + c for c in R]
        rs = [jnp.clip(r[c] - NA_WIN_R // 2, 0, rows - kh) for c in R]
        q0 = [pl.multiple_of(HEAD_ROWS + r[c] * GRID_W, GRID_W) for c in R]
        k0 = [pl.multiple_of(HEAD_ROWS + rs[c] * GRID_W, GRID_W) for c in R]
        q_r = [q_ref[0, pl.ds(q0[c], GRID_W), :] for c in R]
        s = [_dot_nt(q_r[c], k_ref[0, pl.ds(k0[c], kh * GRID_W), :]) for c in R]
        s_met = [_dot_nt(q_r[c], km) for c in R]
        p, p_met, l = [], [], []
        for c in R:
            sc = s[c] * scale + bias_ref[0, r[c] - rs[c]]
            sm = s_met[c] * scale + mb
            m = jnp.maximum(jnp.max(sc, axis=-1, keepdims=True), jnp.max(sm, axis=-1, keepdims=True))
            pc = jnp.exp(sc - m)
            pm = jnp.exp(sm - m)
            l.append(jnp.sum(pc, axis=-1, keepdims=True) + jnp.sum(pm, axis=-1, keepdims=True))
            p.append(pc.astype(BF16))
            p_met.append(pm.astype(BF16))
        o = [_dot(p[c], v_ref[0, pl.ds(k0[c], kh * GRID_W), :]) + _dot(p_met[c], vm) for c in R]
        for c in R:
            o_ref[0, pl.ds(q0[c], GRID_W), :] = (o[c] / l[c]).astype(o_ref.dtype)
        return 0

    lax.fori_loop(0, rows // group, group_body, 0)


def _na_bias_table(rpb):
    kh = NA_WIN_R
    cols = np.arange(GRID_W)
    col_start = np.clip(cols - NA_WIN_C // 2, 0, GRID_W - NA_WIN_C)
    valid = (cols[None, :] >= col_start[:, None]) & (cols[None, :] < col_start[:, None] + NA_WIN_C)
    col_off = np.clip(cols[None, :] - cols[:, None], -(NA_WIN_C - 1), NA_WIN_C - 1) + NA_WIN_C - 1
    H, n_ro, n_co = rpb.shape
    onehot = (col_off[None] == np.arange(n_co)[:, None, None]).astype(np.float32)
    a = jnp.einsum('hrc,cqk->hqrk', rpb.astype(F32), jnp.asarray(onehot), precision=lax.Precision.HIGHEST)
    a = jnp.where(jnp.asarray(valid)[None, :, None, :], a, NEG_INF)
    tabs = [a[:, :, NA_WIN_R - 1 - d:NA_WIN_R - 1 - d + kh, :].reshape(H, GRID_W, kh * GRID_W)
            for d in range(NA_WIN_R)]
    return jnp.stack(tabs, axis=1)


def _na_attention(qkv, bias, meta_bias):
    B, Tp, _ = qkv.shape
    rows = (Tp - HEAD_ROWS) // GRID_W
    assert rows >= NA_WIN_R
    H = NA_HEADS
    return pl.pallas_call(
        functools.partial(_na_kernel, rows=rows),
        grid=(B, H),
        in_specs=[pl.BlockSpec((1, Tp, NA_HEAD_DIM), lambda b, h: (b, 0, h)),
                  pl.BlockSpec((1, Tp, NA_HEAD_DIM), lambda b, h: (b, 0, H + h)),
                  pl.BlockSpec((1, Tp, NA_HEAD_DIM), lambda b, h: (b, 0, 2 * H + h)),
                  pl.BlockSpec((1, NA_WIN_R, GRID_W, NA_WIN_R * GRID_W), lambda b, h: (h, 0, 0, 0)),
                  pl.BlockSpec((1, 1, N_META), lambda b, h: (h, 0, 0))],
        out_specs=pl.BlockSpec((1, Tp, NA_HEAD_DIM), lambda b, h: (b, 0, h)),
        out_shape=jax.ShapeDtypeStruct((B, Tp, H * NA_HEAD_DIM), BF16),
        compiler_params=_cparams("parallel", "parallel"),
    )(qkv, qkv, qkv, bias, meta_bias.astype(F32).reshape(H, 1, N_META))


def _na_mixer(hn, h, B, Tp, w_qkv, bias, meta_bias, w_o, nw):
    qkv = _matmul(hn, w_qkv.astype(BF16), BF16)
    o = _na_attention(qkv.reshape(B, Tp, -1), bias, meta_bias)
    return _proj_res(o.reshape(B * Tp, -1), w_o.astype(BF16), h, nw)


def _gdn_gate_kernel(x_ref, w_ref, al_ref, dt_ref, o_ref):
    y = _dot(x_ref[...], w_ref[...])
    lane = lax.broadcasted_iota(jnp.int32, y.shape, 1)
    g = -jnp.exp(al_ref[...]) * _softplus(y + dt_ref[...])
    o_ref[...] = jnp.where(lane < 2 * GDN_V_HEADS, _sigmoid(y), g)


def _gdn_gates(hn, w_ba, a_log, dt_bias):
    M, K = hn.shape
    N = 4 * GDN_V_HEADS
    tm = _pick_tile(M, 1024, 128)
    zeros = jnp.zeros((2 * GDN_V_HEADS,), F32)
    al = jnp.concatenate([zeros, a_log.astype(F32).reshape(-1)]).reshape(1, N)
    dt = jnp.concatenate([zeros, dt_bias.astype(F32).reshape(-1)]).reshape(1, N)
    return pl.pallas_call(
        _gdn_gate_kernel,
        grid=(M // tm,),
        in_specs=[pl.BlockSpec((tm, K), lambda i: (i, 0)), pl.BlockSpec((K, N), lambda i: (0, 0)),
                  pl.BlockSpec((1, N), lambda i: (0, 0)), pl.BlockSpec((1, N), lambda i: (0, 0))],
        out_specs=pl.BlockSpec((tm, N), lambda i: (i, 0)),
        out_shape=jax.ShapeDtypeStruct((M, N), F32),
        compiler_params=_cparams("parallel"),
    )(hn, w_ba, al, dt)


def _split3_dot(x, m):
    hi = x.astype(BF16)
    r1 = x - hi.astype(F32)
    mid = r1.astype(BF16)
    lo = (r1 - mid.astype(F32)).astype(BF16)
    return _dot(hi, m) + _dot(mid, m) + _dot(lo, m)


def _gdn_kernel(q_ref, k_ref, v_ref, z_ref, bgc_ref, bgr_ref, nw_ref, o_ref,
                S_s, oacc_s, u_s, wq_s, kst_s, qkd_s, gam_s, *, Tp, HPS):
    hb = pl.program_id(1)
    C = CHUNK
    NC = Tp // C
    HV = GDN_V_HEADS
    DK = GDN_DK
    G = 4
    W = G * C
    P1_GROUP = 6 if HPS == 1 else 4
    scale = DK ** -0.5

    oacc_s[...] = jnp.zeros_like(oacc_s)
    S_s[...] = jnp.zeros_like(S_s)

    ri = lax.broadcasted_iota(jnp.int32, (C, W), 0)
    cl = lax.broadcasted_iota(jnp.int32, (C, W), 1)
    cj = jnp.bitwise_and(cl, C - 1)
    blk = jnp.right_shift(cl, 6)
    dd = jnp.where(blk >= 2, cj - ri, ri - cj)
    incl4 = dd >= 0
    strict4 = dd > 0
    eye4 = (ri == cj).astype(F32)
    cum4 = jnp.where(dd <= 0, 1.0, 0.0).astype(BF16)
    bdm = jnp.where(jnp.right_shift(lax.broadcasted_iota(jnp.int32, (W, W), 0), 6)
                    == jnp.right_shift(lax.broadcasted_iota(jnp.int32, (W, W), 1), 6), 1.0, 0.0).astype(BF16)
    blk_row = jnp.right_shift(lax.broadcasted_iota(jnp.int32, (1, W), 1), 6)
    lane = lax.broadcasted_iota(jnp.int32, (C, 4 * HV), 1)
    rowi = lax.broadcasted_iota(jnp.int32, (C, 4 * HV), 0)

    def col(x, c):
        return jnp.sum(jnp.where(lane == c, x, 0.0), axis=1, keepdims=True)

    def by_block(parts, b):
        return jnp.where(b == 0, parts[0], jnp.where(b == 1, parts[1], jnp.where(b == 2, parts[2], parts[3])))

    def blockdiag(xb):
        return jnp.concatenate([xb] * G, axis=0) * bdm

    def phase1(ns):
        items = [(hd, n) for n in ns for hd in range(HPS)]
        K = range(len(items))
        r0 = [pl.multiple_of(n * C, C) for (hd, n) in items]
        q = [q_ref[0, pl.ds(r0[i], C), items[i][0] * DK:(items[i][0] + 1) * DK] for i in K]
        k = [k_ref[0, pl.ds(r0[i], C), items[i][0] * DK:(items[i][0] + 1) * DK] for i in K]
        bg = [bgc_ref[0, pl.ds(r0[i], C), :] for i in K]
        v = [v_ref[0, pl.ds(r0[i], C), items[i][0] * 2 * DK:(items[i][0] + 1) * 2 * DK].astype(F32) for i in K]
        rows = [bgr_ref[0, n, hd] for (hd, n) in items]
        cum_rows = [_split3_dot(rows[i], cum4) for i in K]
        k4 = [jnp.concatenate([k[i]] * G, axis=0) for i in K]
        kk4 = [_dot_nt(k[i], k4[i]) for i in K]
        qk4 = [_dot_nt(q[i], k4[i]) * scale for i in K]
        beta_c, gc_c, decay4, L = [], [], [], []
        for i in K:
            pre = bg[i]
            for s in (1, 2, 4, 8, 16, 32):
                pre = pre + jnp.where(rowi >= s, pltpu.roll(pre, s, 0), 0.0)
            suf = pre[C - 1:C] - pre + bg[i]
            gc_r = by_block([cum_rows[i][4 + g:5 + g] for g in range(G)], blk_row)
            bc, gc = [], []
            for g in range(G):
                d, j = g // 2, g % 2
                vh = 2 * (hb * HPS + items[i][0]) + j
                bc.append(col(bg[i], d * HV + vh))
                gc.append(col(suf if d else pre, 2 * HV + d * HV + vh))
            beta_c.append(bc)
            gc_c.append(gc)
            dec = jnp.where(incl4, jnp.exp(jnp.where(incl4, by_block(gc, blk) - gc_r, 0.0)), 0.0)
            decay4.append(dec)
            L.append(jnp.where(strict4, kk4[i] * dec, 0.0) * by_block(bc, blk))
        Tm = [eye4 - L[i] for i in K]
        Lb = [L[i].astype(BF16) for i in K]
        P = [_dot(Lb[i], blockdiag(Lb[i])) for i in K]
        for lvl in range(5):
            Pb = [P[i].astype(BF16) for i in K]
            bd = [blockdiag(Pb[i]) for i in K]
            if lvl < 4:
                tp = [_dot(jnp.concatenate([Tm[i].astype(BF16), Pb[i]], axis=0), bd[i]) for i in K]
                Tm = [Tm[i] + tp[i][:C] for i in K]
                P = [tp[i][C:] for i in K]
            else:
                tp = [_dot(Tm[i].astype(BF16), bd[i]) for i in K]
                Tm = [Tm[i] + tp[i] for i in K]
        rhs, kst, qst, gam = [], [], [], []
        for i in K:
            qf = q[i].astype(F32)
            kf = k[i].astype(F32)
            rhs_i, kst_i, qst_i, gam_i = [], [], [], []
            for g in range(G):
                j = g % 2
                bc, gc = beta_c[i][g], gc_c[i][g]
                e_c = jnp.exp(gc)
                rhs_i.append(jnp.concatenate([v[i][:, j * DK:(j + 1) * DK] * bc, kf * (bc * e_c)], axis=1))
                g_last = gc[0:1] if g >= 2 else gc[C - 1:C]
                kst_i.append((kf * jnp.exp(g_last - gc)).astype(BF16))
                qst_i.append((qf * (scale * e_c)).astype(BF16))
                gam_i.append(jnp.broadcast_to(jnp.exp(g_last), (1, DK)))
            rhs.append(jnp.concatenate(rhs_i, axis=0).astype(BF16))
            kst.append(kst_i)
            qst.append(qst_i)
            gam.append(gam_i)
        sol = [_dot(blockdiag(Tm[i].astype(BF16)), rhs[i]) for i in K]
        for i in K:
            hd, n = items[i]
            u_s[hd, n] = sol[i][:, :DK]
            qkd_s[hd, n] = jnp.where(incl4, qk4[i] * decay4[i], 0.0).astype(BF16)
            for g in range(G):
                wq_s[hd, n, g, 0:C, :] = sol[i][g * C:(g + 1) * C, DK:].astype(BF16)
                wq_s[hd, n, g, C:2 * C, :] = qst[i][g]
                kst_s[hd, n, g] = kst[i][g]
                gam_s[hd, n, g:g + 1, :] = gam[i][g]

    def phase1_group(i, _):
        phase1([P1_GROUP * i + c for c in range(P1_GROUP)])
        return 0

    lax.fori_loop(0, NC // P1_GROUP, phase1_group, 0)
    if NC % P1_GROUP:
        phase1(list(range(NC - NC % P1_GROUP, NC)))

    nw = nw_ref[...]

    def gated_norm(o, z):
        ys = []
        for j in range(2 * HPS):
            oj = o[:, j * DK:(j + 1) * DK]
            zj = z[:, j * DK:(j + 1) * DK].astype(F32)
            ys.append(oj * lax.rsqrt(jnp.mean(oj * oj, axis=-1, keepdims=True) + NORM_EPS) * nw * (zj * _sigmoid(zj)))
        return jnp.concatenate(ys, axis=1)

    def finalize(n):
        r0 = pl.multiple_of(n * C, C)
        o_ref[0, pl.ds(r0, C), :] = gated_norm(oacc_s[pl.ds(r0, C), :], z_ref[0, pl.ds(r0, C), :]).astype(o_ref.dtype)

    def phase2(it, finalize_previous):
        if finalize_previous:
            finalize(it - 1)
            finalize(NC - it)
        ops = []
        for hd in range(HPS):
            for g in range(G):
                n = it if g < 2 else NC - 1 - it
                ops.append((S_s[hd * G + g], wq_s[hd, n, g], u_s[hd, n, g * C:(g + 1) * C, :],
                            qkd_s[hd, n, :, g * C:(g + 1) * C], gam_s[hd, n, g:g + 1, :], kst_s[hd, n, g]))
        NI = HPS * G
        r_f = pl.multiple_of(it * C, C)
        r_b = pl.multiple_of((NC - 1 - it) * C, C)
        o_f = oacc_s[pl.ds(r_f, C), :]
        ws = [_dot(wq, S.astype(BF16)) for (S, wq, u, qkd, gam, kst) in ops]
        vb = [(ops[g][2] - ws[g][:C]).astype(BF16) for g in range(NI)]
        outs = [ws[g][C:] + _dot(ops[g][3], vb[g]) for g in range(NI)]
        states = [ops[g][0] * ops[g][4] + _dot_tn(ops[g][5], vb[g]) for g in range(NI)]
        for g in range(NI):
            S_s[g] = states[g]
        fwd = [outs[hd * G + j] for hd in range(HPS) for j in range(2)]
        bwd = [outs[hd * G + 2 + j] for hd in range(HPS) for j in range(2)]
        o_f = o_f + jnp.concatenate(fwd, axis=1)
        oacc_s[pl.ds(r_f, C), :] = o_f
        oacc_s[pl.ds(r_b, C), :] = oacc_s[pl.ds(r_b, C), :] + jnp.concatenate(bwd, axis=1)
        return 0

    lax.fori_loop(0, NC // 2 + 1, lambda it, c: phase2(it, False), 0)
    lax.fori_loop(NC // 2 + 1, NC, lambda it, c: phase2(it, True), 0)
    finalize(NC - 1)
    finalize(0)


_GDN_CHUNK_BYTES = CHUNK * GDN_DK * (4 * 4 + 8 * 2 + 4 * 2 + 4 * 2) + 8 * GDN_DK * 4
_GDN_PAIR_BUDGET = 26 * 1024 * 1024


def _gdn_delta(qkv, z, bg, norm_w):
    B, Tp, _ = qkv.shape
    NC = Tp // CHUNK
    KH = GDN_QK_HEADS
    HV = GDN_V_HEADS
    C = CHUNK
    HPS = 2 if NC * _GDN_CHUNK_BYTES * 2 <= _GDN_PAIR_BUDGET else 1
    NH = KH // HPS
    bgr = bg.reshape(B, NC, CHUNK, 4, KH, 2).transpose(0, 1, 4, 3, 5, 2).reshape(B, NC, KH, 8, CHUNK)
    return pl.pallas_call(
        functools.partial(_gdn_kernel, Tp=Tp, HPS=HPS),
        grid=(B, NH),
        in_specs=[pl.BlockSpec((1, Tp, HPS * GDN_DK), lambda b, h: (b, 0, h)),
                  pl.BlockSpec((1, Tp, HPS * GDN_DK), lambda b, h: (b, 0, NH + h)),
                  pl.BlockSpec((1, Tp, 2 * HPS * GDN_DK), lambda b, h: (b, 0, NH + h)),
                  pl.BlockSpec((1, Tp, 2 * HPS * GDN_DK), lambda b, h: (b, 0, h)),
                  pl.BlockSpec((1, Tp, 4 * HV), lambda b, h: (b, 0, 0)),
                  pl.BlockSpec((1, NC, HPS, 8, CHUNK), lambda b, h: (b, 0, h, 0, 0)),
                  pl.BlockSpec((1, GDN_DK), lambda b, h: (0, 0))],
        out_specs=pl.BlockSpec((1, Tp, 2 * HPS * GDN_DK), lambda b, h: (b, 0, h)),
        out_shape=jax.ShapeDtypeStruct((B, Tp, GDN_VAL_DIM), BF16),
        scratch_shapes=[pltpu.VMEM((4 * HPS, GDN_DK, GDN_DK), F32),
                        pltpu.VMEM((Tp, 2 * HPS * GDN_DK), F32),
                        pltpu.VMEM((HPS, NC, 4 * C, GDN_DK), F32),
                        pltpu.VMEM((HPS, NC, 4, 2 * C, GDN_DK), BF16),
                        pltpu.VMEM((HPS, NC, 4, C, GDN_DK), BF16),
                        pltpu.VMEM((HPS, NC, C, 4 * C), BF16),
                        pltpu.VMEM((HPS, NC, 8, GDN_DK), F32)],
        compiler_params=_cparams("parallel", "parallel"),
    )(qkv, qkv, qkv, z, bg, bgr, norm_w.astype(F32).reshape(1, GDN_DK))


def _gdn_mixer(hn, h, B, Tp, w_in, conv_w, a_log, dt_bias, norm_w, w_out, nw):
    nz = GDN_CONV_DIM + GDN_VAL_DIM
    w_in = w_in.astype(BF16)
    qkv = _proj_conv(hn, w_in[:, :GDN_CONV_DIM], conv_w, Tp, 2 * GDN_KEY_DIM)
    z = _matmul(hn, w_in[:, GDN_CONV_DIM:nz], BF16)
    bg = _gdn_gates(hn, w_in[:, nz:], a_log, dt_bias)
    o = _gdn_delta(qkv.reshape(B, Tp, -1), z.reshape(B, Tp, -1), bg.reshape(B, Tp, -1), norm_w)
    return _proj_res(o.reshape(B * Tp, -1), w_out.astype(BF16), h, nw)


def _trunk(x, meta_tokens, mix_norm, ffn_norm, final_norm, rg, na, gdn, ffn):
    B, T, D = x.shape
    Tp = T + HEAD_ROWS
    depth = mix_norm.shape[0]
    h3 = jnp.concatenate([jnp.zeros((B, PAD, D), F32),
                          jnp.broadcast_to(meta_tokens.astype(F32)[None], (B, N_META, D)),
                          x.astype(F32)], axis=1)
    h = h3.reshape(B * Tp, D)
    hn = _rmsnorm(h, mix_norm[0], BF16)
    for i in range(depth):
        kind, j = i % 3, i // 3
        if kind == 0:
            h, hn = _rglru_mixer(hn, h, B, Tp, *[p[j] for p in rg], ffn_norm[i])
        elif kind == 1:
            h, hn = _na_mixer(hn, h, B, Tp, *[p[j] for p in na], ffn_norm[i])
        else:
            h, hn = _gdn_mixer(hn, h, B, Tp, *[p[j] for p in gdn], ffn_norm[i])
        w_gate, w_up, conv_w, conv_b, w_down = [p[i] for p in ffn]
        next_norm = mix_norm[i + 1] if i + 1 < depth else final_norm
        h, hn = _conv_ffn(hn, h, w_gate.astype(BF16), w_up.astype(BF16), conv_w, conv_b,
                          w_down.astype(BF16), next_norm)
    return _final_norm(h, final_norm, B, T)


def kernel(x_prompt, x_sample, meta_tokens, mix_norm, ffn_norm, final_norm, rg_w_in, rg_conv_w, rg_conv_b, rg_w_a, rg_b_a, rg_w_i, rg_b_i, rg_lam, rg_w_out, na_w_qkv, na_rpb, na_meta_bias, na_w_o, gdn_w_in, gdn_conv_w, gdn_a_log, gdn_dt_bias, gdn_norm_w, gdn_w_out, ffn_w_gate, ffn_w_up, ffn_conv_w, ffn_conv_b, ffn_w_down):
    rg = (rg_w_in, rg_conv_w, rg_conv_b, rg_w_a, rg_b_a, rg_w_i, rg_b_i, rg_lam, rg_w_out)
    na_bias = jnp.stack([_na_bias_table(na_rpb[j]) for j in range(na_rpb.shape[0])])
    na = (na_w_qkv, na_bias, na_meta_bias, na_w_o)
    gdn = (gdn_w_in, gdn_conv_w, gdn_a_log, gdn_dt_bias, gdn_norm_w, gdn_w_out)
    ffn = (ffn_w_gate, ffn_w_up, ffn_conv_w, ffn_conv_b, ffn_w_down)
    y_prompt = _trunk(x_prompt, meta_tokens, mix_norm, ffn_norm, final_norm, rg, na, gdn, ffn)
    y_sample = _trunk(x_sample, meta_tokens, mix_norm, ffn_norm, final_norm, rg, na, gdn, ffn)
    return (y_prompt, y_sample)
```

```python
import functools

import numpy as np
import jax
import jax.numpy as jnp
from jax import lax
from jax.experimental import pallas as pl
from jax.experimental.pallas import tpu as pltpu

D_MODEL = 2048
N_META = 16
PAD = 48
HEAD_ROWS = PAD + N_META
CHUNK = 64
GRID_W = 64
NORM_EPS = 1e-6

RG_BLOCK = 256
RG_C = 8.0

NA_HEAD_DIM = 128
NA_HEADS = D_MODEL // NA_HEAD_DIM
NA_WIN_R = 8
NA_WIN_C = 16
NA_ROW_GROUP = 16
NEG_INF = -1e30

GDN_DK = 128
GDN_QK_HEADS = D_MODEL // GDN_DK
GDN_V_HEADS = 2 * GDN_QK_HEADS
GDN_KEY_DIM = GDN_QK_HEADS * GDN_DK
GDN_VAL_DIM = GDN_V_HEADS * GDN_DK
GDN_CONV_DIM = 2 * GDN_KEY_DIM + GDN_VAL_DIM

V7X_VMEM_LIMIT = 56 * 1024 * 1024
HALO = 16

BF16 = jnp.bfloat16
F32 = jnp.float32


def _cparams(*sem):
    return pltpu.CompilerParams(dimension_semantics=sem, vmem_limit_bytes=V7X_VMEM_LIMIT)


def _pick_tile(n, cap, mult):
    best = None
    for t in range(mult, cap + 1, mult):
        if n % t == 0:
            best = t
    assert best is not None, (n, cap, mult)
    return best


def _sigmoid(x):
    return 1.0 / (1.0 + jnp.exp(-x))


def _softplus(x):
    return jnp.maximum(x, 0.0) + jnp.log(1.0 + jnp.exp(-jnp.abs(x)))


def _gelu_tanh(x):
    return 0.5 * x * (1.0 + jnp.tanh(0.7978845608028654 * (x + 0.044715 * x * x * x)))


def _rms_scale(x, w):
    ms = jnp.mean(x * x, axis=-1, keepdims=True)
    return x * lax.rsqrt(ms + NORM_EPS) * w


def _dot(a, b):
    return jnp.dot(a, b, preferred_element_type=F32)


def _dot_nt(a, b):
    return lax.dot_general(a, b, (((1,), (1,)), ((), ())), preferred_element_type=F32)


def _dot_tn(a, b):
    return lax.dot_general(a, b, (((0,), (0,)), ((), ())), preferred_element_type=F32)


def _rmsnorm_kernel(x_ref, w_ref, o_ref):
    o_ref[...] = _rms_scale(x_ref[...], w_ref[...]).astype(o_ref.dtype)


def _rmsnorm(h, w, out_dtype):
    M, D = h.shape
    tm = _pick_tile(M, 1024, 64)
    return pl.pallas_call(
        _rmsnorm_kernel,
        grid=(M // tm,),
        in_specs=[pl.BlockSpec((tm, D), lambda i: (i, 0)), pl.BlockSpec((1, D), lambda i: (0, 0))],
        out_specs=pl.BlockSpec((tm, D), lambda i: (i, 0)),
        out_shape=jax.ShapeDtypeStruct((M, D), out_dtype),
        compiler_params=_cparams("parallel"),
    )(h, w.reshape(1, D))


def _final_norm_kernel(x_ref, w_ref, o_ref):
    o_ref[0] = _rms_scale(x_ref[...], w_ref[...])


def _final_norm(h, w, B, T):
    D = h.shape[1]
    Tp = T + HEAD_ROWS
    tt = 512
    assert T % tt == 0
    return pl.pallas_call(
        _final_norm_kernel,
        grid=(B, T // tt),
        in_specs=[pl.BlockSpec((pl.Element(tt), pl.Element(D)),
                               lambda b, j: (pl.multiple_of(b * Tp + HEAD_ROWS + j * tt, CHUNK), 0)),
                  pl.BlockSpec((1, D), lambda b, j: (0, 0))],
        out_specs=pl.BlockSpec((1, tt, D), lambda b, j: (b, j, 0)),
        out_shape=jax.ShapeDtypeStruct((B, T, D), F32),
        compiler_params=_cparams("parallel", "parallel"),
    )(h, w.reshape(1, D))


def _matmul_kernel(x_ref, w_ref, o_ref, *, act):
    acc = _dot(x_ref[...], w_ref[...])
    if act == "gelu":
        acc = _gelu_tanh(acc)
    o_ref[...] = acc.astype(o_ref.dtype)


def _matmul(x, w, out_dtype, act=None):
    M, K = x.shape
    N = w.shape[1]
    tm = _pick_tile(M, 1024, 128)
    tn = _pick_tile(N, 1024, 128)
    return pl.pallas_call(
        functools.partial(_matmul_kernel, act=act),
        grid=(M // tm, N // tn),
        in_specs=[pl.BlockSpec((tm, K), lambda i, j: (i, 0)), pl.BlockSpec((K, tn), lambda i, j: (0, j))],
        out_specs=pl.BlockSpec((tm, tn), lambda i, j: (i, j)),
        out_shape=jax.ShapeDtypeStruct((M, N), out_dtype),
        compiler_params=_cparams("parallel", "parallel"),
    )(x, w)


def _proj_res_kernel(x_ref, w_ref, h_ref, nw_ref, hout_ref, hn_ref):
    hnew = h_ref[...] + _dot(x_ref[...], w_ref[...])
    hout_ref[...] = hnew
    hn_ref[...] = _rms_scale(hnew, nw_ref[...]).astype(hn_ref.dtype)


def _proj_res(x, w, h, nw):
    M, K = x.shape
    D = w.shape[1]
    tm = _pick_tile(M, 704 if K <= D_MODEL else 512, 64)
    return pl.pallas_call(
        _proj_res_kernel,
        grid=(M // tm,),
        in_specs=[pl.BlockSpec((tm, K), lambda i: (i, 0)),
                  pl.BlockSpec((K, D), lambda i: (0, 0), pipeline_mode=pl.Buffered(1)),
                  pl.BlockSpec((tm, D), lambda i: (i, 0)),
                  pl.BlockSpec((1, D), lambda i: (0, 0))],
        out_specs=[pl.BlockSpec((tm, D), lambda i: (i, 0)), pl.BlockSpec((tm, D), lambda i: (i, 0))],
        out_shape=[jax.ShapeDtypeStruct((M, D), F32), jax.ShapeDtypeStruct((M, D), BF16)],
        compiler_params=_cparams("parallel"),
    )(x, w, h, nw.reshape(1, D))


def _proj_conv_kernel(xp_ref, xm_ref, xn_ref, w_ref, cw_ref, o_ref, xext_ref, *, tm, tc, ni, Tp, n_norm):
    i = pl.program_id(0)
    j = pl.program_id(1)
    npc = o_ref.shape[1] // tc

    @pl.when(j == 0)
    def _():
        xext_ref[0:HALO, :] = xp_ref[...]
        xext_ref[HALO:HALO + tm, :] = xm_ref[...]
        xext_ref[HALO + tm:2 * HALO + tm, :] = xn_ref[...]

    @pl.when(jnp.logical_and(j == 0, i == ni - 1))
    def _():
        xext_ref[HALO + tm:2 * HALO + tm, :] = jnp.zeros((HALO, xext_ref.shape[1]), xext_ref.dtype)

    rel = i * tm - (i * tm // Tp) * Tp + lax.broadcasted_iota(jnp.int32, (tm, 1), 0)
    live = jnp.where(rel >= Tp, rel - Tp, rel) >= PAD
    cw = cw_ref[...]
    xe = xext_ref[...]

    def piece_matmul(c):
        return _dot(xe, w_ref[:, c * tc:(c + 1) * tc])

    def piece_finish(c, g):
        cwc = cw[:, c * tc:(c + 1) * tc]
        a = (cwc[0:1] * g[HALO - 1:HALO - 1 + tm] + cwc[1:2] * g[HALO:HALO + tm]
             + cwc[2:3] * g[HALO + 1:HALO + 1 + tm] + cwc[3:4] * g[HALO + 2:HALO + 2 + tm])
        y = jnp.where(live, a * _sigmoid(a), 0.0)
        normalise = j * npc + c < n_norm
        heads = []
        for hh in range(tc // GDN_DK):
            yh = y[:, hh * GDN_DK:(hh + 1) * GDN_DK]
            nrm = lax.rsqrt(jnp.sum(yh * yh, axis=-1, keepdims=True) + 1e-6)
            heads.append(yh * jnp.where(normalise, nrm, 1.0))
        o_ref[:, c * tc:(c + 1) * tc] = jnp.concatenate(heads, axis=1).astype(o_ref.dtype)

    g = piece_matmul(0)
    for c in range(npc):
        g_next = piece_matmul(c + 1) if c + 1 < npc else None
        piece_finish(c, g)
        g = g_next


def _proj_conv(x, w, conv_w, Tp, n_norm_cols):
    M, K = x.shape
    N = w.shape[1]
    tm = _pick_tile(M, 1024, 64)
    assert tm <= Tp
    tn, tc = 2048, 512
    ni = M // tm
    hb = tm // HALO
    nhb = M // HALO
    return pl.pallas_call(
        functools.partial(_proj_conv_kernel, tm=tm, tc=tc, ni=ni, Tp=Tp, n_norm=n_norm_cols // tc),
        grid=(ni, N // tn),
        in_specs=[pl.BlockSpec((HALO, K), lambda i, j: (jnp.maximum(i * hb - 1, 0), 0)),
                  pl.BlockSpec((tm, K), lambda i, j: (i, 0)),
                  pl.BlockSpec((HALO, K), lambda i, j: (jnp.minimum((i + 1) * hb, nhb - 1), 0)),
                  pl.BlockSpec((K, tn), lambda i, j: (0, j)),
                  pl.BlockSpec((4, tn), lambda i, j: (0, j))],
        out_specs=pl.BlockSpec((tm, tn), lambda i, j: (i, j)),
        out_shape=jax.ShapeDtypeStruct((M, N), BF16),
        scratch_shapes=[pltpu.VMEM((tm + 2 * HALO, K), BF16)],
        compiler_params=_cparams("parallel", "arbitrary"),
    )(x, x, x, w, conv_w)


def _ffn_kernel(xp_ref, xm_ref, xn_ref, wg_ref, wu_ref, cw_ref, cb_ref, wd_ref, h_ref, nw_ref,
                hout_ref, hn_ref, xext_ref, *, tm, nf, ni):
    i = pl.program_id(0)
    f = pl.program_id(1)
    th = tm // 2

    @pl.when(f == 0)
    def _():
        xext_ref[0:HALO, :] = xp_ref[...]
        xext_ref[HALO:HALO + tm, :] = xm_ref[...]
        xext_ref[HALO + tm:2 * HALO + tm, :] = xn_ref[...]
        hout_ref[...] = h_ref[...]

    cw = cw_ref[...]
    cb = cb_ref[...]
    g, up = [], []
    for s in range(2):
        g.append(_dot(xext_ref[s * th:s * th + th + 2 * HALO, :], wg_ref[...]))
        up.append(_dot(xm_ref[s * th:(s + 1) * th, :], wu_ref[...]))
    row = lax.broadcasted_iota(jnp.int32, (th, 1), 0)
    y = []
    for s in range(2):
        g_prev = g[s][HALO - 1:HALO - 1 + th]
        g_here = g[s][HALO:HALO + th]
        g_next = g[s][HALO + 1:HALO + 1 + th]
        if s == 1:
            g_next = jnp.where(jnp.logical_and(i == ni - 1, row == th - 1), 0.0, g_next)
        a = cw[0:1] * g_prev + cw[1:2] * g_here + cw[2:3] * g_next + cb
        y.append((a * _sigmoid(a) * up[s]).astype(BF16))
    down = [_dot(y[s], wd_ref[...]) for s in range(2)]
    for s in range(2):
        hout_ref[s * th:(s + 1) * th, :] += down[s]

    @pl.when(f == nf - 1)
    def _():
        hn_ref[...] = _rms_scale(hout_ref[...], nw_ref[...]).astype(hn_ref.dtype)


def _conv_ffn(hn, h, w_gate, w_up, conv_w, conv_b, w_down, nw):
    M, D = hn.shape
    F = w_gate.shape[1]
    tm = _pick_tile(M, 704, 64)
    tf = _pick_tile(F, 512, 128)
    ni, nf = M // tm, F // tf
    hb = tm // HALO
    nhb = M // HALO
    return pl.pallas_call(
        functools.partial(_ffn_kernel, tm=tm, nf=nf, ni=ni),
        grid=(ni, nf),
        in_specs=[pl.BlockSpec((HALO, D), lambda i, f: (jnp.maximum(i * hb - 1, 0), 0)),
                  pl.BlockSpec((tm, D), lambda i, f: (i, 0)),
                  pl.BlockSpec((HALO, D), lambda i, f: (jnp.minimum((i + 1) * hb, nhb - 1), 0)),
                  pl.BlockSpec((D, tf), lambda i, f: (0, f)),
                  pl.BlockSpec((D, tf), lambda i, f: (0, f)),
                  pl.BlockSpec((3, tf), lambda i, f: (0, f)),
                  pl.BlockSpec((1, tf), lambda i, f: (0, f)),
                  pl.BlockSpec((tf, D), lambda i, f: (f, 0)),
                  pl.BlockSpec((tm, D), lambda i, f: (i, 0)),
                  pl.BlockSpec((1, D), lambda i, f: (0, 0))],
        out_specs=[pl.BlockSpec((tm, D), lambda i, f: (i, 0)), pl.BlockSpec((tm, D), lambda i, f: (i, 0))],
        out_shape=[jax.ShapeDtypeStruct((M, D), F32), jax.ShapeDtypeStruct((M, D), BF16)],
        scratch_shapes=[pltpu.VMEM((tm + 2 * HALO, D), BF16)],
        compiler_params=_cparams("parallel", "arbitrary"),
    )(hn, hn, hn, w_gate, w_up, conv_w, conv_b.reshape(1, F), w_down, h, nw.reshape(1, D))


def _scan8(a, u, row, reverse):
    for s in (1, 2, 4):
        if reverse:
            a_sh = pltpu.roll(a, 8 - s, 0)
            u_sh = pltpu.roll(u, 8 - s, 0)
            m = row < 8 - s
        else:
            a_sh = pltpu.roll(a, s, 0)
            u_sh = pltpu.roll(u, s, 0)
            m = row >= s
        u = jnp.where(m, a * u_sh + u, u)
        a = jnp.where(m, a * a_sh, a)
    return a, u


def _rg_kernel(rec_ref, gate_ref, cw_ref, cb_ref, wa_ref, ba_ref, wi_ref, bi_ref, lam_ref, o_ref,
               xc_s, hs_s, *, Tp, rc):
    nch = Tp // rc
    W = RG_BLOCK
    cw = cw_ref[...]
    cb = cb_ref[...]
    row8 = lax.broadcasted_iota(jnp.int32, (8, W), 0)

    def gates(xc, d):
        xb = xc.astype(BF16)
        r = _sigmoid(_dot(xb, wa_ref[d, 0]) + ba_ref[d])
        ig = _sigmoid(_dot(xb, wi_ref[d, 0]) + bi_ref[d])
        log_a = (-RG_C) * r * _softplus(-lam_ref[d])
        a = jnp.exp(log_a)
        u = jnp.sqrt(1.0 - a * a) * (ig * xc)
        return a, u

    def fwd_chunk(c, carry):
        r0 = pl.multiple_of(c * rc, CHUNK)
        top = rec_ref[0, pl.ds(pl.multiple_of(jnp.maximum(r0 - 8, 0), 8), 8), :]
        main = rec_ref[0, pl.ds(r0, rc), :]
        bot = rec_ref[0, pl.ds(pl.multiple_of(jnp.minimum(r0 + rc, Tp - 8), 8), 8), :]
        bot = jnp.where(r0 + rc < Tp, bot, 0.0)
        xw = jnp.concatenate([top, main, bot], axis=0)
        xc = cw[0:1] * xw[7:7 + rc] + cw[1:2] * xw[8:8 + rc] + cw[2:3] * xw[9:9 + rc] + cw[3:4] * xw[10:10 + rc] + cb
        t = r0 + lax.broadcasted_iota(jnp.int32, (rc, 1), 0)
        xc = jnp.where(t >= PAD, xc, 0.0)
        xc_s[pl.ds(r0, rc), :] = xc
        a, u = gates(xc, 0)
        hs = []
        for k in range(rc // 8):
            A, U = _scan8(a[8 * k:8 * k + 8], u[8 * k:8 * k + 8], row8, False)
            h = A * carry + U
            carry = h[7:8]
            hs.append(h)
        hs_s[pl.ds(r0, rc), :] = jnp.concatenate(hs, axis=0)
        return carry

    lax.fori_loop(0, nch, fwd_chunk, jnp.zeros((1, W), F32))

    def bwd_chunk(cc, carry):
        r0 = pl.multiple_of((nch - 1 - cc) * rc, CHUNK)
        a, u = gates(xc_s[pl.ds(r0, rc), :], 1)
        hs = [None] * (rc // 8)
        for k in reversed(range(rc // 8)):
            A, U = _scan8(a[8 * k:8 * k + 8], u[8 * k:8 * k + 8], row8, True)
            h = A * carry + U
            carry = h[0:1]
            hs[k] = h
        hsum = hs_s[pl.ds(r0, rc), :] + jnp.concatenate(hs, axis=0)
        o_ref[0, pl.ds(r0, rc), :] = (gate_ref[0, pl.ds(r0, rc), :].astype(F32) * hsum).astype(o_ref.dtype)
        return carry

    lax.fori_loop(0, nch, bwd_chunk, jnp.zeros((1, W), F32))


def _rg_scan(rec, gate, conv_w, conv_b, w_a, b_a, w_i, b_i, lam):
    B, Tp, W = rec.shape
    nb = W // RG_BLOCK
    rc = _pick_tile(Tp, 384, 64)
    blk = lambda b, n: (b, 0, n)
    vec = lambda b, n: (0, n)
    vec3 = lambda b, n: (0, 0, n)
    return pl.pallas_call(
        functools.partial(_rg_kernel, Tp=Tp, rc=rc),
        grid=(B, nb),
        in_specs=[pl.BlockSpec((1, Tp, RG_BLOCK), blk),
                  pl.BlockSpec((1, Tp, RG_BLOCK), blk),
                  pl.BlockSpec((4, RG_BLOCK), vec),
                  pl.BlockSpec((1, RG_BLOCK), vec),
                  pl.BlockSpec((2, 1, RG_BLOCK, RG_BLOCK), lambda b, n: (0, n, 0, 0)),
                  pl.BlockSpec((2, 1, RG_BLOCK), vec3),
                  pl.BlockSpec((2, 1, RG_BLOCK, RG_BLOCK), lambda b, n: (0, n, 0, 0)),
                  pl.BlockSpec((2, 1, RG_BLOCK), vec3),
                  pl.BlockSpec((2, 1, RG_BLOCK), vec3)],
        out_specs=pl.BlockSpec((1, Tp, RG_BLOCK), blk),
        out_shape=jax.ShapeDtypeStruct((B, Tp, W), BF16),
        scratch_shapes=[pltpu.VMEM((Tp, RG_BLOCK), F32), pltpu.VMEM((Tp, RG_BLOCK), F32)],
        compiler_params=_cparams("parallel", "parallel"),
    )(rec, gate, conv_w, conv_b.reshape(1, W), w_a.astype(BF16), b_a.reshape(2, 1, W),
      w_i.astype(BF16), b_i.reshape(2, 1, W), lam.reshape(2, 1, W))


def _rglru_mixer(hn, h, B, Tp, w_in, conv_w, conv_b, w_a, b_a, w_i, b_i, lam, w_out, nw):
    W = D_MODEL
    w_in = w_in.astype(BF16)
    gate = _matmul(hn, w_in[:, :W], BF16, act="gelu")
    rec = _matmul(hn, w_in[:, W:], F32)
    y = _rg_scan(rec.reshape(B, Tp, W), gate.reshape(B, Tp, W), conv_w, conv_b, w_a, b_a, w_i, b_i, lam)
    return _proj_res(y.reshape(B * Tp, W), w_out.astype(BF16), h, nw)


def _na_kernel(q_ref, k_ref, v_ref, bias_ref, mb_ref, o_ref, *, rows):
    scale = NA_HEAD_DIM ** -0.5
    kh = NA_WIN_R
    mb = mb_ref[0]
    km = k_ref[0, PAD:HEAD_ROWS, :]
    vm = v_ref[0, PAD:HEAD_ROWS, :]

    o_ref[0, 0:PAD, :] = jnp.zeros((PAD, NA_HEAD_DIM), o_ref.dtype)
    qm = q_ref[0, PAD:HEAD_ROWS, :]
    s_m = _dot_nt(qm, km) * scale + mb
    p_m = jnp.exp(s_m - jnp.max(s_m, axis=-1, keepdims=True))
    o_m = _dot(p_m.astype(BF16), vm) / jnp.sum(p_m, axis=-1, keepdims=True)
    o_ref[0, PAD:HEAD_ROWS, :] = o_m.astype(o_ref.dtype)

    group = np.gcd(rows, NA_ROW_GROUP)
    R = range(group)

    def group_body(i, _):
        r = [i * group + c for c in R]
        rs = [jnp.clip(r[c] - NA_WIN_R // 2, 0, rows - kh) for c in R]
        q0 = [pl.multiple_of(HEAD_ROWS + r[c] * GRID_W, GRID_W) for c in R]
        k0 = [pl.multiple_of(HEAD_ROWS + rs[c] * GRID_W, GRID_W) for c in R]
        q_r = [q_ref[0, pl.ds(q0[c], GRID_W), :] for c in R]
        s = [_dot_nt(q_r[c], k_ref[0, pl.ds(k0[c], kh * GRID_W), :]) for c in R]
        s_met = [_dot_nt(q_r[c], km) for c in R]
        p, p_met, l = [], [], []
        for c in R:
            sc = s[c] * scale + bias_ref[0, r[c] - rs[c]]
            sm = s_met[c] * scale + mb
            m = jnp.maximum(jnp.max(sc, axis=-1, keepdims=True), jnp.max(sm, axis=-1, keepdims=True))
            pc = jnp.exp(sc - m)
            pm = jnp.exp(sm - m)
            l.append(jnp.sum(pc, axis=-1, keepdims=True) + jnp.sum(pm, axis=-1, keepdims=True))
            p.append(pc.astype(BF16))
            p_met.append(pm.astype(BF16))
        o = [_dot(p[c], v_ref[0, pl.ds(k0[c], kh * GRID_W), :]) + _dot(p_met[c], vm) for c in R]
        for c in R:
            o_ref[0, pl.ds(q0[c], GRID_W), :] = (o[c] / l[c]).astype(o_ref.dtype)
        return 0

    lax.fori_loop(0, rows // group, group_body, 0)


def _na_bias_table(rpb):
    kh = NA_WIN_R
    cols = np.arange(GRID_W)
    col_start = np.clip(cols - NA_WIN_C // 2, 0, GRID_W - NA_WIN_C)
    valid = (cols[None, :] >= col_start[:, None]) & (cols[None, :] < col_start[:, None] + NA_WIN_C)
    col_off = np.clip(cols[None, :] - cols[:, None], -(NA_WIN_C - 1), NA_WIN_C - 1) + NA_WIN_C - 1
    H, n_ro, n_co = rpb.shape
    onehot = (col_off[None] == np.arange(n_co)[:, None, None]).astype(np.float32)
    a = jnp.einsum('hrc,cqk->hqrk', rpb.astype(F32), jnp.asarray(onehot), precision=lax.Precision.HIGHEST)
    a = jnp.where(jnp.asarray(valid)[None, :, None, :], a, NEG_INF)
    tabs = [a[:, :, NA_WIN_R - 1 - d:NA_WIN_R - 1 - d + kh, :].reshape(H, GRID_W, kh * GRID_W)
            for d in range(NA_WIN_R)]
    return jnp.stack(tabs, axis=1)


def _na_attention(qkv, bias, meta_bias):
    B, Tp, _ = qkv.shape
    rows = (Tp - HEAD_ROWS) // GRID_W
    assert rows >= NA_WIN_R
    H = NA_HEADS
    return pl.pallas_call(
        functools.partial(_na_kernel, rows=rows),
        grid=(B, H),
        in_specs=[pl.BlockSpec((1, Tp, NA_HEAD_DIM), lambda b, h: (b, 0, h)),
                  pl.BlockSpec((1, Tp, NA_HEAD_DIM), lambda b, h: (b, 0, H + h)),
                  pl.BlockSpec((1, Tp, NA_HEAD_DIM), lambda b, h: (b, 0, 2 * H + h)),
                  pl.BlockSpec((1, NA_WIN_R, GRID_W, NA_WIN_R * GRID_W), lambda b, h: (h, 0, 0, 0)),
                  pl.BlockSpec((1, 1, N_META), lambda b, h: (h, 0, 0))],
        out_specs=pl.BlockSpec((1, Tp, NA_HEAD_DIM), lambda b, h: (b, 0, h)),
        out_shape=jax.ShapeDtypeStruct((B, Tp, H * NA_HEAD_DIM), BF16),
        compiler_params=_cparams("parallel", "parallel"),
    )(qkv, qkv, qkv, bias, meta_bias.astype(F32).reshape(H, 1, N_META))


def _na_mixer(hn, h, B, Tp, w_qkv, bias, meta_bias, w_o, nw):
    qkv = _matmul(hn, w_qkv.astype(BF16), BF16)
    o = _na_attention(qkv.reshape(B, Tp, -1), bias, meta_bias)
    return _proj_res(o.reshape(B * Tp, -1), w_o.astype(BF16), h, nw)


def _gdn_gate_kernel(x_ref, w_ref, al_ref, dt_ref, o_ref):
    y = _dot(x_ref[...], w_ref[...])
    lane = lax.broadcasted_iota(jnp.int32, y.shape, 1)
    g = -jnp.exp(al_ref[...]) * _softplus(y + dt_ref[...])
    o_ref[...] = jnp.where(lane < 2 * GDN_V_HEADS, _sigmoid(y), g)


def _gdn_gates(hn, w_ba, a_log, dt_bias):
    M, K = hn.shape
    N = 4 * GDN_V_HEADS
    tm = _pick_tile(M, 1024, 128)
    zeros = jnp.zeros((2 * GDN_V_HEADS,), F32)
    al = jnp.concatenate([zeros, a_log.astype(F32).reshape(-1)]).reshape(1, N)
    dt = jnp.concatenate([zeros, dt_bias.astype(F32).reshape(-1)]).reshape(1, N)
    return pl.pallas_call(
        _gdn_gate_kernel,
        grid=(M // tm,),
        in_specs=[pl.BlockSpec((tm, K), lambda i: (i, 0)), pl.BlockSpec((K, N), lambda i: (0, 0)),
                  pl.BlockSpec((1, N), lambda i: (0, 0)), pl.BlockSpec((1, N), lambda i: (0, 0))],
        out_specs=pl.BlockSpec((tm, N), lambda i: (i, 0)),
        out_shape=jax.ShapeDtypeStruct((M, N), F32),
        compiler_params=_cparams("parallel"),
    )(hn, w_ba, al, dt)


def _split3_dot(x, m):
    hi = x.astype(BF16)
    r1 = x - hi.astype(F32)
    mid = r1.astype(BF16)
    lo = (r1 - mid.astype(F32)).astype(BF16)
    return _dot(hi, m) + _dot(mid, m) + _dot(lo, m)


def _gdn_kernel(q_ref, k_ref, v_ref, z_ref, bgc_ref, bgr_ref, nw_ref, o_ref,
                S_s, oacc_s, u_s, wq_s, kst_s, qkd_s, gam_s, *, Tp, HPS):
    hb = pl.program_id(1)
    C = CHUNK
    NC = Tp // C
    HV = GDN_V_HEADS
    DK = GDN_DK
    G = 4
    W = G * C
    P1_GROUP = 6 if HPS == 1 else 4
    scale = DK ** -0.5

    oacc_s[...] = jnp.zeros_like(oacc_s)
    S_s[...] = jnp.zeros_like(S_s)

    ri = lax.broadcasted_iota(jnp.int32, (C, W), 0)
    cl = lax.broadcasted_iota(jnp.int32, (C, W), 1)
    cj = jnp.bitwise_and(cl, C - 1)
    blk = jnp.right_shift(cl, 6)
    dd = jnp.where(blk >= 2, cj - ri, ri - cj)
    incl4 = dd >= 0
    strict4 = dd > 0
    eye4 = (ri == cj).astype(F32)
    cum4 = jnp.where(dd <= 0, 1.0, 0.0).astype(BF16)
    bdm = jnp.where(jnp.right_shift(lax.broadcasted_iota(jnp.int32, (W, W), 0), 6)
                    == jnp.right_shift(lax.broadcasted_iota(jnp.int32, (W, W), 1), 6), 1.0, 0.0).astype(BF16)
    blk_row = jnp.right_shift(lax.broadcasted_iota(jnp.int32, (1, W), 1), 6)
    lane = lax.broadcasted_iota(jnp.int32, (C, 4 * HV), 1)
    rowi = lax.broadcasted_iota(jnp.int32, (C, 4 * HV), 0)

    def col(x, c):
        return jnp.sum(jnp.where(lane == c, x, 0.0), axis=1, keepdims=True)

    def by_block(parts, b):
        return jnp.where(b == 0, parts[0], jnp.where(b == 1, parts[1], jnp.where(b == 2, parts[2], parts[3])))

    def blockdiag(xb):
        return jnp.concatenate([xb] * G, axis=0) * bdm

    def phase1(ns):
        items = [(hd, n) for n in ns for hd in range(HPS)]
        K = range(len(items))
        r0 = [pl.multiple_of(n * C, C) for (hd, n) in items]
        q = [q_ref[0, pl.ds(r0[i], C), items[i][0] * DK:(items[i][0] + 1) * DK] for i in K]
        k = [k_ref[0, pl.ds(r0[i], C), items[i][0] * DK:(items[i][0] + 1) * DK] for i in K]
        bg = [bgc_ref[0, pl.ds(r0[i], C), :] for i in K]
        v = [v_ref[0, pl.ds(r0[i], C), items[i][0] * 2 * DK:(items[i][0] + 1) * 2 * DK].astype(F32) for i in K]
        rows = [bgr_ref[0, n, hd] for (hd, n) in items]
        cum_rows = [_split3_dot(rows[i], cum4) for i in K]
        k4 = [jnp.concatenate([k[i]] * G, axis=0) for i in K]
        kk4 = [_dot_nt(k[i], k4[i]) for i in K]
        qk4 = [_dot_nt(q[i], k4[i]) * scale for i in K]
        beta_c, gc_c, decay4, L = [], [], [], []
        for i in K:
            pre = bg[i]
            for s in (1, 2, 4, 8, 16, 32):
                pre = pre + jnp.where(rowi >= s, pltpu.roll(pre, s, 0), 0.0)
            suf = pre[C - 1:C] - pre + bg[i]
            gc_r = by_block([cum_rows[i][4 + g:5 + g] for g in range(G)], blk_row)
            bc, gc = [], []
            for g in range(G):
                d, j = g // 2, g % 2
                vh = 2 * (hb * HPS + items[i][0]) + j
                bc.append(col(bg[i], d * HV + vh))
                gc.append(col(suf if d else pre, 2 * HV + d * HV + vh))
            beta_c.append(bc)
            gc_c.append(gc)
            dec = jnp.where(incl4, jnp.exp(jnp.where(incl4, by_block(gc, blk) - gc_r, 0.0)), 0.0)
            decay4.append(dec)
            L.append(jnp.where(strict4, kk4[i] * dec, 0.0) * by_block(bc, blk))
        Tm = [eye4 - L[i] for i in K]
        Lb = [L[i].astype(BF16) for i in K]
        P = [_dot(Lb[i], blockdiag(Lb[i])) for i in K]
        for lvl in range(5):
            Pb = [P[i].astype(BF16) for i in K]
            bd = [blockdiag(Pb[i]) for i in K]
            if lvl < 4:
                tp = [_dot(jnp.concatenate([Tm[i].astype(BF16), Pb[i]], axis=0), bd[i]) for i in K]
                Tm = [Tm[i] + tp[i][:C] for i in K]
                P = [tp[i][C:] for i in K]
            else:
                tp = [_dot(Tm[i].astype(BF16), bd[i]) for i in K]
                Tm = [Tm[i] + tp[i] for i in K]
        rhs, kst, qst, gam = [], [], [], []
        for i in K:
            qf = q[i].astype(F32)
            kf = k[i].astype(F32)
            rhs_i, kst_i, qst_i, gam_i = [], [], [], []
            for g in range(G):
                j = g % 2
                bc, gc = beta_c[i][g], gc_c[i][g]
                e_c = jnp.exp(gc)
                rhs_i.append(jnp.concatenate([v[i][:, j * DK:(j + 1) * DK] * bc, kf * (bc * e_c)], axis=1))
                g_last = gc[0:1] if g >= 2 else gc[C - 1:C]
                kst_i.append((kf * jnp.exp(g_last - gc)).astype(BF16))
                qst_i.append((qf * (scale * e_c)).astype(BF16))
                gam_i.append(jnp.broadcast_to(jnp.exp(g_last), (1, DK)))
            rhs.append(jnp.concatenate(rhs_i, axis=0).astype(BF16))
            kst.append(kst_i)
            qst.append(qst_i)
            gam.append(gam_i)
        sol = [_dot(blockdiag(Tm[i].astype(BF16)), rhs[i]) for i in K]
        for i in K:
            hd, n = items[i]
            u_s[hd, n] = sol[i][:, :DK]
            qkd_s[hd, n] = jnp.where(incl4, qk4[i] * decay4[i], 0.0).astype(BF16)
            for g in range(G):
                wq_s[hd, n, g, 0:C, :] = sol[i][g * C:(g + 1) * C, DK:].astype(BF16)
                wq_s[hd, n, g, C:2 * C, :] = qst[i][g]
                kst_s[hd, n, g] = kst[i][g]
                gam_s[hd, n, g:g + 1, :] = gam[i][g]

    def phase1_group(i, _):
        phase1([P1_GROUP * i + c for c in range(P1_GROUP)])
        return 0

    lax.fori_loop(0, NC // P1_GROUP, phase1_group, 0)
    if NC % P1_GROUP:
        phase1(list(range(NC - NC % P1_GROUP, NC)))

    nw = nw_ref[...]

    def gated_norm(o, z):
        ys = []
        for j in range(2 * HPS):
            oj = o[:, j * DK:(j + 1) * DK]
            zj = z[:, j * DK:(j + 1) * DK].astype(F32)
            ys.append(oj * lax.rsqrt(jnp.mean(oj * oj, axis=-1, keepdims=True) + NORM_EPS) * nw * (zj * _sigmoid(zj)))
        return jnp.concatenate(ys, axis=1)

    def finalize(n):
        r0 = pl.multiple_of(n * C, C)
        o_ref[0, pl.ds(r0, C), :] = gated_norm(oacc_s[pl.ds(r0, C), :], z_ref[0, pl.ds(r0, C), :]).astype(o_ref.dtype)

    def phase2(it, finalize_previous):
        if finalize_previous:
            finalize(it - 1)
            finalize(NC - it)
        ops = []
        for hd in range(HPS):
            for g in range(G):
                n = it if g < 2 else NC - 1 - it
                ops.append((S_s[hd * G + g], wq_s[hd, n, g], u_s[hd, n, g * C:(g + 1) * C, :],
                            qkd_s[hd, n, :, g * C:(g + 1) * C], gam_s[hd, n, g:g + 1, :], kst_s[hd, n, g]))
        NI = HPS * G
        r_f = pl.multiple_of(it * C, C)
        r_b = pl.multiple_of((NC - 1 - it) * C, C)
        o_f = oacc_s[pl.ds(r_f, C), :]
        ws = [_dot(wq, S.astype(BF16)) for (S, wq, u, qkd, gam, kst) in ops]
        vb = [(ops[g][2] - ws[g][:C]).astype(BF16) for g in range(NI)]
        outs = [ws[g][C:] + _dot(ops[g][3], vb[g]) for g in range(NI)]
        states = [ops[g][0] * ops[g][4] + _dot_tn(ops[g][5], vb[g]) for g in range(NI)]
        for g in range(NI):
            S_s[g] = states[g]
        fwd = [outs[hd * G + j] for hd in range(HPS) for j in range(2)]
        bwd = [outs[hd * G + 2 + j] for hd in range(HPS) for j in range(2)]
        o_f = o_f + jnp.concatenate(fwd, axis=1)
        oacc_s[pl.ds(r_f, C), :] = o_f
        oacc_s[pl.ds(r_b, C), :] = oacc_s[pl.ds(r_b, C), :] + jnp.concatenate(bwd, axis=1)
        return 0

    lax.fori_loop(0, NC // 2 + 1, lambda it, c: phase2(it, False), 0)
    lax.fori_loop(NC // 2 + 1, NC, lambda it, c: phase2(it, True), 0)
    finalize(NC - 1)
    finalize(0)


_GDN_CHUNK_BYTES = CHUNK * GDN_DK * (4 * 4 + 8 * 2 + 4 * 2 + 4 * 2) + 8 * GDN_DK * 4
_GDN_PAIR_BUDGET = 26 * 1024 * 1024


def _gdn_delta(qkv, z, bg, norm_w):
    B, Tp, _ = qkv.shape
    NC = Tp // CHUNK
    KH = GDN_QK_HEADS
    HV = GDN_V_HEADS
    C = CHUNK
    HPS = 2 if NC * _GDN_CHUNK_BYTES * 2 <= _GDN_PAIR_BUDGET else 1
    NH = KH // HPS
    bgr = bg.reshape(B, NC, CHUNK, 4, KH, 2).transpose(0, 1, 4, 3, 5, 2).reshape(B, NC, KH, 8, CHUNK)
    return pl.pallas_call(
        functools.partial(_gdn_kernel, Tp=Tp, HPS=HPS),
        grid=(B, NH),
        in_specs=[pl.BlockSpec((1, Tp, HPS * GDN_DK), lambda b, h: (b, 0, h)),
                  pl.BlockSpec((1, Tp, HPS * GDN_DK), lambda b, h: (b, 0, NH + h)),
                  pl.BlockSpec((1, Tp, 2 * HPS * GDN_DK), lambda b, h: (b, 0, NH + h)),
                  pl.BlockSpec((1, Tp, 2 * HPS * GDN_DK), lambda b, h: (b, 0, h)),
                  pl.BlockSpec((1, Tp, 4 * HV), lambda b, h: (b, 0, 0)),
                  pl.BlockSpec((1, NC, HPS, 8, CHUNK), lambda b, h: (b, 0, h, 0, 0)),
                  pl.BlockSpec((1, GDN_DK), lambda b, h: (0, 0))],
        out_specs=pl.BlockSpec((1, Tp, 2 * HPS * GDN_DK), lambda b, h: (b, 0, h)),
        out_shape=jax.ShapeDtypeStruct((B, Tp, GDN_VAL_DIM), BF16),
        scratch_shapes=[pltpu.VMEM((4 * HPS, GDN_DK, GDN_DK), F32),
                        pltpu.VMEM((Tp, 2 * HPS * GDN_DK), F32),
                        pltpu.VMEM((HPS, NC, 4 * C, GDN_DK), F32),
                        pltpu.VMEM((HPS, NC, 4, 2 * C, GDN_DK), BF16),
                        pltpu.VMEM((HPS, NC, 4, C, GDN_DK), BF16),
                        pltpu.VMEM((HPS, NC, C, 4 * C), BF16),
                        pltpu.VMEM((HPS, NC, 8, GDN_DK), F32)],
        compiler_params=_cparams("parallel", "parallel"),
    )(qkv, qkv, qkv, z, bg, bgr, norm_w.astype(F32).reshape(1, GDN_DK))


def _gdn_mixer(hn, h, B, Tp, w_in, conv_w, a_log, dt_bias, norm_w, w_out, nw):
    nz = GDN_CONV_DIM + GDN_VAL_DIM
    w_in = w_in.astype(BF16)
    qkv = _proj_conv(hn, w_in[:, :GDN_CONV_DIM], conv_w, Tp, 2 * GDN_KEY_DIM)
    z = _matmul(hn, w_in[:, GDN_CONV_DIM:nz], BF16)
    bg = _gdn_gates(hn, w_in[:, nz:], a_log, dt_bias)
    o = _gdn_delta(qkv.reshape(B, Tp, -1), z.reshape(B, Tp, -1), bg.reshape(B, Tp, -1), norm_w)
    return _proj_res(o.reshape(B * Tp, -1), w_out.astype(BF16), h, nw)


def _trunk(x, meta_tokens, mix_norm, ffn_norm, final_norm, rg, na, gdn, ffn):
    B, T, D = x.shape
    Tp = T + HEAD_ROWS
    depth = mix_norm.shape[0]
    h3 = jnp.concatenate([jnp.zeros((B, PAD, D), F32),
                          jnp.broadcast_to(meta_tokens.astype(F32)[None], (B, N_META, D)),
                          x.astype(F32)], axis=1)
    h = h3.reshape(B * Tp, D)
    hn = _rmsnorm(h, mix_norm[0], BF16)
    for i in range(depth):
        kind, j = i % 3, i // 3
        if kind == 0:
            h, hn = _rglru_mixer(hn, h, B, Tp, *[p[j] for p in rg], ffn_norm[i])
        elif kind == 1:
            h, hn = _na_mixer(hn, h, B, Tp, *[p[j] for p in na], ffn_norm[i])
        else:
            h, hn = _gdn_mixer(hn, h, B, Tp, *[p[j] for p in gdn], ffn_norm[i])
        w_gate, w_up, conv_w, conv_b, w_down = [p[i] for p in ffn]
        next_norm = mix_norm[i + 1] if i + 1 < depth else final_norm
        h, hn = _conv_ffn(hn, h, w_gate.astype(BF16), w_up.astype(BF16), conv_w, conv_b,
                          w_down.astype(BF16), next_norm)
    return _final_norm(h, final_norm, B, T)


def kernel(x_prompt, x_sample, meta_tokens, mix_norm, ffn_norm, final_norm, rg_w_in, rg_conv_w, rg_conv_b, rg_w_a, rg_b_a, rg_w_i, rg_b_i, rg_lam, rg_w_out, na_w_qkv, na_rpb, na_meta_bias, na_w_o, gdn_w_in, gdn_conv_w, gdn_a_log, gdn_dt_bias, gdn_norm_w, gdn_w_out, ffn_w_gate, ffn_w_up, ffn_conv_w, ffn_conv_b, ffn_w_down):
    rg = (rg_w_in, rg_conv_w, rg_conv_b, rg_w_a, rg_b_a, rg_w_i, rg_b_i, rg_lam, rg_w_out)
    na_bias = jnp.stack([_na_bias_table(na_rpb[j]) for j in range(na_rpb.shape[0])])
    na = (na_w_qkv, na_bias, na_meta_bias, na_w_o)
    gdn = (gdn_w_in, gdn_conv_w, gdn_a_log, gdn_dt_bias, gdn_norm_w, gdn_w_out)
    ffn = (ffn_w_gate, ffn_w_up, ffn_conv_w, ffn_conv_b, ffn_w_down)
    y_prompt = _trunk(x_prompt, meta_tokens, mix_norm, ffn_norm, final_norm, rg, na, gdn, ffn)
    y_sample = _trunk(x_sample, meta_tokens, mix_norm, ffn_norm, final_norm, rg, na, gdn, ffn)
    return (y_prompt, y_sample)
```

```python
import functools

import numpy as np
import jax
import jax.numpy as jnp
from jax import lax
from jax.experimental import pallas as pl
from jax.experimental.pallas import tpu as pltpu

D_MODEL = 2048
N_META = 16
PAD = 48
HEAD_ROWS = PAD + N_META
CHUNK = 64
GRID_W = 64
NORM_EPS = 1e-6

RG_BLOCK = 256
RG_C = 8.0

NA_HEAD_DIM = 128
NA_HEADS = D_MODEL // NA_HEAD_DIM
NA_WIN_R = 8
NA_WIN_C = 16
NA_ROW_GROUP = 16
NEG_INF = -1e30

GDN_DK = 128
GDN_QK_HEADS = D_MODEL // GDN_DK
GDN_V_HEADS = 2 * GDN_QK_HEADS
GDN_KEY_DIM = GDN_QK_HEADS * GDN_DK
GDN_VAL_DIM = GDN_V_HEADS * GDN_DK
GDN_CONV_DIM = 2 * GDN_KEY_DIM + GDN_VAL_DIM

V7X_VMEM_LIMIT = 56 * 1024 * 1024
HALO = 16

BF16 = jnp.bfloat16
F32 = jnp.float32


def _cparams(*sem):
    return pltpu.CompilerParams(dimension_semantics=sem, vmem_limit_bytes=V7X_VMEM_LIMIT)


def _pick_tile(n, cap, mult):
    best = None
    for t in range(mult, cap + 1, mult):
        if n % t == 0:
            best = t
    assert best is not None, (n, cap, mult)
    return best


def _sigmoid(x):
    return 1.0 / (1.0 + jnp.exp(-x))


def _softplus(x):
    return jnp.maximum(x, 0.0) + jnp.log(1.0 + jnp.exp(-jnp.abs(x)))


def _gelu_tanh(x):
    return 0.5 * x * (1.0 + jnp.tanh(0.7978845608028654 * (x + 0.044715 * x * x * x)))


def _rms_scale(x, w):
    ms = jnp.mean(x * x, axis=-1, keepdims=True)
    return x * lax.rsqrt(ms + NORM_EPS) * w


def _dot(a, b):
    return jnp.dot(a, b, preferred_element_type=F32)


def _dot_nt(a, b):
    return lax.dot_general(a, b, (((1,), (1,)), ((), ())), preferred_element_type=F32)


def _dot_tn(a, b):
    return lax.dot_general(a, b, (((0,), (0,)), ((), ())), preferred_element_type=F32)


def _rmsnorm_kernel(x_ref, w_ref, o_ref):
    o_ref[...] = _rms_scale(x_ref[...], w_ref[...]).astype(o_ref.dtype)


def _rmsnorm(h, w, out_dtype):
    M, D = h.shape
    tm = _pick_tile(M, 1024, 64)
    return pl.pallas_call(
        _rmsnorm_kernel,
        grid=(M // tm,),
        in_specs=[pl.BlockSpec((tm, D), lambda i: (i, 0)), pl.BlockSpec((1, D), lambda i: (0, 0))],
        out_specs=pl.BlockSpec((tm, D), lambda i: (i, 0)),
        out_shape=jax.ShapeDtypeStruct((M, D), out_dtype),
        compiler_params=_cparams("parallel"),
    )(h, w.reshape(1, D))


def _final_norm_kernel(x_ref, w_ref, o_ref):
    o_ref[0] = _rms_scale(x_ref[...], w_ref[...])


def _final_norm(h, w, B, T):
    D = h.shape[1]
    Tp = T + HEAD_ROWS
    tt = 512
    assert T % tt == 0
    return pl.pallas_call(
        _final_norm_kernel,
        grid=(B, T // tt),
        in_specs=[pl.BlockSpec((pl.Element(tt), pl.Element(D)),
                               lambda b, j: (pl.multiple_of(b * Tp + HEAD_ROWS + j * tt, CHUNK), 0)),
                  pl.BlockSpec((1, D), lambda b, j: (0, 0))],
        out_specs=pl.BlockSpec((1, tt, D), lambda b, j: (b, j, 0)),
        out_shape=jax.ShapeDtypeStruct((B, T, D), F32),
        compiler_params=_cparams("parallel", "parallel"),
    )(h, w.reshape(1, D))


def _matmul_kernel(x_ref, w_ref, o_ref, *, act):
    acc = _dot(x_ref[...], w_ref[...])
    if act == "gelu":
        acc = _gelu_tanh(acc)
    o_ref[...] = acc.astype(o_ref.dtype)


def _matmul(x, w, out_dtype, act=None):
    M, K = x.shape
    N = w.shape[1]
    tm = _pick_tile(M, 1024, 128)
    tn = _pick_tile(N, 1024, 128)
    return pl.pallas_call(
        functools.partial(_matmul_kernel, act=act),
        grid=(M // tm, N // tn),
        in_specs=[pl.BlockSpec((tm, K), lambda i, j: (i, 0)), pl.BlockSpec((K, tn), lambda i, j: (0, j))],
        out_specs=pl.BlockSpec((tm, tn), lambda i, j: (i, j)),
        out_shape=jax.ShapeDtypeStruct((M, N), out_dtype),
        compiler_params=_cparams("parallel", "parallel"),
    )(x, w)


def _proj_res_kernel(x_ref, w_ref, h_ref, nw_ref, hout_ref, hn_ref):
    hnew = h_ref[...] + _dot(x_ref[...], w_ref[...])
    hout_ref[...] = hnew
    hn_ref[...] = _rms_scale(hnew, nw_ref[...]).astype(hn_ref.dtype)


def _proj_res(x, w, h, nw):
    M, K = x.shape
    D = w.shape[1]
    tm = _pick_tile(M, 704 if K <= D_MODEL else 512, 64)
    return pl.pallas_call(
        _proj_res_kernel,
        grid=(M // tm,),
        in_specs=[pl.BlockSpec((tm, K), lambda i: (i, 0)),
                  pl.BlockSpec((K, D), lambda i: (0, 0), pipeline_mode=pl.Buffered(1)),
                  pl.BlockSpec((tm, D), lambda i: (i, 0)),
                  pl.BlockSpec((1, D), lambda i: (0, 0))],
        out_specs=[pl.BlockSpec((tm, D), lambda i: (i, 0)), pl.BlockSpec((tm, D), lambda i: (i, 0))],
        out_shape=[jax.ShapeDtypeStruct((M, D), F32), jax.ShapeDtypeStruct((M, D), BF16)],
        compiler_params=_cparams("parallel"),
    )(x, w, h, nw.reshape(1, D))


def _proj_conv_kernel(xp_ref, xm_ref, xn_ref, w_ref, cw_ref, o_ref, xext_ref, *, tm, tc, ni, Tp, n_norm):
    i = pl.program_id(0)
    j = pl.program_id(1)
    npc = o_ref.shape[1] // tc

    @pl.when(j == 0)
    def _():
        xext_ref[0:HALO, :] = xp_ref[...]
        xext_ref[HALO:HALO + tm, :] = xm_ref[...]
        xext_ref[HALO + tm:2 * HALO + tm, :] = xn_ref[...]

    @pl.when(jnp.logical_and(j == 0, i == ni - 1))
    def _():
        xext_ref[HALO + tm:2 * HALO + tm, :] = jnp.zeros((HALO, xext_ref.shape[1]), xext_ref.dtype)

    rel = i * tm - (i * tm // Tp) * Tp + lax.broadcasted_iota(jnp.int32, (tm, 1), 0)
    live = jnp.where(rel >= Tp, rel - Tp, rel) >= PAD
    cw = cw_ref[...]
    xe = xext_ref[...]

    def piece_matmul(c):
        return _dot(xe, w_ref[:, c * tc:(c + 1) * tc])

    def piece_finish(c, g):
        cwc = cw[:, c * tc:(c + 1) * tc]
        a = (cwc[0:1] * g[HALO - 1:HALO - 1 + tm] + cwc[1:2] * g[HALO:HALO + tm]
             + cwc[2:3] * g[HALO + 1:HALO + 1 + tm] + cwc[3:4] * g[HALO + 2:HALO + 2 + tm])
        y = jnp.where(live, a * _sigmoid(a), 0.0)
        normalise = j * npc + c < n_norm
        heads = []
        for hh in range(tc // GDN_DK):
            yh = y[:, hh * GDN_DK:(hh + 1) * GDN_DK]
            nrm = lax.rsqrt(jnp.sum(yh * yh, axis=-1, keepdims=True) + 1e-6)
            heads.append(yh * jnp.where(normalise, nrm, 1.0))
        o_ref[:, c * tc:(c + 1) * tc] = jnp.concatenate(heads, axis=1).astype(o_ref.dtype)

    g = piece_matmul(0)
    for c in range(npc):
        g_next = piece_matmul(c + 1) if c + 1 < npc else None
        piece_finish(c, g)
        g = g_next


def _proj_conv(x, w, conv_w, Tp, n_norm_cols):
    M, K = x.shape
    N = w.shape[1]
    tm = _pick_tile(M, 1024, 64)
    assert tm <= Tp
    tn, tc = 2048, 512
    ni = M // tm
    hb = tm // HALO
    nhb = M // HALO
    return pl.pallas_call(
        functools.partial(_proj_conv_kernel, tm=tm, tc=tc, ni=ni, Tp=Tp, n_norm=n_norm_cols // tc),
        grid=(ni, N // tn),
        in_specs=[pl.BlockSpec((HALO, K), lambda i, j: (jnp.maximum(i * hb - 1, 0), 0)),
                  pl.BlockSpec((tm, K), lambda i, j: (i, 0)),
                  pl.BlockSpec((HALO, K), lambda i, j: (jnp.minimum((i + 1) * hb, nhb - 1), 0)),
                  pl.BlockSpec((K, tn), lambda i, j: (0, j)),
                  pl.BlockSpec((4, tn), lambda i, j: (0, j))],
        out_specs=pl.BlockSpec((tm, tn), lambda i, j: (i, j)),
        out_shape=jax.ShapeDtypeStruct((M, N), BF16),
        scratch_shapes=[pltpu.VMEM((tm + 2 * HALO, K), BF16)],
        compiler_params=_cparams("parallel", "arbitrary"),
    )(x, x, x, w, conv_w)


def _ffn_kernel(xp_ref, xm_ref, xn_ref, wg_ref, wu_ref, cw_ref, cb_ref, wd_ref, h_ref, nw_ref,
                hout_ref, hn_ref, xext_ref, *, tm, nf, ni):
    i = pl.program_id(0)
    f = pl.program_id(1)
    th = tm // 2

    @pl.when(f == 0)
    def _():
        xext_ref[0:HALO, :] = xp_ref[...]
        xext_ref[HALO:HALO + tm, :] = xm_ref[...]
        xext_ref[HALO + tm:2 * HALO + tm, :] = xn_ref[...]
        hout_ref[...] = h_ref[...]

    cw = cw_ref[...]
    cb = cb_ref[...]
    g, up = [], []
    for s in range(2):
        g.append(_dot(xext_ref[s * th:s * th + th + 2 * HALO, :], wg_ref[...]))
        up.append(_dot(xm_ref[s * th:(s + 1) * th, :], wu_ref[...]))
    row = lax.broadcasted_iota(jnp.int32, (th, 1), 0)
    y = []
    for s in range(2):
        g_prev = g[s][HALO - 1:HALO - 1 + th]
        g_here = g[s][HALO:HALO + th]
        g_next = g[s][HALO + 1:HALO + 1 + th]
        if s == 1:
            g_next = jnp.where(jnp.logical_and(i == ni - 1, row == th - 1), 0.0, g_next)
        a = cw[0:1] * g_prev + cw[1:2] * g_here + cw[2:3] * g_next + cb
        y.append((a * _sigmoid(a) * up[s]).astype(BF16))
    down = [_dot(y[s], wd_ref[...]) for s in range(2)]
    for s in range(2):
        hout_ref[s * th:(s + 1) * th, :] += down[s]

    @pl.when(f == nf - 1)
    def _():
        hn_ref[...] = _rms_scale(hout_ref[...], nw_ref[...]).astype(hn_ref.dtype)


def _conv_ffn(hn, h, w_gate, w_up, conv_w, conv_b, w_down, nw):
    M, D = hn.shape
    F = w_gate.shape[1]
    tm = _pick_tile(M, 704, 64)
    tf = _pick_tile(F, 512, 128)
    ni, nf = M // tm, F // tf
    hb = tm // HALO
    nhb = M // HALO
    return pl.pallas_call(
        functools.partial(_ffn_kernel, tm=tm, nf=nf, ni=ni),
        grid=(ni, nf),
        in_specs=[pl.BlockSpec((HALO, D), lambda i, f: (jnp.maximum(i * hb - 1, 0), 0)),
                  pl.BlockSpec((tm, D), lambda i, f: (i, 0)),
                  pl.BlockSpec((HALO, D), lambda i, f: (jnp.minimum((i + 1) * hb, nhb - 1), 0)),
                  pl.BlockSpec((D, tf), lambda i, f: (0, f)),
                  pl.BlockSpec((D, tf), lambda i, f: (0, f)),
                  pl.BlockSpec((3, tf), lambda i, f: (0, f)),
                  pl.BlockSpec((1, tf), lambda i, f: (0, f)),
                  pl.BlockSpec((tf, D), lambda i, f: (f, 0)),
                  pl.BlockSpec((tm, D), lambda i, f: (i, 0)),
                  pl.BlockSpec((1, D), lambda i, f: (0, 0))],
        out_specs=[pl.BlockSpec((tm, D), lambda i, f: (i, 0)), pl.BlockSpec((tm, D), lambda i, f: (i, 0))],
        out_shape=[jax.ShapeDtypeStruct((M, D), F32), jax.ShapeDtypeStruct((M, D), BF16)],
        scratch_shapes=[pltpu.VMEM((tm + 2 * HALO, D), BF16)],
        compiler_params=_cparams("parallel", "arbitrary"),
    )(hn, hn, hn, w_gate, w_up, conv_w, conv_b.reshape(1, F), w_down, h, nw.reshape(1, D))


def _scan8(a, u, row, reverse):
    for s in (1, 2, 4):
        if reverse:
            a_sh = pltpu.roll(a, 8 - s, 0)
            u_sh = pltpu.roll(u, 8 - s, 0)
            m = row < 8 - s
        else:
            a_sh = pltpu.roll(a, s, 0)
            u_sh = pltpu.roll(u, s, 0)
            m = row >= s
        u = jnp.where(m, a * u_sh + u, u)
        a = jnp.where(m, a * a_sh, a)
    return a, u


def _rg_kernel(rec_ref, gate_ref, cw_ref, cb_ref, wa_ref, ba_ref, wi_ref, bi_ref, lam_ref, o_ref,
               xc_s, hf_s, hb_s, *, Tp, rc):
    nch = Tp // rc
    W = RG_BLOCK
    T8 = rc // 8
    cw = cw_ref[...]
    cb = cb_ref[...]
    row8 = lax.broadcasted_iota(jnp.int32, (8, W), 0)

    def rows(c):
        return pl.multiple_of(c * rc, CHUNK)

    def conv(r0):
        top = rec_ref[0, pl.ds(pl.multiple_of(jnp.maximum(r0 - 8, 0), 8), 8), :]
        main = rec_ref[0, pl.ds(r0, rc), :]
        bot = rec_ref[0, pl.ds(pl.multiple_of(jnp.minimum(r0 + rc, Tp - 8), 8), 8), :]
        bot = jnp.where(r0 + rc < Tp, bot, 0.0)
        xw = jnp.concatenate([top, main, bot], axis=0)
        xc = cw[0:1] * xw[7:7 + rc] + cw[1:2] * xw[8:8 + rc] + cw[2:3] * xw[9:9 + rc] + cw[3:4] * xw[10:10 + rc] + cb
        t = r0 + lax.broadcasted_iota(jnp.int32, (rc, 1), 0)
        return jnp.where(t >= PAD, xc, 0.0)

    def advance(x_f, x_b, carry):
        xs = (x_f, x_b)
        xb16 = [x.astype(BF16) for x in xs]
        pre = [(_dot(xb16[d], wa_ref[d, 0]), _dot(xb16[d], wi_ref[d, 0])) for d in range(2)]
        au = []
        for d in range(2):
            r = _sigmoid(pre[d][0] + ba_ref[d])
            ig = _sigmoid(pre[d][1] + bi_ref[d])
            a = jnp.exp((-RG_C) * r * _softplus(-lam_ref[d]))
            au.append((a, jnp.sqrt(1.0 - a * a) * (ig * xs[d])))
        c_f, c_b = carry
        h_f, h_b = [None] * T8, [None] * T8
        for k in range(T8):
            A, U = _scan8(au[0][0][8 * k:8 * k + 8], au[0][1][8 * k:8 * k + 8], row8, False)
            h_f[k] = A * c_f + U
            c_f = h_f[k][7:8]
            kb = T8 - 1 - k
            A, U = _scan8(au[1][0][8 * kb:8 * kb + 8], au[1][1][8 * kb:8 * kb + 8], row8, True)
            h_b[kb] = A * c_b + U
            c_b = h_b[kb][0:1]
        return jnp.concatenate(h_f, axis=0), jnp.concatenate(h_b, axis=0), (c_f, c_b)

    def emit(r0, hsum):
        o_ref[0, pl.ds(r0, rc), :] = (gate_ref[0, pl.ds(r0, rc), :].astype(F32) * hsum).astype(o_ref.dtype)

    def first_half(c, carry):
        r_f, r_b = rows(c), rows(nch - 1 - c)
        x_f, x_b = conv(r_f), conv(r_b)
        h_f, h_b, carry = advance(x_f, x_b, carry)
        xc_s[pl.ds(r_f, rc), :] = x_f
        xc_s[pl.ds(r_b, rc), :] = x_b
        hf_s[pl.ds(r_f, rc), :] = h_f
        hb_s[pl.ds(r_b, rc), :] = h_b
        return carry

    carry = lax.fori_loop(0, nch // 2, first_half, (jnp.zeros((1, W), F32), jnp.zeros((1, W), F32)))

    if nch % 2:
        r_m = (nch // 2) * rc
        x_m = conv(r_m)
        h_f, h_b, carry = advance(x_m, x_m, carry)
        emit(r_m, h_f + h_b)

    def second_half(c, carry):
        r_f, r_b = rows(c), rows(nch - 1 - c)
        h_f, h_b, carry = advance(xc_s[pl.ds(r_f, rc), :], xc_s[pl.ds(r_b, rc), :], carry)
        emit(r_f, h_f + hb_s[pl.ds(r_f, rc), :])
        emit(r_b, hf_s[pl.ds(r_b, rc), :] + h_b)
        return carry

    lax.fori_loop((nch + 1) // 2, nch, second_half, carry)


def _rg_scan(rec, gate, conv_w, conv_b, w_a, b_a, w_i, b_i, lam):
    B, Tp, W = rec.shape
    nb = W // RG_BLOCK
    rc = _pick_tile(Tp, 384, 64)
    blk = lambda b, n: (b, 0, n)
    vec = lambda b, n: (0, n)
    vec3 = lambda b, n: (0, 0, n)
    return pl.pallas_call(
        functools.partial(_rg_kernel, Tp=Tp, rc=rc),
        grid=(B, nb),
        in_specs=[pl.BlockSpec((1, Tp, RG_BLOCK), blk),
                  pl.BlockSpec((1, Tp, RG_BLOCK), blk),
                  pl.BlockSpec((4, RG_BLOCK), vec),
                  pl.BlockSpec((1, RG_BLOCK), vec),
                  pl.BlockSpec((2, 1, RG_BLOCK, RG_BLOCK), lambda b, n: (0, n, 0, 0)),
                  pl.BlockSpec((2, 1, RG_BLOCK), vec3),
                  pl.BlockSpec((2, 1, RG_BLOCK, RG_BLOCK), lambda b, n: (0, n, 0, 0)),
                  pl.BlockSpec((2, 1, RG_BLOCK), vec3),
                  pl.BlockSpec((2, 1, RG_BLOCK), vec3)],
        out_specs=pl.BlockSpec((1, Tp, RG_BLOCK), blk),
        out_shape=jax.ShapeDtypeStruct((B, Tp, W), BF16),
        scratch_shapes=[pltpu.VMEM((Tp, RG_BLOCK), F32)] * 3,
        compiler_params=_cparams("parallel", "parallel"),
    )(rec, gate, conv_w, conv_b.reshape(1, W), w_a.astype(BF16), b_a.reshape(2, 1, W),
      w_i.astype(BF16), b_i.reshape(2, 1, W), lam.reshape(2, 1, W))


def _rglru_mixer(hn, h, B, Tp, w_in, conv_w, conv_b, w_a, b_a, w_i, b_i, lam, w_out, nw):
    W = D_MODEL
    w_in = w_in.astype(BF16)
    gate = _matmul(hn, w_in[:, :W], BF16, act="gelu")
    rec = _matmul(hn, w_in[:, W:], F32)
    y = _rg_scan(rec.reshape(B, Tp, W), gate.reshape(B, Tp, W), conv_w, conv_b, w_a, b_a, w_i, b_i, lam)
    return _proj_res(y.reshape(B * Tp, W), w_out.astype(BF16), h, nw)


def _na_kernel(q_ref, k_ref, v_ref, bias_ref, mb_ref, o_ref, *, rows):
    scale = NA_HEAD_DIM ** -0.5
    kh = NA_WIN_R
    mb = mb_ref[0]
    km = k_ref[0, PAD:HEAD_ROWS, :]
    vm = v_ref[0, PAD:HEAD_ROWS, :]

    o_ref[0, 0:PAD, :] = jnp.zeros((PAD, NA_HEAD_DIM), o_ref.dtype)
    qm = q_ref[0, PAD:HEAD_ROWS, :]
    s_m = _dot_nt(qm, km) * scale + mb
    p_m = jnp.exp(s_m - jnp.max(s_m, axis=-1, keepdims=True))
    o_m = _dot(p_m.astype(BF16), vm) / jnp.sum(p_m, axis=-1, keepdims=True)
    o_ref[0, PAD:HEAD_ROWS, :] = o_m.astype(o_ref.dtype)

    group = np.gcd(rows, NA_ROW_GROUP)
    R = range(group)

    def group_body(i, _):
        r = [i * group + c for c in R]
        rs = [jnp.clip(r[c] - NA_WIN_R // 2, 0, rows - kh) for c in R]
        q0 = [pl.multiple_of(HEAD_ROWS + r[c] * GRID_W, GRID_W) for c in R]
        k0 = [pl.multiple_of(HEAD_ROWS + rs[c] * GRID_W, GRID_W) for c in R]
        q_r = [q_ref[0, pl.ds(q0[c], GRID_W), :] for c in R]
        s = [_dot_nt(q_r[c], k_ref[0, pl.ds(k0[c], kh * GRID_W), :]) for c in R]
        s_met = [_dot_nt(q_r[c], km) for c in R]
        p, p_met, l = [], [], []
        for c in R:
            sc = s[c] * scale + bias_ref[0, r[c] - rs[c]]
            sm = s_met[c] * scale + mb
            m = jnp.maximum(jnp.max(sc, axis=-1, keepdims=True), jnp.max(sm, axis=-1, keepdims=True))
            pc = jnp.exp(sc - m)
            pm = jnp.exp(sm - m)
            l.append(jnp.sum(pc, axis=-1, keepdims=True) + jnp.sum(pm, axis=-1, keepdims=True))
            p.append(pc.astype(BF16))
            p_met.append(pm.astype(BF16))
        o = [_dot(p[c], v_ref[0, pl.ds(k0[c], kh * GRID_W), :]) + _dot(p_met[c], vm) for c in R]
        for c in R:
            o_ref[0, pl.ds(q0[c], GRID_W), :] = (o[c] / l[c]).astype(o_ref.dtype)
        return 0

    lax.fori_loop(0, rows // group, group_body, 0)


def _na_bias_table(rpb):
    kh = NA_WIN_R
    cols = np.arange(GRID_W)
    col_start = np.clip(cols - NA_WIN_C // 2, 0, GRID_W - NA_WIN_C)
    valid = (cols[None, :] >= col_start[:, None]) & (cols[None, :] < col_start[:, None] + NA_WIN_C)
    col_off = np.clip(cols[None, :] - cols[:, None], -(NA_WIN_C - 1), NA_WIN_C - 1) + NA_WIN_C - 1
    H, n_ro, n_co = rpb.shape
    onehot = (col_off[None] == np.arange(n_co)[:, None, None]).astype(np.float32)
    a = jnp.einsum('hrc,cqk->hqrk', rpb.astype(F32), jnp.asarray(onehot), precision=lax.Precision.HIGHEST)
    a = jnp.where(jnp.asarray(valid)[None, :, None, :], a, NEG_INF)
    tabs = [a[:, :, NA_WIN_R - 1 - d:NA_WIN_R - 1 - d + kh, :].reshape(H, GRID_W, kh * GRID_W)
            for d in range(NA_WIN_R)]
    return jnp.stack(tabs, axis=1)


def _na_attention(qkv, bias, meta_bias):
    B, Tp, _ = qkv.shape
    rows = (Tp - HEAD_ROWS) // GRID_W
    assert rows >= NA_WIN_R
    H = NA_HEADS
    return pl.pallas_call(
        functools.partial(_na_kernel, rows=rows),
        grid=(B, H),
        in_specs=[pl.BlockSpec((1, Tp, NA_HEAD_DIM), lambda b, h: (b, 0, h)),
                  pl.BlockSpec((1, Tp, NA_HEAD_DIM), lambda b, h: (b, 0, H + h)),
                  pl.BlockSpec((1, Tp, NA_HEAD_DIM), lambda b, h: (b, 0, 2 * H + h)),
                  pl.BlockSpec((1, NA_WIN_R, GRID_W, NA_WIN_R * GRID_W), lambda b, h: (h, 0, 0, 0)),
                  pl.BlockSpec((1, 1, N_META), lambda b, h: (h, 0, 0))],
        out_specs=pl.BlockSpec((1, Tp, NA_HEAD_DIM), lambda b, h: (b, 0, h)),
        out_shape=jax.ShapeDtypeStruct((B, Tp, H * NA_HEAD_DIM), BF16),
        compiler_params=_cparams("parallel", "parallel"),
    )(qkv, qkv, qkv, bias, meta_bias.astype(F32).reshape(H, 1, N_META))


def _na_mixer(hn, h, B, Tp, w_qkv, bias, meta_bias, w_o, nw):
    qkv = _matmul(hn, w_qkv.astype(BF16), BF16)
    o = _na_attention(qkv.reshape(B, Tp, -1), bias, meta_bias)
    return _proj_res(o.reshape(B * Tp, -1), w_o.astype(BF16), h, nw)


def _gdn_gate_kernel(x_ref, w_ref, al_ref, dt_ref, o_ref):
    y = _dot(x_ref[...], w_ref[...])
    lane = lax.broadcasted_iota(jnp.int32, y.shape, 1)
    g = -jnp.exp(al_ref[...]) * _softplus(y + dt_ref[...])
    o_ref[...] = jnp.where(lane < 2 * GDN_V_HEADS, _sigmoid(y), g)


def _gdn_gates(hn, w_ba, a_log, dt_bias):
    M, K = hn.shape
    N = 4 * GDN_V_HEADS
    tm = _pick_tile(M, 1024, 128)
    zeros = jnp.zeros((2 * GDN_V_HEADS,), F32)
    al = jnp.concatenate([zeros, a_log.astype(F32).reshape(-1)]).reshape(1, N)
    dt = jnp.concatenate([zeros, dt_bias.astype(F32).reshape(-1)]).reshape(1, N)
    return pl.pallas_call(
        _gdn_gate_kernel,
        grid=(M // tm,),
        in_specs=[pl.BlockSpec((tm, K), lambda i: (i, 0)), pl.BlockSpec((K, N), lambda i: (0, 0)),
                  pl.BlockSpec((1, N), lambda i: (0, 0)), pl.BlockSpec((1, N), lambda i: (0, 0))],
        out_specs=pl.BlockSpec((tm, N), lambda i: (i, 0)),
        out_shape=jax.ShapeDtypeStruct((M, N), F32),
        compiler_params=_cparams("parallel"),
    )(hn, w_ba, al, dt)


def _split3_dot(x, m):
    hi = x.astype(BF16)
    r1 = x - hi.astype(F32)
    mid = r1.astype(BF16)
    lo = (r1 - mid.astype(F32)).astype(BF16)
    return _dot(hi, m) + _dot(mid, m) + _dot(lo, m)


def _gdn_kernel(q_ref, k_ref, v_ref, z_ref, bgc_ref, bgr_ref, nw_ref, o_ref,
                S_s, oacc_s, u_s, wq_s, kst_s, qkd_s, gam_s, *, Tp, HPS):
    hb = pl.program_id(1)
    C = CHUNK
    NC = Tp // C
    HV = GDN_V_HEADS
    DK = GDN_DK
    G = 4
    W = G * C
    P1_GROUP = 6 if HPS == 1 else 4
    scale = DK ** -0.5

    oacc_s[...] = jnp.zeros_like(oacc_s)
    S_s[...] = jnp.zeros_like(S_s)

    ri = lax.broadcasted_iota(jnp.int32, (C, W), 0)
    cl = lax.broadcasted_iota(jnp.int32, (C, W), 1)
    cj = jnp.bitwise_and(cl, C - 1)
    blk = jnp.right_shift(cl, 6)
    dd = jnp.where(blk >= 2, cj - ri, ri - cj)
    incl4 = dd >= 0
    strict4 = dd > 0
    eye4 = (ri == cj).astype(F32)
    cum4 = jnp.where(dd <= 0, 1.0, 0.0).astype(BF16)
    bdm = jnp.where(jnp.right_shift(lax.broadcasted_iota(jnp.int32, (W, W), 0), 6)
                    == jnp.right_shift(lax.broadcasted_iota(jnp.int32, (W, W), 1), 6), 1.0, 0.0).astype(BF16)
    blk_row = jnp.right_shift(lax.broadcasted_iota(jnp.int32, (1, W), 1), 6)
    lane = lax.broadcasted_iota(jnp.int32, (C, 4 * HV), 1)
    rowi = lax.broadcasted_iota(jnp.int32, (C, 4 * HV), 0)

    def col(x, c):
        return jnp.sum(jnp.where(lane == c, x, 0.0), axis=1, keepdims=True)

    def by_block(parts, b):
        return jnp.where(b == 0, parts[0], jnp.where(b == 1, parts[1], jnp.where(b == 2, parts[2], parts[3])))

    def blockdiag(xb):
        return jnp.concatenate([xb] * G, axis=0) * bdm

    def phase1(ns):
        items = [(hd, n) for n in ns for hd in range(HPS)]
        K = range(len(items))
        r0 = [pl.multiple_of(n * C, C) for (hd, n) in items]
        q = [q_ref[0, pl.ds(r0[i], C), items[i][0] * DK:(items[i][0] + 1) * DK] for i in K]
        k = [k_ref[0, pl.ds(r0[i], C), items[i][0] * DK:(items[i][0] + 1) * DK] for i in K]
        bg = [bgc_ref[0, pl.ds(r0[i], C), :] for i in K]
        v = [v_ref[0, pl.ds(r0[i], C), items[i][0] * 2 * DK:(items[i][0] + 1) * 2 * DK].astype(F32) for i in K]
        rows = [bgr_ref[0, n, hd] for (hd, n) in items]
        cum_rows = [_split3_dot(rows[i], cum4) for i in K]
        k4 = [jnp.concatenate([k[i]] * G, axis=0) for i in K]
        kk4 = [_dot_nt(k[i], k4[i]) for i in K]
        qk4 = [_dot_nt(q[i], k4[i]) * scale for i in K]
        beta_c, gc_c, decay4, L = [], [], [], []
        for i in K:
            pre = bg[i]
            for s in (1, 2, 4, 8, 16, 32):
                pre = pre + jnp.where(rowi >= s, pltpu.roll(pre, s, 0), 0.0)
            suf = pre[C - 1:C] - pre + bg[i]
            gc_r = by_block([cum_rows[i][4 + g:5 + g] for g in range(G)], blk_row)
            bc, gc = [], []
            for g in range(G):
                d, j = g // 2, g % 2
                vh = 2 * (hb * HPS + items[i][0]) + j
                bc.append(col(bg[i], d * HV + vh))
                gc.append(col(suf if d else pre, 2 * HV + d * HV + vh))
            beta_c.append(bc)
            gc_c.append(gc)
            dec = jnp.where(incl4, jnp.exp(jnp.where(incl4, by_block(gc, blk) - gc_r, 0.0)), 0.0)
            decay4.append(dec)
            L.append(jnp.where(strict4, kk4[i] * dec, 0.0) * by_block(bc, blk))
        Tm = [eye4 - L[i] for i in K]
        Lb = [L[i].astype(BF16) for i in K]
        P = [_dot(Lb[i], blockdiag(Lb[i])) for i in K]
        for lvl in range(5):
            Pb = [P[i].astype(BF16) for i in K]
            bd = [blockdiag(Pb[i]) for i in K]
            if lvl < 4:
                tp = [_dot(jnp.concatenate([Tm[i].astype(BF16), Pb[i]], axis=0), bd[i]) for i in K]
                Tm = [Tm[i] + tp[i][:C] for i in K]
                P = [tp[i][C:] for i in K]
            else:
                tp = [_dot(Tm[i].astype(BF16), bd[i]) for i in K]
                Tm = [Tm[i] + tp[i] for i in K]
        rhs, kst, qst, gam = [], [], [], []
        for i in K:
            qf = q[i].astype(F32)
            kf = k[i].astype(F32)
            rhs_i, kst_i, qst_i, gam_i = [], [], [], []
            for g in range(G):
                j = g % 2
                bc, gc = beta_c[i][g], gc_c[i][g]
                e_c = jnp.exp(gc)
                rhs_i.append(jnp.concatenate([v[i][:, j * DK:(j + 1) * DK] * bc, kf * (bc * e_c)], axis=1))
                g_last = gc[0:1] if g >= 2 else gc[C - 1:C]
                kst_i.append((kf * jnp.exp(g_last - gc)).astype(BF16))
                qst_i.append((qf * (scale * e_c)).astype(BF16))
                gam_i.append(jnp.broadcast_to(jnp.exp(g_last), (1, DK)))
            rhs.append(jnp.concatenate(rhs_i, axis=0).astype(BF16))
            kst.append(kst_i)
            qst.append(qst_i)
            gam.append(gam_i)
        sol = [_dot(blockdiag(Tm[i].astype(BF16)), rhs[i]) for i in K]
        for i in K:
            hd, n = items[i]
            u_s[hd, n] = sol[i][:, :DK]
            qkd_s[hd, n] = jnp.where(incl4, qk4[i] * decay4[i], 0.0).astype(BF16)
            for g in range(G):
                wq_s[hd, n, g, 0:C, :] = sol[i][g * C:(g + 1) * C, DK:].astype(BF16)
                wq_s[hd, n, g, C:2 * C, :] = qst[i][g]
                kst_s[hd, n, g] = kst[i][g]
                gam_s[hd, n, g:g + 1, :] = gam[i][g]

    def phase1_group(i, _):
        phase1([P1_GROUP * i + c for c in range(P1_GROUP)])
        return 0

    lax.fori_loop(0, NC // P1_GROUP, phase1_group, 0)
    if NC % P1_GROUP:
        phase1(list(range(NC - NC % P1_GROUP, NC)))

    nw = nw_ref[...]

    def gated_norm(o, z):
        ys = []
        for j in range(2 * HPS):
            oj = o[:, j * DK:(j + 1) * DK]
            zj = z[:, j * DK:(j + 1) * DK].astype(F32)
            ys.append(oj * lax.rsqrt(jnp.mean(oj * oj, axis=-1, keepdims=True) + NORM_EPS) * nw * (zj * _sigmoid(zj)))
        return jnp.concatenate(ys, axis=1)

    def finalize(n):
        r0 = pl.multiple_of(n * C, C)
        o_ref[0, pl.ds(r0, C), :] = gated_norm(oacc_s[pl.ds(r0, C), :], z_ref[0, pl.ds(r0, C), :]).astype(o_ref.dtype)

    def phase2(it, finalize_previous):
        if finalize_previous:
            finalize(it - 1)
            finalize(NC - it)
        ops = []
        for hd in range(HPS):
            for g in range(G):
                n = it if g < 2 else NC - 1 - it
                ops.append((S_s[hd * G + g], wq_s[hd, n, g], u_s[hd, n, g * C:(g + 1) * C, :],
                            qkd_s[hd, n, :, g * C:(g + 1) * C], gam_s[hd, n, g:g + 1, :], kst_s[hd, n, g]))
        NI = HPS * G
        r_f = pl.multiple_of(it * C, C)
        r_b = pl.multiple_of((NC - 1 - it) * C, C)
        o_f = oacc_s[pl.ds(r_f, C), :]
        ws = [_dot(wq, S.astype(BF16)) for (S, wq, u, qkd, gam, kst) in ops]
        vb = [(ops[g][2] - ws[g][:C]).astype(BF16) for g in range(NI)]
        outs = [ws[g][C:] + _dot(ops[g][3], vb[g]) for g in range(NI)]
        states = [ops[g][0] * ops[g][4] + _dot_tn(ops[g][5], vb[g]) for g in range(NI)]
        for g in range(NI):
            S_s[g] = states[g]
        fwd = [outs[hd * G + j] for hd in range(HPS) for j in range(2)]
        bwd = [outs[hd * G + 2 + j] for hd in range(HPS) for j in range(2)]
        o_f = o_f + jnp.concatenate(fwd, axis=1)
        oacc_s[pl.ds(r_f, C), :] = o_f
        oacc_s[pl.ds(r_b, C), :] = oacc_s[pl.ds(r_b, C), :] + jnp.concatenate(bwd, axis=1)
        return 0

    lax.fori_loop(0, NC // 2 + 1, lambda it, c: phase2(it, False), 0)
    lax.fori_loop(NC // 2 + 1, NC, lambda it, c: phase2(it, True), 0)
    finalize(NC - 1)
    finalize(0)


_GDN_CHUNK_BYTES = CHUNK * GDN_DK * (4 * 4 + 8 * 2 + 4 * 2 + 4 * 2) + 8 * GDN_DK * 4
_GDN_PAIR_BUDGET = 26 * 1024 * 1024


def _gdn_delta(qkv, z, bg, norm_w):
    B, Tp, _ = qkv.shape
    NC = Tp // CHUNK
    KH = GDN_QK_HEADS
    HV = GDN_V_HEADS
    C = CHUNK
    HPS = 2 if NC * _GDN_CHUNK_BYTES * 2 <= _GDN_PAIR_BUDGET else 1
    NH = KH // HPS
    bgr = bg.reshape(B, NC, CHUNK, 4, KH, 2).transpose(0, 1, 4, 3, 5, 2).reshape(B, NC, KH, 8, CHUNK)
    return pl.pallas_call(
        functools.partial(_gdn_kernel, Tp=Tp, HPS=HPS),
        grid=(B, NH),
        in_specs=[pl.BlockSpec((1, Tp, HPS * GDN_DK), lambda b, h: (b, 0, h)),
                  pl.BlockSpec((1, Tp, HPS * GDN_DK), lambda b, h: (b, 0, NH + h)),
                  pl.BlockSpec((1, Tp, 2 * HPS * GDN_DK), lambda b, h: (b, 0, NH + h)),
                  pl.BlockSpec((1, Tp, 2 * HPS * GDN_DK), lambda b, h: (b, 0, h)),
                  pl.BlockSpec((1, Tp, 4 * HV), lambda b, h: (b, 0, 0)),
                  pl.BlockSpec((1, NC, HPS, 8, CHUNK), lambda b, h: (b, 0, h, 0, 0)),
                  pl.BlockSpec((1, GDN_DK), lambda b, h: (0, 0))],
        out_specs=pl.BlockSpec((1, Tp, 2 * HPS * GDN_DK), lambda b, h: (b, 0, h)),
        out_shape=jax.ShapeDtypeStruct((B, Tp, GDN_VAL_DIM), BF16),
        scratch_shapes=[pltpu.VMEM((4 * HPS, GDN_DK, GDN_DK), F32),
                        pltpu.VMEM((Tp, 2 * HPS * GDN_DK), F32),
                        pltpu.VMEM((HPS, NC, 4 * C, GDN_DK), F32),
                        pltpu.VMEM((HPS, NC, 4, 2 * C, GDN_DK), BF16),
                        pltpu.VMEM((HPS, NC, 4, C, GDN_DK), BF16),
                        pltpu.VMEM((HPS, NC, C, 4 * C), BF16),
                        pltpu.VMEM((HPS, NC, 8, GDN_DK), F32)],
        compiler_params=_cparams("parallel", "parallel"),
    )(qkv, qkv, qkv, z, bg, bgr, norm_w.astype(F32).reshape(1, GDN_DK))


def _gdn_mixer(hn, h, B, Tp, w_in, conv_w, a_log, dt_bias, norm_w, w_out, nw):
    nz = GDN_CONV_DIM + GDN_VAL_DIM
    w_in = w_in.astype(BF16)
    qkv = _proj_conv(hn, w_in[:, :GDN_CONV_DIM], conv_w, Tp, 2 * GDN_KEY_DIM)
    z = _matmul(hn, w_in[:, GDN_CONV_DIM:nz], BF16)
    bg = _gdn_gates(hn, w_in[:, nz:], a_log, dt_bias)
    o = _gdn_delta(qkv.reshape(B, Tp, -1), z.reshape(B, Tp, -1), bg.reshape(B, Tp, -1), norm_w)
    return _proj_res(o.reshape(B * Tp, -1), w_out.astype(BF16), h, nw)


def _trunk(x, meta_tokens, mix_norm, ffn_norm, final_norm, rg, na, gdn, ffn):
    B, T, D = x.shape
    Tp = T + HEAD_ROWS
    depth = mix_norm.shape[0]
    h3 = jnp.concatenate([jnp.zeros((B, PAD, D), F32),
                          jnp.broadcast_to(meta_tokens.astype(F32)[None], (B, N_META, D)),
                          x.astype(F32)], axis=1)
    h = h3.reshape(B * Tp, D)
    hn = _rmsnorm(h, mix_norm[0], BF16)
    for i in range(depth):
        kind, j = i % 3, i // 3
        if kind == 0:
            h, hn = _rglru_mixer(hn, h, B, Tp, *[p[j] for p in rg], ffn_norm[i])
        elif kind == 1:
            h, hn = _na_mixer(hn, h, B, Tp, *[p[j] for p in na], ffn_norm[i])
        else:
            h, hn = _gdn_mixer(hn, h, B, Tp, *[p[j] for p in gdn], ffn_norm[i])
        w_gate, w_up, conv_w, conv_b, w_down = [p[i] for p in ffn]
        next_norm = mix_norm[i + 1] if i + 1 < depth else final_norm
        h, hn = _conv_ffn(hn, h, w_gate.astype(BF16), w_up.astype(BF16), conv_w, conv_b,
                          w_down.astype(BF16), next_norm)
    return _final_norm(h, final_norm, B, T)


def kernel(x_prompt, x_sample, meta_tokens, mix_norm, ffn_norm, final_norm, rg_w_in, rg_conv_w, rg_conv_b, rg_w_a, rg_b_a, rg_w_i, rg_b_i, rg_lam, rg_w_out, na_w_qkv, na_rpb, na_meta_bias, na_w_o, gdn_w_in, gdn_conv_w, gdn_a_log, gdn_dt_bias, gdn_norm_w, gdn_w_out, ffn_w_gate, ffn_w_up, ffn_conv_w, ffn_conv_b, ffn_w_down):
    rg = (rg_w_in, rg_conv_w, rg_conv_b, rg_w_a, rg_b_a, rg_w_i, rg_b_i, rg_lam, rg_w_out)
    na_bias = jnp.stack([_na_bias_table(na_rpb[j]) for j in range(na_rpb.shape[0])])
    na = (na_w_qkv, na_bias, na_meta_bias, na_w_o)
    gdn = (gdn_w_in, gdn_conv_w, gdn_a_log, gdn_dt_bias, gdn_norm_w, gdn_w_out)
    ffn = (ffn_w_gate, ffn_w_up, ffn_conv_w, ffn_conv_b, ffn_w_down)
    y_prompt = _trunk(x_prompt, meta_tokens, mix_norm, ffn_norm, final_norm, rg, na, gdn, ffn)
    y_sample = _trunk(x_sample, meta_tokens, mix_norm, ffn_norm, final_norm, rg, na, gdn, ffn)
    return (y_prompt, y_sample)
```

```python
import functools

import numpy as np
import jax
import jax.numpy as jnp
from jax import lax
from jax.experimental import pallas as pl
from jax.experimental.pallas import tpu as pltpu

D_MODEL = 2048
N_META = 16
PAD = 48
HEAD_ROWS = PAD + N_META
CHUNK = 64
GRID_W = 64
NORM_EPS = 1e-6

RG_BLOCK = 256
RG_C = 8.0

NA_HEAD_DIM = 128
NA_HEADS = D_MODEL // NA_HEAD_DIM
NA_WIN_R = 8
NA_WIN_C = 16
NA_ROW_GROUP = 16
NEG_INF = -1e30

GDN_DK = 128
GDN_QK_HEADS = D_MODEL // GDN_DK
GDN_V_HEADS = 2 * GDN_QK_HEADS
GDN_KEY_DIM = GDN_QK_HEADS * GDN_DK
GDN_VAL_DIM = GDN_V_HEADS * GDN_DK
GDN_CONV_DIM = 2 * GDN_KEY_DIM + GDN_VAL_DIM

V7X_VMEM_LIMIT = 56 * 1024 * 1024
HALO = 16

BF16 = jnp.bfloat16
F32 = jnp.float32


def _cparams(*sem):
    return pltpu.CompilerParams(dimension_semantics=sem, vmem_limit_bytes=V7X_VMEM_LIMIT)


def _pick_tile(n, cap, mult):
    best = None
    for t in range(mult, cap + 1, mult):
        if n % t == 0:
            best = t
    assert best is not None, (n, cap, mult)
    return best


def _sigmoid(x):
    return 1.0 / (1.0 + jnp.exp(-x))


def _softplus(x):
    return jnp.maximum(x, 0.0) + jnp.log(1.0 + jnp.exp(-jnp.abs(x)))


def _gelu_tanh(x):
    return 0.5 * x * (1.0 + jnp.tanh(0.7978845608028654 * (x + 0.044715 * x * x * x)))


def _rms_scale(x, w):
    ms = jnp.mean(x * x, axis=-1, keepdims=True)
    return x * lax.rsqrt(ms + NORM_EPS) * w


def _dot(a, b):
    return jnp.dot(a, b, preferred_element_type=F32)


def _dot_nt(a, b):
    return lax.dot_general(a, b, (((1,), (1,)), ((), ())), preferred_element_type=F32)


def _dot_tn(a, b):
    return lax.dot_general(a, b, (((0,), (0,)), ((), ())), preferred_element_type=F32)


def _rmsnorm_kernel(x_ref, w_ref, o_ref):
    o_ref[...] = _rms_scale(x_ref[...], w_ref[...]).astype(o_ref.dtype)


def _rmsnorm(h, w, out_dtype):
    M, D = h.shape
    tm = _pick_tile(M, 1024, 64)
    return pl.pallas_call(
        _rmsnorm_kernel,
        grid=(M // tm,),
        in_specs=[pl.BlockSpec((tm, D), lambda i: (i, 0)), pl.BlockSpec((1, D), lambda i: (0, 0))],
        out_specs=pl.BlockSpec((tm, D), lambda i: (i, 0)),
        out_shape=jax.ShapeDtypeStruct((M, D), out_dtype),
        compiler_params=_cparams("parallel"),
    )(h, w.reshape(1, D))


def _final_norm_kernel(x_ref, w_ref, o_ref):
    o_ref[0] = _rms_scale(x_ref[...], w_ref[...])


def _final_norm(h, w, B, T):
    D = h.shape[1]
    Tp = T + HEAD_ROWS
    tt = 512
    assert T % tt == 0
    return pl.pallas_call(
        _final_norm_kernel,
        grid=(B, T // tt),
        in_specs=[pl.BlockSpec((pl.Element(tt), pl.Element(D)),
                               lambda b, j: (pl.multiple_of(b * Tp + HEAD_ROWS + j * tt, CHUNK), 0)),
                  pl.BlockSpec((1, D), lambda b, j: (0, 0))],
        out_specs=pl.BlockSpec((1, tt, D), lambda b, j: (b, j, 0)),
        out_shape=jax.ShapeDtypeStruct((B, T, D), F32),
        compiler_params=_cparams("parallel", "parallel"),
    )(h, w.reshape(1, D))


def _matmul_kernel(x_ref, w_ref, o_ref, *, act):
    acc = _dot(x_ref[...], w_ref[...])
    if act == "gelu":
        acc = _gelu_tanh(acc)
    o_ref[...] = acc.astype(o_ref.dtype)


def _matmul(x, w, out_dtype, act=None):
    M, K = x.shape
    N = w.shape[1]
    tm = _pick_tile(M, 1024, 128)
    tn = _pick_tile(N, 1024, 128)
    return pl.pallas_call(
        functools.partial(_matmul_kernel, act=act),
        grid=(M // tm, N // tn),
        in_specs=[pl.BlockSpec((tm, K), lambda i, j: (i, 0)), pl.BlockSpec((K, tn), lambda i, j: (0, j))],
        out_specs=pl.BlockSpec((tm, tn), lambda i, j: (i, j)),
        out_shape=jax.ShapeDtypeStruct((M, N), out_dtype),
        compiler_params=_cparams("parallel", "parallel"),
    )(x, w)


def _proj_res_kernel(x_ref, w_ref, h_ref, nw_ref, hout_ref, hn_ref):
    hnew = h_ref[...] + _dot(x_ref[...], w_ref[...])
    hout_ref[...] = hnew
    hn_ref[...] = _rms_scale(hnew, nw_ref[...]).astype(hn_ref.dtype)


def _proj_res(x, w, h, nw):
    M, K = x.shape
    D = w.shape[1]
    tm = _pick_tile(M, 704 if K <= D_MODEL else 512, 64)
    return pl.pallas_call(
        _proj_res_kernel,
        grid=(M // tm,),
        in_specs=[pl.BlockSpec((tm, K), lambda i: (i, 0)),
                  pl.BlockSpec((K, D), lambda i: (0, 0), pipeline_mode=pl.Buffered(1)),
                  pl.BlockSpec((tm, D), lambda i: (i, 0)),
                  pl.BlockSpec((1, D), lambda i: (0, 0))],
        out_specs=[pl.BlockSpec((tm, D), lambda i: (i, 0)), pl.BlockSpec((tm, D), lambda i: (i, 0))],
        out_shape=[jax.ShapeDtypeStruct((M, D), F32), jax.ShapeDtypeStruct((M, D), BF16)],
        compiler_params=_cparams("parallel"),
    )(x, w, h, nw.reshape(1, D))


def _proj_conv_kernel(xp_ref, xm_ref, xn_ref, w_ref, cw_ref, o_ref, xext_ref, *, tm, tc, ni, Tp, n_norm):
    i = pl.program_id(0)
    j = pl.program_id(1)
    npc = o_ref.shape[1] // tc

    @pl.when(j == 0)
    def _():
        xext_ref[0:HALO, :] = xp_ref[...]
        xext_ref[HALO:HALO + tm, :] = xm_ref[...]
        xext_ref[HALO + tm:2 * HALO + tm, :] = xn_ref[...]

    @pl.when(jnp.logical_and(j == 0, i == ni - 1))
    def _():
        xext_ref[HALO + tm:2 * HALO + tm, :] = jnp.zeros((HALO, xext_ref.shape[1]), xext_ref.dtype)

    rel = i * tm - (i * tm // Tp) * Tp + lax.broadcasted_iota(jnp.int32, (tm, 1), 0)
    live = jnp.where(rel >= Tp, rel - Tp, rel) >= PAD
    cw = cw_ref[...]
    xe = xext_ref[...]

    def piece_matmul(c):
        return _dot(xe, w_ref[:, c * tc:(c + 1) * tc])

    def piece_finish(c, g):
        cwc = cw[:, c * tc:(c + 1) * tc]
        a = (cwc[0:1] * g[HALO - 1:HALO - 1 + tm] + cwc[1:2] * g[HALO:HALO + tm]
             + cwc[2:3] * g[HALO + 1:HALO + 1 + tm] + cwc[3:4] * g[HALO + 2:HALO + 2 + tm])
        y = jnp.where(live, a * _sigmoid(a), 0.0)
        normalise = j * npc + c < n_norm
        heads = []
        for hh in range(tc // GDN_DK):
            yh = y[:, hh * GDN_DK:(hh + 1) * GDN_DK]
            nrm = lax.rsqrt(jnp.sum(yh * yh, axis=-1, keepdims=True) + 1e-6)
            heads.append(yh * jnp.where(normalise, nrm, 1.0))
        o_ref[:, c * tc:(c + 1) * tc] = jnp.concatenate(heads, axis=1).astype(o_ref.dtype)

    g = piece_matmul(0)
    for c in range(npc):
        g_next = piece_matmul(c + 1) if c + 1 < npc else None
        piece_finish(c, g)
        g = g_next


def _proj_conv(x, w, conv_w, Tp, n_norm_cols):
    M, K = x.shape
    N = w.shape[1]
    tm = _pick_tile(M, 1024, 64)
    assert tm <= Tp
    tn, tc = 2048, 512
    ni = M // tm
    hb = tm // HALO
    nhb = M // HALO
    return pl.pallas_call(
        functools.partial(_proj_conv_kernel, tm=tm, tc=tc, ni=ni, Tp=Tp, n_norm=n_norm_cols // tc),
        grid=(ni, N // tn),
        in_specs=[pl.BlockSpec((HALO, K), lambda i, j: (jnp.maximum(i * hb - 1, 0), 0)),
                  pl.BlockSpec((tm, K), lambda i, j: (i, 0)),
                  pl.BlockSpec((HALO, K), lambda i, j: (jnp.minimum((i + 1) * hb, nhb - 1), 0)),
                  pl.BlockSpec((K, tn), lambda i, j: (0, j)),
                  pl.BlockSpec((4, tn), lambda i, j: (0, j))],
        out_specs=pl.BlockSpec((tm, tn), lambda i, j: (i, j)),
        out_shape=jax.ShapeDtypeStruct((M, N), BF16),
        scratch_shapes=[pltpu.VMEM((tm + 2 * HALO, K), BF16)],
        compiler_params=_cparams("parallel", "arbitrary"),
    )(x, x, x, w, conv_w)


def _ffn_kernel(xp_ref, xm_ref, xn_ref, wg_ref, wu_ref, cw_ref, cb_ref, wd_ref, h_ref, nw_ref,
                hout_ref, hn_ref, xext_ref, *, tm, nf, ni):
    i = pl.program_id(0)
    f = pl.program_id(1)
    th = tm // 2

    @pl.when(f == 0)
    def _():
        xext_ref[0:HALO, :] = xp_ref[...]
        xext_ref[HALO:HALO + tm, :] = xm_ref[...]
        xext_ref[HALO + tm:2 * HALO + tm, :] = xn_ref[...]
        hout_ref[...] = h_ref[...]

    cw = cw_ref[...]
    cb = cb_ref[...]
    g, up = [], []
    for s in range(2):
        g.append(_dot(xext_ref[s * th:s * th + th + 2 * HALO, :], wg_ref[...]))
        up.append(_dot(xm_ref[s * th:(s + 1) * th, :], wu_ref[...]))
    row = lax.broadcasted_iota(jnp.int32, (th, 1), 0)
    y = []
    for s in range(2):
        g_prev = g[s][HALO - 1:HALO - 1 + th]
        g_here = g[s][HALO:HALO + th]
        g_next = g[s][HALO + 1:HALO + 1 + th]
        if s == 1:
            g_next = jnp.where(jnp.logical_and(i == ni - 1, row == th - 1), 0.0, g_next)
        a = cw[0:1] * g_prev + cw[1:2] * g_here + cw[2:3] * g_next + cb
        y.append((a * _sigmoid(a) * up[s]).astype(BF16))
    down = [_dot(y[s], wd_ref[...]) for s in range(2)]
    for s in range(2):
        hout_ref[s * th:(s + 1) * th, :] += down[s]

    @pl.when(f == nf - 1)
    def _():
        hn_ref[...] = _rms_scale(hout_ref[...], nw_ref[...]).astype(hn_ref.dtype)


def _conv_ffn(hn, h, w_gate, w_up, conv_w, conv_b, w_down, nw):
    M, D = hn.shape
    F = w_gate.shape[1]
    tm = _pick_tile(M, 704, 64)
    tf = _pick_tile(F, 512, 128)
    ni, nf = M // tm, F // tf
    hb = tm // HALO
    nhb = M // HALO
    return pl.pallas_call(
        functools.partial(_ffn_kernel, tm=tm, nf=nf, ni=ni),
        grid=(ni, nf),
        in_specs=[pl.BlockSpec((HALO, D), lambda i, f: (jnp.maximum(i * hb - 1, 0), 0)),
                  pl.BlockSpec((tm, D), lambda i, f: (i, 0)),
                  pl.BlockSpec((HALO, D), lambda i, f: (jnp.minimum((i + 1) * hb, nhb - 1), 0)),
                  pl.BlockSpec((D, tf), lambda i, f: (0, f)),
                  pl.BlockSpec((D, tf), lambda i, f: (0, f)),
                  pl.BlockSpec((3, tf), lambda i, f: (0, f)),
                  pl.BlockSpec((1, tf), lambda i, f: (0, f)),
                  pl.BlockSpec((tf, D), lambda i, f: (f, 0)),
                  pl.BlockSpec((tm, D), lambda i, f: (i, 0)),
                  pl.BlockSpec((1, D), lambda i, f: (0, 0))],
        out_specs=[pl.BlockSpec((tm, D), lambda i, f: (i, 0)), pl.BlockSpec((tm, D), lambda i, f: (i, 0))],
        out_shape=[jax.ShapeDtypeStruct((M, D), F32), jax.ShapeDtypeStruct((M, D), BF16)],
        scratch_shapes=[pltpu.VMEM((tm + 2 * HALO, D), BF16)],
        compiler_params=_cparams("parallel", "arbitrary"),
    )(hn, hn, hn, w_gate, w_up, conv_w, conv_b.reshape(1, F), w_down, h, nw.reshape(1, D))


def _scan8(a, u, row, reverse):
    for s in (1, 2, 4):
        if reverse:
            a_sh = pltpu.roll(a, 8 - s, 0)
            u_sh = pltpu.roll(u, 8 - s, 0)
            m = row < 8 - s
        else:
            a_sh = pltpu.roll(a, s, 0)
            u_sh = pltpu.roll(u, s, 0)
            m = row >= s
        u = jnp.where(m, a * u_sh + u, u)
        a = jnp.where(m, a * a_sh, a)
    return a, u


def _rg_kernel(rec_ref, gate_ref, cw_ref, cb_ref, wa_ref, ba_ref, wi_ref, bi_ref, lam_ref, o_ref,
               xc_s, hf_s, hb_s, *, Tp, rc):
    nch = Tp // rc
    W = RG_BLOCK
    T8 = rc // 8
    cw = cw_ref[...]
    cb = cb_ref[...]
    row8 = lax.broadcasted_iota(jnp.int32, (8, W), 0)

    def rows(c):
        return pl.multiple_of(c * rc, CHUNK)

    def conv(r0):
        top = rec_ref[0, pl.ds(pl.multiple_of(jnp.maximum(r0 - 8, 0), 8), 8), :]
        main = rec_ref[0, pl.ds(r0, rc), :]
        bot = rec_ref[0, pl.ds(pl.multiple_of(jnp.minimum(r0 + rc, Tp - 8), 8), 8), :]
        bot = jnp.where(r0 + rc < Tp, bot, 0.0)
        xw = jnp.concatenate([top, main, bot], axis=0)
        xc = cw[0:1] * xw[7:7 + rc] + cw[1:2] * xw[8:8 + rc] + cw[2:3] * xw[9:9 + rc] + cw[3:4] * xw[10:10 + rc] + cb
        t = r0 + lax.broadcasted_iota(jnp.int32, (rc, 1), 0)
        return jnp.where(t >= PAD, xc, 0.0)

    def advance(x_f, x_b, carry):
        xs = (x_f, x_b)
        xb16 = [x.astype(BF16) for x in xs]
        pre = [(_dot(xb16[d], wa_ref[d, 0]), _dot(xb16[d], wi_ref[d, 0])) for d in range(2)]
        au = []
        for d in range(2):
            r = _sigmoid(pre[d][0] + ba_ref[d])
            ig = _sigmoid(pre[d][1] + bi_ref[d])
            a = jnp.exp((-RG_C) * r * _softplus(-lam_ref[d]))
            au.append((a, jnp.sqrt(1.0 - a * a) * (ig * xs[d])))
        c_f, c_b = carry
        h_f, h_b = [None] * T8, [None] * T8
        for k in range(T8):
            A, U = _scan8(au[0][0][8 * k:8 * k + 8], au[0][1][8 * k:8 * k + 8], row8, False)
            h_f[k] = A * c_f + U
            c_f = h_f[k][7:8]
            kb = T8 - 1 - k
            A, U = _scan8(au[1][0][8 * kb:8 * kb + 8], au[1][1][8 * kb:8 * kb + 8], row8, True)
            h_b[kb] = A * c_b + U
            c_b = h_b[kb][0:1]
        return jnp.concatenate(h_f, axis=0), jnp.concatenate(h_b, axis=0), (c_f, c_b)

    def emit(r0, hsum):
        o_ref[0, pl.ds(r0, rc), :] = (gate_ref[0, pl.ds(r0, rc), :].astype(F32) * hsum).astype(o_ref.dtype)

    def first_half(c, carry):
        r_f, r_b = rows(c), rows(nch - 1 - c)
        x_f, x_b = conv(r_f), conv(r_b)
        h_f, h_b, carry = advance(x_f, x_b, carry)
        xc_s[pl.ds(r_f, rc), :] = x_f
        xc_s[pl.ds(r_b, rc), :] = x_b
        hf_s[pl.ds(r_f, rc), :] = h_f
        hb_s[pl.ds(r_b, rc), :] = h_b
        return carry

    carry = lax.fori_loop(0, nch // 2, first_half, (jnp.zeros((1, W), F32), jnp.zeros((1, W), F32)))

    if nch % 2:
        r_m = (nch // 2) * rc
        x_m = conv(r_m)
        h_f, h_b, carry = advance(x_m, x_m, carry)
        emit(r_m, h_f + h_b)

    def second_half(c, carry):
        r_f, r_b = rows(c), rows(nch - 1 - c)
        h_f, h_b, carry = advance(xc_s[pl.ds(r_f, rc), :], xc_s[pl.ds(r_b, rc), :], carry)
        emit(r_f, h_f + hb_s[pl.ds(r_f, rc), :])
        emit(r_b, hf_s[pl.ds(r_b, rc), :] + h_b)
        return carry

    lax.fori_loop((nch + 1) // 2, nch, second_half, carry)


def _rg_scan(rec, gate, conv_w, conv_b, w_a, b_a, w_i, b_i, lam):
    B, Tp, W = rec.shape
    nb = W // RG_BLOCK
    rc = _pick_tile(Tp, 384, 64)
    blk = lambda b, n: (b, 0, n)
    vec = lambda b, n: (0, n)
    vec3 = lambda b, n: (0, 0, n)
    return pl.pallas_call(
        functools.partial(_rg_kernel, Tp=Tp, rc=rc),
        grid=(B, nb),
        in_specs=[pl.BlockSpec((1, Tp, RG_BLOCK), blk),
                  pl.BlockSpec((1, Tp, RG_BLOCK), blk),
                  pl.BlockSpec((4, RG_BLOCK), vec),
                  pl.BlockSpec((1, RG_BLOCK), vec),
                  pl.BlockSpec((2, 1, RG_BLOCK, RG_BLOCK), lambda b, n: (0, n, 0, 0)),
                  pl.BlockSpec((2, 1, RG_BLOCK), vec3),
                  pl.BlockSpec((2, 1, RG_BLOCK, RG_BLOCK), lambda b, n: (0, n, 0, 0)),
                  pl.BlockSpec((2, 1, RG_BLOCK), vec3),
                  pl.BlockSpec((2, 1, RG_BLOCK), vec3)],
        out_specs=pl.BlockSpec((1, Tp, RG_BLOCK), blk),
        out_shape=jax.ShapeDtypeStruct((B, Tp, W), BF16),
        scratch_shapes=[pltpu.VMEM((Tp, RG_BLOCK), F32)] * 3,
        compiler_params=_cparams("parallel", "parallel"),
    )(rec, gate, conv_w, conv_b.reshape(1, W), w_a.astype(BF16), b_a.reshape(2, 1, W),
      w_i.astype(BF16), b_i.reshape(2, 1, W), lam.reshape(2, 1, W))


def _rglru_mixer(hn, h, B, Tp, w_in, conv_w, conv_b, w_a, b_a, w_i, b_i, lam, w_out, nw):
    W = D_MODEL
    w_in = w_in.astype(BF16)
    gate = _matmul(hn, w_in[:, :W], BF16, act="gelu")
    rec = _matmul(hn, w_in[:, W:], F32)
    y = _rg_scan(rec.reshape(B, Tp, W), gate.reshape(B, Tp, W), conv_w, conv_b, w_a, b_a, w_i, b_i, lam)
    return _proj_res(y.reshape(B * Tp, W), w_out.astype(BF16), h, nw)


def _na_kernel(q_ref, k_ref, v_ref, bias_ref, mb_ref, o_ref, *, rows):
    scale = NA_HEAD_DIM ** -0.5
    kh = NA_WIN_R
    mb = mb_ref[0]
    km = k_ref[0, PAD:HEAD_ROWS, :]
    vm = v_ref[0, PAD:HEAD_ROWS, :]

    o_ref[0, 0:PAD, :] = jnp.zeros((PAD, NA_HEAD_DIM), o_ref.dtype)
    qm = q_ref[0, PAD:HEAD_ROWS, :]
    s_m = _dot_nt(qm, km) * scale + mb
    p_m = jnp.exp(s_m - jnp.max(s_m, axis=-1, keepdims=True))
    o_m = _dot(p_m.astype(BF16), vm) / jnp.sum(p_m, axis=-1, keepdims=True)
    o_ref[0, PAD:HEAD_ROWS, :] = o_m.astype(o_ref.dtype)

    group = np.gcd(rows, NA_ROW_GROUP)
    R = range(group)

    def group_body(i, _):
        r = [i * group + c for c in R]
        rs = [jnp.clip(r[c] - NA_WIN_R // 2, 0, rows - kh) for c in R]
        q0 = [pl.multiple_of(HEAD_ROWS + r[c] * GRID_W, GRID_W) for c in R]
        k0 = [pl.multiple_of(HEAD_ROWS + rs[c] * GRID_W, GRID_W) for c in R]
        q_r = [q_ref[0, pl.ds(q0[c], GRID_W), :] for c in R]
        s = [_dot_nt(q_r[c], k_ref[0, pl.ds(k0[c], kh * GRID_W), :]) for c in R]
        s_met = [_dot_nt(q_r[c], km) for c in R]
        p, p_met, l = [], [], []
        for c in R:
            sc = s[c] * scale + bias_ref[0, r[c] - rs[c]]
            sm = s_met[c] * scale + mb
            m = jnp.maximum(jnp.max(sc, axis=-1, keepdims=True), jnp.max(sm, axis=-1, keepdims=True))
            pc = jnp.exp(sc - m)
            pm = jnp.exp(sm - m)
            l.append(jnp.sum(pc, axis=-1, keepdims=True) + jnp.sum(pm, axis=-1, keepdims=True))
            p.append(pc.astype(BF16))
            p_met.append(pm.astype(BF16))
        o = [_dot(p[c], v_ref[0, pl.ds(k0[c], kh * GRID_W), :]) + _dot(p_met[c], vm) for c in R]
        for c in R:
            o_ref[0, pl.ds(q0[c], GRID_W), :] = (o[c] / l[c]).astype(o_ref.dtype)
        return 0

    lax.fori_loop(0, rows // group, group_body, 0)


def _na_bias_table(rpb):
    kh = NA_WIN_R
    cols = np.arange(GRID_W)
    col_start = np.clip(cols - NA_WIN_C // 2, 0, GRID_W - NA_WIN_C)
    valid = (cols[None, :] >= col_start[:, None]) & (cols[None, :] < col_start[:, None] + NA_WIN_C)
    col_off = np.clip(cols[None, :] - cols[:, None], -(NA_WIN_C - 1), NA_WIN_C - 1) + NA_WIN_C - 1
    H, n_ro, n_co = rpb.shape
    onehot = (col_off[None] == np.arange(n_co)[:, None, None]).astype(np.float32)
    a = jnp.einsum('hrc,cqk->hqrk', rpb.astype(F32), jnp.asarray(onehot), precision=lax.Precision.HIGHEST)
    a = jnp.where(jnp.asarray(valid)[None, :, None, :], a, NEG_INF)
    tabs = [a[:, :, NA_WIN_R - 1 - d:NA_WIN_R - 1 - d + kh, :].reshape(H, GRID_W, kh * GRID_W)
            for d in range(NA_WIN_R)]
    return jnp.stack(tabs, axis=1)


def _na_attention(qkv, bias, meta_bias):
    B, Tp, _ = qkv.shape
    rows = (Tp - HEAD_ROWS) // GRID_W
    assert rows >= NA_WIN_R
    H = NA_HEADS
    return pl.pallas_call(
        functools.partial(_na_kernel, rows=rows),
        grid=(B, H),
        in_specs=[pl.BlockSpec((1, Tp, NA_HEAD_DIM), lambda b, h: (b, 0, h)),
                  pl.BlockSpec((1, Tp, NA_HEAD_DIM), lambda b, h: (b, 0, H + h)),
                  pl.BlockSpec((1, Tp, NA_HEAD_DIM), lambda b, h: (b, 0, 2 * H + h)),
                  pl.BlockSpec((1, NA_WIN_R, GRID_W, NA_WIN_R * GRID_W), lambda b, h: (h, 0, 0, 0)),
                  pl.BlockSpec((1, 1, N_META), lambda b, h: (h, 0, 0))],
        out_specs=pl.BlockSpec((1, Tp, NA_HEAD_DIM), lambda b, h: (b, 0, h)),
        out_shape=jax.ShapeDtypeStruct((B, Tp, H * NA_HEAD_DIM), BF16),
        compiler_params=_cparams("parallel", "parallel"),
    )(qkv, qkv, qkv, bias, meta_bias.astype(F32).reshape(H, 1, N_META))


def _na_mixer(hn, h, B, Tp, w_qkv, bias, meta_bias, w_o, nw):
    qkv = _matmul(hn, w_qkv.astype(BF16), BF16)
    o = _na_attention(qkv.reshape(B, Tp, -1), bias, meta_bias)
    return _proj_res(o.reshape(B * Tp, -1), w_o.astype(BF16), h, nw)


def _gdn_gate_kernel(x_ref, w_ref, al_ref, dt_ref, o_ref):
    y = _dot(x_ref[...], w_ref[...])
    lane = lax.broadcasted_iota(jnp.int32, y.shape, 1)
    g = -jnp.exp(al_ref[...]) * _softplus(y + dt_ref[...])
    o_ref[...] = jnp.where(lane < 2 * GDN_V_HEADS, _sigmoid(y), g)


def _gdn_gates(hn, w_ba, a_log, dt_bias):
    M, K = hn.shape
    N = 4 * GDN_V_HEADS
    tm = _pick_tile(M, 1024, 128)
    zeros = jnp.zeros((2 * GDN_V_HEADS,), F32)
    al = jnp.concatenate([zeros, a_log.astype(F32).reshape(-1)]).reshape(1, N)
    dt = jnp.concatenate([zeros, dt_bias.astype(F32).reshape(-1)]).reshape(1, N)
    return pl.pallas_call(
        _gdn_gate_kernel,
        grid=(M // tm,),
        in_specs=[pl.BlockSpec((tm, K), lambda i: (i, 0)), pl.BlockSpec((K, N), lambda i: (0, 0)),
                  pl.BlockSpec((1, N), lambda i: (0, 0)), pl.BlockSpec((1, N), lambda i: (0, 0))],
        out_specs=pl.BlockSpec((tm, N), lambda i: (i, 0)),
        out_shape=jax.ShapeDtypeStruct((M, N), F32),
        compiler_params=_cparams("parallel"),
    )(hn, w_ba, al, dt)


def _split3_dot(x, m):
    hi = x.astype(BF16)
    r1 = x - hi.astype(F32)
    mid = r1.astype(BF16)
    lo = (r1 - mid.astype(F32)).astype(BF16)
    return _dot(hi, m) + _dot(mid, m) + _dot(lo, m)


def _gdn_kernel(q_ref, k_ref, v_ref, z_ref, bgc_ref, bgr_ref, nw_ref, o_ref,
                S_s, oacc_s, u_s, wq_s, kst_s, qkd_s, gam_s, *, Tp, HPS):
    hb = pl.program_id(1)
    C = CHUNK
    NC = Tp // C
    HV = GDN_V_HEADS
    DK = GDN_DK
    G = 4
    W = G * C
    P1_GROUP = 8 if HPS == 1 else 5
    scale = DK ** -0.5

    oacc_s[...] = jnp.zeros_like(oacc_s)
    S_s[...] = jnp.zeros_like(S_s)

    ri = lax.broadcasted_iota(jnp.int32, (C, W), 0)
    cl = lax.broadcasted_iota(jnp.int32, (C, W), 1)
    cj = jnp.bitwise_and(cl, C - 1)
    blk = jnp.right_shift(cl, 6)
    dd = jnp.where(blk >= 2, cj - ri, ri - cj)
    incl4 = dd >= 0
    strict4 = dd > 0
    eye4 = (ri == cj).astype(F32)
    cum4 = jnp.where(dd <= 0, 1.0, 0.0).astype(BF16)
    bdm = jnp.where(jnp.right_shift(lax.broadcasted_iota(jnp.int32, (W, W), 0), 6)
                    == jnp.right_shift(lax.broadcasted_iota(jnp.int32, (W, W), 1), 6), 1.0, 0.0).astype(BF16)
    blk_row = jnp.right_shift(lax.broadcasted_iota(jnp.int32, (1, W), 1), 6)
    lane = lax.broadcasted_iota(jnp.int32, (C, 4 * HV), 1)
    rowi = lax.broadcasted_iota(jnp.int32, (C, 4 * HV), 0)

    def col(x, c):
        return jnp.sum(jnp.where(lane == c, x, 0.0), axis=1, keepdims=True)

    def by_block(parts, b):
        return jnp.where(b == 0, parts[0], jnp.where(b == 1, parts[1], jnp.where(b == 2, parts[2], parts[3])))

    def blockdiag(xb):
        return jnp.concatenate([xb] * G, axis=0) * bdm

    def phase1(ns):
        items = [(hd, n) for n in ns for hd in range(HPS)]
        K = range(len(items))
        r0 = [pl.multiple_of(n * C, C) for (hd, n) in items]
        q = [q_ref[0, pl.ds(r0[i], C), items[i][0] * DK:(items[i][0] + 1) * DK] for i in K]
        k = [k_ref[0, pl.ds(r0[i], C), items[i][0] * DK:(items[i][0] + 1) * DK] for i in K]
        bg = [bgc_ref[0, pl.ds(r0[i], C), :] for i in K]
        v = [v_ref[0, pl.ds(r0[i], C), items[i][0] * 2 * DK:(items[i][0] + 1) * 2 * DK].astype(F32) for i in K]
        rows = [bgr_ref[0, n, hd] for (hd, n) in items]
        cum_rows = [_split3_dot(rows[i], cum4) for i in K]
        k4 = [jnp.concatenate([k[i]] * G, axis=0) for i in K]
        kk4 = [_dot_nt(k[i], k4[i]) for i in K]
        qk4 = [_dot_nt(q[i], k4[i]) * scale for i in K]
        beta_c, gc_c, decay4, L = [], [], [], []
        for i in K:
            pre = bg[i]
            for s in (1, 2, 4, 8, 16, 32):
                pre = pre + jnp.where(rowi >= s, pltpu.roll(pre, s, 0), 0.0)
            suf = pre[C - 1:C] - pre + bg[i]
            gc_r = by_block([cum_rows[i][4 + g:5 + g] for g in range(G)], blk_row)
            bc, gc = [], []
            for g in range(G):
                d, j = g // 2, g % 2
                vh = 2 * (hb * HPS + items[i][0]) + j
                bc.append(col(bg[i], d * HV + vh))
                gc.append(col(suf if d else pre, 2 * HV + d * HV + vh))
            beta_c.append(bc)
            gc_c.append(gc)
            dec = jnp.where(incl4, jnp.exp(jnp.where(incl4, by_block(gc, blk) - gc_r, 0.0)), 0.0)
            decay4.append(dec)
            L.append(jnp.where(strict4, kk4[i] * dec, 0.0) * by_block(bc, blk))
        Tm = [eye4 - L[i] for i in K]
        Lb = [L[i].astype(BF16) for i in K]
        P = [_dot(Lb[i], blockdiag(Lb[i])) for i in K]
        for lvl in range(5):
            Pb = [P[i].astype(BF16) for i in K]
            bd = [blockdiag(Pb[i]) for i in K]
            if lvl < 4:
                tp = [_dot(jnp.concatenate([Tm[i].astype(BF16), Pb[i]], axis=0), bd[i]) for i in K]
                Tm = [Tm[i] + tp[i][:C] for i in K]
                P = [tp[i][C:] for i in K]
            else:
                tp = [_dot(Tm[i].astype(BF16), bd[i]) for i in K]
                Tm = [Tm[i] + tp[i] for i in K]
        rhs, kst, qst, gam = [], [], [], []
        for i in K:
            qf = q[i].astype(F32)
            kf = k[i].astype(F32)
            rhs_i, kst_i, qst_i, gam_i = [], [], [], []
            for g in range(G):
                j = g % 2
                bc, gc = beta_c[i][g], gc_c[i][g]
                e_c = jnp.exp(gc)
                rhs_i.append(jnp.concatenate([v[i][:, j * DK:(j + 1) * DK] * bc, kf * (bc * e_c)], axis=1))
                g_last = gc[0:1] if g >= 2 else gc[C - 1:C]
                kst_i.append((kf * jnp.exp(g_last - gc)).astype(BF16))
                qst_i.append((qf * (scale * e_c)).astype(BF16))
                gam_i.append(jnp.broadcast_to(jnp.exp(g_last), (1, DK)))
            rhs.append(jnp.concatenate(rhs_i, axis=0).astype(BF16))
            kst.append(kst_i)
            qst.append(qst_i)
            gam.append(gam_i)
        sol = [_dot(blockdiag(Tm[i].astype(BF16)), rhs[i]) for i in K]
        for i in K:
            hd, n = items[i]
            u_s[hd, n] = sol[i][:, :DK]
            qkd_s[hd, n] = jnp.where(incl4, qk4[i] * decay4[i], 0.0).astype(BF16)
            for g in range(G):
                wq_s[hd, n, g, 0:C, :] = sol[i][g * C:(g + 1) * C, DK:].astype(BF16)
                wq_s[hd, n, g, C:2 * C, :] = qst[i][g]
                kst_s[hd, n, g] = kst[i][g]
                gam_s[hd, n, g:g + 1, :] = gam[i][g]

    def phase1_group(i, _):
        phase1([P1_GROUP * i + c for c in range(P1_GROUP)])
        return 0

    lax.fori_loop(0, NC // P1_GROUP, phase1_group, 0)
    if NC % P1_GROUP:
        phase1(list(range(NC - NC % P1_GROUP, NC)))

    nw = nw_ref[...]

    def gated_norm(o, z):
        ys = []
        for j in range(2 * HPS):
            oj = o[:, j * DK:(j + 1) * DK]
            zj = z[:, j * DK:(j + 1) * DK].astype(F32)
            ys.append(oj * lax.rsqrt(jnp.mean(oj * oj, axis=-1, keepdims=True) + NORM_EPS) * nw * (zj * _sigmoid(zj)))
        return jnp.concatenate(ys, axis=1)

    def finalize(n):
        r0 = pl.multiple_of(n * C, C)
        o_ref[0, pl.ds(r0, C), :] = gated_norm(oacc_s[pl.ds(r0, C), :], z_ref[0, pl.ds(r0, C), :]).astype(o_ref.dtype)

    def phase2(it, finalize_previous):
        if finalize_previous:
            finalize(it - 1)
            finalize(NC - it)
        ops = []
        for hd in range(HPS):
            for g in range(G):
                n = it if g < 2 else NC - 1 - it
                ops.append((S_s[hd * G + g], wq_s[hd, n, g], u_s[hd, n, g * C:(g + 1) * C, :],
                            qkd_s[hd, n, :, g * C:(g + 1) * C], gam_s[hd, n, g:g + 1, :], kst_s[hd, n, g]))
        NI = HPS * G
        r_f = pl.multiple_of(it * C, C)
        r_b = pl.multiple_of((NC - 1 - it) * C, C)
        o_f = oacc_s[pl.ds(r_f, C), :]
        ws = [_dot(wq, S.astype(BF16)) for (S, wq, u, qkd, gam, kst) in ops]
        vb = [(ops[g][2] - ws[g][:C]).astype(BF16) for g in range(NI)]
        outs = [ws[g][C:] + _dot(ops[g][3], vb[g]) for g in range(NI)]
        states = [ops[g][0] * ops[g][4] + _dot_tn(ops[g][5], vb[g]) for g in range(NI)]
        for g in range(NI):
            S_s[g] = states[g]
        fwd = [outs[hd * G + j] for hd in range(HPS) for j in range(2)]
        bwd = [outs[hd * G + 2 + j] for hd in range(HPS) for j in range(2)]
        o_f = o_f + jnp.concatenate(fwd, axis=1)
        oacc_s[pl.ds(r_f, C), :] = o_f
        oacc_s[pl.ds(r_b, C), :] = oacc_s[pl.ds(r_b, C), :] + jnp.concatenate(bwd, axis=1)
        return 0

    lax.fori_loop(0, NC // 2 + 1, lambda it, c: phase2(it, False), 0)
    lax.fori_loop(NC // 2 + 1, NC, lambda it, c: phase2(it, True), 0)
    finalize(NC - 1)
    finalize(0)


_GDN_CHUNK_BYTES = CHUNK * GDN_DK * (4 * 4 + 8 * 2 + 4 * 2 + 4 * 2) + 8 * GDN_DK * 4
_GDN_PAIR_BUDGET = 26 * 1024 * 1024


def _gdn_delta(qkv, z, bg, norm_w):
    B, Tp, _ = qkv.shape
    NC = Tp // CHUNK
    KH = GDN_QK_HEADS
    HV = GDN_V_HEADS
    C = CHUNK
    HPS = 2 if NC * _GDN_CHUNK_BYTES * 2 <= _GDN_PAIR_BUDGET else 1
    NH = KH // HPS
    bgr = bg.reshape(B, NC, CHUNK, 4, KH, 2).transpose(0, 1, 4, 3, 5, 2).reshape(B, NC, KH, 8, CHUNK)
    return pl.pallas_call(
        functools.partial(_gdn_kernel, Tp=Tp, HPS=HPS),
        grid=(B, NH),
        in_specs=[pl.BlockSpec((1, Tp, HPS * GDN_DK), lambda b, h: (b, 0, h)),
                  pl.BlockSpec((1, Tp, HPS * GDN_DK), lambda b, h: (b, 0, NH + h)),
                  pl.BlockSpec((1, Tp, 2 * HPS * GDN_DK), lambda b, h: (b, 0, NH + h)),
                  pl.BlockSpec((1, Tp, 2 * HPS * GDN_DK), lambda b, h: (b, 0, h)),
                  pl.BlockSpec((1, Tp, 4 * HV), lambda b, h: (b, 0, 0)),
                  pl.BlockSpec((1, NC, HPS, 8, CHUNK), lambda b, h: (b, 0, h, 0, 0)),
                  pl.BlockSpec((1, GDN_DK), lambda b, h: (0, 0))],
        out_specs=pl.BlockSpec((1, Tp, 2 * HPS * GDN_DK), lambda b, h: (b, 0, h)),
        out_shape=jax.ShapeDtypeStruct((B, Tp, GDN_VAL_DIM), BF16),
        scratch_shapes=[pltpu.VMEM((4 * HPS, GDN_DK, GDN_DK), F32),
                        pltpu.VMEM((Tp, 2 * HPS * GDN_DK), F32),
                        pltpu.VMEM((HPS, NC, 4 * C, GDN_DK), F32),
                        pltpu.VMEM((HPS, NC, 4, 2 * C, GDN_DK), BF16),
                        pltpu.VMEM((HPS, NC, 4, C, GDN_DK), BF16),
                        pltpu.VMEM((HPS, NC, C, 4 * C), BF16),
                        pltpu.VMEM((HPS, NC, 8, GDN_DK), F32)],
        compiler_params=_cparams("parallel", "parallel"),
    )(qkv, qkv, qkv, z, bg, bgr, norm_w.astype(F32).reshape(1, GDN_DK))


def _gdn_mixer(hn, h, B, Tp, w_in, conv_w, a_log, dt_bias, norm_w, w_out, nw):
    nz = GDN_CONV_DIM + GDN_VAL_DIM
    w_in = w_in.astype(BF16)
    qkv = _proj_conv(hn, w_in[:, :GDN_CONV_DIM], conv_w, Tp, 2 * GDN_KEY_DIM)
    z = _matmul(hn, w_in[:, GDN_CONV_DIM:nz], BF16)
    bg = _gdn_gates(hn, w_in[:, nz:], a_log, dt_bias)
    o = _gdn_delta(qkv.reshape(B, Tp, -1), z.reshape(B, Tp, -1), bg.reshape(B, Tp, -1), norm_w)
    return _proj_res(o.reshape(B * Tp, -1), w_out.astype(BF16), h, nw)


def _trunk(x, meta_tokens, mix_norm, ffn_norm, final_norm, rg, na, gdn, ffn):
    B, T, D = x.shape
    Tp = T + HEAD_ROWS
    depth = mix_norm.shape[0]
    h3 = jnp.concatenate([jnp.zeros((B, PAD, D), F32),
                          jnp.broadcast_to(meta_tokens.astype(F32)[None], (B, N_META, D)),
                          x.astype(F32)], axis=1)
    h = h3.reshape(B * Tp, D)
    hn = _rmsnorm(h, mix_norm[0], BF16)
    for i in range(depth):
        kind, j = i % 3, i // 3
        if kind == 0:
            h, hn = _rglru_mixer(hn, h, B, Tp, *[p[j] for p in rg], ffn_norm[i])
        elif kind == 1:
            h, hn = _na_mixer(hn, h, B, Tp, *[p[j] for p in na], ffn_norm[i])
        else:
            h, hn = _gdn_mixer(hn, h, B, Tp, *[p[j] for p in gdn], ffn_norm[i])
        w_gate, w_up, conv_w, conv_b, w_down = [p[i] for p in ffn]
        next_norm = mix_norm[i + 1] if i + 1 < depth else final_norm
        h, hn = _conv_ffn(hn, h, w_gate.astype(BF16), w_up.astype(BF16), conv_w, conv_b,
                          w_down.astype(BF16), next_norm)
    return _final_norm(h, final_norm, B, T)


def kernel(x_prompt, x_sample, meta_tokens, mix_norm, ffn_norm, final_norm, rg_w_in, rg_conv_w, rg_conv_b, rg_w_a, rg_b_a, rg_w_i, rg_b_i, rg_lam, rg_w_out, na_w_qkv, na_rpb, na_meta_bias, na_w_o, gdn_w_in, gdn_conv_w, gdn_a_log, gdn_dt_bias, gdn_norm_w, gdn_w_out, ffn_w_gate, ffn_w_up, ffn_conv_w, ffn_conv_b, ffn_w_down):
    rg = (rg_w_in, rg_conv_w, rg_conv_b, rg_w_a, rg_b_a, rg_w_i, rg_b_i, rg_lam, rg_w_out)
    na_bias = jnp.stack([_na_bias_table(na_rpb[j]) for j in range(na_rpb.shape[0])])
    na = (na_w_qkv, na_bias, na_meta_bias, na_w_o)
    gdn = (gdn_w_in, gdn_conv_w, gdn_a_log, gdn_dt_bias, gdn_norm_w, gdn_w_out)
    ffn = (ffn_w_gate, ffn_w_up, ffn_conv_w, ffn_conv_b, ffn_w_down)
    y_prompt = _trunk(x_prompt, meta_tokens, mix_norm, ffn_norm, final_norm, rg, na, gdn, ffn)
    y_sample = _trunk(x_sample, meta_tokens, mix_norm, ffn_norm, final_norm, rg, na, gdn, ffn)
    return (y_prompt, y_sample)
```

```python
import functools

import numpy as np
import jax
import jax.numpy as jnp
from jax import lax
from jax.experimental import pallas as pl
from jax.experimental.pallas import tpu as pltpu

D_MODEL = 2048
N_META = 16
PAD = 48
HEAD_ROWS = PAD + N_META
CHUNK = 64
GRID_W = 64
NORM_EPS = 1e-6

RG_BLOCK = 256
RG_C = 8.0

NA_HEAD_DIM = 128
NA_HEADS = D_MODEL // NA_HEAD_DIM
NA_WIN_R = 8
NA_WIN_C = 16
NA_ROW_GROUP = 32
NEG_INF = -1e30

GDN_DK = 128
GDN_QK_HEADS = D_MODEL // GDN_DK
GDN_V_HEADS = 2 * GDN_QK_HEADS
GDN_KEY_DIM = GDN_QK_HEADS * GDN_DK
GDN_VAL_DIM = GDN_V_HEADS * GDN_DK
GDN_CONV_DIM = 2 * GDN_KEY_DIM + GDN_VAL_DIM

V7X_VMEM_LIMIT = 56 * 1024 * 1024
HALO = 16

BF16 = jnp.bfloat16
F32 = jnp.float32


def _cparams(*sem):
    return pltpu.CompilerParams(dimension_semantics=sem, vmem_limit_bytes=V7X_VMEM_LIMIT)


def _pick_tile(n, cap, mult):
    best = None
    for t in range(mult, cap + 1, mult):
        if n % t == 0:
            best = t
    assert best is not None, (n, cap, mult)
    return best


def _sigmoid(x):
    return 1.0 / (1.0 + jnp.exp(-x))


def _softplus(x):
    return jnp.maximum(x, 0.0) + jnp.log(1.0 + jnp.exp(-jnp.abs(x)))


def _gelu_tanh(x):
    return 0.5 * x * (1.0 + jnp.tanh(0.7978845608028654 * (x + 0.044715 * x * x * x)))


def _rms_scale(x, w):
    ms = jnp.mean(x * x, axis=-1, keepdims=True)
    return x * lax.rsqrt(ms + NORM_EPS) * w


def _dot(a, b):
    return jnp.dot(a, b, preferred_element_type=F32)


def _dot_nt(a, b):
    return lax.dot_general(a, b, (((1,), (1,)), ((), ())), preferred_element_type=F32)


def _dot_tn(a, b):
    return lax.dot_general(a, b, (((0,), (0,)), ((), ())), preferred_element_type=F32)


def _rmsnorm_kernel(x_ref, w_ref, o_ref):
    o_ref[...] = _rms_scale(x_ref[...], w_ref[...]).astype(o_ref.dtype)


def _rmsnorm(h, w, out_dtype):
    M, D = h.shape
    tm = _pick_tile(M, 1024, 64)
    return pl.pallas_call(
        _rmsnorm_kernel,
        grid=(M // tm,),
        in_specs=[pl.BlockSpec((tm, D), lambda i: (i, 0)), pl.BlockSpec((1, D), lambda i: (0, 0))],
        out_specs=pl.BlockSpec((tm, D), lambda i: (i, 0)),
        out_shape=jax.ShapeDtypeStruct((M, D), out_dtype),
        compiler_params=_cparams("parallel"),
    )(h, w.reshape(1, D))


def _final_norm_kernel(x_ref, w_ref, o_ref):
    o_ref[0] = _rms_scale(x_ref[...], w_ref[...])


def _final_norm(h, w, B, T):
    D = h.shape[1]
    Tp = T + HEAD_ROWS
    tt = 512
    assert T % tt == 0
    return pl.pallas_call(
        _final_norm_kernel,
        grid=(B, T // tt),
        in_specs=[pl.BlockSpec((pl.Element(tt), pl.Element(D)),
                               lambda b, j: (pl.multiple_of(b * Tp + HEAD_ROWS + j * tt, CHUNK), 0)),
                  pl.BlockSpec((1, D), lambda b, j: (0, 0))],
        out_specs=pl.BlockSpec((1, tt, D), lambda b, j: (b, j, 0)),
        out_shape=jax.ShapeDtypeStruct((B, T, D), F32),
        compiler_params=_cparams("parallel", "parallel"),
    )(h, w.reshape(1, D))


def _matmul_kernel(x_ref, w_ref, o_ref, *, act):
    acc = _dot(x_ref[...], w_ref[...])
    if act == "gelu":
        acc = _gelu_tanh(acc)
    o_ref[...] = acc.astype(o_ref.dtype)


def _matmul(x, w, out_dtype, act=None):
    M, K = x.shape
    N = w.shape[1]
    tm = _pick_tile(M, 1024, 128)
    tn = _pick_tile(N, 1024, 128)
    return pl.pallas_call(
        functools.partial(_matmul_kernel, act=act),
        grid=(M // tm, N // tn),
        in_specs=[pl.BlockSpec((tm, K), lambda i, j: (i, 0)), pl.BlockSpec((K, tn), lambda i, j: (0, j))],
        out_specs=pl.BlockSpec((tm, tn), lambda i, j: (i, j)),
        out_shape=jax.ShapeDtypeStruct((M, N), out_dtype),
        compiler_params=_cparams("parallel", "parallel"),
    )(x, w)


def _proj_res_kernel(x_ref, w_ref, h_ref, nw_ref, hout_ref, hn_ref):
    hnew = h_ref[...] + _dot(x_ref[...], w_ref[...])
    hout_ref[...] = hnew
    hn_ref[...] = _rms_scale(hnew, nw_ref[...]).astype(hn_ref.dtype)


def _proj_res(x, w, h, nw):
    M, K = x.shape
    D = w.shape[1]
    tm = _pick_tile(M, 704 if K <= D_MODEL else 512, 64)
    return pl.pallas_call(
        _proj_res_kernel,
        grid=(M // tm,),
        in_specs=[pl.BlockSpec((tm, K), lambda i: (i, 0)),
                  pl.BlockSpec((K, D), lambda i: (0, 0), pipeline_mode=pl.Buffered(1)),
                  pl.BlockSpec((tm, D), lambda i: (i, 0)),
                  pl.BlockSpec((1, D), lambda i: (0, 0))],
        out_specs=[pl.BlockSpec((tm, D), lambda i: (i, 0)), pl.BlockSpec((tm, D), lambda i: (i, 0))],
        out_shape=[jax.ShapeDtypeStruct((M, D), F32), jax.ShapeDtypeStruct((M, D), BF16)],
        compiler_params=_cparams("parallel"),
    )(x, w, h, nw.reshape(1, D))


def _proj_conv_kernel(xp_ref, xm_ref, xn_ref, w_ref, cw_ref, o_ref, xext_ref, *, tm, tc, ni, Tp, n_norm):
    i = pl.program_id(0)
    j = pl.program_id(1)
    npc = o_ref.shape[1] // tc

    @pl.when(j == 0)
    def _():
        xext_ref[0:HALO, :] = xp_ref[...]
        xext_ref[HALO:HALO + tm, :] = xm_ref[...]
        xext_ref[HALO + tm:2 * HALO + tm, :] = xn_ref[...]

    @pl.when(jnp.logical_and(j == 0, i == ni - 1))
    def _():
        xext_ref[HALO + tm:2 * HALO + tm, :] = jnp.zeros((HALO, xext_ref.shape[1]), xext_ref.dtype)

    rel = i * tm - (i * tm // Tp) * Tp + lax.broadcasted_iota(jnp.int32, (tm, 1), 0)
    live = jnp.where(rel >= Tp, rel - Tp, rel) >= PAD
    cw = cw_ref[...]
    xe = xext_ref[...]

    def piece_matmul(c):
        return _dot(xe, w_ref[:, c * tc:(c + 1) * tc])

    def piece_finish(c, g):
        cwc = cw[:, c * tc:(c + 1) * tc]
        a = (cwc[0:1] * g[HALO - 1:HALO - 1 + tm] + cwc[1:2] * g[HALO:HALO + tm]
             + cwc[2:3] * g[HALO + 1:HALO + 1 + tm] + cwc[3:4] * g[HALO + 2:HALO + 2 + tm])
        y = jnp.where(live, a * _sigmoid(a), 0.0)
        normalise = j * npc + c < n_norm
        heads = []
        for hh in range(tc // GDN_DK):
            yh = y[:, hh * GDN_DK:(hh + 1) * GDN_DK]
            nrm = lax.rsqrt(jnp.sum(yh * yh, axis=-1, keepdims=True) + 1e-6)
            heads.append(yh * jnp.where(normalise, nrm, 1.0))
        o_ref[:, c * tc:(c + 1) * tc] = jnp.concatenate(heads, axis=1).astype(o_ref.dtype)

    g = piece_matmul(0)
    for c in range(npc):
        g_next = piece_matmul(c + 1) if c + 1 < npc else None
        piece_finish(c, g)
        g = g_next


def _proj_conv(x, w, conv_w, Tp, n_norm_cols):
    M, K = x.shape
    N = w.shape[1]
    tm = _pick_tile(M, 1024, 64)
    assert tm <= Tp
    tn, tc = 2048, 512
    ni = M // tm
    hb = tm // HALO
    nhb = M // HALO
    return pl.pallas_call(
        functools.partial(_proj_conv_kernel, tm=tm, tc=tc, ni=ni, Tp=Tp, n_norm=n_norm_cols // tc),
        grid=(ni, N // tn),
        in_specs=[pl.BlockSpec((HALO, K), lambda i, j: (jnp.maximum(i * hb - 1, 0), 0)),
                  pl.BlockSpec((tm, K), lambda i, j: (i, 0)),
                  pl.BlockSpec((HALO, K), lambda i, j: (jnp.minimum((i + 1) * hb, nhb - 1), 0)),
                  pl.BlockSpec((K, tn), lambda i, j: (0, j)),
                  pl.BlockSpec((4, tn), lambda i, j: (0, j))],
        out_specs=pl.BlockSpec((tm, tn), lambda i, j: (i, j)),
        out_shape=jax.ShapeDtypeStruct((M, N), BF16),
        scratch_shapes=[pltpu.VMEM((tm + 2 * HALO, K), BF16)],
        compiler_params=_cparams("parallel", "arbitrary"),
    )(x, x, x, w, conv_w)


def _ffn_kernel(xp_ref, xm_ref, xn_ref, wg_ref, wu_ref, cw_ref, cb_ref, wd_ref, h_ref, nw_ref,
                hout_ref, hn_ref, xext_ref, *, tm, nf, ni):
    i = pl.program_id(0)
    f = pl.program_id(1)
    th = tm // 2

    @pl.when(f == 0)
    def _():
        xext_ref[0:HALO, :] = xp_ref[...]
        xext_ref[HALO:HALO + tm, :] = xm_ref[...]
        xext_ref[HALO + tm:2 * HALO + tm, :] = xn_ref[...]
        hout_ref[...] = h_ref[...]

    cw = cw_ref[...]
    cb = cb_ref[...]
    g, up = [], []
    for s in range(2):
        g.append(_dot(xext_ref[s * th:s * th + th + 2 * HALO, :], wg_ref[...]))
        up.append(_dot(xm_ref[s * th:(s + 1) * th, :], wu_ref[...]))
    row = lax.broadcasted_iota(jnp.int32, (th, 1), 0)
    y = []
    for s in range(2):
        g_prev = g[s][HALO - 1:HALO - 1 + th]
        g_here = g[s][HALO:HALO + th]
        g_next = g[s][HALO + 1:HALO + 1 + th]
        if s == 1:
            g_next = jnp.where(jnp.logical_and(i == ni - 1, row == th - 1), 0.0, g_next)
        a = cw[0:1] * g_prev + cw[1:2] * g_here + cw[2:3] * g_next + cb
        y.append((a * _sigmoid(a) * up[s]).astype(BF16))
    down = [_dot(y[s], wd_ref[...]) for s in range(2)]
    for s in range(2):
        hout_ref[s * th:(s + 1) * th, :] += down[s]

    @pl.when(f == nf - 1)
    def _():
        hn_ref[...] = _rms_scale(hout_ref[...], nw_ref[...]).astype(hn_ref.dtype)


def _conv_ffn(hn, h, w_gate, w_up, conv_w, conv_b, w_down, nw):
    M, D = hn.shape
    F = w_gate.shape[1]
    tm = _pick_tile(M, 704, 64)
    tf = _pick_tile(F, 512, 128)
    ni, nf = M // tm, F // tf
    hb = tm // HALO
    nhb = M // HALO
    return pl.pallas_call(
        functools.partial(_ffn_kernel, tm=tm, nf=nf, ni=ni),
        grid=(ni, nf),
        in_specs=[pl.BlockSpec((HALO, D), lambda i, f: (jnp.maximum(i * hb - 1, 0), 0)),
                  pl.BlockSpec((tm, D), lambda i, f: (i, 0)),
                  pl.BlockSpec((HALO, D), lambda i, f: (jnp.minimum((i + 1) * hb, nhb - 1), 0)),
                  pl.BlockSpec((D, tf), lambda i, f: (0, f)),
                  pl.BlockSpec((D, tf), lambda i, f: (0, f)),
                  pl.BlockSpec((3, tf), lambda i, f: (0, f)),
                  pl.BlockSpec((1, tf), lambda i, f: (0, f)),
                  pl.BlockSpec((tf, D), lambda i, f: (f, 0)),
                  pl.BlockSpec((tm, D), lambda i, f: (i, 0)),
                  pl.BlockSpec((1, D), lambda i, f: (0, 0))],
        out_specs=[pl.BlockSpec((tm, D), lambda i, f: (i, 0)), pl.BlockSpec((tm, D), lambda i, f: (i, 0))],
        out_shape=[jax.ShapeDtypeStruct((M, D), F32), jax.ShapeDtypeStruct((M, D), BF16)],
        scratch_shapes=[pltpu.VMEM((tm + 2 * HALO, D), BF16)],
        compiler_params=_cparams("parallel", "arbitrary"),
    )(hn, hn, hn, w_gate, w_up, conv_w, conv_b.reshape(1, F), w_down, h, nw.reshape(1, D))


def _scan8(a, u, row, reverse):
    for s in (1, 2, 4):
        if reverse:
            a_sh = pltpu.roll(a, 8 - s, 0)
            u_sh = pltpu.roll(u, 8 - s, 0)
            m = row < 8 - s
        else:
            a_sh = pltpu.roll(a, s, 0)
            u_sh = pltpu.roll(u, s, 0)
            m = row >= s
        u = jnp.where(m, a * u_sh + u, u)
        a = jnp.where(m, a * a_sh, a)
    return a, u


def _rg_kernel(rec_ref, gate_ref, cw_ref, cb_ref, wa_ref, ba_ref, wi_ref, bi_ref, lam_ref, o_ref,
               xc_s, hf_s, hb_s, *, Tp, rc):
    nch = Tp // rc
    W = RG_BLOCK
    T8 = rc // 8
    cw = cw_ref[...]
    cb = cb_ref[...]
    row8 = lax.broadcasted_iota(jnp.int32, (8, W), 0)

    def rows(c):
        return pl.multiple_of(c * rc, CHUNK)

    def conv(r0):
        top = rec_ref[0, pl.ds(pl.multiple_of(jnp.maximum(r0 - 8, 0), 8), 8), :]
        main = rec_ref[0, pl.ds(r0, rc), :]
        bot = rec_ref[0, pl.ds(pl.multiple_of(jnp.minimum(r0 + rc, Tp - 8), 8), 8), :]
        bot = jnp.where(r0 + rc < Tp, bot, 0.0)
        xw = jnp.concatenate([top, main, bot], axis=0)
        xc = cw[0:1] * xw[7:7 + rc] + cw[1:2] * xw[8:8 + rc] + cw[2:3] * xw[9:9 + rc] + cw[3:4] * xw[10:10 + rc] + cb
        t = r0 + lax.broadcasted_iota(jnp.int32, (rc, 1), 0)
        return jnp.where(t >= PAD, xc, 0.0)

    def advance(x_f, x_b, carry):
        xs = (x_f, x_b)
        xb16 = [x.astype(BF16) for x in xs]
        pre = [(_dot(xb16[d], wa_ref[d, 0]), _dot(xb16[d], wi_ref[d, 0])) for d in range(2)]
        au = []
        for d in range(2):
            r = _sigmoid(pre[d][0] + ba_ref[d])
            ig = _sigmoid(pre[d][1] + bi_ref[d])
            a = jnp.exp((-RG_C) * r * _softplus(-lam_ref[d]))
            au.append((a, jnp.sqrt(1.0 - a * a) * (ig * xs[d])))
        c_f, c_b = carry
        h_f, h_b = [None] * T8, [None] * T8
        for k in range(T8):
            A, U = _scan8(au[0][0][8 * k:8 * k + 8], au[0][1][8 * k:8 * k + 8], row8, False)
            h_f[k] = A * c_f + U
            c_f = h_f[k][7:8]
            kb = T8 - 1 - k
            A, U = _scan8(au[1][0][8 * kb:8 * kb + 8], au[1][1][8 * kb:8 * kb + 8], row8, True)
            h_b[kb] = A * c_b + U
            c_b = h_b[kb][0:1]
        return jnp.concatenate(h_f, axis=0), jnp.concatenate(h_b, axis=0), (c_f, c_b)

    def emit(r0, hsum):
        o_ref[0, pl.ds(r0, rc), :] = (gate_ref[0, pl.ds(r0, rc), :].astype(F32) * hsum).astype(o_ref.dtype)

    def first_half(c, carry):
        r_f, r_b = rows(c), rows(nch - 1 - c)
        x_f, x_b = conv(r_f), conv(r_b)
        h_f, h_b, carry = advance(x_f, x_b, carry)
        xc_s[pl.ds(r_f, rc), :] = x_f
        xc_s[pl.ds(r_b, rc), :] = x_b
        hf_s[pl.ds(r_f, rc), :] = h_f
        hb_s[pl.ds(r_b, rc), :] = h_b
        return carry

    carry = lax.fori_loop(0, nch // 2, first_half, (jnp.zeros((1, W), F32), jnp.zeros((1, W), F32)))

    if nch % 2:
        r_m = (nch // 2) * rc
        x_m = conv(r_m)
        h_f, h_b, carry = advance(x_m, x_m, carry)
        emit(r_m, h_f + h_b)

    def second_half(c, carry):
        r_f, r_b = rows(c), rows(nch - 1 - c)
        h_f, h_b, carry = advance(xc_s[pl.ds(r_f, rc), :], xc_s[pl.ds(r_b, rc), :], carry)
        emit(r_f, h_f + hb_s[pl.ds(r_f, rc), :])
        emit(r_b, hf_s[pl.ds(r_b, rc), :] + h_b)
        return carry

    lax.fori_loop((nch + 1) // 2, nch, second_half, carry)


def _rg_scan(rec, gate, conv_w, conv_b, w_a, b_a, w_i, b_i, lam):
    B, Tp, W = rec.shape
    nb = W // RG_BLOCK
    rc = _pick_tile(Tp, 384, 64)
    blk = lambda b, n: (b, 0, n)
    vec = lambda b, n: (0, n)
    vec3 = lambda b, n: (0, 0, n)
    return pl.pallas_call(
        functools.partial(_rg_kernel, Tp=Tp, rc=rc),
        grid=(B, nb),
        in_specs=[pl.BlockSpec((1, Tp, RG_BLOCK), blk),
                  pl.BlockSpec((1, Tp, RG_BLOCK), blk),
                  pl.BlockSpec((4, RG_BLOCK), vec),
                  pl.BlockSpec((1, RG_BLOCK), vec),
                  pl.BlockSpec((2, 1, RG_BLOCK, RG_BLOCK), lambda b, n: (0, n, 0, 0)),
                  pl.BlockSpec((2, 1, RG_BLOCK), vec3),
                  pl.BlockSpec((2, 1, RG_BLOCK, RG_BLOCK), lambda b, n: (0, n, 0, 0)),
                  pl.BlockSpec((2, 1, RG_BLOCK), vec3),
                  pl.BlockSpec((2, 1, RG_BLOCK), vec3)],
        out_specs=pl.BlockSpec((1, Tp, RG_BLOCK), blk),
        out_shape=jax.ShapeDtypeStruct((B, Tp, W), BF16),
        scratch_shapes=[pltpu.VMEM((Tp, RG_BLOCK), F32)] * 3,
        compiler_params=_cparams("parallel", "parallel"),
    )(rec, gate, conv_w, conv_b.reshape(1, W), w_a.astype(BF16), b_a.reshape(2, 1, W),
      w_i.astype(BF16), b_i.reshape(2, 1, W), lam.reshape(2, 1, W))


def _rglru_mixer(hn, h, B, Tp, w_in, conv_w, conv_b, w_a, b_a, w_i, b_i, lam, w_out, nw):
    W = D_MODEL
    w_in = w_in.astype(BF16)
    gate = _matmul(hn, w_in[:, :W], BF16, act="gelu")
    rec = _matmul(hn, w_in[:, W:], F32)
    y = _rg_scan(rec.reshape(B, Tp, W), gate.reshape(B, Tp, W), conv_w, conv_b, w_a, b_a, w_i, b_i, lam)
    return _proj_res(y.reshape(B * Tp, W), w_out.astype(BF16), h, nw)


def _na_kernel(q_ref, k_ref, v_ref, bias_ref, mb_ref, o_ref, *, rows):
    scale = NA_HEAD_DIM ** -0.5
    kh = NA_WIN_R
    mb = mb_ref[0]
    km = k_ref[0, PAD:HEAD_ROWS, :]
    vm = v_ref[0, PAD:HEAD_ROWS, :]

    o_ref[0, 0:PAD, :] = jnp.zeros((PAD, NA_HEAD_DIM), o_ref.dtype)
    qm = q_ref[0, PAD:HEAD_ROWS, :]
    s_m = _dot_nt(qm, km) * scale + mb
    p_m = jnp.exp(s_m - jnp.max(s_m, axis=-1, keepdims=True))
    o_m = _dot(p_m.astype(BF16), vm) / jnp.sum(p_m, axis=-1, keepdims=True)
    o_ref[0, PAD:HEAD_ROWS, :] = o_m.astype(o_ref.dtype)

    group = np.gcd(rows, NA_ROW_GROUP)
    R = range(group)

    def group_body(i, _):
        r = [i * group + c for c in R]
        rs = [jnp.clip(r[c] - NA_WIN_R // 2, 0, rows - kh) for c in R]
        q0 = [pl.multiple_of(HEAD_ROWS + r[c] * GRID_W, GRID_W) for c in R]
        k0 = [pl.multiple_of(HEAD_ROWS + rs[c] * GRID_W, GRID_W) for c in R]
        q_r = [q_ref[0, pl.ds(q0[c], GRID_W), :] for c in R]
        s = [_dot_nt(q_r[c], k_ref[0, pl.ds(k0[c], kh * GRID_W), :]) for c in R]
        s_met = [_dot_nt(q_r[c], km) for c in R]
        p, p_met, l = [], [], []
        for c in R:
            sc = s[c] * scale + bias_ref[0, r[c] - rs[c]]
            sm = s_met[c] * scale + mb
            m = jnp.maximum(jnp.max(sc, axis=-1, keepdims=True), jnp.max(sm, axis=-1, keepdims=True))
            pc = jnp.exp(sc - m)
            pm = jnp.exp(sm - m)
            l.append(jnp.sum(pc, axis=-1, keepdims=True) + jnp.sum(pm, axis=-1, keepdims=True))
            p.append(pc.astype(BF16))
            p_met.append(pm.astype(BF16))
        o = [_dot(p[c], v_ref[0, pl.ds(k0[c], kh * GRID_W), :]) + _dot(p_met[c], vm) for c in R]
        for c in R:
            o_ref[0, pl.ds(q0[c], GRID_W), :] = (o[c] / l[c]).astype(o_ref.dtype)
        return 0

    lax.fori_loop(0, rows // group, group_body, 0)


def _na_bias_table(rpb):
    kh = NA_WIN_R
    cols = np.arange(GRID_W)
    col_start = np.clip(cols - NA_WIN_C // 2, 0, GRID_W - NA_WIN_C)
    valid = (cols[None, :] >= col_start[:, None]) & (cols[None, :] < col_start[:, None] + NA_WIN_C)
    col_off = np.clip(cols[None, :] - cols[:, None], -(NA_WIN_C - 1), NA_WIN_C - 1) + NA_WIN_C - 1
    H, n_ro, n_co = rpb.shape
    onehot = (col_off[None] == np.arange(n_co)[:, None, None]).astype(np.float32)
    a = jnp.einsum('hrc,cqk->hqrk', rpb.astype(F32), jnp.asarray(onehot), precision=lax.Precision.HIGHEST)
    a = jnp.where(jnp.asarray(valid)[None, :, None, :], a, NEG_INF)
    tabs = [a[:, :, NA_WIN_R - 1 - d:NA_WIN_R - 1 - d + kh, :].reshape(H, GRID_W, kh * GRID_W)
            for d in range(NA_WIN_R)]
    return jnp.stack(tabs, axis=1)


def _na_attention(qkv, bias, meta_bias):
    B, Tp, _ = qkv.shape
    rows = (Tp - HEAD_ROWS) // GRID_W
    assert rows >= NA_WIN_R
    H = NA_HEADS
    return pl.pallas_call(
        functools.partial(_na_kernel, rows=rows),
        grid=(B, H),
        in_specs=[pl.BlockSpec((1, Tp, NA_HEAD_DIM), lambda b, h: (b, 0, h)),
                  pl.BlockSpec((1, Tp, NA_HEAD_DIM), lambda b, h: (b, 0, H + h)),
                  pl.BlockSpec((1, Tp, NA_HEAD_DIM), lambda b, h: (b, 0, 2 * H + h)),
                  pl.BlockSpec((1, NA_WIN_R, GRID_W, NA_WIN_R * GRID_W), lambda b, h: (h, 0, 0, 0)),
                  pl.BlockSpec((1, 1, N_META), lambda b, h: (h, 0, 0))],
        out_specs=pl.BlockSpec((1, Tp, NA_HEAD_DIM), lambda b, h: (b, 0, h)),
        out_shape=jax.ShapeDtypeStruct((B, Tp, H * NA_HEAD_DIM), BF16),
        compiler_params=_cparams("parallel", "parallel"),
    )(qkv, qkv, qkv, bias, meta_bias.astype(F32).reshape(H, 1, N_META))


def _na_mixer(hn, h, B, Tp, w_qkv, bias, meta_bias, w_o, nw):
    qkv = _matmul(hn, w_qkv.astype(BF16), BF16)
    o = _na_attention(qkv.reshape(B, Tp, -1), bias, meta_bias)
    return _proj_res(o.reshape(B * Tp, -1), w_o.astype(BF16), h, nw)


def _gdn_gate_kernel(x_ref, w_ref, al_ref, dt_ref, o_ref):
    y = _dot(x_ref[...], w_ref[...])
    lane = lax.broadcasted_iota(jnp.int32, y.shape, 1)
    g = -jnp.exp(al_ref[...]) * _softplus(y + dt_ref[...])
    o_ref[...] = jnp.where(lane < 2 * GDN_V_HEADS, _sigmoid(y), g)


def _gdn_gates(hn, w_ba, a_log, dt_bias):
    M, K = hn.shape
    N = 4 * GDN_V_HEADS
    tm = _pick_tile(M, 1024, 128)
    zeros = jnp.zeros((2 * GDN_V_HEADS,), F32)
    al = jnp.concatenate([zeros, a_log.astype(F32).reshape(-1)]).reshape(1, N)
    dt = jnp.concatenate([zeros, dt_bias.astype(F32).reshape(-1)]).reshape(1, N)
    return pl.pallas_call(
        _gdn_gate_kernel,
        grid=(M // tm,),
        in_specs=[pl.BlockSpec((tm, K), lambda i: (i, 0)), pl.BlockSpec((K, N), lambda i: (0, 0)),
                  pl.BlockSpec((1, N), lambda i: (0, 0)), pl.BlockSpec((1, N), lambda i: (0, 0))],
        out_specs=pl.BlockSpec((tm, N), lambda i: (i, 0)),
        out_shape=jax.ShapeDtypeStruct((M, N), F32),
        compiler_params=_cparams("parallel"),
    )(hn, w_ba, al, dt)


def _split3_dot(x, m):
    hi = x.astype(BF16)
    r1 = x - hi.astype(F32)
    mid = r1.astype(BF16)
    lo = (r1 - mid.astype(F32)).astype(BF16)
    return _dot(hi, m) + _dot(mid, m) + _dot(lo, m)


def _gdn_kernel(q_ref, k_ref, v_ref, z_ref, bgc_ref, bgr_ref, nw_ref, o_ref,
                S_s, oacc_s, u_s, wq_s, kst_s, qkd_s, gam_s, *, Tp, HPS):
    hb = pl.program_id(1)
    C = CHUNK
    NC = Tp // C
    HV = GDN_V_HEADS
    DK = GDN_DK
    G = 4
    W = G * C
    P1_GROUP = 10 if HPS == 1 else 5
    scale = DK ** -0.5

    oacc_s[...] = jnp.zeros_like(oacc_s)
    S_s[...] = jnp.zeros_like(S_s)

    ri = lax.broadcasted_iota(jnp.int32, (C, W), 0)
    cl = lax.broadcasted_iota(jnp.int32, (C, W), 1)
    cj = jnp.bitwise_and(cl, C - 1)
    blk = jnp.right_shift(cl, 6)
    dd = jnp.where(blk >= 2, cj - ri, ri - cj)
    incl4 = dd >= 0
    strict4 = dd > 0
    eye4 = (ri == cj).astype(F32)
    cum4 = jnp.where(dd <= 0, 1.0, 0.0).astype(BF16)
    bdm = jnp.where(jnp.right_shift(lax.broadcasted_iota(jnp.int32, (W, W), 0), 6)
                    == jnp.right_shift(lax.broadcasted_iota(jnp.int32, (W, W), 1), 6), 1.0, 0.0).astype(BF16)
    blk_row = jnp.right_shift(lax.broadcasted_iota(jnp.int32, (1, W), 1), 6)
    lane = lax.broadcasted_iota(jnp.int32, (C, 4 * HV), 1)
    rowi = lax.broadcasted_iota(jnp.int32, (C, 4 * HV), 0)

    def col(x, c):
        return jnp.sum(jnp.where(lane == c, x, 0.0), axis=1, keepdims=True)

    def by_block(parts, b):
        return jnp.where(b == 0, parts[0], jnp.where(b == 1, parts[1], jnp.where(b == 2, parts[2], parts[3])))

    def blockdiag(xb):
        return jnp.concatenate([xb] * G, axis=0) * bdm

    def phase1(ns):
        items = [(hd, n) for n in ns for hd in range(HPS)]
        K = range(len(items))
        r0 = [pl.multiple_of(n * C, C) for (hd, n) in items]
        q = [q_ref[0, pl.ds(r0[i], C), items[i][0] * DK:(items[i][0] + 1) * DK] for i in K]
        k = [k_ref[0, pl.ds(r0[i], C), items[i][0] * DK:(items[i][0] + 1) * DK] for i in K]
        bg = [bgc_ref[0, pl.ds(r0[i], C), :] for i in K]
        v = [v_ref[0, pl.ds(r0[i], C), items[i][0] * 2 * DK:(items[i][0] + 1) * 2 * DK].astype(F32) for i in K]
        rows = [bgr_ref[0, n, hd] for (hd, n) in items]
        cum_rows = [_split3_dot(rows[i], cum4) for i in K]
        k4 = [jnp.concatenate([k[i]] * G, axis=0) for i in K]
        kk4 = [_dot_nt(k[i], k4[i]) for i in K]
        qk4 = [_dot_nt(q[i], k4[i]) * scale for i in K]
        beta_c, gc_c, decay4, L = [], [], [], []
        for i in K:
            pre = bg[i]
            for s in (1, 2, 4, 8, 16, 32):
                pre = pre + jnp.where(rowi >= s, pltpu.roll(pre, s, 0), 0.0)
            suf = pre[C - 1:C] - pre + bg[i]
            gc_r = by_block([cum_rows[i][4 + g:5 + g] for g in range(G)], blk_row)
            bc, gc = [], []
            for g in range(G):
                d, j = g // 2, g % 2
                vh = 2 * (hb * HPS + items[i][0]) + j
                bc.append(col(bg[i], d * HV + vh))
                gc.append(col(suf if d else pre, 2 * HV + d * HV + vh))
            beta_c.append(bc)
            gc_c.append(gc)
            dec = jnp.where(incl4, jnp.exp(jnp.where(incl4, by_block(gc, blk) - gc_r, 0.0)), 0.0)
            decay4.append(dec)
            L.append(jnp.where(strict4, kk4[i] * dec, 0.0) * by_block(bc, blk))
        Tm = [eye4 - L[i] for i in K]
        Lb = [L[i].astype(BF16) for i in K]
        P = [_dot(Lb[i], blockdiag(Lb[i])) for i in K]
        for lvl in range(5):
            Pb = [P[i].astype(BF16) for i in K]
            bd = [blockdiag(Pb[i]) for i in K]
            if lvl < 4:
                tp = [_dot(jnp.concatenate([Tm[i].astype(BF16), Pb[i]], axis=0), bd[i]) for i in K]
                Tm = [Tm[i] + tp[i][:C] for i in K]
                P = [tp[i][C:] for i in K]
            else:
                tp = [_dot(Tm[i].astype(BF16), bd[i]) for i in K]
                Tm = [Tm[i] + tp[i] for i in K]
        rhs, kst, qst, gam = [], [], [], []
        for i in K:
            qf = q[i].astype(F32)
            kf = k[i].astype(F32)
            rhs_i, kst_i, qst_i, gam_i = [], [], [], []
            for g in range(G):
                j = g % 2
                bc, gc = beta_c[i][g], gc_c[i][g]
                e_c = jnp.exp(gc)
                rhs_i.append(jnp.concatenate([v[i][:, j * DK:(j + 1) * DK] * bc, kf * (bc * e_c)], axis=1))
                g_last = gc[0:1] if g >= 2 else gc[C - 1:C]
                kst_i.append((kf * jnp.exp(g_last - gc)).astype(BF16))
                qst_i.append((qf * (scale * e_c)).astype(BF16))
                gam_i.append(jnp.broadcast_to(jnp.exp(g_last), (1, DK)))
            rhs.append(jnp.concatenate(rhs_i, axis=0).astype(BF16))
            kst.append(kst_i)
            qst.append(qst_i)
            gam.append(gam_i)
        sol = [_dot(blockdiag(Tm[i].astype(BF16)), rhs[i]) for i in K]
        for i in K:
            hd, n = items[i]
            u_s[hd, n] = sol[i][:, :DK]
            qkd_s[hd, n] = jnp.where(incl4, qk4[i] * decay4[i], 0.0).astype(BF16)
            for g in range(G):
                wq_s[hd, n, g, 0:C, :] = sol[i][g * C:(g + 1) * C, DK:].astype(BF16)
                wq_s[hd, n, g, C:2 * C, :] = qst[i][g]
                kst_s[hd, n, g] = kst[i][g]
                gam_s[hd, n, g:g + 1, :] = gam[i][g]

    def phase1_group(i, _):
        phase1([P1_GROUP * i + c for c in range(P1_GROUP)])
        return 0

    lax.fori_loop(0, NC // P1_GROUP, phase1_group, 0)
    if NC % P1_GROUP:
        phase1(list(range(NC - NC % P1_GROUP, NC)))

    nw = nw_ref[...]

    def gated_norm(o, z):
        ys = []
        for j in range(2 * HPS):
            oj = o[:, j * DK:(j + 1) * DK]
            zj = z[:, j * DK:(j + 1) * DK].astype(F32)
            ys.append(oj * lax.rsqrt(jnp.mean(oj * oj, axis=-1, keepdims=True) + NORM_EPS) * nw * (zj * _sigmoid(zj)))
        return jnp.concatenate(ys, axis=1)

    def finalize(n):
        r0 = pl.multiple_of(n * C, C)
        o_ref[0, pl.ds(r0, C), :] = gated_norm(oacc_s[pl.ds(r0, C), :], z_ref[0, pl.ds(r0, C), :]).astype(o_ref.dtype)

    def phase2(it, finalize_previous):
        if finalize_previous:
            finalize(it - 1)
            finalize(NC - it)
        ops = []
        for hd in range(HPS):
            for g in range(G):
                n = it if g < 2 else NC - 1 - it
                ops.append((S_s[hd * G + g], wq_s[hd, n, g], u_s[hd, n, g * C:(g + 1) * C, :],
                            qkd_s[hd, n, :, g * C:(g + 1) * C], gam_s[hd, n, g:g + 1, :], kst_s[hd, n, g]))
        NI = HPS * G
        r_f = pl.multiple_of(it * C, C)
        r_b = pl.multiple_of((NC - 1 - it) * C, C)
        o_f = oacc_s[pl.ds(r_f, C), :]
        ws = [_dot(wq, S.astype(BF16)) for (S, wq, u, qkd, gam, kst) in ops]
        vb = [(ops[g][2] - ws[g][:C]).astype(BF16) for g in range(NI)]
        outs = [ws[g][C:] + _dot(ops[g][3], vb[g]) for g in range(NI)]
        states = [ops[g][0] * ops[g][4] + _dot_tn(ops[g][5], vb[g]) for g in range(NI)]
        for g in range(NI):
            S_s[g] = states[g]
        fwd = [outs[hd * G + j] for hd in range(HPS) for j in range(2)]
        bwd = [outs[hd * G + 2 + j] for hd in range(HPS) for j in range(2)]
        o_f = o_f + jnp.concatenate(fwd, axis=1)
        oacc_s[pl.ds(r_f, C), :] = o_f
        oacc_s[pl.ds(r_b, C), :] = oacc_s[pl.ds(r_b, C), :] + jnp.concatenate(bwd, axis=1)
        return 0

    lax.fori_loop(0, NC // 2 + 1, lambda it, c: phase2(it, False), 0)
    lax.fori_loop(NC // 2 + 1, NC, lambda it, c: phase2(it, True), 0)
    finalize(NC - 1)
    finalize(0)


_GDN_CHUNK_BYTES = CHUNK * GDN_DK * (4 * 4 + 8 * 2 + 4 * 2 + 4 * 2) + 8 * GDN_DK * 4
_GDN_PAIR_BUDGET = 26 * 1024 * 1024


def _gdn_delta(qkv, z, bg, norm_w):
    B, Tp, _ = qkv.shape
    NC = Tp // CHUNK
    KH = GDN_QK_HEADS
    HV = GDN_V_HEADS
    C = CHUNK
    HPS = 2 if NC * _GDN_CHUNK_BYTES * 2 <= _GDN_PAIR_BUDGET else 1
    NH = KH // HPS
    bgr = bg.reshape(B, NC, CHUNK, 4, KH, 2).transpose(0, 1, 4, 3, 5, 2).reshape(B, NC, KH, 8, CHUNK)
    return pl.pallas_call(
        functools.partial(_gdn_kernel, Tp=Tp, HPS=HPS),
        grid=(B, NH),
        in_specs=[pl.BlockSpec((1, Tp, HPS * GDN_DK), lambda b, h: (b, 0, h)),
                  pl.BlockSpec((1, Tp, HPS * GDN_DK), lambda b, h: (b, 0, NH + h)),
                  pl.BlockSpec((1, Tp, 2 * HPS * GDN_DK), lambda b, h: (b, 0, NH + h)),
                  pl.BlockSpec((1, Tp, 2 * HPS * GDN_DK), lambda b, h: (b, 0, h)),
                  pl.BlockSpec((1, Tp, 4 * HV), lambda b, h: (b, 0, 0)),
                  pl.BlockSpec((1, NC, HPS, 8, CHUNK), lambda b, h: (b, 0, h, 0, 0)),
                  pl.BlockSpec((1, GDN_DK), lambda b, h: (0, 0))],
        out_specs=pl.BlockSpec((1, Tp, 2 * HPS * GDN_DK), lambda b, h: (b, 0, h)),
        out_shape=jax.ShapeDtypeStruct((B, Tp, GDN_VAL_DIM), BF16),
        scratch_shapes=[pltpu.VMEM((4 * HPS, GDN_DK, GDN_DK), F32),
                        pltpu.VMEM((Tp, 2 * HPS * GDN_DK), F32),
                        pltpu.VMEM((HPS, NC, 4 * C, GDN_DK), F32),
                        pltpu.VMEM((HPS, NC, 4, 2 * C, GDN_DK), BF16),
                        pltpu.VMEM((HPS, NC, 4, C, GDN_DK), BF16),
                        pltpu.VMEM((HPS, NC, C, 4 * C), BF16),
                        pltpu.VMEM((HPS, NC, 8, GDN_DK), F32)],
        compiler_params=_cparams("parallel", "parallel"),
    )(qkv, qkv, qkv, z, bg, bgr, norm_w.astype(F32).reshape(1, GDN_DK))


def _gdn_mixer(hn, h, B, Tp, w_in, conv_w, a_log, dt_bias, norm_w, w_out, nw):
    nz = GDN_CONV_DIM + GDN_VAL_DIM
    w_in = w_in.astype(BF16)
    qkv = _proj_conv(hn, w_in[:, :GDN_CONV_DIM], conv_w, Tp, 2 * GDN_KEY_DIM)
    z = _matmul(hn, w_in[:, GDN_CONV_DIM:nz], BF16)
    bg = _gdn_gates(hn, w_in[:, nz:], a_log, dt_bias)
    o = _gdn_delta(qkv.reshape(B, Tp, -1), z.reshape(B, Tp, -1), bg.reshape(B, Tp, -1), norm_w)
    return _proj_res(o.reshape(B * Tp, -1), w_out.astype(BF16), h, nw)


def _trunk(x, meta_tokens, mix_norm, ffn_norm, final_norm, rg, na, gdn, ffn):
    B, T, D = x.shape
    Tp = T + HEAD_ROWS
    depth = mix_norm.shape[0]
    h3 = jnp.concatenate([jnp.zeros((B, PAD, D), F32),
                          jnp.broadcast_to(meta_tokens.astype(F32)[None], (B, N_META, D)),
                          x.astype(F32)], axis=1)
    h = h3.reshape(B * Tp, D)
    hn = _rmsnorm(h, mix_norm[0], BF16)
    for i in range(depth):
        kind, j = i % 3, i // 3
        if kind == 0:
            h, hn = _rglru_mixer(hn, h, B, Tp, *[p[j] for p in rg], ffn_norm[i])
        elif kind == 1:
            h, hn = _na_mixer(hn, h, B, Tp, *[p[j] for p in na], ffn_norm[i])
        else:
            h, hn = _gdn_mixer(hn, h, B, Tp, *[p[j] for p in gdn], ffn_norm[i])
        w_gate, w_up, conv_w, conv_b, w_down = [p[i] for p in ffn]
        next_norm = mix_norm[i + 1] if i + 1 < depth else final_norm
        h, hn = _conv_ffn(hn, h, w_gate.astype(BF16), w_up.astype(BF16), conv_w, conv_b,
                          w_down.astype(BF16), next_norm)
    return _final_norm(h, final_norm, B, T)


def kernel(x_prompt, x_sample, meta_tokens, mix_norm, ffn_norm, final_norm, rg_w_in, rg_conv_w, rg_conv_b, rg_w_a, rg_b_a, rg_w_i, rg_b_i, rg_lam, rg_w_out, na_w_qkv, na_rpb, na_meta_bias, na_w_o, gdn_w_in, gdn_conv_w, gdn_a_log, gdn_dt_bias, gdn_norm_w, gdn_w_out, ffn_w_gate, ffn_w_up, ffn_conv_w, ffn_conv_b, ffn_w_down):
    rg = (rg_w_in, rg_conv_w, rg_conv_b, rg_w_a, rg_b_a, rg_w_i, rg_b_i, rg_lam, rg_w_out)
    na_bias = jnp.stack([_na_bias_table(na_rpb[j]) for j in range(na_rpb.shape[0])])
    na = (na_w_qkv, na_bias, na_meta_bias, na_w_o)
    gdn = (gdn_w_in, gdn_conv_w, gdn_a_log, gdn_dt_bias, gdn_norm_w, gdn_w_out)
    ffn = (ffn_w_gate, ffn_w_up, ffn_conv_w, ffn_conv_b, ffn_w_down)
    y_prompt = _trunk(x_prompt, meta_tokens, mix_norm, ffn_norm, final_norm, rg, na, gdn, ffn)
    y_sample = _trunk(x_sample, meta_tokens, mix_norm, ffn_norm, final_norm, rg, na, gdn, ffn)
    return (y_prompt, y_sample)
```

```python
import functools

import numpy as np
import jax
import jax.numpy as jnp
from jax import lax
from jax.experimental import pallas as pl
from jax.experimental.pallas import tpu as pltpu

D_MODEL = 2048
N_META = 16
PAD = 48
HEAD_ROWS = PAD + N_META
CHUNK = 64
GRID_W = 64
NORM_EPS = 1e-6

RG_BLOCK = 256
RG_C = 8.0

NA_HEAD_DIM = 128
NA_HEADS = D_MODEL // NA_HEAD_DIM
NA_WIN_R = 8
NA_WIN_C = 16
NA_ROW_GROUP = 32
NEG_INF = -1e30

GDN_DK = 128
GDN_QK_HEADS = D_MODEL // GDN_DK
GDN_V_HEADS = 2 * GDN_QK_HEADS
GDN_KEY_DIM = GDN_QK_HEADS * GDN_DK
GDN_VAL_DIM = GDN_V_HEADS * GDN_DK
GDN_CONV_DIM = 2 * GDN_KEY_DIM + GDN_VAL_DIM

V7X_VMEM_LIMIT = 56 * 1024 * 1024
HALO = 16

BF16 = jnp.bfloat16
F32 = jnp.float32


def _cparams(*sem):
    return pltpu.CompilerParams(dimension_semantics=sem, vmem_limit_bytes=V7X_VMEM_LIMIT)


def _pick_tile(n, cap, mult):
    best = None
    for t in range(mult, cap + 1, mult):
        if n % t == 0:
            best = t
    assert best is not None, (n, cap, mult)
    return best


def _sigmoid(x):
    return 1.0 / (1.0 + jnp.exp(-x))


def _softplus(x):
    return jnp.maximum(x, 0.0) + jnp.log(1.0 + jnp.exp(-jnp.abs(x)))


def _gelu_tanh(x):
    return 0.5 * x * (1.0 + jnp.tanh(0.7978845608028654 * (x + 0.044715 * x * x * x)))


def _rms_scale(x, w):
    ms = jnp.mean(x * x, axis=-1, keepdims=True)
    return x * lax.rsqrt(ms + NORM_EPS) * w


def _dot(a, b):
    return jnp.dot(a, b, preferred_element_type=F32)


def _dot_nt(a, b):
    return lax.dot_general(a, b, (((1,), (1,)), ((), ())), preferred_element_type=F32)


def _dot_tn(a, b):
    return lax.dot_general(a, b, (((0,), (0,)), ((), ())), preferred_element_type=F32)


def _rmsnorm_kernel(x_ref, w_ref, o_ref):
    o_ref[...] = _rms_scale(x_ref[...], w_ref[...]).astype(o_ref.dtype)


def _rmsnorm(h, w, out_dtype):
    M, D = h.shape
    tm = _pick_tile(M, 1024, 64)
    return pl.pallas_call(
        _rmsnorm_kernel,
        grid=(M // tm,),
        in_specs=[pl.BlockSpec((tm, D), lambda i: (i, 0)), pl.BlockSpec((1, D), lambda i: (0, 0))],
        out_specs=pl.BlockSpec((tm, D), lambda i: (i, 0)),
        out_shape=jax.ShapeDtypeStruct((M, D), out_dtype),
        compiler_params=_cparams("parallel"),
    )(h, w.reshape(1, D))


def _final_norm_kernel(x_ref, w_ref, o_ref):
    o_ref[0] = _rms_scale(x_ref[...], w_ref[...])


def _final_norm(h, w, B, T):
    D = h.shape[1]
    Tp = T + HEAD_ROWS
    tt = 512
    assert T % tt == 0
    return pl.pallas_call(
        _final_norm_kernel,
        grid=(B, T // tt),
        in_specs=[pl.BlockSpec((pl.Element(tt), pl.Element(D)),
                               lambda b, j: (pl.multiple_of(b * Tp + HEAD_ROWS + j * tt, CHUNK), 0)),
                  pl.BlockSpec((1, D), lambda b, j: (0, 0))],
        out_specs=pl.BlockSpec((1, tt, D), lambda b, j: (b, j, 0)),
        out_shape=jax.ShapeDtypeStruct((B, T, D), F32),
        compiler_params=_cparams("parallel", "parallel"),
    )(h, w.reshape(1, D))


def _matmul_kernel(x_ref, w_ref, o_ref, *, act):
    acc = _dot(x_ref[...], w_ref[...])
    if act == "gelu":
        acc = _gelu_tanh(acc)
    o_ref[...] = acc.astype(o_ref.dtype)


def _matmul(x, w, out_dtype, act=None):
    M, K = x.shape
    N = w.shape[1]
    tm = _pick_tile(M, 1280, 128)
    tn = _pick_tile(N, 1024, 128)
    return pl.pallas_call(
        functools.partial(_matmul_kernel, act=act),
        grid=(M // tm, N // tn),
        in_specs=[pl.BlockSpec((tm, K), lambda i, j: (i, 0)), pl.BlockSpec((K, tn), lambda i, j: (0, j))],
        out_specs=pl.BlockSpec((tm, tn), lambda i, j: (i, j)),
        out_shape=jax.ShapeDtypeStruct((M, N), out_dtype),
        compiler_params=_cparams("parallel", "parallel"),
    )(x, w)


def _proj_res_kernel(x_ref, w_ref, h_ref, nw_ref, hout_ref, hn_ref):
    hnew = h_ref[...] + _dot(x_ref[...], w_ref[...])
    hout_ref[...] = hnew
    hn_ref[...] = _rms_scale(hnew, nw_ref[...]).astype(hn_ref.dtype)


def _proj_res(x, w, h, nw):
    M, K = x.shape
    D = w.shape[1]
    tm = _pick_tile(M, 704 if K <= D_MODEL else 512, 64)
    return pl.pallas_call(
        _proj_res_kernel,
        grid=(M // tm,),
        in_specs=[pl.BlockSpec((tm, K), lambda i: (i, 0)),
                  pl.BlockSpec((K, D), lambda i: (0, 0), pipeline_mode=pl.Buffered(1)),
                  pl.BlockSpec((tm, D), lambda i: (i, 0)),
                  pl.BlockSpec((1, D), lambda i: (0, 0))],
        out_specs=[pl.BlockSpec((tm, D), lambda i: (i, 0)), pl.BlockSpec((tm, D), lambda i: (i, 0))],
        out_shape=[jax.ShapeDtypeStruct((M, D), F32), jax.ShapeDtypeStruct((M, D), BF16)],
        compiler_params=_cparams("parallel"),
    )(x, w, h, nw.reshape(1, D))


def _proj_conv_kernel(xp_ref, xm_ref, xn_ref, w_ref, cw_ref, o_ref, xext_ref, *, tm, tc, ni, Tp, n_norm):
    i = pl.program_id(0)
    j = pl.program_id(1)
    npc = o_ref.shape[1] // tc

    @pl.when(j == 0)
    def _():
        xext_ref[0:HALO, :] = xp_ref[...]
        xext_ref[HALO:HALO + tm, :] = xm_ref[...]
        xext_ref[HALO + tm:2 * HALO + tm, :] = xn_ref[...]

    @pl.when(jnp.logical_and(j == 0, i == ni - 1))
    def _():
        xext_ref[HALO + tm:2 * HALO + tm, :] = jnp.zeros((HALO, xext_ref.shape[1]), xext_ref.dtype)

    rel = i * tm - (i * tm // Tp) * Tp + lax.broadcasted_iota(jnp.int32, (tm, 1), 0)
    live = jnp.where(rel >= Tp, rel - Tp, rel) >= PAD
    cw = cw_ref[...]
    xe = xext_ref[...]

    def piece_matmul(c):
        return _dot(xe, w_ref[:, c * tc:(c + 1) * tc])

    def piece_finish(c, g):
        cwc = cw[:, c * tc:(c + 1) * tc]
        a = (cwc[0:1] * g[HALO - 1:HALO - 1 + tm] + cwc[1:2] * g[HALO:HALO + tm]
             + cwc[2:3] * g[HALO + 1:HALO + 1 + tm] + cwc[3:4] * g[HALO + 2:HALO + 2 + tm])
        y = jnp.where(live, a * _sigmoid(a), 0.0)
        normalise = j * npc + c < n_norm
        heads = []
        for hh in range(tc // GDN_DK):
            yh = y[:, hh * GDN_DK:(hh + 1) * GDN_DK]
            nrm = lax.rsqrt(jnp.sum(yh * yh, axis=-1, keepdims=True) + 1e-6)
            heads.append(yh * jnp.where(normalise, nrm, 1.0))
        o_ref[:, c * tc:(c + 1) * tc] = jnp.concatenate(heads, axis=1).astype(o_ref.dtype)

    g = piece_matmul(0)
    for c in range(npc):
        g_next = piece_matmul(c + 1) if c + 1 < npc else None
        piece_finish(c, g)
        g = g_next


def _proj_conv(x, w, conv_w, Tp, n_norm_cols):
    M, K = x.shape
    N = w.shape[1]
    tm = _pick_tile(M, 1024, 64)
    assert tm <= Tp
    tn, tc = 2048, 512
    ni = M // tm
    hb = tm // HALO
    nhb = M // HALO
    return pl.pallas_call(
        functools.partial(_proj_conv_kernel, tm=tm, tc=tc, ni=ni, Tp=Tp, n_norm=n_norm_cols // tc),
        grid=(ni, N // tn),
        in_specs=[pl.BlockSpec((HALO, K), lambda i, j: (jnp.maximum(i * hb - 1, 0), 0)),
                  pl.BlockSpec((tm, K), lambda i, j: (i, 0)),
                  pl.BlockSpec((HALO, K), lambda i, j: (jnp.minimum((i + 1) * hb, nhb - 1), 0)),
                  pl.BlockSpec((K, tn), lambda i, j: (0, j)),
                  pl.BlockSpec((4, tn), lambda i, j: (0, j))],
        out_specs=pl.BlockSpec((tm, tn), lambda i, j: (i, j)),
        out_shape=jax.ShapeDtypeStruct((M, N), BF16),
        scratch_shapes=[pltpu.VMEM((tm + 2 * HALO, K), BF16)],
        compiler_params=_cparams("parallel", "arbitrary"),
    )(x, x, x, w, conv_w)


def _ffn_kernel(xp_ref, xm_ref, xn_ref, wg_ref, wu_ref, cw_ref, cb_ref, wd_ref, h_ref, nw_ref,
                hout_ref, hn_ref, xext_ref, *, tm, nf, ni):
    i = pl.program_id(0)
    f = pl.program_id(1)
    th = tm // 2

    @pl.when(f == 0)
    def _():
        xext_ref[0:HALO, :] = xp_ref[...]
        xext_ref[HALO:HALO + tm, :] = xm_ref[...]
        xext_ref[HALO + tm:2 * HALO + tm, :] = xn_ref[...]
        hout_ref[...] = h_ref[...]

    cw = cw_ref[...]
    cb = cb_ref[...]
    g, up = [], []
    for s in range(2):
        g.append(_dot(xext_ref[s * th:s * th + th + 2 * HALO, :], wg_ref[...]))
        up.append(_dot(xm_ref[s * th:(s + 1) * th, :], wu_ref[...]))
    row = lax.broadcasted_iota(jnp.int32, (th, 1), 0)
    y = []
    for s in range(2):
        g_prev = g[s][HALO - 1:HALO - 1 + th]
        g_here = g[s][HALO:HALO + th]
        g_next = g[s][HALO + 1:HALO + 1 + th]
        if s == 1:
            g_next = jnp.where(jnp.logical_and(i == ni - 1, row == th - 1), 0.0, g_next)
        a = cw[0:1] * g_prev + cw[1:2] * g_here + cw[2:3] * g_next + cb
        y.append((a * _sigmoid(a) * up[s]).astype(BF16))
    down = [_dot(y[s], wd_ref[...]) for s in range(2)]
    for s in range(2):
        hout_ref[s * th:(s + 1) * th, :] += down[s]

    @pl.when(f == nf - 1)
    def _():
        hn_ref[...] = _rms_scale(hout_ref[...], nw_ref[...]).astype(hn_ref.dtype)


def _conv_ffn(hn, h, w_gate, w_up, conv_w, conv_b, w_down, nw):
    M, D = hn.shape
    F = w_gate.shape[1]
    tm = _pick_tile(M, 704, 64)
    tf = _pick_tile(F, 512, 128)
    ni, nf = M // tm, F // tf
    hb = tm // HALO
    nhb = M // HALO
    return pl.pallas_call(
        functools.partial(_ffn_kernel, tm=tm, nf=nf, ni=ni),
        grid=(ni, nf),
        in_specs=[pl.BlockSpec((HALO, D), lambda i, f: (jnp.maximum(i * hb - 1, 0), 0)),
                  pl.BlockSpec((tm, D), lambda i, f: (i, 0)),
                  pl.BlockSpec((HALO, D), lambda i, f: (jnp.minimum((i + 1) * hb, nhb - 1), 0)),
                  pl.BlockSpec((D, tf), lambda i, f: (0, f)),
                  pl.BlockSpec((D, tf), lambda i, f: (0, f)),
                  pl.BlockSpec((3, tf), lambda i, f: (0, f)),
                  pl.BlockSpec((1, tf), lambda i, f: (0, f)),
                  pl.BlockSpec((tf, D), lambda i, f: (f, 0)),
                  pl.BlockSpec((tm, D), lambda i, f: (i, 0)),
                  pl.BlockSpec((1, D), lambda i, f: (0, 0))],
        out_specs=[pl.BlockSpec((tm, D), lambda i, f: (i, 0)), pl.BlockSpec((tm, D), lambda i, f: (i, 0))],
        out_shape=[jax.ShapeDtypeStruct((M, D), F32), jax.ShapeDtypeStruct((M, D), BF16)],
        scratch_shapes=[pltpu.VMEM((tm + 2 * HALO, D), BF16)],
        compiler_params=_cparams("parallel", "arbitrary"),
    )(hn, hn, hn, w_gate, w_up, conv_w, conv_b.reshape(1, F), w_down, h, nw.reshape(1, D))


def _scan8(a, u, row, reverse):
    for s in (1, 2, 4):
        if reverse:
            a_sh = pltpu.roll(a, 8 - s, 0)
            u_sh = pltpu.roll(u, 8 - s, 0)
            m = row < 8 - s
        else:
            a_sh = pltpu.roll(a, s, 0)
            u_sh = pltpu.roll(u, s, 0)
            m = row >= s
        u = jnp.where(m, a * u_sh + u, u)
        a = jnp.where(m, a * a_sh, a)
    return a, u


def _rg_kernel(rec_ref, gate_ref, cw_ref, cb_ref, wa_ref, ba_ref, wi_ref, bi_ref, lam_ref, o_ref,
               xc_s, hf_s, hb_s, *, Tp, rc):
    nch = Tp // rc
    W = RG_BLOCK
    T8 = rc // 8
    cw = cw_ref[...]
    cb = cb_ref[...]
    row8 = lax.broadcasted_iota(jnp.int32, (8, W), 0)

    def rows(c):
        return pl.multiple_of(c * rc, CHUNK)

    def conv(r0):
        top = rec_ref[0, pl.ds(pl.multiple_of(jnp.maximum(r0 - 8, 0), 8), 8), :]
        main = rec_ref[0, pl.ds(r0, rc), :]
        bot = rec_ref[0, pl.ds(pl.multiple_of(jnp.minimum(r0 + rc, Tp - 8), 8), 8), :]
        bot = jnp.where(r0 + rc < Tp, bot, 0.0)
        xw = jnp.concatenate([top, main, bot], axis=0)
        xc = cw[0:1] * xw[7:7 + rc] + cw[1:2] * xw[8:8 + rc] + cw[2:3] * xw[9:9 + rc] + cw[3:4] * xw[10:10 + rc] + cb
        t = r0 + lax.broadcasted_iota(jnp.int32, (rc, 1), 0)
        return jnp.where(t >= PAD, xc, 0.0)

    def advance(x_f, x_b, carry):
        xs = (x_f, x_b)
        xb16 = [x.astype(BF16) for x in xs]
        pre = [(_dot(xb16[d], wa_ref[d, 0]), _dot(xb16[d], wi_ref[d, 0])) for d in range(2)]
        au = []
        for d in range(2):
            r = _sigmoid(pre[d][0] + ba_ref[d])
            ig = _sigmoid(pre[d][1] + bi_ref[d])
            a = jnp.exp((-RG_C) * r * _softplus(-lam_ref[d]))
            au.append((a, jnp.sqrt(1.0 - a * a) * (ig * xs[d])))
        c_f, c_b = carry
        h_f, h_b = [None] * T8, [None] * T8
        for k in range(T8):
            A, U = _scan8(au[0][0][8 * k:8 * k + 8], au[0][1][8 * k:8 * k + 8], row8, False)
            h_f[k] = A * c_f + U
            c_f = h_f[k][7:8]
            kb = T8 - 1 - k
            A, U = _scan8(au[1][0][8 * kb:8 * kb + 8], au[1][1][8 * kb:8 * kb + 8], row8, True)
            h_b[kb] = A * c_b + U
            c_b = h_b[kb][0:1]
        return jnp.concatenate(h_f, axis=0), jnp.concatenate(h_b, axis=0), (c_f, c_b)

    def emit(r0, hsum):
        o_ref[0, pl.ds(r0, rc), :] = (gate_ref[0, pl.ds(r0, rc), :].astype(F32) * hsum).astype(o_ref.dtype)

    def first_half(c, carry):
        r_f, r_b = rows(c), rows(nch - 1 - c)
        x_f, x_b = conv(r_f), conv(r_b)
        h_f, h_b, carry = advance(x_f, x_b, carry)
        xc_s[pl.ds(r_f, rc), :] = x_f
        xc_s[pl.ds(r_b, rc), :] = x_b
        hf_s[pl.ds(r_f, rc), :] = h_f
        hb_s[pl.ds(r_b, rc), :] = h_b
        return carry

    carry = lax.fori_loop(0, nch // 2, first_half, (jnp.zeros((1, W), F32), jnp.zeros((1, W), F32)))

    if nch % 2:
        r_m = (nch // 2) * rc
        x_m = conv(r_m)
        h_f, h_b, carry = advance(x_m, x_m, carry)
        emit(r_m, h_f + h_b)

    def second_half(c, carry):
        r_f, r_b = rows(c), rows(nch - 1 - c)
        h_f, h_b, carry = advance(xc_s[pl.ds(r_f, rc), :], xc_s[pl.ds(r_b, rc), :], carry)
        emit(r_f, h_f + hb_s[pl.ds(r_f, rc), :])
        emit(r_b, hf_s[pl.ds(r_b, rc), :] + h_b)
        return carry

    lax.fori_loop((nch + 1) // 2, nch, second_half, carry)


def _rg_scan(rec, gate, conv_w, conv_b, w_a, b_a, w_i, b_i, lam):
    B, Tp, W = rec.shape
    nb = W // RG_BLOCK
    rc = _pick_tile(Tp, 384, 64)
    blk = lambda b, n: (b, 0, n)
    vec = lambda b, n: (0, n)
    vec3 = lambda b, n: (0, 0, n)
    return pl.pallas_call(
        functools.partial(_rg_kernel, Tp=Tp, rc=rc),
        grid=(B, nb),
        in_specs=[pl.BlockSpec((1, Tp, RG_BLOCK), blk),
                  pl.BlockSpec((1, Tp, RG_BLOCK), blk),
                  pl.BlockSpec((4, RG_BLOCK), vec),
                  pl.BlockSpec((1, RG_BLOCK), vec),
                  pl.BlockSpec((2, 1, RG_BLOCK, RG_BLOCK), lambda b, n: (0, n, 0, 0)),
                  pl.BlockSpec((2, 1, RG_BLOCK), vec3),
                  pl.BlockSpec((2, 1, RG_BLOCK, RG_BLOCK), lambda b, n: (0, n, 0, 0)),
                  pl.BlockSpec((2, 1, RG_BLOCK), vec3),
                  pl.BlockSpec((2, 1, RG_BLOCK), vec3)],
        out_specs=pl.BlockSpec((1, Tp, RG_BLOCK), blk),
        out_shape=jax.ShapeDtypeStruct((B, Tp, W), BF16),
        scratch_shapes=[pltpu.VMEM((Tp, RG_BLOCK), F32)] * 3,
        compiler_params=_cparams("parallel", "parallel"),
    )(rec, gate, conv_w, conv_b.reshape(1, W), w_a.astype(BF16), b_a.reshape(2, 1, W),
      w_i.astype(BF16), b_i.reshape(2, 1, W), lam.reshape(2, 1, W))


def _rglru_mixer(hn, h, B, Tp, w_in, conv_w, conv_b, w_a, b_a, w_i, b_i, lam, w_out, nw):
    W = D_MODEL
    w_in = w_in.astype(BF16)
    gate = _matmul(hn, w_in[:, :W], BF16, act="gelu")
    rec = _matmul(hn, w_in[:, W:], F32)
    y = _rg_scan(rec.reshape(B, Tp, W), gate.reshape(B, Tp, W), conv_w, conv_b, w_a, b_a, w_i, b_i, lam)
    return _proj_res(y.reshape(B * Tp, W), w_out.astype(BF16), h, nw)


def _na_kernel(q_ref, k_ref, v_ref, bias_ref, mb_ref, o_ref, *, rows):
    scale = NA_HEAD_DIM ** -0.5
    kh = NA_WIN_R
    mb = mb_ref[0]
    km = k_ref[0, PAD:HEAD_ROWS, :]
    vm = v_ref[0, PAD:HEAD_ROWS, :]

    o_ref[0, 0:PAD, :] = jnp.zeros((PAD, NA_HEAD_DIM), o_ref.dtype)
    qm = q_ref[0, PAD:HEAD_ROWS, :]

    group = np.gcd(rows, NA_ROW_GROUP)
    R = range(group)

    def group_body(i, _):
        r = [i * group + c for c in R]
        rs = [jnp.clip(r[c] - NA_WIN_R // 2, 0, rows - kh) for c in R]
        q0 = [pl.multiple_of(HEAD_ROWS + r[c] * GRID_W, GRID_W) for c in R]
        k0 = [pl.multiple_of(HEAD_ROWS + rs[c] * GRID_W, GRID_W) for c in R]
        q_r = [q_ref[0, pl.ds(q0[c], GRID_W), :] for c in R]
        s = [_dot_nt(q_r[c], k_ref[0, pl.ds(k0[c], kh * GRID_W), :]) for c in R]
        s_met = [_dot_nt(q_r[c], km) for c in R]
        s_m = _dot_nt(qm, km) * scale + mb
        p_m = jnp.exp(s_m - jnp.max(s_m, axis=-1, keepdims=True))
        p, p_met, l = [], [], []
        for c in R:
            sc = s[c] * scale + bias_ref[0, r[c] - rs[c]]
            sm = s_met[c] * scale + mb
            m = jnp.maximum(jnp.max(sc, axis=-1, keepdims=True), jnp.max(sm, axis=-1, keepdims=True))
            pc = jnp.exp(sc - m)
            pm = jnp.exp(sm - m)
            l.append(jnp.sum(pc, axis=-1, keepdims=True) + jnp.sum(pm, axis=-1, keepdims=True))
            p.append(pc.astype(BF16))
            p_met.append(pm.astype(BF16))
        o = [_dot(p[c], v_ref[0, pl.ds(k0[c], kh * GRID_W), :]) + _dot(p_met[c], vm) for c in R]
        o_m = _dot(p_m.astype(BF16), vm) / jnp.sum(p_m, axis=-1, keepdims=True)
        o_ref[0, PAD:HEAD_ROWS, :] = o_m.astype(o_ref.dtype)
        for c in R:
            o_ref[0, pl.ds(q0[c], GRID_W), :] = (o[c] / l[c]).astype(o_ref.dtype)
        return 0

    lax.fori_loop(0, rows // group, group_body, 0)


def _na_bias_table(rpb):
    kh = NA_WIN_R
    cols = np.arange(GRID_W)
    col_start = np.clip(cols - NA_WIN_C // 2, 0, GRID_W - NA_WIN_C)
    valid = (cols[None, :] >= col_start[:, None]) & (cols[None, :] < col_start[:, None] + NA_WIN_C)
    col_off = np.clip(cols[None, :] - cols[:, None], -(NA_WIN_C - 1), NA_WIN_C - 1) + NA_WIN_C - 1
    H, n_ro, n_co = rpb.shape
    onehot = (col_off[None] == np.arange(n_co)[:, None, None]).astype(np.float32)
    a = jnp.einsum('hrc,cqk->hqrk', rpb.astype(F32), jnp.asarray(onehot), precision=lax.Precision.HIGHEST)
    a = jnp.where(jnp.asarray(valid)[None, :, None, :], a, NEG_INF)
    tabs = [a[:, :, NA_WIN_R - 1 - d:NA_WIN_R - 1 - d + kh, :].reshape(H, GRID_W, kh * GRID_W)
            for d in range(NA_WIN_R)]
    return jnp.stack(tabs, axis=1)


def _na_attention(qkv, bias, meta_bias):
    B, Tp, _ = qkv.shape
    rows = (Tp - HEAD_ROWS) // GRID_W
    assert rows >= NA_WIN_R
    H = NA_HEADS
    return pl.pallas_call(
        functools.partial(_na_kernel, rows=rows),
        grid=(B, H),
        in_specs=[pl.BlockSpec((1, Tp, NA_HEAD_DIM), lambda b, h: (b, 0, h)),
                  pl.BlockSpec((1, Tp, NA_HEAD_DIM), lambda b, h: (b, 0, H + h)),
                  pl.BlockSpec((1, Tp, NA_HEAD_DIM), lambda b, h: (b, 0, 2 * H + h)),
                  pl.BlockSpec((1, NA_WIN_R, GRID_W, NA_WIN_R * GRID_W), lambda b, h: (h, 0, 0, 0)),
                  pl.BlockSpec((1, 1, N_META), lambda b, h: (h, 0, 0))],
        out_specs=pl.BlockSpec((1, Tp, NA_HEAD_DIM), lambda b, h: (b, 0, h)),
        out_shape=jax.ShapeDtypeStruct((B, Tp, H * NA_HEAD_DIM), BF16),
        compiler_params=_cparams("parallel", "parallel"),
    )(qkv, qkv, qkv, bias, meta_bias.astype(F32).reshape(H, 1, N_META))


def _na_mixer(hn, h, B, Tp, w_qkv, bias, meta_bias, w_o, nw):
    qkv = _matmul(hn, w_qkv.astype(BF16), BF16)
    o = _na_attention(qkv.reshape(B, Tp, -1), bias, meta_bias)
    return _proj_res(o.reshape(B * Tp, -1), w_o.astype(BF16), h, nw)


def _gdn_gate_kernel(x_ref, w_ref, al_ref, dt_ref, o_ref):
    y = _dot(x_ref[...], w_ref[...])
    lane = lax.broadcasted_iota(jnp.int32, y.shape, 1)
    g = -jnp.exp(al_ref[...]) * _softplus(y + dt_ref[...])
    o_ref[...] = jnp.where(lane < 2 * GDN_V_HEADS, _sigmoid(y), g)


def _gdn_gates(hn, w_ba, a_log, dt_bias):
    M, K = hn.shape
    N = 4 * GDN_V_HEADS
    tm = _pick_tile(M, 1024, 128)
    zeros = jnp.zeros((2 * GDN_V_HEADS,), F32)
    al = jnp.concatenate([zeros, a_log.astype(F32).reshape(-1)]).reshape(1, N)
    dt = jnp.concatenate([zeros, dt_bias.astype(F32).reshape(-1)]).reshape(1, N)
    return pl.pallas_call(
        _gdn_gate_kernel,
        grid=(M // tm,),
        in_specs=[pl.BlockSpec((tm, K), lambda i: (i, 0)), pl.BlockSpec((K, N), lambda i: (0, 0)),
                  pl.BlockSpec((1, N), lambda i: (0, 0)), pl.BlockSpec((1, N), lambda i: (0, 0))],
        out_specs=pl.BlockSpec((tm, N), lambda i: (i, 0)),
        out_shape=jax.ShapeDtypeStruct((M, N), F32),
        compiler_params=_cparams("parallel"),
    )(hn, w_ba, al, dt)


def _split3_dot(x, m):
    hi = x.astype(BF16)
    r1 = x - hi.astype(F32)
    mid = r1.astype(BF16)
    lo = (r1 - mid.astype(F32)).astype(BF16)
    return _dot(hi, m) + _dot(mid, m) + _dot(lo, m)


def _gdn_kernel(q_ref, k_ref, v_ref, z_ref, bgc_ref, bgr_ref, nw_ref, o_ref,
                S_s, oacc_s, u_s, wq_s, kst_s, qkd_s, gam_s, *, Tp, HPS):
    hb = pl.program_id(1)
    C = CHUNK
    NC = Tp // C
    HV = GDN_V_HEADS
    DK = GDN_DK
    G = 4
    W = G * C
    P1_GROUP = 10 if HPS == 1 else 5
    scale = DK ** -0.5

    oacc_s[...] = jnp.zeros_like(oacc_s)
    S_s[...] = jnp.zeros_like(S_s)

    ri = lax.broadcasted_iota(jnp.int32, (C, W), 0)
    cl = lax.broadcasted_iota(jnp.int32, (C, W), 1)
    cj = jnp.bitwise_and(cl, C - 1)
    blk = jnp.right_shift(cl, 6)
    dd = jnp.where(blk >= 2, cj - ri, ri - cj)
    incl4 = dd >= 0
    strict4 = dd > 0
    eye4 = (ri == cj).astype(F32)
    cum4 = jnp.where(dd <= 0, 1.0, 0.0).astype(BF16)
    bdm = jnp.where(jnp.right_shift(lax.broadcasted_iota(jnp.int32, (W, W), 0), 6)
                    == jnp.right_shift(lax.broadcasted_iota(jnp.int32, (W, W), 1), 6), 1.0, 0.0).astype(BF16)
    blk_row = jnp.right_shift(lax.broadcasted_iota(jnp.int32, (1, W), 1), 6)
    lane = lax.broadcasted_iota(jnp.int32, (C, 4 * HV), 1)
    rowi = lax.broadcasted_iota(jnp.int32, (C, 4 * HV), 0)

    def col(x, c):
        return jnp.sum(jnp.where(lane == c, x, 0.0), axis=1, keepdims=True)

    def by_block(parts, b):
        return jnp.where(b == 0, parts[0], jnp.where(b == 1, parts[1], jnp.where(b == 2, parts[2], parts[3])))

    def blockdiag(xb):
        return jnp.concatenate([xb] * G, axis=0) * bdm

    def phase1(ns):
        items = [(hd, n) for n in ns for hd in range(HPS)]
        K = range(len(items))
        r0 = [pl.multiple_of(n * C, C) for (hd, n) in items]
        q = [q_ref[0, pl.ds(r0[i], C), items[i][0] * DK:(items[i][0] + 1) * DK] for i in K]
        k = [k_ref[0, pl.ds(r0[i], C), items[i][0] * DK:(items[i][0] + 1) * DK] for i in K]
        bg = [bgc_ref[0, pl.ds(r0[i], C), :] for i in K]
        v = [v_ref[0, pl.ds(r0[i], C), items[i][0] * 2 * DK:(items[i][0] + 1) * 2 * DK].astype(F32) for i in K]
        rows = [bgr_ref[0, n, hd] for (hd, n) in items]
        cum_rows = [_split3_dot(rows[i], cum4) for i in K]
        k4 = [jnp.concatenate([k[i]] * G, axis=0) for i in K]
        kk4 = [_dot_nt(k[i], k4[i]) for i in K]
        qk4 = [_dot_nt(q[i], k4[i]) * scale for i in K]
        beta_c, gc_c, decay4, L = [], [], [], []
        for i in K:
            pre = bg[i]
            for s in (1, 2, 4, 8, 16, 32):
                pre = pre + jnp.where(rowi >= s, pltpu.roll(pre, s, 0), 0.0)
            suf = pre[C - 1:C] - pre + bg[i]
            gc_r = by_block([cum_rows[i][4 + g:5 + g] for g in range(G)], blk_row)
            bc, gc = [], []
            for g in range(G):
                d, j = g // 2, g % 2
                vh = 2 * (hb * HPS + items[i][0]) + j
                bc.append(col(bg[i], d * HV + vh))
                gc.append(col(suf if d else pre, 2 * HV + d * HV + vh))
            beta_c.append(bc)
            gc_c.append(gc)
            dec = jnp.where(incl4, jnp.exp(jnp.where(incl4, by_block(gc, blk) - gc_r, 0.0)), 0.0)
            decay4.append(dec)
            L.append(jnp.where(strict4, kk4[i] * dec, 0.0) * by_block(bc, blk))
        Tm = [eye4 - L[i] for i in K]
        Lb = [L[i].astype(BF16) for i in K]
        P = [_dot(Lb[i], blockdiag(Lb[i])) for i in K]
        for lvl in range(5):
            Pb = [P[i].astype(BF16) for i in K]
            bd = [blockdiag(Pb[i]) for i in K]
            if lvl < 4:
                tp = [_dot(jnp.concatenate([Tm[i].astype(BF16), Pb[i]], axis=0), bd[i]) for i in K]
                Tm = [Tm[i] + tp[i][:C] for i in K]
                P = [tp[i][C:] for i in K]
            else:
                tp = [_dot(Tm[i].astype(BF16), bd[i]) for i in K]
                Tm = [Tm[i] + tp[i] for i in K]
        rhs, kst, qst, gam = [], [], [], []
        for i in K:
            qf = q[i].astype(F32)
            kf = k[i].astype(F32)
            rhs_i, kst_i, qst_i, gam_i = [], [], [], []
            for g in range(G):
                j = g % 2
                bc, gc = beta_c[i][g], gc_c[i][g]
                e_c = jnp.exp(gc)
                rhs_i.append(jnp.concatenate([v[i][:, j * DK:(j + 1) * DK] * bc, kf * (bc * e_c)], axis=1))
                g_last = gc[0:1] if g >= 2 else gc[C - 1:C]
                kst_i.append((kf * jnp.exp(g_last - gc)).astype(BF16))
                qst_i.append((qf * (scale * e_c)).astype(BF16))
                gam_i.append(jnp.broadcast_to(jnp.exp(g_last), (1, DK)))
            rhs.append(jnp.concatenate(rhs_i, axis=0).astype(BF16))
            kst.append(kst_i)
            qst.append(qst_i)
            gam.append(gam_i)
        sol = [_dot(blockdiag(Tm[i].astype(BF16)), rhs[i]) for i in K]
        for i in K:
            hd, n = items[i]
            u_s[hd, n] = sol[i][:, :DK]
            qkd_s[hd, n] = jnp.where(incl4, qk4[i] * decay4[i], 0.0).astype(BF16)
            for g in range(G):
                wq_s[hd, n, g, 0:C, :] = sol[i][g * C:(g + 1) * C, DK:].astype(BF16)
                wq_s[hd, n, g, C:2 * C, :] = qst[i][g]
                kst_s[hd, n, g] = kst[i][g]
                gam_s[hd, n, g:g + 1, :] = gam[i][g]

    def phase1_group(i, _):
        phase1([P1_GROUP * i + c for c in range(P1_GROUP)])
        return 0

    lax.fori_loop(0, NC // P1_GROUP, phase1_group, 0)
    if NC % P1_GROUP:
        phase1(list(range(NC - NC % P1_GROUP, NC)))

    nw = nw_ref[...]

    def gated_norm(o, z):
        ys = []
        for j in range(2 * HPS):
            oj = o[:, j * DK:(j + 1) * DK]
            zj = z[:, j * DK:(j + 1) * DK].astype(F32)
            ys.append(oj * lax.rsqrt(jnp.mean(oj * oj, axis=-1, keepdims=True) + NORM_EPS) * nw * (zj * _sigmoid(zj)))
        return jnp.concatenate(ys, axis=1)

    def finalize(n):
        r0 = pl.multiple_of(n * C, C)
        o_ref[0, pl.ds(r0, C), :] = gated_norm(oacc_s[pl.ds(r0, C), :], z_ref[0, pl.ds(r0, C), :]).astype(o_ref.dtype)

    def phase2(it, finalize_previous):
        if finalize_previous:
            finalize(it - 1)
            finalize(NC - it)
        ops = []
        for hd in range(HPS):
            for g in range(G):
                n = it if g < 2 else NC - 1 - it
                ops.append((S_s[hd * G + g], wq_s[hd, n, g], u_s[hd, n, g * C:(g + 1) * C, :],
                            qkd_s[hd, n, :, g * C:(g + 1) * C], gam_s[hd, n, g:g + 1, :], kst_s[hd, n, g]))
        NI = HPS * G
        r_f = pl.multiple_of(it * C, C)
        r_b = pl.multiple_of((NC - 1 - it) * C, C)
        o_f = oacc_s[pl.ds(r_f, C), :]
        ws = [_dot(wq, S.astype(BF16)) for (S, wq, u, qkd, gam, kst) in ops]
        vb = [(ops[g][2] - ws[g][:C]).astype(BF16) for g in range(NI)]
        outs = [ws[g][C:] + _dot(ops[g][3], vb[g]) for g in range(NI)]
        states = [ops[g][0] * ops[g][4] + _dot_tn(ops[g][5], vb[g]) for g in range(NI)]
        for g in range(NI):
            S_s[g] = states[g]
        fwd = [outs[hd * G + j] for hd in range(HPS) for j in range(2)]
        bwd = [outs[hd * G + 2 + j] for hd in range(HPS) for j in range(2)]
        o_f = o_f + jnp.concatenate(fwd, axis=1)
        oacc_s[pl.ds(r_f, C), :] = o_f
        oacc_s[pl.ds(r_b, C), :] = oacc_s[pl.ds(r_b, C), :] + jnp.concatenate(bwd, axis=1)
        return 0

    lax.fori_loop(0, NC // 2 + 1, lambda it, c: phase2(it, False), 0)
    lax.fori_loop(NC // 2 + 1, NC, lambda it, c: phase2(it, True), 0)
    finalize(NC - 1)
    finalize(0)


_GDN_CHUNK_BYTES = CHUNK * GDN_DK * (4 * 4 + 8 * 2 + 4 * 2 + 4 * 2) + 8 * GDN_DK * 4
_GDN_PAIR_BUDGET = 26 * 1024 * 1024


def _gdn_delta(qkv, z, bg, norm_w):
    B, Tp, _ = qkv.shape
    NC = Tp // CHUNK
    KH = GDN_QK_HEADS
    HV = GDN_V_HEADS
    C = CHUNK
    HPS = 2 if NC * _GDN_CHUNK_BYTES * 2 <= _GDN_PAIR_BUDGET else 1
    NH = KH // HPS
    bgr = bg.reshape(B, NC, CHUNK, 4, KH, 2).transpose(0, 1, 4, 3, 5, 2).reshape(B, NC, KH, 8, CHUNK)
    return pl.pallas_call(
        functools.partial(_gdn_kernel, Tp=Tp, HPS=HPS),
        grid=(B, NH),
        in_specs=[pl.BlockSpec((1, Tp, HPS * GDN_DK), lambda b, h: (b, 0, h)),
                  pl.BlockSpec((1, Tp, HPS * GDN_DK), lambda b, h: (b, 0, NH + h)),
                  pl.BlockSpec((1, Tp, 2 * HPS * GDN_DK), lambda b, h: (b, 0, NH + h)),
                  pl.BlockSpec((1, Tp, 2 * HPS * GDN_DK), lambda b, h: (b, 0, h)),
                  pl.BlockSpec((1, Tp, 4 * HV), lambda b, h: (b, 0, 0)),
                  pl.BlockSpec((1, NC, HPS, 8, CHUNK), lambda b, h: (b, 0, h, 0, 0)),
                  pl.BlockSpec((1, GDN_DK), lambda b, h: (0, 0))],
        out_specs=pl.BlockSpec((1, Tp, 2 * HPS * GDN_DK), lambda b, h: (b, 0, h)),
        out_shape=jax.ShapeDtypeStruct((B, Tp, GDN_VAL_DIM), BF16),
        scratch_shapes=[pltpu.VMEM((4 * HPS, GDN_DK, GDN_DK), F32),
                        pltpu.VMEM((Tp, 2 * HPS * GDN_DK), F32),
                        pltpu.VMEM((HPS, NC, 4 * C, GDN_DK), F32),
                        pltpu.VMEM((HPS, NC, 4, 2 * C, GDN_DK), BF16),
                        pltpu.VMEM((HPS, NC, 4, C, GDN_DK), BF16),
                        pltpu.VMEM((HPS, NC, C, 4 * C), BF16),
                        pltpu.VMEM((HPS, NC, 8, GDN_DK), F32)],
        compiler_params=_cparams("parallel", "parallel"),
    )(qkv, qkv, qkv, z, bg, bgr, norm_w.astype(F32).reshape(1, GDN_DK))


def _gdn_mixer(hn, h, B, Tp, w_in, conv_w, a_log, dt_bias, norm_w, w_out, nw):
    nz = GDN_CONV_DIM + GDN_VAL_DIM
    w_in = w_in.astype(BF16)
    qkv = _proj_conv(hn, w_in[:, :GDN_CONV_DIM], conv_w, Tp, 2 * GDN_KEY_DIM)
    z = _matmul(hn, w_in[:, GDN_CONV_DIM:nz], BF16)
    bg = _gdn_gates(hn, w_in[:, nz:], a_log, dt_bias)
    o = _gdn_delta(qkv.reshape(B, Tp, -1), z.reshape(B, Tp, -1), bg.reshape(B, Tp, -1), norm_w)
    return _proj_res(o.reshape(B * Tp, -1), w_out.astype(BF16), h, nw)


def _trunk(x, meta_tokens, mix_norm, ffn_norm, final_norm, rg, na, gdn, ffn):
    B, T, D = x.shape
    Tp = T + HEAD_ROWS
    depth = mix_norm.shape[0]
    h3 = jnp.concatenate([jnp.zeros((B, PAD, D), F32),
                          jnp.broadcast_to(meta_tokens.astype(F32)[None], (B, N_META, D)),
                          x.astype(F32)], axis=1)
    h = h3.reshape(B * Tp, D)
    hn = _rmsnorm(h, mix_norm[0], BF16)
    for i in range(depth):
        kind, j = i % 3, i // 3
        if kind == 0:
            h, hn = _rglru_mixer(hn, h, B, Tp, *[p[j] for p in rg], ffn_norm[i])
        elif kind == 1:
            h, hn = _na_mixer(hn, h, B, Tp, *[p[j] for p in na], ffn_norm[i])
        else:
            h, hn = _gdn_mixer(hn, h, B, Tp, *[p[j] for p in gdn], ffn_norm[i])
        w_gate, w_up, conv_w, conv_b, w_down = [p[i] for p in ffn]
        next_norm = mix_norm[i + 1] if i + 1 < depth else final_norm
        h, hn = _conv_ffn(hn, h, w_gate.astype(BF16), w_up.astype(BF16), conv_w, conv_b,
                          w_down.astype(BF16), next_norm)
    return _final_norm(h, final_norm, B, T)


def kernel(x_prompt, x_sample, meta_tokens, mix_norm, ffn_norm, final_norm, rg_w_in, rg_conv_w, rg_conv_b, rg_w_a, rg_b_a, rg_w_i, rg_b_i, rg_lam, rg_w_out, na_w_qkv, na_rpb, na_meta_bias, na_w_o, gdn_w_in, gdn_conv_w, gdn_a_log, gdn_dt_bias, gdn_norm_w, gdn_w_out, ffn_w_gate, ffn_w_up, ffn_conv_w, ffn_conv_b, ffn_w_down):
    rg = (rg_w_in, rg_conv_w, rg_conv_b, rg_w_a, rg_b_a, rg_w_i, rg_b_i, rg_lam, rg_w_out)
    na_bias = jnp.stack([_na_bias_table(na_rpb[j]) for j in range(na_rpb.shape[0])])
    na = (na_w_qkv, na_bias, na_meta_bias, na_w_o)
    gdn = (gdn_w_in, gdn_conv_w, gdn_a_log, gdn_dt_bias, gdn_norm_w, gdn_w_out)
    ffn = (ffn_w_gate, ffn_w_up, ffn_conv_w, ffn_conv_b, ffn_w_down)
    y_prompt = _trunk(x_prompt, meta_tokens, mix_norm, ffn_norm, final_norm, rg, na, gdn, ffn)
    y_sample = _trunk(x_sample, meta_tokens, mix_norm, ffn_norm, final_norm, rg, na, gdn, ffn)
    return (y_prompt, y_sample)
```

```python
import functools

import numpy as np
import jax
import jax.numpy as jnp
from jax import lax
from jax.experimental import pallas as pl
from jax.experimental.pallas import tpu as pltpu

D_MODEL = 2048
N_META = 16
PAD = 48
HEAD_ROWS = PAD + N_META
CHUNK = 64
GRID_W = 64
NORM_EPS = 1e-6

RG_BLOCK = 256
RG_C = 8.0

NA_HEAD_DIM = 128
NA_HEADS = D_MODEL // NA_HEAD_DIM
NA_WIN_R = 8
NA_WIN_C = 16
NA_ROW_GROUP = 32
NEG_INF = -1e30

GDN_DK = 128
GDN_QK_HEADS = D_MODEL // GDN_DK
GDN_V_HEADS = 2 * GDN_QK_HEADS
GDN_KEY_DIM = GDN_QK_HEADS * GDN_DK
GDN_VAL_DIM = GDN_V_HEADS * GDN_DK
GDN_CONV_DIM = 2 * GDN_KEY_DIM + GDN_VAL_DIM

V7X_VMEM_LIMIT = 56 * 1024 * 1024
HALO = 16

BF16 = jnp.bfloat16
F32 = jnp.float32


def _cparams(*sem):
    return pltpu.CompilerParams(dimension_semantics=sem, vmem_limit_bytes=V7X_VMEM_LIMIT)


def _pick_tile(n, cap, mult):
    best = None
    for t in range(mult, cap + 1, mult):
        if n % t == 0:
            best = t
    assert best is not None, (n, cap, mult)
    return best


def _sigmoid(x):
    return 1.0 / (1.0 + jnp.exp(-x))


def _softplus(x):
    return jnp.maximum(x, 0.0) + jnp.log(1.0 + jnp.exp(-jnp.abs(x)))


def _gelu_tanh(x):
    return 0.5 * x * (1.0 + jnp.tanh(0.7978845608028654 * (x + 0.044715 * x * x * x)))


def _rms_scale(x, w):
    ms = jnp.mean(x * x, axis=-1, keepdims=True)
    return x * lax.rsqrt(ms + NORM_EPS) * w


def _dot(a, b):
    return jnp.dot(a, b, preferred_element_type=F32)


def _dot_nt(a, b):
    return lax.dot_general(a, b, (((1,), (1,)), ((), ())), preferred_element_type=F32)


def _dot_tn(a, b):
    return lax.dot_general(a, b, (((0,), (0,)), ((), ())), preferred_element_type=F32)


def _rmsnorm_kernel(x_ref, w_ref, o_ref):
    o_ref[...] = _rms_scale(x_ref[...], w_ref[...]).astype(o_ref.dtype)


def _rmsnorm(h, w, out_dtype):
    M, D = h.shape
    tm = _pick_tile(M, 1024, 64)
    return pl.pallas_call(
        _rmsnorm_kernel,
        grid=(M // tm,),
        in_specs=[pl.BlockSpec((tm, D), lambda i: (i, 0)), pl.BlockSpec((1, D), lambda i: (0, 0))],
        out_specs=pl.BlockSpec((tm, D), lambda i: (i, 0)),
        out_shape=jax.ShapeDtypeStruct((M, D), out_dtype),
        compiler_params=_cparams("parallel"),
    )(h, w.reshape(1, D))


def _final_norm_kernel(x_ref, w_ref, o_ref):
    o_ref[0] = _rms_scale(x_ref[...], w_ref[...])


def _final_norm(h, w, B, T):
    D = h.shape[1]
    Tp = T + HEAD_ROWS
    tt = 512
    assert T % tt == 0
    return pl.pallas_call(
        _final_norm_kernel,
        grid=(B, T // tt),
        in_specs=[pl.BlockSpec((pl.Element(tt), pl.Element(D)),
                               lambda b, j: (pl.multiple_of(b * Tp + HEAD_ROWS + j * tt, CHUNK), 0)),
                  pl.BlockSpec((1, D), lambda b, j: (0, 0))],
        out_specs=pl.BlockSpec((1, tt, D), lambda b, j: (b, j, 0)),
        out_shape=jax.ShapeDtypeStruct((B, T, D), F32),
        compiler_params=_cparams("parallel", "parallel"),
    )(h, w.reshape(1, D))


def _matmul_kernel(x_ref, w_ref, o_ref):
    o_ref[...] = _dot(x_ref[...], w_ref[...]).astype(o_ref.dtype)


def _matmul(x, w, out_dtype):
    M, K = x.shape
    N = w.shape[1]
    tm = _pick_tile(M, 1280, 128)
    tn = _pick_tile(N, 1024, 128)
    return pl.pallas_call(
        _matmul_kernel,
        grid=(M // tm, N // tn),
        in_specs=[pl.BlockSpec((tm, K), lambda i, j: (i, 0)), pl.BlockSpec((K, tn), lambda i, j: (0, j))],
        out_specs=pl.BlockSpec((tm, tn), lambda i, j: (i, j)),
        out_shape=jax.ShapeDtypeStruct((M, N), out_dtype),
        compiler_params=_cparams("parallel", "parallel"),
    )(x, w)


def _gate_rec_kernel(x_ref, w_ref, gate_ref, rec_ref, *, nh):
    j = pl.program_id(1)
    acc = _dot(x_ref[...], w_ref[...])

    @pl.when(j < nh)
    def _():
        gate_ref[...] = _gelu_tanh(acc).astype(gate_ref.dtype)

    @pl.when(j >= nh)
    def _():
        rec_ref[...] = acc


def _gate_rec(x, w):
    M, K = x.shape
    W = w.shape[1] // 2
    tm = _pick_tile(M, 1280, 128)
    tn = _pick_tile(W, 1024, 128)
    nh = W // tn
    return pl.pallas_call(
        functools.partial(_gate_rec_kernel, nh=nh),
        grid=(M // tm, 2 * nh),
        in_specs=[pl.BlockSpec((tm, K), lambda i, j: (i, 0)), pl.BlockSpec((K, tn), lambda i, j: (0, j))],
        out_specs=[pl.BlockSpec((tm, tn), lambda i, j: (i, jnp.minimum(j, nh - 1))),
                   pl.BlockSpec((tm, tn), lambda i, j: (i, jnp.maximum(j - nh, 0)))],
        out_shape=[jax.ShapeDtypeStruct((M, W), BF16), jax.ShapeDtypeStruct((M, W), F32)],
        compiler_params=_cparams("parallel", "arbitrary"),
    )(x, w)


def _proj_res_kernel(x_ref, w_ref, h_ref, nw_ref, hout_ref, hn_ref):
    hnew = h_ref[...] + _dot(x_ref[...], w_ref[...])
    hout_ref[...] = hnew
    hn_ref[...] = _rms_scale(hnew, nw_ref[...]).astype(hn_ref.dtype)


def _proj_res(x, w, h, nw):
    M, K = x.shape
    D = w.shape[1]
    tm = _pick_tile(M, 704 if K <= D_MODEL else 512, 64)
    return pl.pallas_call(
        _proj_res_kernel,
        grid=(M // tm,),
        in_specs=[pl.BlockSpec((tm, K), lambda i: (i, 0)),
                  pl.BlockSpec((K, D), lambda i: (0, 0), pipeline_mode=pl.Buffered(1)),
                  pl.BlockSpec((tm, D), lambda i: (i, 0)),
                  pl.BlockSpec((1, D), lambda i: (0, 0))],
        out_specs=[pl.BlockSpec((tm, D), lambda i: (i, 0)), pl.BlockSpec((tm, D), lambda i: (i, 0))],
        out_shape=[jax.ShapeDtypeStruct((M, D), F32), jax.ShapeDtypeStruct((M, D), BF16)],
        compiler_params=_cparams("parallel"),
    )(x, w, h, nw.reshape(1, D))


def _proj_conv_kernel(xp_ref, xm_ref, xn_ref, w_ref, cw_ref, o_ref, xext_ref, *, tm, tc, ni, Tp, n_norm):
    i = pl.program_id(0)
    j = pl.program_id(1)
    npc = o_ref.shape[1] // tc

    @pl.when(j == 0)
    def _():
        xext_ref[0:HALO, :] = xp_ref[...]
        xext_ref[HALO:HALO + tm, :] = xm_ref[...]
        xext_ref[HALO + tm:2 * HALO + tm, :] = xn_ref[...]

    @pl.when(jnp.logical_and(j == 0, i == ni - 1))
    def _():
        xext_ref[HALO + tm:2 * HALO + tm, :] = jnp.zeros((HALO, xext_ref.shape[1]), xext_ref.dtype)

    rel = i * tm - (i * tm // Tp) * Tp + lax.broadcasted_iota(jnp.int32, (tm, 1), 0)
    live = jnp.where(rel >= Tp, rel - Tp, rel) >= PAD
    cw = cw_ref[...]
    xe = xext_ref[...]

    def piece_matmul(c):
        return _dot(xe, w_ref[:, c * tc:(c + 1) * tc])

    def piece_finish(c, g):
        cwc = cw[:, c * tc:(c + 1) * tc]
        a = (cwc[0:1] * g[HALO - 1:HALO - 1 + tm] + cwc[1:2] * g[HALO:HALO + tm]
             + cwc[2:3] * g[HALO + 1:HALO + 1 + tm] + cwc[3:4] * g[HALO + 2:HALO + 2 + tm])
        y = jnp.where(live, a * _sigmoid(a), 0.0)
        normalise = j * npc + c < n_norm
        heads = []
        for hh in range(tc // GDN_DK):
            yh = y[:, hh * GDN_DK:(hh + 1) * GDN_DK]
            nrm = lax.rsqrt(jnp.sum(yh * yh, axis=-1, keepdims=True) + 1e-6)
            heads.append(yh * jnp.where(normalise, nrm, 1.0))
        o_ref[:, c * tc:(c + 1) * tc] = jnp.concatenate(heads, axis=1).astype(o_ref.dtype)

    g = piece_matmul(0)
    for c in range(npc):
        g_next = piece_matmul(c + 1) if c + 1 < npc else None
        piece_finish(c, g)
        g = g_next


def _proj_conv(x, w, conv_w, Tp, n_norm_cols):
    M, K = x.shape
    N = w.shape[1]
    tm = _pick_tile(M, 1024, 64)
    assert tm <= Tp
    tn, tc = 2048, 512
    ni = M // tm
    hb = tm // HALO
    nhb = M // HALO
    return pl.pallas_call(
        functools.partial(_proj_conv_kernel, tm=tm, tc=tc, ni=ni, Tp=Tp, n_norm=n_norm_cols // tc),
        grid=(ni, N // tn),
        in_specs=[pl.BlockSpec((HALO, K), lambda i, j: (jnp.maximum(i * hb - 1, 0), 0)),
                  pl.BlockSpec((tm, K), lambda i, j: (i, 0)),
                  pl.BlockSpec((HALO, K), lambda i, j: (jnp.minimum((i + 1) * hb, nhb - 1), 0)),
                  pl.BlockSpec((K, tn), lambda i, j: (0, j)),
                  pl.BlockSpec((4, tn), lambda i, j: (0, j))],
        out_specs=pl.BlockSpec((tm, tn), lambda i, j: (i, j)),
        out_shape=jax.ShapeDtypeStruct((M, N), BF16),
        scratch_shapes=[pltpu.VMEM((tm + 2 * HALO, K), BF16)],
        compiler_params=_cparams("parallel", "arbitrary"),
    )(x, x, x, w, conv_w)


def _ffn_kernel(xp_ref, xm_ref, xn_ref, wg_ref, wu_ref, cw_ref, cb_ref, wd_ref, h_ref, nw_ref,
                hout_ref, hn_ref, xext_ref, *, tm, nf, ni):
    i = pl.program_id(0)
    f = pl.program_id(1)
    th = tm // 2

    @pl.when(f == 0)
    def _():
        xext_ref[0:HALO, :] = xp_ref[...]
        xext_ref[HALO:HALO + tm, :] = xm_ref[...]
        xext_ref[HALO + tm:2 * HALO + tm, :] = xn_ref[...]
        hout_ref[...] = h_ref[...]

    cw = cw_ref[...]
    cb = cb_ref[...]
    g, up = [], []
    for s in range(2):
        g.append(_dot(xext_ref[s * th:s * th + th + 2 * HALO, :], wg_ref[...]))
        up.append(_dot(xm_ref[s * th:(s + 1) * th, :], wu_ref[...]))
    row = lax.broadcasted_iota(jnp.int32, (th, 1), 0)
    y = []
    for s in range(2):
        g_prev = g[s][HALO - 1:HALO - 1 + th]
        g_here = g[s][HALO:HALO + th]
        g_next = g[s][HALO + 1:HALO + 1 + th]
        if s == 1:
            g_next = jnp.where(jnp.logical_and(i == ni - 1, row == th - 1), 0.0, g_next)
        a = cw[0:1] * g_prev + cw[1:2] * g_here + cw[2:3] * g_next + cb
        y.append((a * _sigmoid(a) * up[s]).astype(BF16))
    down = [_dot(y[s], wd_ref[...]) for s in range(2)]
    for s in range(2):
        hout_ref[s * th:(s + 1) * th, :] += down[s]

    @pl.when(f == nf - 1)
    def _():
        hn_ref[...] = _rms_scale(hout_ref[...], nw_ref[...]).astype(hn_ref.dtype)


def _conv_ffn(hn, h, w_gate, w_up, conv_w, conv_b, w_down, nw):
    M, D = hn.shape
    F = w_gate.shape[1]
    tm = _pick_tile(M, 704, 64)
    tf = _pick_tile(F, 512, 128)
    ni, nf = M // tm, F // tf
    hb = tm // HALO
    nhb = M // HALO
    return pl.pallas_call(
        functools.partial(_ffn_kernel, tm=tm, nf=nf, ni=ni),
        grid=(ni, nf),
        in_specs=[pl.BlockSpec((HALO, D), lambda i, f: (jnp.maximum(i * hb - 1, 0), 0)),
                  pl.BlockSpec((tm, D), lambda i, f: (i, 0)),
                  pl.BlockSpec((HALO, D), lambda i, f: (jnp.minimum((i + 1) * hb, nhb - 1), 0)),
                  pl.BlockSpec((D, tf), lambda i, f: (0, f)),
                  pl.BlockSpec((D, tf), lambda i, f: (0, f)),
                  pl.BlockSpec((3, tf), lambda i, f: (0, f)),
                  pl.BlockSpec((1, tf), lambda i, f: (0, f)),
                  pl.BlockSpec((tf, D), lambda i, f: (f, 0)),
                  pl.BlockSpec((tm, D), lambda i, f: (i, 0)),
                  pl.BlockSpec((1, D), lambda i, f: (0, 0))],
        out_specs=[pl.BlockSpec((tm, D), lambda i, f: (i, 0)), pl.BlockSpec((tm, D), lambda i, f: (i, 0))],
        out_shape=[jax.ShapeDtypeStruct((M, D), F32), jax.ShapeDtypeStruct((M, D), BF16)],
        scratch_shapes=[pltpu.VMEM((tm + 2 * HALO, D), BF16)],
        compiler_params=_cparams("parallel", "arbitrary"),
    )(hn, hn, hn, w_gate, w_up, conv_w, conv_b.reshape(1, F), w_down, h, nw.reshape(1, D))


def _scan8(a, u, row, reverse):
    for s in (1, 2, 4):
        if reverse:
            a_sh = pltpu.roll(a, 8 - s, 0)
            u_sh = pltpu.roll(u, 8 - s, 0)
            m = row < 8 - s
        else:
            a_sh = pltpu.roll(a, s, 0)
            u_sh = pltpu.roll(u, s, 0)
            m = row >= s
        u = jnp.where(m, a * u_sh + u, u)
        a = jnp.where(m, a * a_sh, a)
    return a, u


def _rg_kernel(rec_ref, gate_ref, cw_ref, cb_ref, wa_ref, ba_ref, wi_ref, bi_ref, lam_ref, o_ref,
               xc_s, hf_s, hb_s, *, Tp, rc):
    nch = Tp // rc
    W = RG_BLOCK
    T8 = rc // 8
    cw = cw_ref[...]
    cb = cb_ref[...]
    row8 = lax.broadcasted_iota(jnp.int32, (8, W), 0)

    def rows(c):
        return pl.multiple_of(c * rc, CHUNK)

    def conv(r0):
        top = rec_ref[0, pl.ds(pl.multiple_of(jnp.maximum(r0 - 8, 0), 8), 8), :]
        main = rec_ref[0, pl.ds(r0, rc), :]
        bot = rec_ref[0, pl.ds(pl.multiple_of(jnp.minimum(r0 + rc, Tp - 8), 8), 8), :]
        bot = jnp.where(r0 + rc < Tp, bot, 0.0)
        xw = jnp.concatenate([top, main, bot], axis=0)
        xc = cw[0:1] * xw[7:7 + rc] + cw[1:2] * xw[8:8 + rc] + cw[2:3] * xw[9:9 + rc] + cw[3:4] * xw[10:10 + rc] + cb
        t = r0 + lax.broadcasted_iota(jnp.int32, (rc, 1), 0)
        return jnp.where(t >= PAD, xc, 0.0)

    def advance(x_f, x_b, carry):
        xs = (x_f, x_b)
        xb16 = [x.astype(BF16) for x in xs]
        pre = [(_dot(xb16[d], wa_ref[d, 0]), _dot(xb16[d], wi_ref[d, 0])) for d in range(2)]
        au = []
        for d in range(2):
            r = _sigmoid(pre[d][0] + ba_ref[d])
            ig = _sigmoid(pre[d][1] + bi_ref[d])
            a = jnp.exp((-RG_C) * r * _softplus(-lam_ref[d]))
            au.append((a, jnp.sqrt(1.0 - a * a) * (ig * xs[d])))
        c_f, c_b = carry
        h_f, h_b = [None] * T8, [None] * T8
        for k in range(T8):
            A, U = _scan8(au[0][0][8 * k:8 * k + 8], au[0][1][8 * k:8 * k + 8], row8, False)
            h_f[k] = A * c_f + U
            c_f = h_f[k][7:8]
            kb = T8 - 1 - k
            A, U = _scan8(au[1][0][8 * kb:8 * kb + 8], au[1][1][8 * kb:8 * kb + 8], row8, True)
            h_b[kb] = A * c_b + U
            c_b = h_b[kb][0:1]
        return jnp.concatenate(h_f, axis=0), jnp.concatenate(h_b, axis=0), (c_f, c_b)

    def emit(r0, hsum):
        o_ref[0, pl.ds(r0, rc), :] = (gate_ref[0, pl.ds(r0, rc), :].astype(F32) * hsum).astype(o_ref.dtype)

    def first_half(c, carry):
        r_f, r_b = rows(c), rows(nch - 1 - c)
        x_f, x_b = conv(r_f), conv(r_b)
        h_f, h_b, carry = advance(x_f, x_b, carry)
        xc_s[pl.ds(r_f, rc), :] = x_f
        xc_s[pl.ds(r_b, rc), :] = x_b
        hf_s[pl.ds(r_f, rc), :] = h_f
        hb_s[pl.ds(r_b, rc), :] = h_b
        return carry

    carry = lax.fori_loop(0, nch // 2, first_half, (jnp.zeros((1, W), F32), jnp.zeros((1, W), F32)))

    if nch % 2:
        r_m = (nch // 2) * rc
        x_m = conv(r_m)
        h_f, h_b, carry = advance(x_m, x_m, carry)
        emit(r_m, h_f + h_b)

    def second_half(c, carry):
        r_f, r_b = rows(c), rows(nch - 1 - c)
        h_f, h_b, carry = advance(xc_s[pl.ds(r_f, rc), :], xc_s[pl.ds(r_b, rc), :], carry)
        emit(r_f, h_f + hb_s[pl.ds(r_f, rc), :])
        emit(r_b, hf_s[pl.ds(r_b, rc), :] + h_b)
        return carry

    lax.fori_loop((nch + 1) // 2, nch, second_half, carry)


def _rg_scan(rec, gate, conv_w, conv_b, w_a, b_a, w_i, b_i, lam):
    B, Tp, W = rec.shape
    nb = W // RG_BLOCK
    rc = _pick_tile(Tp, 384, 64)
    blk = lambda b, n: (b, 0, n)
    vec = lambda b, n: (0, n)
    vec3 = lambda b, n: (0, 0, n)
    return pl.pallas_call(
        functools.partial(_rg_kernel, Tp=Tp, rc=rc),
        grid=(B, nb),
        in_specs=[pl.BlockSpec((1, Tp, RG_BLOCK), blk),
                  pl.BlockSpec((1, Tp, RG_BLOCK), blk),
                  pl.BlockSpec((4, RG_BLOCK), vec),
                  pl.BlockSpec((1, RG_BLOCK), vec),
                  pl.BlockSpec((2, 1, RG_BLOCK, RG_BLOCK), lambda b, n: (0, n, 0, 0)),
                  pl.BlockSpec((2, 1, RG_BLOCK), vec3),
                  pl.BlockSpec((2, 1, RG_BLOCK, RG_BLOCK), lambda b, n: (0, n, 0, 0)),
                  pl.BlockSpec((2, 1, RG_BLOCK), vec3),
                  pl.BlockSpec((2, 1, RG_BLOCK), vec3)],
        out_specs=pl.BlockSpec((1, Tp, RG_BLOCK), blk),
        out_shape=jax.ShapeDtypeStruct((B, Tp, W), BF16),
        scratch_shapes=[pltpu.VMEM((Tp, RG_BLOCK), F32)] * 3,
        compiler_params=_cparams("parallel", "parallel"),
    )(rec, gate, conv_w, conv_b.reshape(1, W), w_a.astype(BF16), b_a.reshape(2, 1, W),
      w_i.astype(BF16), b_i.reshape(2, 1, W), lam.reshape(2, 1, W))


def _rglru_mixer(hn, h, B, Tp, w_in, conv_w, conv_b, w_a, b_a, w_i, b_i, lam, w_out, nw):
    W = D_MODEL
    gate, rec = _gate_rec(hn, w_in.astype(BF16))
    y = _rg_scan(rec.reshape(B, Tp, W), gate.reshape(B, Tp, W), conv_w, conv_b, w_a, b_a, w_i, b_i, lam)
    return _proj_res(y.reshape(B * Tp, W), w_out.astype(BF16), h, nw)


def _na_kernel(q_ref, k_ref, v_ref, bias_ref, mb_ref, o_ref, *, rows):
    scale = NA_HEAD_DIM ** -0.5
    kh = NA_WIN_R
    mb = mb_ref[0]
    km = k_ref[0, PAD:HEAD_ROWS, :]
    vm = v_ref[0, PAD:HEAD_ROWS, :]

    o_ref[0, 0:PAD, :] = jnp.zeros((PAD, NA_HEAD_DIM), o_ref.dtype)
    qm = q_ref[0, PAD:HEAD_ROWS, :]

    group = np.gcd(rows, NA_ROW_GROUP)
    R = range(group)

    def group_body(i, _):
        r = [i * group + c for c in R]
        rs = [jnp.clip(r[c] - NA_WIN_R // 2, 0, rows - kh) for c in R]
        q0 = [pl.multiple_of(HEAD_ROWS + r[c] * GRID_W, GRID_W) for c in R]
        k0 = [pl.multiple_of(HEAD_ROWS + rs[c] * GRID_W, GRID_W) for c in R]
        q_r = [q_ref[0, pl.ds(q0[c], GRID_W), :] for c in R]
        s = [_dot_nt(q_r[c], k_ref[0, pl.ds(k0[c], kh * GRID_W), :]) for c in R]
        s_met = [_dot_nt(q_r[c], km) for c in R]
        s_m = _dot_nt(qm, km) * scale + mb
        p_m = jnp.exp(s_m - jnp.max(s_m, axis=-1, keepdims=True))
        p, p_met, l = [], [], []
        for c in R:
            sc = s[c] * scale + bias_ref[0, r[c] - rs[c]]
            sm = s_met[c] * scale + mb
            m = jnp.maximum(jnp.max(sc, axis=-1, keepdims=True), jnp.max(sm, axis=-1, keepdims=True))
            pc = jnp.exp(sc - m)
            pm = jnp.exp(sm - m)
            l.append(jnp.sum(pc, axis=-1, keepdims=True) + jnp.sum(pm, axis=-1, keepdims=True))
            p.append(pc.astype(BF16))
            p_met.append(pm.astype(BF16))
        o = [_dot(p[c], v_ref[0, pl.ds(k0[c], kh * GRID_W), :]) + _dot(p_met[c], vm) for c in R]
        o_m = _dot(p_m.astype(BF16), vm) / jnp.sum(p_m, axis=-1, keepdims=True)
        o_ref[0, PAD:HEAD_ROWS, :] = o_m.astype(o_ref.dtype)
        for c in R:
            o_ref[0, pl.ds(q0[c], GRID_W), :] = (o[c] / l[c]).astype(o_ref.dtype)
        return 0

    lax.fori_loop(0, rows // group, group_body, 0)


def _na_bias_table(rpb):
    kh = NA_WIN_R
    cols = np.arange(GRID_W)
    col_start = np.clip(cols - NA_WIN_C // 2, 0, GRID_W - NA_WIN_C)
    valid = (cols[None, :] >= col_start[:, None]) & (cols[None, :] < col_start[:, None] + NA_WIN_C)
    col_off = np.clip(cols[None, :] - cols[:, None], -(NA_WIN_C - 1), NA_WIN_C - 1) + NA_WIN_C - 1
    H, n_ro, n_co = rpb.shape
    onehot = (col_off[None] == np.arange(n_co)[:, None, None]).astype(np.float32)
    a = jnp.einsum('hrc,cqk->hqrk', rpb.astype(F32), jnp.asarray(onehot), precision=lax.Precision.HIGHEST)
    a = jnp.where(jnp.asarray(valid)[None, :, None, :], a, NEG_INF)
    tabs = [a[:, :, NA_WIN_R - 1 - d:NA_WIN_R - 1 - d + kh, :].reshape(H, GRID_W, kh * GRID_W)
            for d in range(NA_WIN_R)]
    return jnp.stack(tabs, axis=1)


def _na_attention(qkv, bias, meta_bias):
    B, Tp, _ = qkv.shape
    rows = (Tp - HEAD_ROWS) // GRID_W
    assert rows >= NA_WIN_R
    H = NA_HEADS
    return pl.pallas_call(
        functools.partial(_na_kernel, rows=rows),
        grid=(B, H),
        in_specs=[pl.BlockSpec((1, Tp, NA_HEAD_DIM), lambda b, h: (b, 0, h)),
                  pl.BlockSpec((1, Tp, NA_HEAD_DIM), lambda b, h: (b, 0, H + h)),
                  pl.BlockSpec((1, Tp, NA_HEAD_DIM), lambda b, h: (b, 0, 2 * H + h)),
                  pl.BlockSpec((1, NA_WIN_R, GRID_W, NA_WIN_R * GRID_W), lambda b, h: (h, 0, 0, 0)),
                  pl.BlockSpec((1, 1, N_META), lambda b, h: (h, 0, 0))],
        out_specs=pl.BlockSpec((1, Tp, NA_HEAD_DIM), lambda b, h: (b, 0, h)),
        out_shape=jax.ShapeDtypeStruct((B, Tp, H * NA_HEAD_DIM), BF16),
        compiler_params=_cparams("parallel", "parallel"),
    )(qkv, qkv, qkv, bias, meta_bias.astype(F32).reshape(H, 1, N_META))


def _na_mixer(hn, h, B, Tp, w_qkv, bias, meta_bias, w_o, nw):
    qkv = _matmul(hn, w_qkv.astype(BF16), BF16)
    o = _na_attention(qkv.reshape(B, Tp, -1), bias, meta_bias)
    return _proj_res(o.reshape(B * Tp, -1), w_o.astype(BF16), h, nw)


def _gdn_gate_kernel(x_ref, w_ref, al_ref, dt_ref, o_ref):
    y = _dot(x_ref[...], w_ref[...])
    lane = lax.broadcasted_iota(jnp.int32, y.shape, 1)
    g = -jnp.exp(al_ref[...]) * _softplus(y + dt_ref[...])
    o_ref[...] = jnp.where(lane < 2 * GDN_V_HEADS, _sigmoid(y), g)


def _gdn_gates(hn, w_ba, a_log, dt_bias):
    M, K = hn.shape
    N = 4 * GDN_V_HEADS
    tm = _pick_tile(M, 1024, 128)
    zeros = jnp.zeros((2 * GDN_V_HEADS,), F32)
    al = jnp.concatenate([zeros, a_log.astype(F32).reshape(-1)]).reshape(1, N)
    dt = jnp.concatenate([zeros, dt_bias.astype(F32).reshape(-1)]).reshape(1, N)
    return pl.pallas_call(
        _gdn_gate_kernel,
        grid=(M // tm,),
        in_specs=[pl.BlockSpec((tm, K), lambda i: (i, 0)), pl.BlockSpec((K, N), lambda i: (0, 0)),
                  pl.BlockSpec((1, N), lambda i: (0, 0)), pl.BlockSpec((1, N), lambda i: (0, 0))],
        out_specs=pl.BlockSpec((tm, N), lambda i: (i, 0)),
        out_shape=jax.ShapeDtypeStruct((M, N), F32),
        compiler_params=_cparams("parallel"),
    )(hn, w_ba, al, dt)


def _split3_dot(x, m):
    hi = x.astype(BF16)
    r1 = x - hi.astype(F32)
    mid = r1.astype(BF16)
    lo = (r1 - mid.astype(F32)).astype(BF16)
    return _dot(hi, m) + _dot(mid, m) + _dot(lo, m)


def _gdn_kernel(q_ref, k_ref, v_ref, z_ref, bgc_ref, bgr_ref, nw_ref, o_ref,
                S_s, oacc_s, u_s, wq_s, kst_s, qkd_s, gam_s, *, Tp, HPS):
    hb = pl.program_id(1)
    C = CHUNK
    NC = Tp // C
    HV = GDN_V_HEADS
    DK = GDN_DK
    G = 4
    W = G * C
    P1_GROUP = 10 if HPS == 1 else 5
    scale = DK ** -0.5

    oacc_s[...] = jnp.zeros_like(oacc_s)
    S_s[...] = jnp.zeros_like(S_s)

    ri = lax.broadcasted_iota(jnp.int32, (C, W), 0)
    cl = lax.broadcasted_iota(jnp.int32, (C, W), 1)
    cj = jnp.bitwise_and(cl, C - 1)
    blk = jnp.right_shift(cl, 6)
    dd = jnp.where(blk >= 2, cj - ri, ri - cj)
    incl4 = dd >= 0
    strict4 = dd > 0
    eye4 = (ri == cj).astype(F32)
    cum4 = jnp.where(dd <= 0, 1.0, 0.0).astype(BF16)
    bdm = jnp.where(jnp.right_shift(lax.broadcasted_iota(jnp.int32, (W, W), 0), 6)
                    == jnp.right_shift(lax.broadcasted_iota(jnp.int32, (W, W), 1), 6), 1.0, 0.0).astype(BF16)
    blk_row = jnp.right_shift(lax.broadcasted_iota(jnp.int32, (1, W), 1), 6)
    lane = lax.broadcasted_iota(jnp.int32, (C, 4 * HV), 1)
    rowi = lax.broadcasted_iota(jnp.int32, (C, 4 * HV), 0)

    def col(x, c):
        return jnp.sum(jnp.where(lane == c, x, 0.0), axis=1, keepdims=True)

    def by_block(parts, b):
        return jnp.where(b == 0, parts[0], jnp.where(b == 1, parts[1], jnp.where(b == 2, parts[2], parts[3])))

    def blockdiag(xb):
        return jnp.concatenate([xb] * G, axis=0) * bdm

    def phase1(ns):
        items = [(hd, n) for n in ns for hd in range(HPS)]
        K = range(len(items))
        r0 = [pl.multiple_of(n * C, C) for (hd, n) in items]
        q = [q_ref[0, pl.ds(r0[i], C), items[i][0] * DK:(items[i][0] + 1) * DK] for i in K]
        k = [k_ref[0, pl.ds(r0[i], C), items[i][0] * DK:(items[i][0] + 1) * DK] for i in K]
        bg = [bgc_ref[0, pl.ds(r0[i], C), :] for i in K]
        v = [v_ref[0, pl.ds(r0[i], C), items[i][0] * 2 * DK:(items[i][0] + 1) * 2 * DK].astype(F32) for i in K]
        rows = [bgr_ref[0, n, hd] for (hd, n) in items]
        cum_rows = [_split3_dot(rows[i], cum4) for i in K]
        k4 = [jnp.concatenate([k[i]] * G, axis=0) for i in K]
        kk4 = [_dot_nt(k[i], k4[i]) for i in K]
        qk4 = [_dot_nt(q[i], k4[i]) * scale for i in K]
        beta_c, gc_c, decay4, L = [], [], [], []
        for i in K:
            pre = bg[i]
            for s in (1, 2, 4, 8, 16, 32):
                pre = pre + jnp.where(rowi >= s, pltpu.roll(pre, s, 0), 0.0)
            suf = pre[C - 1:C] - pre + bg[i]
            gc_r = by_block([cum_rows[i][4 + g:5 + g] for g in range(G)], blk_row)
            bc, gc = [], []
            for g in range(G):
                d, j = g // 2, g % 2
                vh = 2 * (hb * HPS + items[i][0]) + j
                bc.append(col(bg[i], d * HV + vh))
                gc.append(col(suf if d else pre, 2 * HV + d * HV + vh))
            beta_c.append(bc)
            gc_c.append(gc)
            dec = jnp.where(incl4, jnp.exp(jnp.where(incl4, by_block(gc, blk) - gc_r, 0.0)), 0.0)
            decay4.append(dec)
            L.append(jnp.where(strict4, kk4[i] * dec, 0.0) * by_block(bc, blk))
        Tm = [eye4 - L[i] for i in K]
        Lb = [L[i].astype(BF16) for i in K]
        P = [_dot(Lb[i], blockdiag(Lb[i])) for i in K]
        for lvl in range(5):
            Pb = [P[i].astype(BF16) for i in K]
            bd = [blockdiag(Pb[i]) for i in K]
            if lvl < 4:
                tp = [_dot(jnp.concatenate([Tm[i].astype(BF16), Pb[i]], axis=0), bd[i]) for i in K]
                Tm = [Tm[i] + tp[i][:C] for i in K]
                P = [tp[i][C:] for i in K]
            else:
                tp = [_dot(Tm[i].astype(BF16), bd[i]) for i in K]
                Tm = [Tm[i] + tp[i] for i in K]
        rhs, kst, qst, gam = [], [], [], []
        for i in K:
            qf = q[i].astype(F32)
            kf = k[i].astype(F32)
            rhs_i, kst_i, qst_i, gam_i = [], [], [], []
            for g in range(G):
                j = g % 2
                bc, gc = beta_c[i][g], gc_c[i][g]
                e_c = jnp.exp(gc)
                rhs_i.append(jnp.concatenate([v[i][:, j * DK:(j + 1) * DK] * bc, kf * (bc * e_c)], axis=1))
                g_last = gc[0:1] if g >= 2 else gc[C - 1:C]
                kst_i.append((kf * jnp.exp(g_last - gc)).astype(BF16))
                qst_i.append((qf * (scale * e_c)).astype(BF16))
                gam_i.append(jnp.broadcast_to(jnp.exp(g_last), (1, DK)))
            rhs.append(jnp.concatenate(rhs_i, axis=0).astype(BF16))
            kst.append(kst_i)
            qst.append(qst_i)
            gam.append(gam_i)
        sol = [_dot(blockdiag(Tm[i].astype(BF16)), rhs[i]) for i in K]
        for i in K:
            hd, n = items[i]
            u_s[hd, n] = sol[i][:, :DK]
            qkd_s[hd, n] = jnp.where(incl4, qk4[i] * decay4[i], 0.0).astype(BF16)
            for g in range(G):
                wq_s[hd, n, g, 0:C, :] = sol[i][g * C:(g + 1) * C, DK:].astype(BF16)
                wq_s[hd, n, g, C:2 * C, :] = qst[i][g]
                kst_s[hd, n, g] = kst[i][g]
                gam_s[hd, n, g:g + 1, :] = gam[i][g]

    def phase1_group(i, _):
        phase1([P1_GROUP * i + c for c in range(P1_GROUP)])
        return 0

    lax.fori_loop(0, NC // P1_GROUP, phase1_group, 0)
    if NC % P1_GROUP:
        phase1(list(range(NC - NC % P1_GROUP, NC)))

    nw = nw_ref[...]

    def gated_norm(o, z):
        ys = []
        for j in range(2 * HPS):
            oj = o[:, j * DK:(j + 1) * DK]
            zj = z[:, j * DK:(j + 1) * DK].astype(F32)
            ys.append(oj * lax.rsqrt(jnp.mean(oj * oj, axis=-1, keepdims=True) + NORM_EPS) * nw * (zj * _sigmoid(zj)))
        return jnp.concatenate(ys, axis=1)

    def finalize(n):
        r0 = pl.multiple_of(n * C, C)
        o_ref[0, pl.ds(r0, C), :] = gated_norm(oacc_s[pl.ds(r0, C), :], z_ref[0, pl.ds(r0, C), :]).astype(o_ref.dtype)

    def phase2(it, finalize_previous):
        if finalize_previous:
            finalize(it - 1)
            finalize(NC - it)
        ops = []
        for hd in range(HPS):
            for g in range(G):
                n = it if g < 2 else NC - 1 - it
                ops.append((S_s[hd * G + g], wq_s[hd, n, g], u_s[hd, n, g * C:(g + 1) * C, :],
                            qkd_s[hd, n, :, g * C:(g + 1) * C], gam_s[hd, n, g:g + 1, :], kst_s[hd, n, g]))
        NI = HPS * G
        r_f = pl.multiple_of(it * C, C)
        r_b = pl.multiple_of((NC - 1 - it) * C, C)
        o_f = oacc_s[pl.ds(r_f, C), :]
        ws = [_dot(wq, S.astype(BF16)) for (S, wq, u, qkd, gam, kst) in ops]
        vb = [(ops[g][2] - ws[g][:C]).astype(BF16) for g in range(NI)]
        outs = [ws[g][C:] + _dot(ops[g][3], vb[g]) for g in range(NI)]
        states = [ops[g][0] * ops[g][4] + _dot_tn(ops[g][5], vb[g]) for g in range(NI)]
        for g in range(NI):
            S_s[g] = states[g]
        fwd = [outs[hd * G + j] for hd in range(HPS) for j in range(2)]
        bwd = [outs[hd * G + 2 + j] for hd in range(HPS) for j in range(2)]
        o_f = o_f + jnp.concatenate(fwd, axis=1)
        oacc_s[pl.ds(r_f, C), :] = o_f
        oacc_s[pl.ds(r_b, C), :] = oacc_s[pl.ds(r_b, C), :] + jnp.concatenate(bwd, axis=1)
        return 0

    lax.fori_loop(0, NC // 2 + 1, lambda it, c: phase2(it, False), 0)
    lax.fori_loop(NC // 2 + 1, NC, lambda it, c: phase2(it, True), 0)
    finalize(NC - 1)
    finalize(0)


_GDN_CHUNK_BYTES = CHUNK * GDN_DK * (4 * 4 + 8 * 2 + 4 * 2 + 4 * 2) + 8 * GDN_DK * 4
_GDN_PAIR_BUDGET = 26 * 1024 * 1024


def _gdn_delta(qkv, z, bg, norm_w):
    B, Tp, _ = qkv.shape
    NC = Tp // CHUNK
    KH = GDN_QK_HEADS
    HV = GDN_V_HEADS
    C = CHUNK
    HPS = 2 if NC * _GDN_CHUNK_BYTES * 2 <= _GDN_PAIR_BUDGET else 1
    NH = KH // HPS
    bgr = bg.reshape(B, NC, CHUNK, 4, KH, 2).transpose(0, 1, 4, 3, 5, 2).reshape(B, NC, KH, 8, CHUNK)
    return pl.pallas_call(
        functools.partial(_gdn_kernel, Tp=Tp, HPS=HPS),
        grid=(B, NH),
        in_specs=[pl.BlockSpec((1, Tp, HPS * GDN_DK), lambda b, h: (b, 0, h)),
                  pl.BlockSpec((1, Tp, HPS * GDN_DK), lambda b, h: (b, 0, NH + h)),
                  pl.BlockSpec((1, Tp, 2 * HPS * GDN_DK), lambda b, h: (b, 0, NH + h)),
                  pl.BlockSpec((1, Tp, 2 * HPS * GDN_DK), lambda b, h: (b, 0, h)),
                  pl.BlockSpec((1, Tp, 4 * HV), lambda b, h: (b, 0, 0)),
                  pl.BlockSpec((1, NC, HPS, 8, CHUNK), lambda b, h: (b, 0, h, 0, 0)),
                  pl.BlockSpec((1, GDN_DK), lambda b, h: (0, 0))],
        out_specs=pl.BlockSpec((1, Tp, 2 * HPS * GDN_DK), lambda b, h: (b, 0, h)),
        out_shape=jax.ShapeDtypeStruct((B, Tp, GDN_VAL_DIM), BF16),
        scratch_shapes=[pltpu.VMEM((4 * HPS, GDN_DK, GDN_DK), F32),
                        pltpu.VMEM((Tp, 2 * HPS * GDN_DK), F32),
                        pltpu.VMEM((HPS, NC, 4 * C, GDN_DK), F32),
                        pltpu.VMEM((HPS, NC, 4, 2 * C, GDN_DK), BF16),
                        pltpu.VMEM((HPS, NC, 4, C, GDN_DK), BF16),
                        pltpu.VMEM((HPS, NC, C, 4 * C), BF16),
                        pltpu.VMEM((HPS, NC, 8, GDN_DK), F32)],
        compiler_params=_cparams("parallel", "parallel"),
    )(qkv, qkv, qkv, z, bg, bgr, norm_w.astype(F32).reshape(1, GDN_DK))


def _gdn_mixer(hn, h, B, Tp, w_in, conv_w, a_log, dt_bias, norm_w, w_out, nw):
    nz = GDN_CONV_DIM + GDN_VAL_DIM
    w_in = w_in.astype(BF16)
    qkv = _proj_conv(hn, w_in[:, :GDN_CONV_DIM], conv_w, Tp, 2 * GDN_KEY_DIM)
    z = _matmul(hn, w_in[:, GDN_CONV_DIM:nz], BF16)
    bg = _gdn_gates(hn, w_in[:, nz:], a_log, dt_bias)
    o = _gdn_delta(qkv.reshape(B, Tp, -1), z.reshape(B, Tp, -1), bg.reshape(B, Tp, -1), norm_w)
    return _proj_res(o.reshape(B * Tp, -1), w_out.astype(BF16), h, nw)


def _trunk(x, meta_tokens, mix_norm, ffn_norm, final_norm, rg, na, gdn, ffn):
    B, T, D = x.shape
    Tp = T + HEAD_ROWS
    depth = mix_norm.shape[0]
    h3 = jnp.concatenate([jnp.zeros((B, PAD, D), F32),
                          jnp.broadcast_to(meta_tokens.astype(F32)[None], (B, N_META, D)),
                          x.astype(F32)], axis=1)
    h = h3.reshape(B * Tp, D)
    hn = _rmsnorm(h, mix_norm[0], BF16)
    for i in range(depth):
        kind, j = i % 3, i // 3
        if kind == 0:
            h, hn = _rglru_mixer(hn, h, B, Tp, *[p[j] for p in rg], ffn_norm[i])
        elif kind == 1:
            h, hn = _na_mixer(hn, h, B, Tp, *[p[j] for p in na], ffn_norm[i])
        else:
            h, hn = _gdn_mixer(hn, h, B, Tp, *[p[j] for p in gdn], ffn_norm[i])
        w_gate, w_up, conv_w, conv_b, w_down = [p[i] for p in ffn]
        next_norm = mix_norm[i + 1] if i + 1 < depth else final_norm
        h, hn = _conv_ffn(hn, h, w_gate.astype(BF16), w_up.astype(BF16), conv_w, conv_b,
                          w_down.astype(BF16), next_norm)
    return _final_norm(h, final_norm, B, T)


def kernel(x_prompt, x_sample, meta_tokens, mix_norm, ffn_norm, final_norm, rg_w_in, rg_conv_w, rg_conv_b, rg_w_a, rg_b_a, rg_w_i, rg_b_i, rg_lam, rg_w_out, na_w_qkv, na_rpb, na_meta_bias, na_w_o, gdn_w_in, gdn_conv_w, gdn_a_log, gdn_dt_bias, gdn_norm_w, gdn_w_out, ffn_w_gate, ffn_w_up, ffn_conv_w, ffn_conv_b, ffn_w_down):
    rg = (rg_w_in, rg_conv_w, rg_conv_b, rg_w_a, rg_b_a, rg_w_i, rg_b_i, rg_lam, rg_w_out)
    na_bias = jnp.stack([_na_bias_table(na_rpb[j]) for j in range(na_rpb.shape[0])])
    na = (na_w_qkv, na_bias, na_meta_bias, na_w_o)
    gdn = (gdn_w_in, gdn_conv_w, gdn_a_log, gdn_dt_bias, gdn_norm_w, gdn_w_out)
    ffn = (ffn_w_gate, ffn_w_up, ffn_conv_w, ffn_conv_b, ffn_w_down)
    y_prompt = _trunk(x_prompt, meta_tokens, mix_norm, ffn_norm, final_norm, rg, na, gdn, ffn)
    y_sample = _trunk(x_sample, meta_tokens, mix_norm, ffn_norm, final_norm, rg, na, gdn, ffn)
    return (y_prompt, y_sample)
```

```python
import functools

import numpy as np
import jax
import jax.numpy as jnp
from jax import lax
from jax.experimental import pallas as pl
from jax.experimental.pallas import tpu as pltpu

D_MODEL = 2048
N_META = 16
PAD = 48
HEAD_ROWS = PAD + N_META
CHUNK = 64
GRID_W = 64
NORM_EPS = 1e-6

RG_BLOCK = 256
RG_C = 8.0

NA_HEAD_DIM = 128
NA_HEADS = D_MODEL // NA_HEAD_DIM
NA_WIN_R = 8
NA_WIN_C = 16
NA_ROW_GROUP = 32
NEG_INF = -1e30

GDN_DK = 128
GDN_QK_HEADS = D_MODEL // GDN_DK
GDN_V_HEADS = 2 * GDN_QK_HEADS
GDN_KEY_DIM = GDN_QK_HEADS * GDN_DK
GDN_VAL_DIM = GDN_V_HEADS * GDN_DK
GDN_CONV_DIM = 2 * GDN_KEY_DIM + GDN_VAL_DIM

V7X_VMEM_LIMIT = 56 * 1024 * 1024
HALO = 16

BF16 = jnp.bfloat16
F32 = jnp.float32


def _cparams(*sem):
    return pltpu.CompilerParams(dimension_semantics=sem, vmem_limit_bytes=V7X_VMEM_LIMIT)


def _pick_tile(n, cap, mult):
    best = None
    for t in range(mult, cap + 1, mult):
        if n % t == 0:
            best = t
    assert best is not None, (n, cap, mult)
    return best


def _sigmoid(x):
    return 1.0 / (1.0 + jnp.exp(-x))


def _softplus(x):
    return jnp.maximum(x, 0.0) + jnp.log(1.0 + jnp.exp(-jnp.abs(x)))


def _gelu_tanh(x):
    return 0.5 * x * (1.0 + jnp.tanh(0.7978845608028654 * (x + 0.044715 * x * x * x)))


def _rms_scale(x, w):
    ms = jnp.mean(x * x, axis=-1, keepdims=True)
    return x * lax.rsqrt(ms + NORM_EPS) * w


def _dot(a, b):
    return jnp.dot(a, b, preferred_element_type=F32)


def _dot_nt(a, b):
    return lax.dot_general(a, b, (((1,), (1,)), ((), ())), preferred_element_type=F32)


def _dot_tn(a, b):
    return lax.dot_general(a, b, (((0,), (0,)), ((), ())), preferred_element_type=F32)


def _rmsnorm_kernel(x_ref, w_ref, o_ref):
    o_ref[...] = _rms_scale(x_ref[...], w_ref[...]).astype(o_ref.dtype)


def _rmsnorm(h, w, out_dtype):
    M, D = h.shape
    tm = _pick_tile(M, 1024, 64)
    return pl.pallas_call(
        _rmsnorm_kernel,
        grid=(M // tm,),
        in_specs=[pl.BlockSpec((tm, D), lambda i: (i, 0)), pl.BlockSpec((1, D), lambda i: (0, 0))],
        out_specs=pl.BlockSpec((tm, D), lambda i: (i, 0)),
        out_shape=jax.ShapeDtypeStruct((M, D), out_dtype),
        compiler_params=_cparams("parallel"),
    )(h, w.reshape(1, D))


def _final_norm_kernel(x_ref, w_ref, o_ref):
    o_ref[0] = _rms_scale(x_ref[...], w_ref[...])


def _final_norm(h, w, B, T):
    D = h.shape[1]
    Tp = T + HEAD_ROWS
    tt = 512
    assert T % tt == 0
    return pl.pallas_call(
        _final_norm_kernel,
        grid=(B, T // tt),
        in_specs=[pl.BlockSpec((pl.Element(tt), pl.Element(D)),
                               lambda b, j: (pl.multiple_of(b * Tp + HEAD_ROWS + j * tt, CHUNK), 0)),
                  pl.BlockSpec((1, D), lambda b, j: (0, 0))],
        out_specs=pl.BlockSpec((1, tt, D), lambda b, j: (b, j, 0)),
        out_shape=jax.ShapeDtypeStruct((B, T, D), F32),
        compiler_params=_cparams("parallel", "parallel"),
    )(h, w.reshape(1, D))


def _matmul_kernel(x_ref, w_ref, o_ref, *, act):
    acc = _dot(x_ref[...], w_ref[...])
    if act == "gelu":
        acc = _gelu_tanh(acc)
    o_ref[...] = acc.astype(o_ref.dtype)


def _matmul(x, w, out_dtype, act=None):
    M, K = x.shape
    N = w.shape[1]
    tm = _pick_tile(M, 1280, 128)
    tn = _pick_tile(N, 1024, 128)
    return pl.pallas_call(
        functools.partial(_matmul_kernel, act=act),
        grid=(M // tm, N // tn),
        in_specs=[pl.BlockSpec((tm, K), lambda i, j: (i, 0)), pl.BlockSpec((K, tn), lambda i, j: (0, j))],
        out_specs=pl.BlockSpec((tm, tn), lambda i, j: (i, j)),
        out_shape=jax.ShapeDtypeStruct((M, N), out_dtype),
        compiler_params=_cparams("parallel", "parallel"),
    )(x, w)


def _proj_res_kernel(x_ref, w_ref, h_ref, nw_ref, hout_ref, hn_ref):
    hnew = h_ref[...] + _dot(x_ref[...], w_ref[...])
    hout_ref[...] = hnew
    hn_ref[...] = _rms_scale(hnew, nw_ref[...]).astype(hn_ref.dtype)


def _proj_res(x, w, h, nw):
    M, K = x.shape
    D = w.shape[1]
    tm = _pick_tile(M, 704 if K <= D_MODEL else 512, 64)
    return pl.pallas_call(
        _proj_res_kernel,
        grid=(M // tm,),
        in_specs=[pl.BlockSpec((tm, K), lambda i: (i, 0)),
                  pl.BlockSpec((K, D), lambda i: (0, 0), pipeline_mode=pl.Buffered(1)),
                  pl.BlockSpec((tm, D), lambda i: (i, 0)),
                  pl.BlockSpec((1, D), lambda i: (0, 0))],
        out_specs=[pl.BlockSpec((tm, D), lambda i: (i, 0)), pl.BlockSpec((tm, D), lambda i: (i, 0))],
        out_shape=[jax.ShapeDtypeStruct((M, D), F32), jax.ShapeDtypeStruct((M, D), BF16)],
        compiler_params=_cparams("parallel"),
    )(x, w, h, nw.reshape(1, D))


def _proj_conv_kernel(xp_ref, xm_ref, xn_ref, w_ref, cw_ref, o_ref, xext_ref, *, tm, tc, ni, Tp, n_norm):
    i = pl.program_id(0)
    j = pl.program_id(1)
    npc = o_ref.shape[1] // tc

    @pl.when(j == 0)
    def _():
        xext_ref[0:HALO, :] = xp_ref[...]
        xext_ref[HALO:HALO + tm, :] = xm_ref[...]
        xext_ref[HALO + tm:2 * HALO + tm, :] = xn_ref[...]

    @pl.when(jnp.logical_and(j == 0, i == ni - 1))
    def _():
        xext_ref[HALO + tm:2 * HALO + tm, :] = jnp.zeros((HALO, xext_ref.shape[1]), xext_ref.dtype)

    rel = i * tm - (i * tm // Tp) * Tp + lax.broadcasted_iota(jnp.int32, (tm, 1), 0)
    live = jnp.where(rel >= Tp, rel - Tp, rel) >= PAD
    cw = cw_ref[...]
    xe = xext_ref[...]

    def piece_matmul(c):
        return _dot(xe, w_ref[:, c * tc:(c + 1) * tc])

    def piece_finish(c, g):
        cwc = cw[:, c * tc:(c + 1) * tc]
        a = (cwc[0:1] * g[HALO - 1:HALO - 1 + tm] + cwc[1:2] * g[HALO:HALO + tm]
             + cwc[2:3] * g[HALO + 1:HALO + 1 + tm] + cwc[3:4] * g[HALO + 2:HALO + 2 + tm])
        y = jnp.where(live, a * _sigmoid(a), 0.0)
        normalise = j * npc + c < n_norm
        heads = []
        for hh in range(tc // GDN_DK):
            yh = y[:, hh * GDN_DK:(hh + 1) * GDN_DK]
            nrm = lax.rsqrt(jnp.sum(yh * yh, axis=-1, keepdims=True) + 1e-6)
            heads.append(yh * jnp.where(normalise, nrm, 1.0))
        o_ref[:, c * tc:(c + 1) * tc] = jnp.concatenate(heads, axis=1).astype(o_ref.dtype)

    g = piece_matmul(0)
    for c in range(npc):
        g_next = piece_matmul(c + 1) if c + 1 < npc else None
        piece_finish(c, g)
        g = g_next


def _proj_conv(x, w, conv_w, Tp, n_norm_cols):
    M, K = x.shape
    N = w.shape[1]
    tm = _pick_tile(M, 1024, 64)
    assert tm <= Tp
    tn, tc = 2048, 512
    ni = M // tm
    hb = tm // HALO
    nhb = M // HALO
    return pl.pallas_call(
        functools.partial(_proj_conv_kernel, tm=tm, tc=tc, ni=ni, Tp=Tp, n_norm=n_norm_cols // tc),
        grid=(ni, N // tn),
        in_specs=[pl.BlockSpec((HALO, K), lambda i, j: (jnp.maximum(i * hb - 1, 0), 0)),
                  pl.BlockSpec((tm, K), lambda i, j: (i, 0)),
                  pl.BlockSpec((HALO, K), lambda i, j: (jnp.minimum((i + 1) * hb, nhb - 1), 0)),
                  pl.BlockSpec((K, tn), lambda i, j: (0, j)),
                  pl.BlockSpec((4, tn), lambda i, j: (0, j))],
        out_specs=pl.BlockSpec((tm, tn), lambda i, j: (i, j)),
        out_shape=jax.ShapeDtypeStruct((M, N), BF16),
        scratch_shapes=[pltpu.VMEM((tm + 2 * HALO, K), BF16)],
        compiler_params=_cparams("parallel", "arbitrary"),
    )(x, x, x, w, conv_w)


def _ffn_kernel(xp_ref, xm_ref, xn_ref, wg_ref, wu_ref, cw_ref, cb_ref, wd_ref, h_ref, nw_ref,
                hout_ref, hn_ref, xext_ref, *, tm, nf, ni):
    i = pl.program_id(0)
    f = pl.program_id(1)
    th = tm // 2

    @pl.when(f == 0)
    def _():
        xext_ref[0:HALO, :] = xp_ref[...]
        xext_ref[HALO:HALO + tm, :] = xm_ref[...]
        xext_ref[HALO + tm:2 * HALO + tm, :] = xn_ref[...]
        hout_ref[...] = h_ref[...]

    cw = cw_ref[...]
    cb = cb_ref[...]
    g, up = [], []
    for s in range(2):
        g.append(_dot(xext_ref[s * th:s * th + th + 2 * HALO, :], wg_ref[...]))
        up.append(_dot(xm_ref[s * th:(s + 1) * th, :], wu_ref[...]))
    row = lax.broadcasted_iota(jnp.int32, (th, 1), 0)
    y = []
    for s in range(2):
        g_prev = g[s][HALO - 1:HALO - 1 + th]
        g_here = g[s][HALO:HALO + th]
        g_next = g[s][HALO + 1:HALO + 1 + th]
        if s == 1:
            g_next = jnp.where(jnp.logical_and(i == ni - 1, row == th - 1), 0.0, g_next)
        a = cw[0:1] * g_prev + cw[1:2] * g_here + cw[2:3] * g_next + cb
        y.append((a * _sigmoid(a) * up[s]).astype(BF16))
    down = [_dot(y[s], wd_ref[...]) for s in range(2)]
    for s in range(2):
        hout_ref[s * th:(s + 1) * th, :] += down[s]

    @pl.when(f == nf - 1)
    def _():
        hn_ref[...] = _rms_scale(hout_ref[...], nw_ref[...]).astype(hn_ref.dtype)


def _conv_ffn(hn, h, w_gate, w_up, conv_w, conv_b, w_down, nw):
    M, D = hn.shape
    F = w_gate.shape[1]
    tm = _pick_tile(M, 704, 64)
    tf = _pick_tile(F, 512, 128)
    ni, nf = M // tm, F // tf
    hb = tm // HALO
    nhb = M // HALO
    return pl.pallas_call(
        functools.partial(_ffn_kernel, tm=tm, nf=nf, ni=ni),
        grid=(ni, nf),
        in_specs=[pl.BlockSpec((HALO, D), lambda i, f: (jnp.maximum(i * hb - 1, 0), 0)),
                  pl.BlockSpec((tm, D), lambda i, f: (i, 0)),
                  pl.BlockSpec((HALO, D), lambda i, f: (jnp.minimum((i + 1) * hb, nhb - 1), 0)),
                  pl.BlockSpec((D, tf), lambda i, f: (0, f)),
                  pl.BlockSpec((D, tf), lambda i, f: (0, f)),
                  pl.BlockSpec((3, tf), lambda i, f: (0, f)),
                  pl.BlockSpec((1, tf), lambda i, f: (0, f)),
                  pl.BlockSpec((tf, D), lambda i, f: (f, 0)),
                  pl.BlockSpec((tm, D), lambda i, f: (i, 0)),
                  pl.BlockSpec((1, D), lambda i, f: (0, 0))],
        out_specs=[pl.BlockSpec((tm, D), lambda i, f: (i, 0)), pl.BlockSpec((tm, D), lambda i, f: (i, 0))],
        out_shape=[jax.ShapeDtypeStruct((M, D), F32), jax.ShapeDtypeStruct((M, D), BF16)],
        scratch_shapes=[pltpu.VMEM((tm + 2 * HALO, D), BF16)],
        compiler_params=_cparams("parallel", "arbitrary"),
    )(hn, hn, hn, w_gate, w_up, conv_w, conv_b.reshape(1, F), w_down, h, nw.reshape(1, D))


def _scan8(a, u, row, reverse):
    for s in (1, 2, 4):
        if reverse:
            a_sh = pltpu.roll(a, 8 - s, 0)
            u_sh = pltpu.roll(u, 8 - s, 0)
            m = row < 8 - s
        else:
            a_sh = pltpu.roll(a, s, 0)
            u_sh = pltpu.roll(u, s, 0)
            m = row >= s
        u = jnp.where(m, a * u_sh + u, u)
        a = jnp.where(m, a * a_sh, a)
    return a, u


def _rg_kernel(rec_ref, gate_ref, cw_ref, cb_ref, wa_ref, ba_ref, wi_ref, bi_ref, lam_ref, o_ref,
               xc_s, hf_s, hb_s, *, Tp, rc):
    nch = Tp // rc
    W = RG_BLOCK
    T8 = rc // 8
    cw = cw_ref[...]
    cb = cb_ref[...]
    row8 = lax.broadcasted_iota(jnp.int32, (8, W), 0)

    def rows(c):
        return pl.multiple_of(c * rc, CHUNK)

    def conv(r0):
        top = rec_ref[0, pl.ds(pl.multiple_of(jnp.maximum(r0 - 8, 0), 8), 8), :]
        main = rec_ref[0, pl.ds(r0, rc), :]
        bot = rec_ref[0, pl.ds(pl.multiple_of(jnp.minimum(r0 + rc, Tp - 8), 8), 8), :]
        bot = jnp.where(r0 + rc < Tp, bot, 0.0)
        xw = jnp.concatenate([top, main, bot], axis=0)
        xc = cw[0:1] * xw[7:7 + rc] + cw[1:2] * xw[8:8 + rc] + cw[2:3] * xw[9:9 + rc] + cw[3:4] * xw[10:10 + rc] + cb
        t = r0 + lax.broadcasted_iota(jnp.int32, (rc, 1), 0)
        return jnp.where(t >= PAD, xc, 0.0)

    def advance(x_f, x_b, carry):
        xs = (x_f, x_b)
        xb16 = [x.astype(BF16) for x in xs]
        pre = [(_dot(xb16[d], wa_ref[d, 0]), _dot(xb16[d], wi_ref[d, 0])) for d in range(2)]
        au = []
        for d in range(2):
            r = _sigmoid(pre[d][0] + ba_ref[d])
            ig = _sigmoid(pre[d][1] + bi_ref[d])
            a = jnp.exp((-RG_C) * r * _softplus(-lam_ref[d]))
            au.append((a, jnp.sqrt(1.0 - a * a) * (ig * xs[d])))
        c_f, c_b = carry
        h_f, h_b = [None] * T8, [None] * T8
        for k in range(T8):
            A, U = _scan8(au[0][0][8 * k:8 * k + 8], au[0][1][8 * k:8 * k + 8], row8, False)
            h_f[k] = A * c_f + U
            c_f = h_f[k][7:8]
            kb = T8 - 1 - k
            A, U = _scan8(au[1][0][8 * kb:8 * kb + 8], au[1][1][8 * kb:8 * kb + 8], row8, True)
            h_b[kb] = A * c_b + U
            c_b = h_b[kb][0:1]
        return jnp.concatenate(h_f, axis=0), jnp.concatenate(h_b, axis=0), (c_f, c_b)

    def emit(r0, hsum):
        o_ref[0, pl.ds(r0, rc), :] = (gate_ref[0, pl.ds(r0, rc), :].astype(F32) * hsum).astype(o_ref.dtype)

    def first_half(c, carry):
        r_f, r_b = rows(c), rows(nch - 1 - c)
        x_f, x_b = conv(r_f), conv(r_b)
        h_f, h_b, carry = advance(x_f, x_b, carry)
        xc_s[pl.ds(r_f, rc), :] = x_f
        xc_s[pl.ds(r_b, rc), :] = x_b
        hf_s[pl.ds(r_f, rc), :] = h_f
        hb_s[pl.ds(r_b, rc), :] = h_b
        return carry

    carry = lax.fori_loop(0, nch // 2, first_half, (jnp.zeros((1, W), F32), jnp.zeros((1, W), F32)))

    if nch % 2:
        r_m = (nch // 2) * rc
        x_m = conv(r_m)
        h_f, h_b, carry = advance(x_m, x_m, carry)
        emit(r_m, h_f + h_b)

    def second_half(c, carry):
        r_f, r_b = rows(c), rows(nch - 1 - c)
        h_f, h_b, carry = advance(xc_s[pl.ds(r_f, rc), :], xc_s[pl.ds(r_b, rc), :], carry)
        emit(r_f, h_f + hb_s[pl.ds(r_f, rc), :])
        emit(r_b, hf_s[pl.ds(r_b, rc), :] + h_b)
        return carry

    lax.fori_loop((nch + 1) // 2, nch, second_half, carry)


def _rg_scan(rec, gate, conv_w, conv_b, w_a, b_a, w_i, b_i, lam):
    B, Tp, W = rec.shape
    nb = W // RG_BLOCK
    rc = _pick_tile(Tp, 384, 64)
    blk = lambda b, n: (b, 0, n)
    vec = lambda b, n: (0, n)
    vec3 = lambda b, n: (0, 0, n)
    return pl.pallas_call(
        functools.partial(_rg_kernel, Tp=Tp, rc=rc),
        grid=(B, nb),
        in_specs=[pl.BlockSpec((1, Tp, RG_BLOCK), blk),
                  pl.BlockSpec((1, Tp, RG_BLOCK), blk),
                  pl.BlockSpec((4, RG_BLOCK), vec),
                  pl.BlockSpec((1, RG_BLOCK), vec),
                  pl.BlockSpec((2, 1, RG_BLOCK, RG_BLOCK), lambda b, n: (0, n, 0, 0)),
                  pl.BlockSpec((2, 1, RG_BLOCK), vec3),
                  pl.BlockSpec((2, 1, RG_BLOCK, RG_BLOCK), lambda b, n: (0, n, 0, 0)),
                  pl.BlockSpec((2, 1, RG_BLOCK), vec3),
                  pl.BlockSpec((2, 1, RG_BLOCK), vec3)],
        out_specs=pl.BlockSpec((1, Tp, RG_BLOCK), blk),
        out_shape=jax.ShapeDtypeStruct((B, Tp, W), BF16),
        scratch_shapes=[pltpu.VMEM((Tp, RG_BLOCK), F32)] * 3,
        compiler_params=_cparams("parallel", "parallel"),
    )(rec, gate, conv_w, conv_b.reshape(1, W), w_a.astype(BF16), b_a.reshape(2, 1, W),
      w_i.astype(BF16), b_i.reshape(2, 1, W), lam.reshape(2, 1, W))


def _rglru_mixer(hn, h, B, Tp, w_in, conv_w, conv_b, w_a, b_a, w_i, b_i, lam, w_out, nw):
    W = D_MODEL
    w_in = w_in.astype(BF16)
    gate = _matmul(hn, w_in[:, :W], BF16, act="gelu")
    rec = _matmul(hn, w_in[:, W:], F32)
    y = _rg_scan(rec.reshape(B, Tp, W), gate.reshape(B, Tp, W), conv_w, conv_b, w_a, b_a, w_i, b_i, lam)
    return _proj_res(y.reshape(B * Tp, W), w_out.astype(BF16), h, nw)


def _na_kernel(q_ref, k_ref, v_ref, bias_ref, mb_ref, o_ref, *, rows):
    scale = NA_HEAD_DIM ** -0.5
    kh = NA_WIN_R
    mb = mb_ref[0]
    km = k_ref[0, PAD:HEAD_ROWS, :]
    vm = v_ref[0, PAD:HEAD_ROWS, :]

    o_ref[0, 0:PAD, :] = jnp.zeros((PAD, NA_HEAD_DIM), o_ref.dtype)
    qm = q_ref[0, PAD:HEAD_ROWS, :]

    group = np.gcd(rows, NA_ROW_GROUP)
    R = range(group)

    def group_body(i, _):
        r = [i * group + c for c in R]
        rs = [jnp.clip(r[c] - NA_WIN_R // 2, 0, rows - kh) for c in R]
        q0 = [pl.multiple_of(HEAD_ROWS + r[c] * GRID_W, GRID_W) for c in R]
        k0 = [pl.multiple_of(HEAD_ROWS + rs[c] * GRID_W, GRID_W) for c in R]
        q_r = [q_ref[0, pl.ds(q0[c], GRID_W), :] for c in R]
        s = [_dot_nt(q_r[c], k_ref[0, pl.ds(k0[c], kh * GRID_W), :]) for c in R]
        s_met = [_dot_nt(q_r[c], km) for c in R]
        s_m = _dot_nt(qm, km) * scale + mb
        p_m = jnp.exp(s_m - jnp.max(s_m, axis=-1, keepdims=True))
        p, p_met, l = [], [], []
        for c in R:
            sc = s[c] * scale + bias_ref[0, r[c] - rs[c]]
            sm = s_met[c] * scale + mb
            m = jnp.maximum(jnp.max(sc, axis=-1, keepdims=True), jnp.max(sm, axis=-1, keepdims=True))
            pc = jnp.exp(sc - m)
            pm = jnp.exp(sm - m)
            l.append(jnp.sum(pc, axis=-1, keepdims=True) + jnp.sum(pm, axis=-1, keepdims=True))
            p.append(pc.astype(BF16))
            p_met.append(pm.astype(BF16))
        o = [_dot(p[c], v_ref[0, pl.ds(k0[c], kh * GRID_W), :]) + _dot(p_met[c], vm) for c in R]
        o_m = _dot(p_m.astype(BF16), vm) / jnp.sum(p_m, axis=-1, keepdims=True)
        o_ref[0, PAD:HEAD_ROWS, :] = o_m.astype(o_ref.dtype)
        for c in R:
            o_ref[0, pl.ds(q0[c], GRID_W), :] = (o[c] / l[c]).astype(o_ref.dtype)
        return 0

    lax.fori_loop(0, rows // group, group_body, 0)


def _na_bias_table(rpb):
    kh = NA_WIN_R
    cols = np.arange(GRID_W)
    col_start = np.clip(cols - NA_WIN_C // 2, 0, GRID_W - NA_WIN_C)
    valid = (cols[None, :] >= col_start[:, None]) & (cols[None, :] < col_start[:, None] + NA_WIN_C)
    col_off = np.clip(cols[None, :] - cols[:, None], -(NA_WIN_C - 1), NA_WIN_C - 1) + NA_WIN_C - 1
    H, n_ro, n_co = rpb.shape
    onehot = (col_off[None] == np.arange(n_co)[:, None, None]).astype(np.float32)
    a = jnp.einsum('hrc,cqk->hqrk', rpb.astype(F32), jnp.asarray(onehot), precision=lax.Precision.HIGHEST)
    a = jnp.where(jnp.asarray(valid)[None, :, None, :], a, NEG_INF)
    tabs = [a[:, :, NA_WIN_R - 1 - d:NA_WIN_R - 1 - d + kh, :].reshape(H, GRID_W, kh * GRID_W)
            for d in range(NA_WIN_R)]
    return jnp.stack(tabs, axis=1)


def _na_attention(qkv, bias, meta_bias):
    B, Tp, _ = qkv.shape
    rows = (Tp - HEAD_ROWS) // GRID_W
    assert rows >= NA_WIN_R
    H = NA_HEADS
    return pl.pallas_call(
        functools.partial(_na_kernel, rows=rows),
        grid=(H, B),
        in_specs=[pl.BlockSpec((1, Tp, NA_HEAD_DIM), lambda h, b: (b, 0, h)),
                  pl.BlockSpec((1, Tp, NA_HEAD_DIM), lambda h, b: (b, 0, H + h)),
                  pl.BlockSpec((1, Tp, NA_HEAD_DIM), lambda h, b: (b, 0, 2 * H + h)),
                  pl.BlockSpec((1, NA_WIN_R, GRID_W, NA_WIN_R * GRID_W), lambda h, b: (h, 0, 0, 0)),
                  pl.BlockSpec((1, 1, N_META), lambda h, b: (h, 0, 0))],
        out_specs=pl.BlockSpec((1, Tp, NA_HEAD_DIM), lambda h, b: (b, 0, h)),
        out_shape=jax.ShapeDtypeStruct((B, Tp, H * NA_HEAD_DIM), BF16),
        compiler_params=_cparams("parallel", "parallel"),
    )(qkv, qkv, qkv, bias, meta_bias.astype(F32).reshape(H, 1, N_META))


def _na_mixer(hn, h, B, Tp, w_qkv, bias, meta_bias, w_o, nw):
    qkv = _matmul(hn, w_qkv.astype(BF16), BF16)
    o = _na_attention(qkv.reshape(B, Tp, -1), bias, meta_bias)
    return _proj_res(o.reshape(B * Tp, -1), w_o.astype(BF16), h, nw)


def _gdn_gate_kernel(x_ref, w_ref, al_ref, dt_ref, o_ref):
    y = _dot(x_ref[...], w_ref[...])
    lane = lax.broadcasted_iota(jnp.int32, y.shape, 1)
    g = -jnp.exp(al_ref[...]) * _softplus(y + dt_ref[...])
    o_ref[...] = jnp.where(lane < 2 * GDN_V_HEADS, _sigmoid(y), g)


def _gdn_gates(hn, w_ba, a_log, dt_bias):
    M, K = hn.shape
    N = 4 * GDN_V_HEADS
    tm = _pick_tile(M, 1024, 128)
    zeros = jnp.zeros((2 * GDN_V_HEADS,), F32)
    al = jnp.concatenate([zeros, a_log.astype(F32).reshape(-1)]).reshape(1, N)
    dt = jnp.concatenate([zeros, dt_bias.astype(F32).reshape(-1)]).reshape(1, N)
    return pl.pallas_call(
        _gdn_gate_kernel,
        grid=(M // tm,),
        in_specs=[pl.BlockSpec((tm, K), lambda i: (i, 0)), pl.BlockSpec((K, N), lambda i: (0, 0)),
                  pl.BlockSpec((1, N), lambda i: (0, 0)), pl.BlockSpec((1, N), lambda i: (0, 0))],
        out_specs=pl.BlockSpec((tm, N), lambda i: (i, 0)),
        out_shape=jax.ShapeDtypeStruct((M, N), F32),
        compiler_params=_cparams("parallel"),
    )(hn, w_ba, al, dt)


def _split3_dot(x, m):
    hi = x.astype(BF16)
    r1 = x - hi.astype(F32)
    mid = r1.astype(BF16)
    lo = (r1 - mid.astype(F32)).astype(BF16)
    return _dot(hi, m) + _dot(mid, m) + _dot(lo, m)


def _gdn_kernel(q_ref, k_ref, v_ref, z_ref, bgc_ref, bgr_ref, nw_ref, o_ref,
                S_s, oacc_s, u_s, wq_s, kst_s, qkd_s, gam_s, *, Tp, HPS):
    hb = pl.program_id(1)
    C = CHUNK
    NC = Tp // C
    HV = GDN_V_HEADS
    DK = GDN_DK
    G = 4
    W = G * C
    P1_GROUP = 10 if HPS == 1 else 5
    scale = DK ** -0.5

    oacc_s[...] = jnp.zeros_like(oacc_s)
    S_s[...] = jnp.zeros_like(S_s)

    ri = lax.broadcasted_iota(jnp.int32, (C, W), 0)
    cl = lax.broadcasted_iota(jnp.int32, (C, W), 1)
    cj = jnp.bitwise_and(cl, C - 1)
    blk = jnp.right_shift(cl, 6)
    dd = jnp.where(blk >= 2, cj - ri, ri - cj)
    incl4 = dd >= 0
    strict4 = dd > 0
    eye4 = (ri == cj).astype(F32)
    cum4 = jnp.where(dd <= 0, 1.0, 0.0).astype(BF16)
    bdm = jnp.where(jnp.right_shift(lax.broadcasted_iota(jnp.int32, (W, W), 0), 6)
                    == jnp.right_shift(lax.broadcasted_iota(jnp.int32, (W, W), 1), 6), 1.0, 0.0).astype(BF16)
    blk_row = jnp.right_shift(lax.broadcasted_iota(jnp.int32, (1, W), 1), 6)
    lane = lax.broadcasted_iota(jnp.int32, (C, 4 * HV), 1)
    rowi = lax.broadcasted_iota(jnp.int32, (C, 4 * HV), 0)

    def col(x, c):
        return jnp.sum(jnp.where(lane == c, x, 0.0), axis=1, keepdims=True)

    def by_block(parts, b):
        return jnp.where(b == 0, parts[0], jnp.where(b == 1, parts[1], jnp.where(b == 2, parts[2], parts[3])))

    def blockdiag(xb):
        return jnp.concatenate([xb] * G, axis=0) * bdm

    def phase1(ns):
        items = [(hd, n) for n in ns for hd in range(HPS)]
        K = range(len(items))
        r0 = [pl.multiple_of(n * C, C) for (hd, n) in items]
        q = [q_ref[0, pl.ds(r0[i], C), items[i][0] * DK:(items[i][0] + 1) * DK] for i in K]
        k = [k_ref[0, pl.ds(r0[i], C), items[i][0] * DK:(items[i][0] + 1) * DK] for i in K]
        bg = [bgc_ref[0, pl.ds(r0[i], C), :] for i in K]
        v = [v_ref[0, pl.ds(r0[i], C), items[i][0] * 2 * DK:(items[i][0] + 1) * 2 * DK].astype(F32) for i in K]
        rows = [bgr_ref[0, n, hd] for (hd, n) in items]
        cum_rows = [_split3_dot(rows[i], cum4) for i in K]
        k4 = [jnp.concatenate([k[i]] * G, axis=0) for i in K]
        kk4 = [_dot_nt(k[i], k4[i]) for i in K]
        qk4 = [_dot_nt(q[i], k4[i]) * scale for i in K]
        beta_c, gc_c, decay4, L = [], [], [], []
        for i in K:
            pre = bg[i]
            for s in (1, 2, 4, 8, 16, 32):
                pre = pre + jnp.where(rowi >= s, pltpu.roll(pre, s, 0), 0.0)
            suf = pre[C - 1:C] - pre + bg[i]
            gc_r = by_block([cum_rows[i][4 + g:5 + g] for g in range(G)], blk_row)
            bc, gc = [], []
            for g in range(G):
                d, j = g // 2, g % 2
                vh = 2 * (hb * HPS + items[i][0]) + j
                bc.append(col(bg[i], d * HV + vh))
                gc.append(col(suf if d else pre, 2 * HV + d * HV + vh))
            beta_c.append(bc)
            gc_c.append(gc)
            dec = jnp.where(incl4, jnp.exp(jnp.where(incl4, by_block(gc, blk) - gc_r, 0.0)), 0.0)
            decay4.append(dec)
            L.append(jnp.where(strict4, kk4[i] * dec, 0.0) * by_block(bc, blk))
        Tm = [eye4 - L[i] for i in K]
        Lb = [L[i].astype(BF16) for i in K]
        P = [_dot(Lb[i], blockdiag(Lb[i])) for i in K]
        for lvl in range(5):
            Pb = [P[i].astype(BF16) for i in K]
            bd = [blockdiag(Pb[i]) for i in K]
            if lvl < 4:
                tp = [_dot(jnp.concatenate([Tm[i].astype(BF16), Pb[i]], axis=0), bd[i]) for i in K]
                Tm = [Tm[i] + tp[i][:C] for i in K]
                P = [tp[i][C:] for i in K]
            else:
                tp = [_dot(Tm[i].astype(BF16), bd[i]) for i in K]
                Tm = [Tm[i] + tp[i] for i in K]
        rhs, kst, qst, gam = [], [], [], []
        for i in K:
            qf = q[i].astype(F32)
            kf = k[i].astype(F32)
            rhs_i, kst_i, qst_i, gam_i = [], [], [], []
            for g in range(G):
                j = g % 2
                bc, gc = beta_c[i][g], gc_c[i][g]
                e_c = jnp.exp(gc)
                rhs_i.append(jnp.concatenate([v[i][:, j * DK:(j + 1) * DK] * bc, kf * (bc * e_c)], axis=1))
                g_last = gc[0:1] if g >= 2 else gc[C - 1:C]
                kst_i.append((kf * jnp.exp(g_last - gc)).astype(BF16))
                qst_i.append((qf * (scale * e_c)).astype(BF16))
                gam_i.append(jnp.broadcast_to(jnp.exp(g_last), (1, DK)))
            rhs.append(jnp.concatenate(rhs_i, axis=0).astype(BF16))
            kst.append(kst_i)
            qst.append(qst_i)
            gam.append(gam_i)
        sol = [_dot(blockdiag(Tm[i].astype(BF16)), rhs[i]) for i in K]
        for i in K:
            hd, n = items[i]
            u_s[hd, n] = sol[i][:, :DK]
            qkd_s[hd, n] = jnp.where(incl4, qk4[i] * decay4[i], 0.0).astype(BF16)
            for g in range(G):
                wq_s[hd, n, g, 0:C, :] = sol[i][g * C:(g + 1) * C, DK:].astype(BF16)
                wq_s[hd, n, g, C:2 * C, :] = qst[i][g]
                kst_s[hd, n, g] = kst[i][g]
                gam_s[hd, n, g:g + 1, :] = gam[i][g]

    def phase1_group(i, _):
        phase1([P1_GROUP * i + c for c in range(P1_GROUP)])
        return 0

    lax.fori_loop(0, NC // P1_GROUP, phase1_group, 0)
    if NC % P1_GROUP:
        phase1(list(range(NC - NC % P1_GROUP, NC)))

    nw = nw_ref[...]

    def gated_norm(o, z):
        ys = []
        for j in range(2 * HPS):
            oj = o[:, j * DK:(j + 1) * DK]
            zj = z[:, j * DK:(j + 1) * DK].astype(F32)
            ys.append(oj * lax.rsqrt(jnp.mean(oj * oj, axis=-1, keepdims=True) + NORM_EPS) * nw * (zj * _sigmoid(zj)))
        return jnp.concatenate(ys, axis=1)

    def finalize(n):
        r0 = pl.multiple_of(n * C, C)
        o_ref[0, pl.ds(r0, C), :] = gated_norm(oacc_s[pl.ds(r0, C), :], z_ref[0, pl.ds(r0, C), :]).astype(o_ref.dtype)

    def phase2(it, finalize_previous):
        if finalize_previous:
            finalize(it - 1)
            finalize(NC - it)
        ops = []
        for hd in range(HPS):
            for g in range(G):
                n = it if g < 2 else NC - 1 - it
                ops.append((S_s[hd * G + g], wq_s[hd, n, g], u_s[hd, n, g * C:(g + 1) * C, :],
                            qkd_s[hd, n, :, g * C:(g + 1) * C], gam_s[hd, n, g:g + 1, :], kst_s[hd, n, g]))
        NI = HPS * G
        r_f = pl.multiple_of(it * C, C)
        r_b = pl.multiple_of((NC - 1 - it) * C, C)
        o_f = oacc_s[pl.ds(r_f, C), :]
        ws = [_dot(wq, S.astype(BF16)) for (S, wq, u, qkd, gam, kst) in ops]
        vb = [(ops[g][2] - ws[g][:C]).astype(BF16) for g in range(NI)]
        outs = [ws[g][C:] + _dot(ops[g][3], vb[g]) for g in range(NI)]
        states = [ops[g][0] * ops[g][4] + _dot_tn(ops[g][5], vb[g]) for g in range(NI)]
        for g in range(NI):
            S_s[g] = states[g]
        fwd = [outs[hd * G + j] for hd in range(HPS) for j in range(2)]
        bwd = [outs[hd * G + 2 + j] for hd in range(HPS) for j in range(2)]
        o_f = o_f + jnp.concatenate(fwd, axis=1)
        oacc_s[pl.ds(r_f, C), :] = o_f
        oacc_s[pl.ds(r_b, C), :] = oacc_s[pl.ds(r_b, C), :] + jnp.concatenate(bwd, axis=1)
        return 0

    lax.fori_loop(0, NC // 2 + 1, lambda it, c: phase2(it, False), 0)
    lax.fori_loop(NC // 2 + 1, NC, lambda it, c: phase2(it, True), 0)
    finalize(NC - 1)
    finalize(0)


_GDN_CHUNK_BYTES = CHUNK * GDN_DK * (4 * 4 + 8 * 2 + 4 * 2 + 4 * 2) + 8 * GDN_DK * 4
_GDN_PAIR_BUDGET = 26 * 1024 * 1024


def _gdn_delta(qkv, z, bg, norm_w):
    B, Tp, _ = qkv.shape
    NC = Tp // CHUNK
    KH = GDN_QK_HEADS
    HV = GDN_V_HEADS
    C = CHUNK
    HPS = 2 if NC * _GDN_CHUNK_BYTES * 2 <= _GDN_PAIR_BUDGET else 1
    NH = KH // HPS
    bgr = bg.reshape(B, NC, CHUNK, 4, KH, 2).transpose(0, 1, 4, 3, 5, 2).reshape(B, NC, KH, 8, CHUNK)
    return pl.pallas_call(
        functools.partial(_gdn_kernel, Tp=Tp, HPS=HPS),
        grid=(B, NH),
        in_specs=[pl.BlockSpec((1, Tp, HPS * GDN_DK), lambda b, h: (b, 0, h)),
                  pl.BlockSpec((1, Tp, HPS * GDN_DK), lambda b, h: (b, 0, NH + h)),
                  pl.BlockSpec((1, Tp, 2 * HPS * GDN_DK), lambda b, h: (b, 0, NH + h)),
                  pl.BlockSpec((1, Tp, 2 * HPS * GDN_DK), lambda b, h: (b, 0, h)),
                  pl.BlockSpec((1, Tp, 4 * HV), lambda b, h: (b, 0, 0)),
                  pl.BlockSpec((1, NC, HPS, 8, CHUNK), lambda b, h: (b, 0, h, 0, 0)),
                  pl.BlockSpec((1, GDN_DK), lambda b, h: (0, 0))],
        out_specs=pl.BlockSpec((1, Tp, 2 * HPS * GDN_DK), lambda b, h: (b, 0, h)),
        out_shape=jax.ShapeDtypeStruct((B, Tp, GDN_VAL_DIM), BF16),
        scratch_shapes=[pltpu.VMEM((4 * HPS, GDN_DK, GDN_DK), F32),
                        pltpu.VMEM((Tp, 2 * HPS * GDN_DK), F32),
                        pltpu.VMEM((HPS, NC, 4 * C, GDN_DK), F32),
                        pltpu.VMEM((HPS, NC, 4, 2 * C, GDN_DK), BF16),
                        pltpu.VMEM((HPS, NC, 4, C, GDN_DK), BF16),
                        pltpu.VMEM((HPS, NC, C, 4 * C), BF16),
                        pltpu.VMEM((HPS, NC, 8, GDN_DK), F32)],
        compiler_params=_cparams("parallel", "parallel"),
    )(qkv, qkv, qkv, z, bg, bgr, norm_w.astype(F32).reshape(1, GDN_DK))


def _gdn_mixer(hn, h, B, Tp, w_in, conv_w, a_log, dt_bias, norm_w, w_out, nw):
    nz = GDN_CONV_DIM + GDN_VAL_DIM
    w_in = w_in.astype(BF16)
    qkv = _proj_conv(hn, w_in[:, :GDN_CONV_DIM], conv_w, Tp, 2 * GDN_KEY_DIM)
    z = _matmul(hn, w_in[:, GDN_CONV_DIM:nz], BF16)
    bg = _gdn_gates(hn, w_in[:, nz:], a_log, dt_bias)
    o = _gdn_delta(qkv.reshape(B, Tp, -1), z.reshape(B, Tp, -1), bg.reshape(B, Tp, -1), norm_w)
    return _proj_res(o.reshape(B * Tp, -1), w_out.astype(BF16), h, nw)


def _trunk(x, meta_tokens, mix_norm, ffn_norm, final_norm, rg, na, gdn, ffn):
    B, T, D = x.shape
    Tp = T + HEAD_ROWS
    depth = mix_norm.shape[0]
    h3 = jnp.concatenate([jnp.zeros((B, PAD, D), F32),
                          jnp.broadcast_to(meta_tokens.astype(F32)[None], (B, N_META, D)),
                          x.astype(F32)], axis=1)
    h = h3.reshape(B * Tp, D)
    hn = _rmsnorm(h, mix_norm[0], BF16)
    for i in range(depth):
        kind, j = i % 3, i // 3
        if kind == 0:
            h, hn = _rglru_mixer(hn, h, B, Tp, *[p[j] for p in rg], ffn_norm[i])
        elif kind == 1:
            h, hn = _na_mixer(hn, h, B, Tp, *[p[j] for p in na], ffn_norm[i])
        else:
            h, hn = _gdn_mixer(hn, h, B, Tp, *[p[j] for p in gdn], ffn_norm[i])
        w_gate, w_up, conv_w, conv_b, w_down = [p[i] for p in ffn]
        next_norm = mix_norm[i + 1] if i + 1 < depth else final_norm
        h, hn = _conv_ffn(hn, h, w_gate.astype(BF16), w_up.astype(BF16), conv_w, conv_b,
                          w_down.astype(BF16), next_norm)
    return _final_norm(h, final_norm, B, T)


def kernel(x_prompt, x_sample, meta_tokens, mix_norm, ffn_norm, final_norm, rg_w_in, rg_conv_w, rg_conv_b, rg_w_a, rg_b_a, rg_w_i, rg_b_i, rg_lam, rg_w_out, na_w_qkv, na_rpb, na_meta_bias, na_w_o, gdn_w_in, gdn_conv_w, gdn_a_log, gdn_dt_bias, gdn_norm_w, gdn_w_out, ffn_w_gate, ffn_w_up, ffn_conv_w, ffn_conv_b, ffn_w_down):
    rg = (rg_w_in, rg_conv_w, rg_conv_b, rg_w_a, rg_b_a, rg_w_i, rg_b_i, rg_lam, rg_w_out)
    na_bias = jnp.stack([_na_bias_table(na_rpb[j]) for j in range(na_rpb.shape[0])])
    na = (na_w_qkv, na_bias, na_meta_bias, na_w_o)
    gdn = (gdn_w_in, gdn_conv_w, gdn_a_log, gdn_dt_bias, gdn_norm_w, gdn_w_out)
    ffn = (ffn_w_gate, ffn_w_up, ffn_conv_w, ffn_conv_b, ffn_w_down)
    y_prompt = _trunk(x_prompt, meta_tokens, mix_norm, ffn_norm, final_norm, rg, na, gdn, ffn)
    y_sample = _trunk(x_sample, meta_tokens, mix_norm, ffn_norm, final_norm, rg, na, gdn, ffn)
    return (y_prompt, y_sample)
```
